```python
import math
import jax
import jax.numpy as jnp
from jax import lax
import numpy as np

D_MODEL = 1024
BATCH = 8
SEQ = 8192
DEPTH = 4

GRID_W = 64
CTX_LEN = 256
N_EVEN = (DEPTH + 1) // 2
N_ODD = DEPTH // 2
N_VRES = N_ODD - 1
EPS = 1e-6

MLA_HEADS = 8
MLA_Q_LORA = 256
MLA_KV_LORA = 128
MLA_NOPE = 64
MLA_ROPE = 32
MLA_V = 64
MLA_QK = MLA_NOPE + MLA_ROPE
MLA_SCALE = MLA_QK ** -0.5
ROPE_BASE = 10000.0
Q_BLOCK = 128

GDN_HEADS = 4
GDN_DK = 128
GDN_DV = 128
GDN_CONV = 5
GDN_CHUNK = 64

RWKV_HEAD = 64
RWKV_HEADS = D_MODEL // RWKV_HEAD
RWKV_DECAY_LORA = max(32, int(round(1.8 * D_MODEL ** 0.5 / 32)) * 32)
RWKV_AAA_LORA = max(32, int(round(1.8 * D_MODEL ** 0.5 / 32)) * 32)
RWKV_MV_LORA = max(32, int(round(1.3 * D_MODEL ** 0.5 / 32)) * 32)
RWKV_GATE_LORA = max(32, int(round(0.6 * D_MODEL ** 0.8 / 32)) * 32)
GN_EPS = 64e-5

MOE_GROUPS = 4
MOE_PER_GROUP = 8
MOE_EXPERTS = MOE_GROUPS * MOE_PER_GROUP
MOE_TOPK = 2
MOE_HIDDEN = 512
MOE_BLOCK = 256

MLA_COLS = MLA_Q_LORA + MLA_KV_LORA + MLA_ROPE
GDN_QKV = GDN_HEADS * (2 * GDN_DK + GDN_DV)
GDN_Z = GDN_HEADS * GDN_DV
GDN_AB = 2 * 2 * GDN_HEADS
IN_COLS = MLA_COLS + GDN_QKV + GDN_Z + GDN_AB
MIX_OUT = MLA_HEADS * MLA_V + GDN_HEADS * GDN_DV

kernel_name = 'hybrid_mla_gdn_rwkv7_hmoe_dit'


def rms_norm(x, gain):
    xf = x.astype(jnp.float32)
    y = xf * lax.rsqrt(jnp.mean(xf * xf, -1, keepdims=True) + EPS)
    return (y * gain.astype(jnp.float32)).astype(x.dtype)


def l2_norm(x):
    xf = x.astype(jnp.float32)
    return (xf * lax.rsqrt(jnp.sum(xf * xf, -1, keepdims=True) + EPS)).astype(x.dtype)


def modulate(x, shift, scale):
    return x * (1 + scale) + shift


def _dir(t, d):
    return jnp.flip(t, axis=1) if d == 1 else t


def axial_rope_angles(n_tok):
    rows = n_tok // GRID_W
    row = jnp.repeat(jnp.arange(rows, dtype=jnp.float32), GRID_W)
    col = jnp.tile(jnp.arange(GRID_W, dtype=jnp.float32), rows)
    n_freq = MLA_ROPE // 4
    inv = ROPE_BASE ** (-jnp.arange(n_freq, dtype=jnp.float32) / n_freq)
    ang = jnp.stack([row[:, None] * inv, col[:, None] * inv], axis=1)
    return jnp.cos(ang), jnp.sin(ang)


def apply_axial_rope(x, cos, sin):
    B, S, H, _ = x.shape
    xr = x.reshape(B, S, H, 2, 2, MLA_ROPE // 4)
    x1, x2 = xr[..., 0, :], xr[..., 1, :]
    c = cos[None, :, None].astype(x.dtype)
    s = sin[None, :, None].astype(x.dtype)
    out = jnp.stack([x1 * c - x2 * s, x2 * c + x1 * s], axis=-2)
    return out.reshape(B, S, H, MLA_ROPE)


def mla_qkv(u, qa_g, w_qb, kva_g, w_kvb, qn_g, kn_g, rope):
    B, T, _ = u.shape
    H = MLA_HEADS
    c_q = rms_norm(u[..., :MLA_Q_LORA], qa_g)
    c_kv = rms_norm(u[..., MLA_Q_LORA:MLA_Q_LORA + MLA_KV_LORA], kva_g)
    k_pe = u[..., MLA_Q_LORA + MLA_KV_LORA:]
    q = (c_q @ w_qb).reshape(B, T, H, MLA_QK)
    kv = (c_kv @ w_kvb).reshape(B, T, H, MLA_NOPE + MLA_V)
    k = jnp.concatenate([kv[..., :MLA_NOPE], jnp.broadcast_to(k_pe[:, :, None, :], (B, T, H, MLA_ROPE))], -1)
    q = rms_norm(q, qn_g)
    k = rms_norm(k, kn_g)
    if rope is not None:
        cos, sin = rope
        q = jnp.concatenate([q[..., :MLA_NOPE], apply_axial_rope(q[..., MLA_NOPE:], cos, sin)], -1)
        k = jnp.concatenate([k[..., :MLA_NOPE], apply_axial_rope(k[..., MLA_NOPE:], cos, sin)], -1)
    return q, k, kv[..., MLA_NOPE:]


def latent_attention(q, k_lat, v_lat, k_ctx, v_ctx):
    B, S, H, Dh = q.shape
    k = jnp.concatenate([k_lat, k_ctx], 1)
    v = jnp.concatenate([v_lat, v_ctx], 1)
    nb = S // Q_BLOCK
    qb = jnp.swapaxes(q.reshape(B, nb, Q_BLOCK, H, Dh), 0, 1)

    def block(qi):
        s = jnp.einsum('bqhd,bkhd->bhqk', qi, k).astype(jnp.float32) * MLA_SCALE
        p = jax.nn.softmax(s, axis=-1).astype(v.dtype)
        return jnp.einsum('bhqk,bkhd->bqhd', p, v)

    o = lax.map(block, qb)
    return jnp.swapaxes(o, 0, 1).reshape(B, S, H * MLA_V)


def context_attention(q, k, v):
    B, L, H, _ = q.shape
    s = jnp.einsum('bqhd,bkhd->bhqk', q, k).astype(jnp.float32) * MLA_SCALE
    p = jax.nn.softmax(s, axis=-1).astype(v.dtype)
    return jnp.einsum('bhqk,bkhd->bqhd', p, v).reshape(B, L, H * MLA_V)


def short_conv(x, w):
    C = x.shape[-1]
    return lax.conv_general_dilated(x, w[:, None, :].astype(x.dtype), window_strides=(1,),
                                    padding=[(GDN_CONV // 2, GDN_CONV // 2)],
                                    dimension_numbers=('NWC', 'WIO', 'NWC'), feature_group_count=C)


def gdn_prep(u, conv_w, a_log, dt_bias):
    B, T, _ = u.shape
    nk = GDN_HEADS * GDN_DK
    qkv = jax.nn.silu(short_conv(u[..., :GDN_QKV], conv_w))
    q = l2_norm(qkv[..., :nk].reshape(B, T, GDN_HEADS, GDN_DK)) * GDN_DK ** -0.5
    k = l2_norm(qkv[..., nk:2 * nk].reshape(B, T, GDN_HEADS, GDN_DK))
    v = qkv[..., 2 * nk:].reshape(B, T, GDN_HEADS, GDN_DV)
    z = u[..., GDN_QKV:GDN_QKV + GDN_Z].reshape(B, T, GDN_HEADS, GDN_DV)
    ab = u[..., GDN_QKV + GDN_Z:].astype(jnp.float32).reshape(B, T, 2, 2, GDN_HEADS)
    g = -jnp.exp(a_log.astype(jnp.float32)) * jax.nn.softplus(ab[:, :, :, 0] + dt_bias.astype(jnp.float32))
    beta = jax.nn.sigmoid(ab[:, :, :, 1])
    return q, k, v, z, g, beta


def gated_delta_chunked(q, k, v, g, beta, s0):
    f32 = jnp.float32
    B, T, H, DK = q.shape
    DV = v.shape[-1]
    C = GDN_CHUNK
    n = T // C

    def blocks(t):
        t = t.astype(f32).reshape((B, n, C) + t.shape[2:])
        return jnp.moveaxis(jnp.moveaxis(t, 1, 0), 2, 3)

    q, k, v, g, beta = (blocks(t) for t in (q, k, v, g, beta))
    gc = jnp.cumsum(g, axis=-1)
    causal = jnp.tril(jnp.ones((C, C), bool))
    strict = jnp.tril(jnp.ones((C, C), bool), -1)
    decay = jnp.exp(jnp.where(causal, gc[..., :, None] - gc[..., None, :], -jnp.inf))
    kb = k * beta[..., None]
    lower = jnp.where(strict, jnp.einsum('nbhid,nbhjd->nbhij', kb, k) * decay, 0.0)
    rhs = jnp.concatenate([v * beta[..., None], kb * jnp.exp(gc)[..., None]], -1)
    sol = lax.linalg.triangular_solve(lower + jnp.eye(C, dtype=f32), rhs, left_side=True, lower=True,
                                      unit_diagonal=True)
    u_, w_ = sol[..., :DV], sol[..., DV:]
    a_qk = jnp.where(causal, jnp.einsum('nbhid,nbhjd->nbhij', q, k) * decay, 0.0)

    def step(S, xs):
        qi, ki, ui, wi, gi, ai = xs
        v_new = ui - jnp.einsum('bhck,bhkv->bhcv', wi, S)
        o = jnp.einsum('bhck,bhkv->bhcv', qi * jnp.exp(gi)[..., None], S) + jnp.einsum('bhij,bhjv->bhiv', ai, v_new)
        g_last = gi[..., -1:]
        S = S * jnp.exp(g_last)[..., None] + jnp.einsum('bhck,bhcv->bhkv', ki * jnp.exp(g_last - gi)[..., None], v_new)
        return S, o

    S, o = lax.scan(step, s0.astype(f32), (q, k, u_, w_, gc, a_qk))
    o = jnp.moveaxis(jnp.moveaxis(o, 3, 2), 0, 1).reshape(B, T, H, DV)
    return o, S


def gdn_bidirectional(uc, ul, conv_w, a_log, dt_bias, out_g):
    qc, kc, vc, zc, gc, bc = gdn_prep(uc, conv_w, a_log, dt_bias)
    ql, kl, vl, zl, gl, bl = gdn_prep(ul, conv_w, a_log, dt_bias)
    B = ul.shape[0]
    oc = 0.0
    ol = 0.0
    for d in range(2):
        s0 = jnp.zeros((B, GDN_HEADS, GDN_DK, GDN_DV), jnp.float32)
        o_c, s_c = gated_delta_chunked(_dir(qc, d), _dir(kc, d), _dir(vc, d), _dir(gc[:, :, d], d),
                                       _dir(bc[:, :, d], d), s0)
        o_l, _ = gated_delta_chunked(_dir(ql, d), _dir(kl, d), _dir(vl, d), _dir(gl[:, :, d], d),
                                     _dir(bl[:, :, d], d), s_c)
        oc = oc + _dir(o_c, d)
        ol = ol + _dir(o_l, d)

    def finish(o, z):
        Bz, T = z.shape[:2]
        return (rms_norm(o, out_g).astype(z.dtype) * jax.nn.silu(z)).reshape(Bz, T, GDN_HEADS * GDN_DV)

    return finish(oc, zc), finish(ol, zl)


def attn_delta_mixer(xc, xl, cos, sin, w_in, w_out, qa_g, w_qb, kva_g, w_kvb, qn_g, kn_g,
                     conv_w, a_log, dt_bias, out_g, need_ctx):
    uc = xc @ w_in
    ul = xl @ w_in
    qc, kc, vc = mla_qkv(uc[..., :MLA_COLS], qa_g, w_qb, kva_g, w_kvb, qn_g, kn_g, None)
    ql, kl, vl = mla_qkv(ul[..., :MLA_COLS], qa_g, w_qb, kva_g, w_kvb, qn_g, kn_g, (cos, sin))
    a_lat = latent_attention(ql, kl, vl, kc, vc)
    d_ctx, d_lat = gdn_bidirectional(uc[..., MLA_COLS:], ul[..., MLA_COLS:], conv_w, a_log, dt_bias, out_g)
    y_lat = jnp.concatenate([a_lat, d_lat], -1) @ w_out
    y_ctx = None
    if need_ctx:
        y_ctx = jnp.concatenate([context_attention(qc, kc, vc), d_ctx], -1) @ w_out
    return y_ctx, y_lat


def centred_shift(x):
    xp = jnp.pad(x, ((0, 0), (1, 1), (0, 0)))
    return 0.5 * (xp[:, :-2] + xp[:, 2:]) - x


def rwkv7_project(x, v_first, mu, wr, wk, wv, w0, w1, w2, a0, a1, a2, g1, g2, k_k, k_a, vres):
    B, T, D = x.shape
    hd = lambda t: t.reshape(B, T, RWKV_HEADS, RWKV_HEAD)
    xx = centred_shift(x)
    xr, xw, xk, xv, xa, xg = [x + xx * mu[i] for i in range(6)]
    r = xr @ wr
    k = xk @ wk
    v = xv @ wv
    if vres is None:
        v_first = v
    else:
        v0, v1, v2 = vres
        v = v + (v_first - v) * jax.nn.sigmoid(v0 + (xv @ v1) @ v2)
    gate = jax.nn.sigmoid(xg @ g1) @ g2
    kk = l2_norm(hd(k * k_k))
    decays, keys, rates = [], [], []
    for d in range(2):
        w_log = -jax.nn.softplus(-(w0[d] + jnp.tanh(xw @ w1[d]) @ w2[d])) - 0.5
        a = jax.nn.sigmoid(a0[d] + (xa @ a1[d]) @ a2[d])
        decays.append(hd(jnp.exp(-jnp.exp(w_log.astype(jnp.float32)))))
        keys.append(hd(k * (1 + (a - 1) * k_a)))
        rates.append(hd(a))
    return hd(r), hd(v), gate, kk, decays, keys, rates, v_first


def rwkv7_scan(r, w, k, v, a, b, s0):
    def step(S, xs):
        rt, wt, kt, vt, at, bt = xs
        sa = jnp.einsum('bhvk,bhk->bhv', S, at)
        S = S * wt[:, :, None, :] + sa[..., None] * bt[:, :, None, :] + vt[..., None] * kt[:, :, None, :]
        return S, jnp.einsum('bhvk,bhk->bhv', S, rt)

    xs = tuple(jnp.moveaxis(t.astype(jnp.float32), 1, 0) for t in (r, w, k, v, a, b))
    S, y = lax.scan(step, s0, xs)
    return jnp.moveaxis(y, 0, 1), S


def rwkv7_scan_inputs(p, d):
    r, v, gate, kk, decays, keys, rates, _ = p
    return (_dir(r, d), _dir(decays[d], d), _dir(keys[d], d), _dir(v, d), _dir(-kk, d), _dir(kk * rates[d], d))


def rwkv7_output(y, p, r_k, ln_w, ln_b, wo):
    r, v, gate, kk, decays, keys, rates, _ = p
    B, T, H, N = y.shape
    mean = jnp.mean(y, -1, keepdims=True)
    var = jnp.mean(jnp.square(y - mean), -1, keepdims=True)
    yn = ((y - mean) * lax.rsqrt(var + GN_EPS)).astype(gate.dtype).reshape(B, T, H * N) * ln_w + ln_b
    k_bonus = 0.5 * (keys[0] + keys[1])
    bonus = (jnp.sum(r * k_bonus * r_k, -1, keepdims=True) * v).reshape(B, T, H * N)
    return ((yn + bonus) * gate) @ wo


def rwkv7_mixer(xc, xl, vf_c, vf_l, mu, wr, wk, wv, wo, w0, w1, w2, a0, a1, a2, g1, g2, k_k, k_a, r_k,
                ln_w, ln_b, vres, need_ctx):
    pc = rwkv7_project(xc, vf_c, mu, wr, wk, wv, w0, w1, w2, a0, a1, a2, g1, g2, k_k, k_a, vres)
    pl = rwkv7_project(xl, vf_l, mu, wr, wk, wv, w0, w1, w2, a0, a1, a2, g1, g2, k_k, k_a, vres)
    B = xl.shape[0]
    yc = 0.0
    yl = 0.0
    for d in range(2):
        s0 = jnp.zeros((B, RWKV_HEADS, RWKV_HEAD, RWKV_HEAD), jnp.float32)
        o_c, s_c = rwkv7_scan(*rwkv7_scan_inputs(pc, d), s0)
        o_l, _ = rwkv7_scan(*rwkv7_scan_inputs(pl, d), s_c)
        yc = yc + _dir(o_c, d)
        yl = yl + _dir(o_l, d)
    y_lat = rwkv7_output(yl, pl, r_k, ln_w, ln_b, wo)
    y_ctx = rwkv7_output(yc, pc, r_k, ln_w, ln_b, wo) if need_ctx else None
    return y_ctx, y_lat, pc[-1], pl[-1]


def hier_moe(h, w_group, b_group, w_expert, b_expert, w1, w3, w2):
    N, D = h.shape
    pg = jax.nn.softmax((h @ w_group).astype(jnp.float32) + b_group.astype(jnp.float32), axis=-1)
    pg_top, g_idx = lax.top_k(pg, 1)
    le = (h @ w_expert).astype(jnp.float32) + b_expert.astype(jnp.float32)
    sel = g_idx * MOE_PER_GROUP + jnp.arange(MOE_PER_GROUP)[None, :]
    pe = jax.nn.softmax(jnp.take_along_axis(le, sel, axis=1), axis=-1)
    pe_top, e_loc = lax.top_k(pe, MOE_TOPK)
    wts = pg_top * pe_top / jnp.sum(pe_top, -1, keepdims=True)
    eid = (g_idx * MOE_PER_GROUP + e_loc).reshape(-1)
    tok = jnp.repeat(jnp.arange(N, dtype=jnp.int32), MOE_TOPK)
    w_f = wts.reshape(-1).astype(h.dtype)
    A = N * MOE_TOPK
    onehot = (eid[:, None] == jnp.arange(MOE_EXPERTS)[None, :]).astype(jnp.int32)
    rank = jnp.take_along_axis(jnp.cumsum(onehot, 0), eid[:, None], 1)[:, 0] - 1
    counts = jnp.sum(onehot, 0)
    padded = (counts + MOE_BLOCK - 1) // MOE_BLOCK * MOE_BLOCK
    pend = jnp.cumsum(padded)
    dest = (pend - padded)[eid] + rank
    n_blocks = -(-A // MOE_BLOCK) + MOE_EXPERTS
    slot_tok = jnp.zeros((n_blocks * MOE_BLOCK,), jnp.int32).at[dest].set(tok)
    slot_w = jnp.zeros((n_blocks * MOE_BLOCK,), h.dtype).at[dest].set(w_f)
    blk_e = jnp.minimum(jnp.searchsorted(pend, jnp.arange(n_blocks, dtype=jnp.int32) * MOE_BLOCK, side='right'),
                        MOE_EXPERTS - 1)

    def body(out, xs):
        toks, ws, e = xs
        xb = h[toks]
        hid = jax.nn.silu(xb @ w1[e]) * (xb @ w3[e])
        return out.at[toks].add((hid @ w2[e]) * ws[:, None]), None

    out, _ = lax.scan(body, jnp.zeros_like(h),
                      (slot_tok.reshape(n_blocks, MOE_BLOCK), slot_w.reshape(n_blocks, MOE_BLOCK), blk_e))
    return out


def setup_inputs(seed: int = 0) -> dict:
    key = jax.random.key(seed)
    keys = iter(jax.random.split(key, 64))
    f32 = jnp.float32

    def normal(shape, std):
        return jax.random.normal(next(keys), shape, f32) * std

    def uniform(shape, lo, hi):
        return jax.random.uniform(next(keys), shape, f32, lo, hi)

    def gain(shape):
        return 1.0 + normal(shape, 0.02)

    D = D_MODEL
    dt = jnp.exp(uniform((N_EVEN, 2, GDN_HEADS), math.log(1e-3), math.log(1e-1)))
    return {
        'x': normal((BATCH, SEQ, D), 1.0),
        'c': normal((BATCH, D), 1.0),
        'ctx': normal((BATCH, CTX_LEN, D), 1.0),
        'c_ctx': normal((D,), 1.0),
        'ada_w': normal((DEPTH, D, 6 * D), 0.5 * D ** -0.5),
        'ada_b': normal((DEPTH, 6 * D), 0.02),
        'norm_mix': gain((DEPTH, D)),
        'norm_ffn': gain((DEPTH, D)),
        'hy_w_in': normal((N_EVEN, D, IN_COLS), D ** -0.5),
        'hy_w_out': normal((N_EVEN, MIX_OUT, D), MIX_OUT ** -0.5),
        'mla_qa_norm': gain((N_EVEN, MLA_Q_LORA)),
        'mla_w_qb': normal((N_EVEN, MLA_Q_LORA, MLA_HEADS * MLA_QK), MLA_Q_LORA ** -0.5),
        'mla_kva_norm': gain((N_EVEN, MLA_KV_LORA)),
        'mla_w_kvb': normal((N_EVEN, MLA_KV_LORA, MLA_HEADS * (MLA_NOPE + MLA_V)), MLA_KV_LORA ** -0.5),
        'mla_q_norm': gain((N_EVEN, MLA_QK)),
        'mla_k_norm': gain((N_EVEN, MLA_QK)),
        'gdn_conv': normal((N_EVEN, GDN_CONV, GDN_QKV), GDN_CONV ** -0.5),
        'gdn_a_log': jnp.log(uniform((N_EVEN, 2, GDN_HEADS), 1.0, 16.0)),
        'gdn_dt_bias': dt + jnp.log(-jnp.expm1(-dt)),
        'gdn_out_norm': gain((N_EVEN, GDN_DV)),
        'rk_mu': uniform((N_ODD, 6, D), 0.0, 1.0),
        'rk_wr': normal((N_ODD, D, D), D ** -0.5),
        'rk_wk': normal((N_ODD, D, D), D ** -0.5),
        'rk_wv': normal((N_ODD, D, D), D ** -0.5),
        'rk_wo': normal((N_ODD, D, D), D ** -0.5),
        'rk_w0': uniform((N_ODD, 2, D), -2.0, 1.0),
        'rk_w1': normal((N_ODD, 2, D, RWKV_DECAY_LORA), D ** -0.5),
        'rk_w2': normal((N_ODD, 2, RWKV_DECAY_LORA, D), 0.5 * RWKV_DECAY_LORA ** -0.5),
        'rk_a0': normal((N_ODD, 2, D), 0.5),
        'rk_a1': normal((N_ODD, 2, D, RWKV_AAA_LORA), D ** -0.5),
        'rk_a2': normal((N_ODD, 2, RWKV_AAA_LORA, D), 0.5 * RWKV_AAA_LORA ** -0.5),
        'rk_g1': normal((N_ODD, D, RWKV_GATE_LORA), D ** -0.5),
        'rk_g2': normal((N_ODD, RWKV_GATE_LORA, D), RWKV_GATE_LORA ** -0.5),
        'rk_kk': 0.85 + normal((N_ODD, D), 0.05),
        'rk_ka': 1.0 + normal((N_ODD, D), 0.05),
        'rk_rk': normal((N_ODD, RWKV_HEADS, RWKV_HEAD), 0.1),
        'rk_ln_w': gain((N_ODD, D)),
        'rk_ln_b': normal((N_ODD, D), 0.02),
        'rk_v0': normal((N_VRES, D), 0.5),
        'rk_v1': normal((N_VRES, D, RWKV_MV_LORA), D ** -0.5),
        'rk_v2': normal((N_VRES, RWKV_MV_LORA, D), 0.5 * RWKV_MV_LORA ** -0.5),
        'moe_w_group': normal((DEPTH, D, MOE_GROUPS), D ** -0.5),
        'moe_b_group': normal((DEPTH, MOE_GROUPS), 0.01),
        'moe_w_expert': normal((DEPTH, D, MOE_EXPERTS), D ** -0.5),
        'moe_b_expert': normal((DEPTH, MOE_EXPERTS), 0.01),
        'moe_w1': normal((DEPTH, MOE_EXPERTS, D, MOE_HIDDEN), D ** -0.5),
        'moe_w3': normal((DEPTH, MOE_EXPERTS, D, MOE_HIDDEN), D ** -0.5),
        'moe_w2': normal((DEPTH, MOE_EXPERTS, MOE_HIDDEN, D), MOE_HIDDEN ** -0.5),
    }


def reference(x, c, ctx, c_ctx, ada_w, ada_b, norm_mix, norm_ffn,
              hy_w_in, hy_w_out, mla_qa_norm, mla_w_qb, mla_kva_norm, mla_w_kvb, mla_q_norm, mla_k_norm,
              gdn_conv, gdn_a_log, gdn_dt_bias, gdn_out_norm,
              rk_mu, rk_wr, rk_wk, rk_wv, rk_wo, rk_w0, rk_w1, rk_w2, rk_a0, rk_a1, rk_a2,
              rk_g1, rk_g2, rk_kk, rk_ka, rk_rk, rk_ln_w, rk_ln_b, rk_v0, rk_v1, rk_v2,
              moe_w_group, moe_b_group, moe_w_expert, moe_b_expert, moe_w1, moe_w3, moe_w2):
    B, S, D = x.shape
    L = ctx.shape[1]
    cos, sin = axial_rope_angles(S)
    sc_lat = jax.nn.silu(c)
    sc_ctx = jax.nn.silu(c_ctx)[None]
    h_lat, h_ctx = x, ctx
    vf_lat = None
    vf_ctx = None
    for l in range(DEPTH):
        need_ctx = l < DEPTH - 1
        m_lat = (sc_lat @ ada_w[l] + ada_b[l]).reshape(B, 1, 6, D)
        m_ctx = (sc_ctx @ ada_w[l] + ada_b[l]).reshape(1, 1, 6, D)
        u_lat = modulate(rms_norm(h_lat, norm_mix[l]), m_lat[:, :, 0], m_lat[:, :, 1])
        u_ctx = modulate(rms_norm(h_ctx, norm_mix[l]), m_ctx[:, :, 0], m_ctx[:, :, 1])
        j = l // 2
        if l % 2 == 0:
            y_ctx, y_lat = attn_delta_mixer(u_ctx, u_lat, cos, sin, hy_w_in[j], hy_w_out[j], mla_qa_norm[j],
                                            mla_w_qb[j], mla_kva_norm[j], mla_w_kvb[j], mla_q_norm[j],
                                            mla_k_norm[j], gdn_conv[j], gdn_a_log[j], gdn_dt_bias[j],
                                            gdn_out_norm[j], need_ctx)
        else:
            vres = None if j == 0 else (rk_v0[j - 1], rk_v1[j - 1], rk_v2[j - 1])
            y_ctx, y_lat, v_c, v_l = rwkv7_mixer(u_ctx, u_lat, vf_ctx, vf_lat, rk_mu[j], rk_wr[j], rk_wk[j],
                                                 rk_wv[j], rk_wo[j], rk_w0[j], rk_w1[j], rk_w2[j], rk_a0[j],
                                                 rk_a1[j], rk_a2[j], rk_g1[j], rk_g2[j], rk_kk[j], rk_ka[j],
                                                 rk_rk[j], rk_ln_w[j], rk_ln_b[j], vres, need_ctx)
            if j == 0:
                vf_ctx, vf_lat = v_c, v_l
        h_lat = h_lat + m_lat[:, :, 2] * y_lat
        f_lat = modulate(rms_norm(h_lat, norm_ffn[l]), m_lat[:, :, 3], m_lat[:, :, 4]).reshape(B * S, D)
        n_ctx_tok = 0
        if need_ctx:
            h_ctx = h_ctx + m_ctx[:, :, 2] * y_ctx
            f_ctx = modulate(rms_norm(h_ctx, norm_ffn[l]), m_ctx[:, :, 3], m_ctx[:, :, 4]).reshape(B * L, D)
            tokens = jnp.concatenate([f_ctx, f_lat], 0)
            n_ctx_tok = B * L
        else:
            tokens = f_lat
        moe_out = hier_moe(tokens, moe_w_group[l], moe_b_group[l], moe_w_expert[l], moe_b_expert[l],
                           moe_w1[l], moe_w3[l], moe_w2[l])
        h_lat = h_lat + m_lat[:, :, 5] * moe_out[n_ctx_tok:].reshape(B, S, D)
        if need_ctx:
            h_ctx = h_ctx + m_ctx[:, :, 5] * moe_out[:n_ctx_tok].reshape(B, L, D)
    return h_lat
```

```python
import functools
import math

import jax
import jax.numpy as jnp
from jax import lax
from jax.experimental import pallas as pl
from jax.experimental.pallas import tpu as pltpu

F32 = jnp.float32
BF16 = jnp.bfloat16
HI = lax.Precision.HIGHEST

DEPTH = 4
GRID_W = 64
EPS = 1e-6

MLA_HEADS = 8
MLA_Q_LORA = 256
MLA_KV_LORA = 128
MLA_NOPE = 64
MLA_ROPE = 32
MLA_V = 64
MLA_QK = MLA_NOPE + MLA_ROPE
MLA_SCALE = MLA_QK ** -0.5
ROPE_BASE = 10000.0
MLA_PAD = 128

GDN_HEADS = 4
GDN_DK = 128
GDN_DV = 128
GDN_CONV = 5
GDN_CHUNK = 64

RWKV_HEAD = 64
RWKV_CHUNK = 64
GN_EPS = 64e-5

MOE_GROUPS = 4
MOE_PER_GROUP = 8
MOE_EXPERTS = MOE_GROUPS * MOE_PER_GROUP
MOE_TOPK = 2
MOE_BLOCK = 256

MLA_COLS = MLA_Q_LORA + MLA_KV_LORA + MLA_ROPE
GDN_QKV = GDN_HEADS * (2 * GDN_DK + GDN_DV)
GDN_Z = GDN_HEADS * GDN_DV
GDN_AB = 2 * 2 * GDN_HEADS

VMEM_LIMIT_BYTES = 48 * 1024 * 1024

GDN_PASSES = 3
RWKV_PASSES = 3


def _cparams(*sem):
    return pltpu.CompilerParams(dimension_semantics=sem, vmem_limit_bytes=VMEM_LIMIT_BYTES)


def _pick(n, cands):
    for c in cands:
        if n % c == 0:
            return c
    return n


def _split(a):
    hi = a.astype(BF16)
    lo = (a - hi.astype(F32)).astype(BF16)
    return hi, lo


def _dg(a, b, dn, passes):
    if passes == 6:
        return lax.dot_general(a, b, dn, precision=HI, preferred_element_type=F32)
    if passes == 1:
        return lax.dot_general(a.astype(BF16), b.astype(BF16), dn, preferred_element_type=F32)
    ah, al = _split(a)
    bh, bl = _split(b)
    d = functools.partial(lax.dot_general, dimension_numbers=dn, preferred_element_type=F32)
    return d(ah, bh) + d(al, bh) + d(ah, bl)


_NN = (((1,), (0,)), ((), ()))
_NT = (((1,), (1,)), ((), ()))
_TN = (((0,), (0,)), ((), ()))
_BNN = (((2,), (1,)), ((0,), (0,)))
_BNT = (((2,), (2,)), ((0,), (0,)))
_BTN = (((1,), (1,)), ((0,), (0,)))


def _mm_body(x_ref, w_ref, o_ref, *, hi):
    if hi:
        o_ref[...] = jnp.dot(x_ref[...], w_ref[...], precision=HI, preferred_element_type=F32)
    else:
        o_ref[...] = jnp.dot(x_ref[...].astype(BF16), w_ref[...].astype(BF16),
                             preferred_element_type=F32)


def mm(x, w, hi=False):
    M, K = x.shape
    N = w.shape[1]
    tm = _pick(M, (512, 256, 128, 64, 32, 16, 8))
    tn = _pick(N, (512, 384, 256, 128))
    return pl.pallas_call(
        functools.partial(_mm_body, hi=hi),
        grid=(M // tm, N // tn),
        in_specs=[pl.BlockSpec((tm, K), lambda i, j: (i, 0)),
                  pl.BlockSpec((K, tn), lambda i, j: (0, j))],
        out_specs=pl.BlockSpec((tm, tn), lambda i, j: (i, j)),
        out_shape=jax.ShapeDtypeStruct((M, N), F32),
        compiler_params=_cparams("parallel", "parallel"),
        name="dense_mm",
    )(x, w)


def _attn_body(q_ref, k_ref, v_ref, o_ref, m_ref, l_ref, acc_ref, *, scale):
    ki = pl.program_id(3)

    @pl.when(ki == 0)
    def _():
        m_ref[...] = jnp.full(m_ref.shape, -1e30, F32)
        l_ref[...] = jnp.zeros(l_ref.shape, F32)
        acc_ref[...] = jnp.zeros(acc_ref.shape, F32)

    v = v_ref[0]
    for h in range(2):
        q = q_ref[0, :, h * MLA_PAD:(h + 1) * MLA_PAD]
        k = k_ref[0, :, h * MLA_PAD:(h + 1) * MLA_PAD]
        s = lax.dot_general(q, k, _NT, preferred_element_type=F32) * scale
        m_prev = m_ref[h]
        m_new = jnp.maximum(m_prev, jnp.max(s, axis=-1, keepdims=True))
        alpha = jnp.exp(m_prev - m_new)
        p = jnp.exp(s - m_new)
        l_ref[h] = alpha * l_ref[h] + jnp.sum(p, axis=-1, keepdims=True)
        acc_ref[h] = alpha * acc_ref[h] + jnp.dot(p.astype(BF16), v, preferred_element_type=F32)
        m_ref[h] = m_new

    @pl.when(ki == pl.num_programs(3) - 1)
    def _():
        lane = lax.broadcasted_iota(jnp.int32, acc_ref.shape[1:], 1)
        o0 = acc_ref[0] / l_ref[0]
        o1 = acc_ref[1] / l_ref[1]
        o_ref[0] = jnp.where(lane < MLA_V, o0, o1)


def attention(q, k, v):
    B, Sq, _ = q.shape
    Sk = k.shape[1]
    tq = _pick(Sq, (512, 256, 128))
    tk = _pick(Sk, (768, 512, 384, 256, 128))
    return pl.pallas_call(
        functools.partial(_attn_body, scale=MLA_SCALE),
        grid=(B, MLA_HEADS // 2, Sq // tq, Sk // tk),
        in_specs=[pl.BlockSpec((1, tq, 2 * MLA_PAD), lambda b, p, i, j: (b, i, p)),
                  pl.BlockSpec((1, tk, 2 * MLA_PAD), lambda b, p, i, j: (b, j, p)),
                  pl.BlockSpec((1, tk, 2 * MLA_V), lambda b, p, i, j: (b, j, p))],
        out_specs=pl.BlockSpec((1, tq, 2 * MLA_V), lambda b, p, i, j: (b, i, p)),
        out_shape=jax.ShapeDtypeStruct((B, Sq, MLA_HEADS * MLA_V), F32),
        scratch_shapes=[pltpu.VMEM((2, tq, 1), F32), pltpu.VMEM((2, tq, 1), F32),
                        pltpu.VMEM((2, tq, 2 * MLA_V), F32)],
        compiler_params=_cparams("parallel", "parallel", "parallel", "arbitrary"),
        name="mla_attention",
    )(q, k, v)


def _tri_masks(C, rev):
    row = lax.broadcasted_iota(jnp.int32, (C, C), 0)
    col = lax.broadcasted_iota(jnp.int32, (C, C), 1)
    if rev:
        return row <= col, row < col
    return row >= col, row > col


def _neumann_inverse(nil, dn, passes):
    C = nil.shape[-1]
    eye = (lax.broadcasted_iota(jnp.int32, (C, C), 0) ==
           lax.broadcasted_iota(jnp.int32, (C, C), 1)).astype(F32)
    x = eye + nil
    p = nil
    for _ in range(int(math.log2(C)) - 1):
        p = _dg(p, p, dn, passes)
        x = x + _dg(x, p, dn, passes)
    return x


def _gdn_body(qf, kf, vf, gcf, bcf, grf, qb, kb, vb, gcb, bcb, grb, of_ref, ob_ref, s_ref, *, passes):
    C = GDN_CHUNK
    H = GDN_HEADS

    @pl.when(pl.program_id(1) == 0)
    def _():
        s_ref[...] = jnp.zeros(s_ref.shape, F32)

    dirs = ((qf, kf, vf, gcf, bcf, grf, of_ref), (qb, kb, vb, gcb, bcb, grb, ob_ref))
    for d, (q_ref, k_ref, v_ref, gc_ref, bc_ref, gr_ref, o_ref) in enumerate(dirs):
        rev = d == 1
        incl, strict = _tri_masks(C, rev)
        tri = incl.astype(F32)
        gcum_col = _dg(tri, gc_ref[0], _NN, 6)
        gcum_row = _dg(gr_ref[0, 0], tri, _NT, 6)
        beta_all = bc_ref[0]
        for h in range(H):
            idx = d * H + h
            gc = gcum_col[:, idx:idx + 1]
            gr = gcum_row[idx:idx + 1, :]
            beta = beta_all[:, idx:idx + 1]
            q = q_ref[0, :, h * GDN_DK:(h + 1) * GDN_DK]
            k = k_ref[0, :, h * GDN_DK:(h + 1) * GDN_DK]
            v = v_ref[0, :, h * GDN_DV:(h + 1) * GDN_DV]
            decay = jnp.exp(jnp.where(incl, gc - gr, -1e30))
            kbeta = k * beta
            lower = jnp.where(strict, _dg(kbeta, k, _NT, passes) * decay, 0.0)
            tinv = _neumann_inverse(-lower, _NN, passes)
            eg = jnp.exp(gc)
            rhs = jnp.concatenate([v * beta, kbeta * eg], axis=-1)
            sol = _dg(tinv, rhs, _NN, passes)
            u = sol[:, :GDN_DV]
            w = sol[:, GDN_DV:]
            aqk = jnp.where(incl, _dg(q, k, _NT, passes) * decay, 0.0)
            s = s_ref[d, h]
            v_new = u - _dg(w, s, _NN, passes)
            o = _dg(q * eg, s, _NN, passes) + _dg(aqk, v_new, _NN, passes)
            g_last = gc[0:1, :] if rev else gc[C - 1:C, :]
            s_ref[d, h] = s * jnp.exp(g_last) + _dg(k * jnp.exp(g_last - gc), v_new, _TN, passes)
            o_ref[0, :, h * GDN_DV:(h + 1) * GDN_DV] = o


def _rev_chunk(i, ncc, nc):
    return jnp.where(i < ncc, ncc - 1 - i, nc - 1 + ncc - i)


def gdn_scan(q, k, v, g, beta, n_ctx):
    B, T, _ = q.shape
    C = GDN_CHUNK
    nc = T // C
    ncc = n_ctx // C
    gcol = g.reshape(B, T, 2 * GDN_HEADS)
    bcol = beta.reshape(B, T, 2 * GDN_HEADS)
    grow = jnp.swapaxes(gcol.reshape(B, nc, C, 2 * GDN_HEADS), 2, 3)
    fwd = lambda b, i: (b, i, 0)
    bwd = lambda b, i: (b, _rev_chunk(i, ncc, nc), 0)
    fwd4 = lambda b, i: (b, i, 0, 0)
    bwd4 = lambda b, i: (b, _rev_chunk(i, ncc, nc), 0, 0)
    wide = q.shape[-1]
    wv = v.shape[-1]

    def specs(m3, m4):
        return [pl.BlockSpec((1, C, wide), m3), pl.BlockSpec((1, C, wide), m3), pl.BlockSpec((1, C, wv), m3),
                pl.BlockSpec((1, C, 2 * GDN_HEADS), m3), pl.BlockSpec((1, C, 2 * GDN_HEADS), m3),
                pl.BlockSpec((1, 1, 2 * GDN_HEADS, C), m4)]

    of, ob = pl.pallas_call(
        functools.partial(_gdn_body, passes=GDN_PASSES),
        grid=(B, nc),
        in_specs=specs(fwd, fwd4) + specs(bwd, bwd4),
        out_specs=[pl.BlockSpec((1, C, wv), fwd), pl.BlockSpec((1, C, wv), bwd)],
        out_shape=[jax.ShapeDtypeStruct((B, T, wv), F32)] * 2,
        scratch_shapes=[pltpu.VMEM((2, GDN_HEADS, GDN_DK, GDN_DV), F32)],
        compiler_params=_cparams("parallel", "arbitrary"),
        name="gdn_scan",
    )(q, k, v, gcol, bcol, grow, q, k, v, gcol, bcol, grow)
    return of + ob


def _rwkv_dir(r, lw, k, v, kk, rate, s, rev, passes):
    H, C, N = r.shape
    incl, strict = _tri_masks(C, rev)
    tri = jnp.broadcast_to(incl.astype(F32), (H, C, C))
    linc = _dg(tri, lw, _BNN, 6)
    lexc = linc - lw
    ltot = linc[:, 0:1, :] if rev else linc[:, C - 1:C, :]
    b = kk * rate
    at = -kk * jnp.exp(lexc)
    rt = r * jnp.exp(linc)
    einv = jnp.exp(-linc)
    bt = b * einv
    kt = k * einv
    etail = jnp.exp(ltot - linc)
    lab = jnp.where(strict, _dg(at, bt, _BNT, passes), 0.0)
    lak = jnp.where(strict, _dg(at, kt, _BNT, passes), 0.0)
    mrb = jnp.where(incl, _dg(rt, bt, _BNT, passes), 0.0)
    mrk = jnp.where(incl, _dg(rt, kt, _BNT, passes), 0.0)
    tinv = _neumann_inverse(lab, _BNN, passes)
    a_s = _dg(at, s, _BNT, passes)
    r_s = _dg(rt, s, _BNT, passes)
    u = _dg(tinv, a_s + _dg(lak, v, _BNN, passes), _BNN, passes)
    y = r_s + _dg(mrb, u, _BNN, passes) + _dg(mrk, v, _BNN, passes)
    s_new = (s * jnp.exp(ltot) + _dg(u, b * etail, _BTN, passes) + _dg(v, k * etail, _BTN, passes))
    return y, s_new


def _rwkv_body(rf, vf, kkf, lwf, kf, af, rb, vb, kkb, lwb, kb, ab, yf_ref, yb_ref, s_ref, *, passes):
    @pl.when(pl.program_id(1) == 0)
    def _():
        s_ref[...] = jnp.zeros(s_ref.shape, F32)

    dirs = ((rf, vf, kkf, lwf, kf, af, yf_ref), (rb, vb, kkb, lwb, kb, ab, yb_ref))
    for d, (r_ref, v_ref, kk_ref, lw_ref, k_ref, a_ref, y_ref) in enumerate(dirs):
        y, s_new = _rwkv_dir(r_ref[0], lw_ref[0], k_ref[0], v_ref[0], kk_ref[0], a_ref[0],
                             s_ref[d], d == 1, passes)
        s_ref[d] = s_new
        y_ref[0] = y


def rwkv_scan(r, v, kk, lw, key, rate, n_ctx):
    B, H, T, N = r.shape
    C = RWKV_CHUNK
    nc = T // C
    ncc = n_ctx // C
    fwd = lambda b, i: (b, 0, i, 0)
    bwd = lambda b, i: (b, 0, _rev_chunk(i, ncc, nc), 0)
    blk = (1, H, C, N)
    yf, yb = pl.pallas_call(
        functools.partial(_rwkv_body, passes=RWKV_PASSES),
        grid=(B, nc),
        in_specs=[pl.BlockSpec(blk, fwd)] * 6 + [pl.BlockSpec(blk, bwd)] * 6,
        out_specs=[pl.BlockSpec(blk, fwd), pl.BlockSpec(blk, bwd)],
        out_shape=[jax.ShapeDtypeStruct((B, H, T, N), F32)] * 2,
        scratch_shapes=[pltpu.VMEM((2, H, N, N), F32)],
        compiler_params=_cparams("parallel", "arbitrary"),
        name="rwkv7_scan",
    )(r, v, kk, lw[0], key[0], rate[0], r, v, kk, lw[1], key[1], rate[1])
    return yf + yb


def _moe_body(be_ref, x_ref, w1_ref, w3_ref, w2_ref, ws_ref, o_ref):
    del be_ref
    x = x_ref[...]
    h1 = jnp.dot(x, w1_ref[0], preferred_element_type=F32)
    h3 = jnp.dot(x, w3_ref[0], preferred_element_type=F32)
    hid = (h1 * jax.nn.sigmoid(h1)) * h3
    y = jnp.dot(hid.astype(BF16), w2_ref[0], preferred_element_type=F32)
    o_ref[...] = y * ws_ref[...]


def moe_experts(xs, slot_w, blk_e, w1, w3, w2):
    n_slots, D = xs.shape
    hid = w1.shape[-1]
    n_blocks = n_slots // MOE_BLOCK
    return pl.pallas_call(
        _moe_body,
        grid_spec=pltpu.PrefetchScalarGridSpec(
            num_scalar_prefetch=1,
            grid=(n_blocks,),
            in_specs=[pl.BlockSpec((MOE_BLOCK, D), lambda i, be: (i, 0)),
                      pl.BlockSpec((1, D, hid), lambda i, be: (be[i], 0, 0)),
                      pl.BlockSpec((1, D, hid), lambda i, be: (be[i], 0, 0)),
                      pl.BlockSpec((1, hid, D), lambda i, be: (be[i], 0, 0)),
                      pl.BlockSpec((MOE_BLOCK, 1), lambda i, be: (i, 0))],
            out_specs=pl.BlockSpec((MOE_BLOCK, D), lambda i, be: (i, 0)),
        ),
        out_shape=jax.ShapeDtypeStruct((n_slots, D), F32),
        compiler_params=_cparams("arbitrary"),
        name="moe_experts",
    )(blk_e, xs, w1, w3, w2, slot_w)


def hier_moe(tokens, w_group, b_group, w_expert, b_expert, w1, w3, w2):
    N, D = tokens.shape
    logits = mm(tokens, jnp.concatenate([w_group, w_expert], axis=1), hi=True)
    pg = jax.nn.softmax(logits[:, :MOE_GROUPS] + b_group, axis=-1)
    pg_top, g_idx = lax.top_k(pg, 1)
    le = logits[:, MOE_GROUPS:] + b_expert
    sel = g_idx * MOE_PER_GROUP + jnp.arange(MOE_PER_GROUP)[None, :]
    pe = jax.nn.softmax(jnp.take_along_axis(le, sel, axis=1), axis=-1)
    pe_top, e_loc = lax.top_k(pe, MOE_TOPK)
    wts = pg_top * pe_top / jnp.sum(pe_top, -1, keepdims=True)
    eid = (g_idx * MOE_PER_GROUP + e_loc).reshape(-1)
    tok = jnp.repeat(jnp.arange(N, dtype=jnp.int32), MOE_TOPK)
    A = N * MOE_TOPK
    onehot = (eid[:, None] == jnp.arange(MOE_EXPERTS)[None, :]).astype(jnp.int32)
    rank = jnp.take_along_axis(jnp.cumsum(onehot, 0), eid[:, None], 1)[:, 0] - 1
    counts = jnp.sum(onehot, 0)
    padded = (counts + MOE_BLOCK - 1) // MOE_BLOCK * MOE_BLOCK
    pend = jnp.cumsum(padded)
    dest = (pend - padded)[eid] + rank
    n_blocks = -(-A // MOE_BLOCK) + MOE_EXPERTS
    n_slots = n_blocks * MOE_BLOCK
    slot_tok = jnp.zeros((n_slots,), jnp.int32).at[dest].set(tok)
    slot_w = jnp.zeros((n_slots,), F32).at[dest].set(wts.reshape(-1))
    blk_e = jnp.minimum(jnp.searchsorted(pend, jnp.arange(n_blocks, dtype=jnp.int32) * MOE_BLOCK, side='right'),
                        MOE_EXPERTS - 1).astype(jnp.int32)
    xs = tokens.astype(BF16)[slot_tok]
    ys = moe_experts(xs, slot_w[:, None], blk_e, w1, w3, w2)
    d2 = dest.reshape(N, MOE_TOPK)
    return ys[d2[:, 0]] + ys[d2[:, 1]]


def _rms(x, g):
    return x * lax.rsqrt(jnp.mean(x * x, -1, keepdims=True) + EPS) * g


def _l2(x):
    return x * lax.rsqrt(jnp.sum(x * x, -1, keepdims=True) + EPS)


def _rope_tables(n_lat, n_ctx):
    rows = n_lat // GRID_W
    row = jnp.repeat(jnp.arange(rows, dtype=F32), GRID_W)
    col = jnp.tile(jnp.arange(GRID_W, dtype=F32), rows)
    n_freq = MLA_ROPE // 4
    inv = ROPE_BASE ** (-jnp.arange(n_freq, dtype=F32) / n_freq)
    ang = jnp.stack([row[:, None] * inv, col[:, None] * inv], axis=1)
    cos = jnp.concatenate([jnp.ones((n_ctx, 2, n_freq), F32), jnp.cos(ang)], 0)
    sin = jnp.concatenate([jnp.zeros((n_ctx, 2, n_freq), F32), jnp.sin(ang)], 0)
    return cos, sin


def _rope(x, cos, sin):
    B, T, H, _ = x.shape
    xr = x.reshape(B, T, H, 2, 2, MLA_ROPE // 4)
    x1, x2 = xr[..., 0, :], xr[..., 1, :]
    c = cos[None, :, None]
    s = sin[None, :, None]
    return jnp.stack([x1 * c - x2 * s, x2 * c + x1 * s], axis=-2).reshape(B, T, H, MLA_ROPE)


def _per_segment(fn, x, n_ctx):
    return jnp.concatenate([fn(x[:, :n_ctx]), fn(x[:, n_ctx:])], axis=1)


def _mla_heads(t, gain, cos, sin):
    B, T, H, _ = t.shape
    t = _rms(t, gain)
    t = jnp.concatenate([t[..., :MLA_NOPE], _rope(t[..., MLA_NOPE:], cos, sin),
                         jnp.zeros((B, T, H, MLA_PAD - MLA_QK), F32)], -1)
    return t.reshape(B, T, H * MLA_PAD).astype(BF16)


def attn_delta_mixer(u, n_ctx, rope, w_in, w_out, qa_g, w_qb, kva_g, w_kvb, qn_g, kn_g,
                     conv_w, a_log, dt_bias, out_g):
    B, T, D = u.shape
    M = B * T
    H = MLA_HEADS
    cos, sin = rope
    ub = u.reshape(M, D).astype(BF16)
    wb = w_in.astype(BF16)
    c0, c1, c2 = MLA_COLS, MLA_COLS + GDN_QKV, MLA_COLS + GDN_QKV + GDN_Z
    um = mm(ub, wb[:, :c0])
    gq = mm(ub, wb[:, c0:c1]).reshape(B, T, GDN_QKV)
    z = mm(ub, wb[:, c1:c2]).reshape(B, T, GDN_HEADS, GDN_DV)
    ab = mm(ub, wb[:, c2:]).reshape(B, T, 2, 2, GDN_HEADS)

    c_q = _rms(um[:, :MLA_Q_LORA], qa_g).astype(BF16)
    c_kv = _rms(um[:, MLA_Q_LORA:MLA_Q_LORA + MLA_KV_LORA], kva_g).astype(BF16)
    k_pe = um[:, MLA_Q_LORA + MLA_KV_LORA:].reshape(B, T, 1, MLA_ROPE)
    q = mm(c_q, w_qb.astype(BF16)).reshape(B, T, H, MLA_QK)
    wkv = w_kvb.reshape(MLA_KV_LORA, H, MLA_NOPE + MLA_V)
    w_kv2 = jnp.concatenate([wkv[:, :, :MLA_NOPE].reshape(MLA_KV_LORA, H * MLA_NOPE),
                             wkv[:, :, MLA_NOPE:].reshape(MLA_KV_LORA, H * MLA_V)], axis=1)
    kv = mm(c_kv, w_kv2.astype(BF16))
    k_nope = kv[:, :H * MLA_NOPE].reshape(B, T, H, MLA_NOPE)
    val = kv[:, H * MLA_NOPE:].reshape(B, T, H * MLA_V).astype(BF16)
    k = jnp.concatenate([k_nope, jnp.broadcast_to(k_pe, (B, T, H, MLA_ROPE))], -1)
    qh = _mla_heads(q, qn_g, cos, sin)
    kh = _mla_heads(k, kn_g, cos, sin)
    a_lat = attention(qh[:, n_ctx:], kh, val)
    a_ctx = attention(qh[:, :n_ctx], kh[:, :n_ctx], val[:, :n_ctx])
    a = jnp.concatenate([a_ctx, a_lat], axis=1)

    def conv(x):
        t = x.shape[1]
        xp = jnp.pad(x, ((0, 0), (GDN_CONV // 2, GDN_CONV // 2), (0, 0)))
        return sum(xp[:, j:j + t] * conv_w[j] for j in range(GDN_CONV))

    qkv = jax.nn.silu(_per_segment(conv, gq, n_ctx))
    nk = GDN_HEADS * GDN_DK
    gq_ = (_l2(qkv[..., :nk].reshape(B, T, GDN_HEADS, GDN_DK)) * GDN_DK ** -0.5).reshape(B, T, nk)
    gk_ = _l2(qkv[..., nk:2 * nk].reshape(B, T, GDN_HEADS, GDN_DK)).reshape(B, T, nk)
    gv_ = qkv[..., 2 * nk:]
    g = -jnp.exp(a_log) * jax.nn.softplus(ab[:, :, :, 0] + dt_bias)
    beta = jax.nn.sigmoid(ab[:, :, :, 1])
    o = gdn_scan(gq_, gk_, gv_, g, beta, n_ctx).reshape(B, T, GDN_HEADS, GDN_DV)
    dl = (_rms(o, out_g) * jax.nn.silu(z)).reshape(B, T, GDN_HEADS * GDN_DV)

    mix = jnp.concatenate([a, dl], axis=-1).reshape(M, -1).astype(BF16)
    return mm(mix, w_out.astype(BF16)).reshape(B, T, D)


def rwkv7_mixer(u, n_ctx, v_first, mu, wr, wk, wv, wo, w0, w1, w2, a0, a1, a2, g1, g2, k_k, k_a, r_k,
                ln_w, ln_b, vres):
    B, T, D = u.shape
    M = B * T
    N = RWKV_HEAD
    H = D // N

    def shift(x):
        xp = jnp.pad(x, ((0, 0), (1, 1), (0, 0)))
        return 0.5 * (xp[:, :-2] + xp[:, 2:]) - x

    xx = _per_segment(shift, u, n_ctx)
    xr, xw, xk, xv, xa, xg = [(u + xx * mu[i]).reshape(M, D).astype(BF16) for i in range(6)]
    r = mm(xr, wr.astype(BF16))
    k = mm(xk, wk.astype(BF16))
    v = mm(xv, wv.astype(BF16))
    if vres is None:
        v_first = v
    else:
        v0, v1, v2 = vres
        v = v + (v_first - v) * jax.nn.sigmoid(v0 + mm(mm(xv, v1.astype(BF16)), v2))
    gate = mm(jax.nn.sigmoid(mm(xg, g1.astype(BF16))), g2)
    hd = lambda t: jnp.swapaxes(t.reshape(B, T, H, N), 1, 2)
    kk = _l2((k * k_k).reshape(M, H, N)).reshape(M, D)
    lws, keys, rates = [], [], []
    for d in range(2):
        w_log = -jax.nn.softplus(-(w0[d] + mm(jnp.tanh(mm(xw, w1[d].astype(BF16))), w2[d]))) - 0.5
        a = jax.nn.sigmoid(a0[d] + mm(mm(xa, a1[d].astype(BF16)), a2[d]))
        lws.append(hd(-jnp.exp(w_log)))
        keys.append(k * (1 + (a - 1) * k_a))
        rates.append(hd(a))
    y = rwkv_scan(hd(r), hd(v), hd(kk), lws, [hd(keys[0]), hd(keys[1])], rates, n_ctx)
    y = jnp.swapaxes(y, 1, 2)
    mean = jnp.mean(y, -1, keepdims=True)
    var = jnp.mean(jnp.square(y - mean), -1, keepdims=True)
    yn = ((y - mean) * lax.rsqrt(var + GN_EPS)).reshape(M, D) * ln_w + ln_b
    k_bonus = 0.5 * (keys[0] + keys[1])
    bonus = (jnp.sum((r * k_bonus).reshape(M, H, N) * r_k, -1, keepdims=True) * v.reshape(M, H, N)).reshape(M, D)
    out = mm(((yn + bonus) * gate).astype(BF16), wo.astype(BF16))
    return out.reshape(B, T, D), v_first


def kernel(x, c, ctx, c_ctx, ada_w, ada_b, norm_mix, norm_ffn, hy_w_in, hy_w_out, mla_qa_norm, mla_w_qb, mla_kva_norm, mla_w_kvb, mla_q_norm, mla_k_norm, gdn_conv, gdn_a_log, gdn_dt_bias, gdn_out_norm, rk_mu, rk_wr, rk_wk, rk_wv, rk_wo, rk_w0, rk_w1, rk_w2, rk_a0, rk_a1, rk_a2, rk_g1, rk_g2, rk_kk, rk_ka, rk_rk, rk_ln_w, rk_ln_b, rk_v0, rk_v1, rk_v2, moe_w_group, moe_b_group, moe_w_expert, moe_b_expert, moe_w1, moe_w3, moe_w2):
    B, S, D = x.shape
    L = ctx.shape[1]
    T = L + S
    depth = ada_w.shape[0]
    rope = _rope_tables(S, L)
    n_rows = -(-(B + 1) // 8) * 8
    sc = jnp.concatenate([jax.nn.silu(c), jax.nn.silu(c_ctx)[None], jnp.zeros((n_rows - B - 1, D), F32)], 0)
    h = jnp.concatenate([ctx, x], axis=1)
    v_first = None
    for l in range(depth):
        m = mm(sc, ada_w[l], hi=True) + ada_b[l]
        m_lat = m[:B].reshape(B, 1, 6, D)
        m_ctx = jnp.broadcast_to(m[B].reshape(1, 1, 6, D), (B, 1, 6, D))
        mod = lambda i: jnp.concatenate([jnp.broadcast_to(m_ctx[:, :, i], (B, L, D)),
                                         jnp.broadcast_to(m_lat[:, :, i], (B, S, D))], axis=1)
        u = _rms(h, norm_mix[l]) * (1 + mod(1)) + mod(0)
        j = l // 2
        if l % 2 == 0:
            y = attn_delta_mixer(u, L, rope, hy_w_in[j], hy_w_out[j], mla_qa_norm[j], mla_w_qb[j],
                                 mla_kva_norm[j], mla_w_kvb[j], mla_q_norm[j], mla_k_norm[j], gdn_conv[j],
                                 gdn_a_log[j], gdn_dt_bias[j], gdn_out_norm[j])
        else:
            vres = None if j == 0 else (rk_v0[j - 1], rk_v1[j - 1], rk_v2[j - 1])
            y, vf = rwkv7_mixer(u, L, v_first, rk_mu[j], rk_wr[j], rk_wk[j], rk_wv[j], rk_wo[j], rk_w0[j],
                                rk_w1[j], rk_w2[j], rk_a0[j], rk_a1[j], rk_a2[j], rk_g1[j], rk_g2[j],
                                rk_kk[j], rk_ka[j], rk_rk[j], rk_ln_w[j], rk_ln_b[j], vres)
            if j == 0:
                v_first = vf
        h = h + mod(2) * y
        f = _rms(h, norm_ffn[l]) * (1 + mod(4)) + mod(3)
        moe_out = hier_moe(f.reshape(B * T, D), moe_w_group[l], moe_b_group[l], moe_w_expert[l],
                           moe_b_expert[l], moe_w1[l].astype(BF16), moe_w3[l].astype(BF16),
                           moe_w2[l].astype(BF16))
        h = h + mod(5) * moe_out.reshape(B, T, D)
    return h[:, L:]
```

```python
import functools
import math

import jax
import jax.numpy as jnp
from jax import lax
from jax.experimental import pallas as pl
from jax.experimental.pallas import tpu as pltpu

F32 = jnp.float32
BF16 = jnp.bfloat16
HI = lax.Precision.HIGHEST

DEPTH = 4
GRID_W = 64
EPS = 1e-6

MLA_HEADS = 8
MLA_Q_LORA = 256
MLA_KV_LORA = 128
MLA_NOPE = 64
MLA_ROPE = 32
MLA_V = 64
MLA_QK = MLA_NOPE + MLA_ROPE
MLA_SCALE = MLA_QK ** -0.5
ROPE_BASE = 10000.0
MLA_PAD = 128

GDN_HEADS = 4
GDN_DK = 128
GDN_DV = 128
GDN_CONV = 5
GDN_CHUNK = 64

RWKV_HEAD = 64
RWKV_CHUNK = 64
GN_EPS = 64e-5

MOE_GROUPS = 4
MOE_PER_GROUP = 8
MOE_EXPERTS = MOE_GROUPS * MOE_PER_GROUP
MOE_TOPK = 2
MOE_BLOCK = 256

MLA_COLS = MLA_Q_LORA + MLA_KV_LORA + MLA_ROPE
GDN_QKV = GDN_HEADS * (2 * GDN_DK + GDN_DV)
GDN_Z = GDN_HEADS * GDN_DV
GDN_AB = 2 * 2 * GDN_HEADS

VMEM_LIMIT_BYTES = 48 * 1024 * 1024

GDN_PASSES = 1
RWKV_PASSES = 1


def _cparams(*sem):
    return pltpu.CompilerParams(dimension_semantics=sem, vmem_limit_bytes=VMEM_LIMIT_BYTES)


def _pick(n, cands):
    for c in cands:
        if n % c == 0:
            return c
    return n


def _split(a):
    hi = a.astype(BF16)
    lo = (a - hi.astype(F32)).astype(BF16)
    return hi, lo


def _dg(a, b, dn, passes):
    if passes == 6:
        return lax.dot_general(a, b, dn, precision=HI, preferred_element_type=F32)
    if passes == 1:
        return lax.dot_general(a.astype(BF16), b.astype(BF16), dn, preferred_element_type=F32)
    ah, al = _split(a)
    bh, bl = _split(b)
    d = functools.partial(lax.dot_general, dimension_numbers=dn, preferred_element_type=F32)
    return d(ah, bh) + d(al, bh) + d(ah, bl)


_NN = (((1,), (0,)), ((), ()))
_NT = (((1,), (1,)), ((), ()))
_TN = (((0,), (0,)), ((), ()))
_BNN = (((2,), (1,)), ((0,), (0,)))
_BNT = (((2,), (2,)), ((0,), (0,)))
_BTN = (((1,), (1,)), ((0,), (0,)))


def _mm_body(x_ref, w_ref, o_ref, *, hi):
    if hi:
        o_ref[...] = jnp.dot(x_ref[...], w_ref[...], precision=HI, preferred_element_type=F32)
    else:
        o_ref[...] = jnp.dot(x_ref[...].astype(BF16), w_ref[...].astype(BF16),
                             preferred_element_type=F32)


def mm(x, w, hi=False):
    M, K = x.shape
    N = w.shape[1]
    tm = _pick(M, (512, 256, 128, 64, 32, 16, 8))
    tn = _pick(N, (512, 384, 256, 128))
    return pl.pallas_call(
        functools.partial(_mm_body, hi=hi),
        grid=(M // tm, N // tn),
        in_specs=[pl.BlockSpec((tm, K), lambda i, j: (i, 0)),
                  pl.BlockSpec((K, tn), lambda i, j: (0, j))],
        out_specs=pl.BlockSpec((tm, tn), lambda i, j: (i, j)),
        out_shape=jax.ShapeDtypeStruct((M, N), F32),
        compiler_params=_cparams("parallel", "parallel"),
        name="dense_mm",
    )(x, w)


def _attn_body(q_ref, k_ref, v_ref, o_ref, m_ref, l_ref, acc_ref, *, scale):
    ki = pl.program_id(3)

    @pl.when(ki == 0)
    def _():
        m_ref[...] = jnp.full(m_ref.shape, -1e30, F32)
        l_ref[...] = jnp.zeros(l_ref.shape, F32)
        acc_ref[...] = jnp.zeros(acc_ref.shape, F32)

    v = v_ref[0]
    for h in range(2):
        q = q_ref[0, :, h * MLA_PAD:(h + 1) * MLA_PAD]
        k = k_ref[0, :, h * MLA_PAD:(h + 1) * MLA_PAD]
        s = lax.dot_general(q, k, _NT, preferred_element_type=F32) * scale
        m_prev = m_ref[h]
        m_new = jnp.maximum(m_prev, jnp.max(s, axis=-1, keepdims=True))
        alpha = jnp.exp(m_prev - m_new)
        p = jnp.exp(s - m_new)
        l_ref[h] = alpha * l_ref[h] + jnp.sum(p, axis=-1, keepdims=True)
        acc_ref[h] = alpha * acc_ref[h] + jnp.dot(p.astype(BF16), v, preferred_element_type=F32)
        m_ref[h] = m_new

    @pl.when(ki == pl.num_programs(3) - 1)
    def _():
        lane = lax.broadcasted_iota(jnp.int32, acc_ref.shape[1:], 1)
        o0 = acc_ref[0] / l_ref[0]
        o1 = acc_ref[1] / l_ref[1]
        o_ref[0] = jnp.where(lane < MLA_V, o0, o1)


def attention(q, k, v):
    B, Sq, _ = q.shape
    Sk = k.shape[1]
    tq = _pick(Sq, (512, 256, 128))
    tk = _pick(Sk, (768, 512, 384, 256, 128))
    return pl.pallas_call(
        functools.partial(_attn_body, scale=MLA_SCALE),
        grid=(B, MLA_HEADS // 2, Sq // tq, Sk // tk),
        in_specs=[pl.BlockSpec((1, tq, 2 * MLA_PAD), lambda b, p, i, j: (b, i, p)),
                  pl.BlockSpec((1, tk, 2 * MLA_PAD), lambda b, p, i, j: (b, j, p)),
                  pl.BlockSpec((1, tk, 2 * MLA_V), lambda b, p, i, j: (b, j, p))],
        out_specs=pl.BlockSpec((1, tq, 2 * MLA_V), lambda b, p, i, j: (b, i, p)),
        out_shape=jax.ShapeDtypeStruct((B, Sq, MLA_HEADS * MLA_V), F32),
        scratch_shapes=[pltpu.VMEM((2, tq, 1), F32), pltpu.VMEM((2, tq, 1), F32),
                        pltpu.VMEM((2, tq, 2 * MLA_V), F32)],
        compiler_params=_cparams("parallel", "parallel", "parallel", "arbitrary"),
        name="mla_attention",
    )(q, k, v)


def _tri_masks(C, rev):
    row = lax.broadcasted_iota(jnp.int32, (C, C), 0)
    col = lax.broadcasted_iota(jnp.int32, (C, C), 1)
    if rev:
        return row <= col, row < col
    return row >= col, row > col


def _neumann_inverse(nil, dn, passes):
    C = nil.shape[-1]
    eye = (lax.broadcasted_iota(jnp.int32, (C, C), 0) ==
           lax.broadcasted_iota(jnp.int32, (C, C), 1)).astype(F32)
    x = eye + nil
    p = nil
    for _ in range(int(math.log2(C)) - 1):
        p = _dg(p, p, dn, passes)
        x = x + _dg(x, p, dn, passes)
    return x


def _gdn_body(qf, kf, vf, gcf, bcf, grf, qb, kb, vb, gcb, bcb, grb, of_ref, ob_ref, s_ref, *, passes):
    C = GDN_CHUNK
    H = GDN_HEADS

    @pl.when(pl.program_id(1) == 0)
    def _():
        s_ref[...] = jnp.zeros(s_ref.shape, F32)

    dirs = ((qf, kf, vf, gcf, bcf, grf, of_ref), (qb, kb, vb, gcb, bcb, grb, ob_ref))
    for d, (q_ref, k_ref, v_ref, gc_ref, bc_ref, gr_ref, o_ref) in enumerate(dirs):
        rev = d == 1
        incl, strict = _tri_masks(C, rev)
        tri = incl.astype(F32)
        gcum_col = _dg(tri, gc_ref[0], _NN, 6)
        gcum_row = _dg(gr_ref[0, 0], tri, _NT, 6)
        beta_all = bc_ref[0]
        for h in range(H):
            idx = d * H + h
            gc = gcum_col[:, idx:idx + 1]
            gr = gcum_row[idx:idx + 1, :]
            beta = beta_all[:, idx:idx + 1]
            q = q_ref[0, :, h * GDN_DK:(h + 1) * GDN_DK]
            k = k_ref[0, :, h * GDN_DK:(h + 1) * GDN_DK]
            v = v_ref[0, :, h * GDN_DV:(h + 1) * GDN_DV]
            decay = jnp.exp(jnp.where(incl, gc - gr, -1e30))
            kbeta = k * beta
            lower = jnp.where(strict, _dg(kbeta, k, _NT, passes) * decay, 0.0)
            tinv = _neumann_inverse(-lower, _NN, passes)
            eg = jnp.exp(gc)
            rhs = jnp.concatenate([v * beta, kbeta * eg], axis=-1)
            sol = _dg(tinv, rhs, _NN, passes)
            u = sol[:, :GDN_DV]
            w = sol[:, GDN_DV:]
            aqk = jnp.where(incl, _dg(q, k, _NT, passes) * decay, 0.0)
            s = s_ref[d, h]
            v_new = u - _dg(w, s, _NN, passes)
            o = _dg(q * eg, s, _NN, passes) + _dg(aqk, v_new, _NN, passes)
            g_last = gc[0:1, :] if rev else gc[C - 1:C, :]
            s_ref[d, h] = s * jnp.exp(g_last) + _dg(k * jnp.exp(g_last - gc), v_new, _TN, passes)
            o_ref[0, :, h * GDN_DV:(h + 1) * GDN_DV] = o


def _rev_chunk(i, ncc, nc):
    return jnp.where(i < ncc, ncc - 1 - i, nc - 1 + ncc - i)


def gdn_scan(q, k, v, g, beta, n_ctx):
    B, T, _ = q.shape
    C = GDN_CHUNK
    nc = T // C
    ncc = n_ctx // C
    gcol = g.reshape(B, T, 2 * GDN_HEADS)
    bcol = beta.reshape(B, T, 2 * GDN_HEADS)
    grow = jnp.swapaxes(gcol.reshape(B, nc, C, 2 * GDN_HEADS), 2, 3)
    fwd = lambda b, i: (b, i, 0)
    bwd = lambda b, i: (b, _rev_chunk(i, ncc, nc), 0)
    fwd4 = lambda b, i: (b, i, 0, 0)
    bwd4 = lambda b, i: (b, _rev_chunk(i, ncc, nc), 0, 0)
    wide = q.shape[-1]
    wv = v.shape[-1]

    def specs(m3, m4):
        return [pl.BlockSpec((1, C, wide), m3), pl.BlockSpec((1, C, wide), m3), pl.BlockSpec((1, C, wv), m3),
                pl.BlockSpec((1, C, 2 * GDN_HEADS), m3), pl.BlockSpec((1, C, 2 * GDN_HEADS), m3),
                pl.BlockSpec((1, 1, 2 * GDN_HEADS, C), m4)]

    of, ob = pl.pallas_call(
        functools.partial(_gdn_body, passes=GDN_PASSES),
        grid=(B, nc),
        in_specs=specs(fwd, fwd4) + specs(bwd, bwd4),
        out_specs=[pl.BlockSpec((1, C, wv), fwd), pl.BlockSpec((1, C, wv), bwd)],
        out_shape=[jax.ShapeDtypeStruct((B, T, wv), F32)] * 2,
        scratch_shapes=[pltpu.VMEM((2, GDN_HEADS, GDN_DK, GDN_DV), F32)],
        compiler_params=_cparams("parallel", "arbitrary"),
        name="gdn_scan",
    )(q, k, v, gcol, bcol, grow, q, k, v, gcol, bcol, grow)
    return of + ob


def _rwkv_dir(r, lw, k, v, kk, rate, s, rev, passes):
    H, C, N = r.shape
    incl, strict = _tri_masks(C, rev)
    tri = jnp.broadcast_to(incl.astype(F32), (H, C, C))
    linc = _dg(tri, lw, _BNN, 6)
    lexc = linc - lw
    ltot = linc[:, 0:1, :] if rev else linc[:, C - 1:C, :]
    b = kk * rate
    at = -kk * jnp.exp(lexc)
    rt = r * jnp.exp(linc)
    einv = jnp.exp(-linc)
    bt = b * einv
    kt = k * einv
    etail = jnp.exp(ltot - linc)
    lab = jnp.where(strict, _dg(at, bt, _BNT, passes), 0.0)
    lak = jnp.where(strict, _dg(at, kt, _BNT, passes), 0.0)
    mrb = jnp.where(incl, _dg(rt, bt, _BNT, passes), 0.0)
    mrk = jnp.where(incl, _dg(rt, kt, _BNT, passes), 0.0)
    tinv = _neumann_inverse(lab, _BNN, passes)
    a_s = _dg(at, s, _BNT, passes)
    r_s = _dg(rt, s, _BNT, passes)
    u = _dg(tinv, a_s + _dg(lak, v, _BNN, passes), _BNN, passes)
    y = r_s + _dg(mrb, u, _BNN, passes) + _dg(mrk, v, _BNN, passes)
    s_new = (s * jnp.exp(ltot) + _dg(u, b * etail, _BTN, passes) + _dg(v, k * etail, _BTN, passes))
    return y, s_new


def _rwkv_body(rf, vf, kkf, lwf, kf, af, rb, vb, kkb, lwb, kb, ab, yf_ref, yb_ref, s_ref, *, passes):
    @pl.when(pl.program_id(1) == 0)
    def _():
        s_ref[...] = jnp.zeros(s_ref.shape, F32)

    dirs = ((rf, vf, kkf, lwf, kf, af, yf_ref), (rb, vb, kkb, lwb, kb, ab, yb_ref))
    for d, (r_ref, v_ref, kk_ref, lw_ref, k_ref, a_ref, y_ref) in enumerate(dirs):
        y, s_new = _rwkv_dir(r_ref[0], lw_ref[0], k_ref[0], v_ref[0], kk_ref[0], a_ref[0],
                             s_ref[d], d == 1, passes)
        s_ref[d] = s_new
        y_ref[0] = y


def rwkv_scan(r, v, kk, lw, key, rate, n_ctx):
    B, H, T, N = r.shape
    C = RWKV_CHUNK
    nc = T // C
    ncc = n_ctx // C
    fwd = lambda b, i: (b, 0, i, 0)
    bwd = lambda b, i: (b, 0, _rev_chunk(i, ncc, nc), 0)
    blk = (1, H, C, N)
    yf, yb = pl.pallas_call(
        functools.partial(_rwkv_body, passes=RWKV_PASSES),
        grid=(B, nc),
        in_specs=[pl.BlockSpec(blk, fwd)] * 6 + [pl.BlockSpec(blk, bwd)] * 6,
        out_specs=[pl.BlockSpec(blk, fwd), pl.BlockSpec(blk, bwd)],
        out_shape=[jax.ShapeDtypeStruct((B, H, T, N), F32)] * 2,
        scratch_shapes=[pltpu.VMEM((2, H, N, N), F32)],
        compiler_params=_cparams("parallel", "arbitrary"),
        name="rwkv7_scan",
    )(r, v, kk, lw[0], key[0], rate[0], r, v, kk, lw[1], key[1], rate[1])
    return yf + yb


def _moe_body(be_ref, x_ref, w1_ref, w3_ref, w2_ref, ws_ref, o_ref):
    del be_ref
    x = x_ref[...]
    h1 = jnp.dot(x, w1_ref[0], preferred_element_type=F32)
    h3 = jnp.dot(x, w3_ref[0], preferred_element_type=F32)
    hid = (h1 * jax.nn.sigmoid(h1)) * h3
    y = jnp.dot(hid.astype(BF16), w2_ref[0], preferred_element_type=F32)
    o_ref[...] = y * ws_ref[...]


def moe_experts(xs, slot_w, blk_e, w1, w3, w2):
    n_slots, D = xs.shape
    hid = w1.shape[-1]
    n_blocks = n_slots // MOE_BLOCK
    return pl.pallas_call(
        _moe_body,
        grid_spec=pltpu.PrefetchScalarGridSpec(
            num_scalar_prefetch=1,
            grid=(n_blocks,),
            in_specs=[pl.BlockSpec((MOE_BLOCK, D), lambda i, be: (i, 0)),
                      pl.BlockSpec((1, D, hid), lambda i, be: (be[i], 0, 0)),
                      pl.BlockSpec((1, D, hid), lambda i, be: (be[i], 0, 0)),
                      pl.BlockSpec((1, hid, D), lambda i, be: (be[i], 0, 0)),
                      pl.BlockSpec((MOE_BLOCK, 1), lambda i, be: (i, 0))],
            out_specs=pl.BlockSpec((MOE_BLOCK, D), lambda i, be: (i, 0)),
        ),
        out_shape=jax.ShapeDtypeStruct((n_slots, D), F32),
        compiler_params=_cparams("arbitrary"),
        name="moe_experts",
    )(blk_e, xs, w1, w3, w2, slot_w)


def hier_moe(tokens, w_group, b_group, w_expert, b_expert, w1, w3, w2):
    N, D = tokens.shape
    logits = mm(tokens, jnp.concatenate([w_group, w_expert], axis=1), hi=True)
    pg = jax.nn.softmax(logits[:, :MOE_GROUPS] + b_group, axis=-1)
    g_idx = jnp.argmax(pg, axis=-1)[:, None].astype(jnp.int32)
    pg_top = jnp.max(pg, axis=-1, keepdims=True)
    le = logits[:, MOE_GROUPS:] + b_expert
    sel = g_idx * MOE_PER_GROUP + jnp.arange(MOE_PER_GROUP)[None, :]
    pe = jax.nn.softmax(jnp.take_along_axis(le, sel, axis=1), axis=-1)
    lane = jnp.arange(MOE_PER_GROUP, dtype=jnp.int32)[None, :]
    e1 = jnp.argmax(pe, axis=-1)[:, None].astype(jnp.int32)
    rest = jnp.where(lane == e1, -jnp.inf, pe)
    e2 = jnp.argmax(rest, axis=-1)[:, None].astype(jnp.int32)
    pe_top = jnp.concatenate([jnp.max(pe, -1, keepdims=True), jnp.max(rest, -1, keepdims=True)], axis=1)
    e_loc = jnp.concatenate([e1, e2], axis=1)
    wts = pg_top * pe_top / jnp.sum(pe_top, -1, keepdims=True)
    eid = (g_idx * MOE_PER_GROUP + e_loc).reshape(-1)
    tok = jnp.repeat(jnp.arange(N, dtype=jnp.int32), MOE_TOPK)
    A = N * MOE_TOPK
    onehot = (eid[:, None] == jnp.arange(MOE_EXPERTS)[None, :]).astype(jnp.int32)
    rank = jnp.take_along_axis(jnp.cumsum(onehot, 0), eid[:, None], 1)[:, 0] - 1
    counts = jnp.sum(onehot, 0)
    padded = (counts + MOE_BLOCK - 1) // MOE_BLOCK * MOE_BLOCK
    pend = jnp.cumsum(padded)
    dest = (pend - padded)[eid] + rank
    n_blocks = -(-A // MOE_BLOCK) + MOE_EXPERTS
    n_slots = n_blocks * MOE_BLOCK
    slot_tok = jnp.zeros((n_slots,), jnp.int32).at[dest].set(tok)
    slot_w = jnp.zeros((n_slots,), F32).at[dest].set(wts.reshape(-1))
    blk_e = jnp.minimum(jnp.searchsorted(pend, jnp.arange(n_blocks, dtype=jnp.int32) * MOE_BLOCK, side='right'),
                        MOE_EXPERTS - 1).astype(jnp.int32)
    xs = tokens.astype(BF16)[slot_tok]
    ys = moe_experts(xs, slot_w[:, None], blk_e, w1, w3, w2)
    d2 = dest.reshape(N, MOE_TOPK)
    return ys[d2[:, 0]] + ys[d2[:, 1]]


def _rms(x, g):
    return x * lax.rsqrt(jnp.mean(x * x, -1, keepdims=True) + EPS) * g


def _l2(x):
    return x * lax.rsqrt(jnp.sum(x * x, -1, keepdims=True) + EPS)


def _rope_tables(n_lat, n_ctx):
    rows = n_lat // GRID_W
    row = jnp.repeat(jnp.arange(rows, dtype=F32), GRID_W)
    col = jnp.tile(jnp.arange(GRID_W, dtype=F32), rows)
    n_freq = MLA_ROPE // 4
    inv = ROPE_BASE ** (-jnp.arange(n_freq, dtype=F32) / n_freq)
    ang = jnp.stack([row[:, None] * inv, col[:, None] * inv], axis=1)
    cos = jnp.concatenate([jnp.ones((n_ctx, 2, n_freq), F32), jnp.cos(ang)], 0)
    sin = jnp.concatenate([jnp.zeros((n_ctx, 2, n_freq), F32), jnp.sin(ang)], 0)
    return cos, sin


def _rope(x, cos, sin):
    B, T, H, _ = x.shape
    xr = x.reshape(B, T, H, 2, 2, MLA_ROPE // 4)
    x1, x2 = xr[..., 0, :], xr[..., 1, :]
    c = cos[None, :, None]
    s = sin[None, :, None]
    return jnp.stack([x1 * c - x2 * s, x2 * c + x1 * s], axis=-2).reshape(B, T, H, MLA_ROPE)


def _per_segment(fn, x, n_ctx):
    return jnp.concatenate([fn(x[:, :n_ctx]), fn(x[:, n_ctx:])], axis=1)


def _mla_heads(t, gain, cos, sin):
    B, T, H, _ = t.shape
    t = _rms(t, gain)
    t = jnp.concatenate([t[..., :MLA_NOPE], _rope(t[..., MLA_NOPE:], cos, sin),
                         jnp.zeros((B, T, H, MLA_PAD - MLA_QK), F32)], -1)
    return t.reshape(B, T, H * MLA_PAD).astype(BF16)


def attn_delta_mixer(u, n_ctx, rope, w_in, w_out, qa_g, w_qb, kva_g, w_kvb, qn_g, kn_g,
                     conv_w, a_log, dt_bias, out_g):
    B, T, D = u.shape
    M = B * T
    H = MLA_HEADS
    cos, sin = rope
    ub = u.reshape(M, D).astype(BF16)
    wb = w_in.astype(BF16)
    c0, c1, c2 = MLA_COLS, MLA_COLS + GDN_QKV, MLA_COLS + GDN_QKV + GDN_Z
    um = mm(ub, wb[:, :c0])
    gq = mm(ub, wb[:, c0:c1]).reshape(B, T, GDN_QKV)
    z = mm(ub, wb[:, c1:c2]).reshape(B, T, GDN_HEADS, GDN_DV)
    ab = mm(ub, wb[:, c2:]).reshape(B, T, 2, 2, GDN_HEADS)

    c_q = _rms(um[:, :MLA_Q_LORA], qa_g).astype(BF16)
    c_kv = _rms(um[:, MLA_Q_LORA:MLA_Q_LORA + MLA_KV_LORA], kva_g).astype(BF16)
    k_pe = um[:, MLA_Q_LORA + MLA_KV_LORA:].reshape(B, T, 1, MLA_ROPE)
    q = mm(c_q, w_qb.astype(BF16)).reshape(B, T, H, MLA_QK)
    wkv = w_kvb.reshape(MLA_KV_LORA, H, MLA_NOPE + MLA_V)
    w_kv2 = jnp.concatenate([wkv[:, :, :MLA_NOPE].reshape(MLA_KV_LORA, H * MLA_NOPE),
                             wkv[:, :, MLA_NOPE:].reshape(MLA_KV_LORA, H * MLA_V)], axis=1)
    kv = mm(c_kv, w_kv2.astype(BF16))
    k_nope = kv[:, :H * MLA_NOPE].reshape(B, T, H, MLA_NOPE)
    val = kv[:, H * MLA_NOPE:].reshape(B, T, H * MLA_V).astype(BF16)
    k = jnp.concatenate([k_nope, jnp.broadcast_to(k_pe, (B, T, H, MLA_ROPE))], -1)
    qh = _mla_heads(q, qn_g, cos, sin)
    kh = _mla_heads(k, kn_g, cos, sin)
    a_lat = attention(qh[:, n_ctx:], kh, val)
    a_ctx = attention(qh[:, :n_ctx], kh[:, :n_ctx], val[:, :n_ctx])
    a = jnp.concatenate([a_ctx, a_lat], axis=1)

    def conv(x):
        t = x.shape[1]
        xp = jnp.pad(x, ((0, 0), (GDN_CONV // 2, GDN_CONV // 2), (0, 0)))
        return sum(xp[:, j:j + t] * conv_w[j] for j in range(GDN_CONV))

    qkv = jax.nn.silu(_per_segment(conv, gq, n_ctx))
    nk = GDN_HEADS * GDN_DK
    gq_ = (_l2(qkv[..., :nk].reshape(B, T, GDN_HEADS, GDN_DK)) * GDN_DK ** -0.5).reshape(B, T, nk)
    gk_ = _l2(qkv[..., nk:2 * nk].reshape(B, T, GDN_HEADS, GDN_DK)).reshape(B, T, nk)
    gv_ = qkv[..., 2 * nk:]
    g = -jnp.exp(a_log) * jax.nn.softplus(ab[:, :, :, 0] + dt_bias)
    beta = jax.nn.sigmoid(ab[:, :, :, 1])
    o = gdn_scan(gq_, gk_, gv_, g, beta, n_ctx).reshape(B, T, GDN_HEADS, GDN_DV)
    dl = (_rms(o, out_g) * jax.nn.silu(z)).reshape(B, T, GDN_HEADS * GDN_DV)

    mix = jnp.concatenate([a, dl], axis=-1).reshape(M, -1).astype(BF16)
    return mm(mix, w_out.astype(BF16)).reshape(B, T, D)


def rwkv7_mixer(u, n_ctx, v_first, mu, wr, wk, wv, wo, w0, w1, w2, a0, a1, a2, g1, g2, k_k, k_a, r_k,
                ln_w, ln_b, vres):
    B, T, D = u.shape
    M = B * T
    N = RWKV_HEAD
    H = D // N

    def shift(x):
        xp = jnp.pad(x, ((0, 0), (1, 1), (0, 0)))
        return 0.5 * (xp[:, :-2] + xp[:, 2:]) - x

    xx = _per_segment(shift, u, n_ctx)
    xr, xw, xk, xv, xa, xg = [(u + xx * mu[i]).reshape(M, D).astype(BF16) for i in range(6)]
    r = mm(xr, wr.astype(BF16))
    k = mm(xk, wk.astype(BF16))
    v = mm(xv, wv.astype(BF16))
    if vres is None:
        v_first = v
    else:
        v0, v1, v2 = vres
        v = v + (v_first - v) * jax.nn.sigmoid(v0 + mm(mm(xv, v1.astype(BF16)), v2))
    gate = mm(jax.nn.sigmoid(mm(xg, g1.astype(BF16))), g2)
    hd = lambda t: jnp.swapaxes(t.reshape(B, T, H, N), 1, 2)
    kk = _l2((k * k_k).reshape(M, H, N)).reshape(M, D)
    lws, keys, rates = [], [], []
    for d in range(2):
        w_log = -jax.nn.softplus(-(w0[d] + mm(jnp.tanh(mm(xw, w1[d].astype(BF16))), w2[d]))) - 0.5
        a = jax.nn.sigmoid(a0[d] + mm(mm(xa, a1[d].astype(BF16)), a2[d]))
        lws.append(hd(-jnp.exp(w_log)))
        keys.append(k * (1 + (a - 1) * k_a))
        rates.append(hd(a))
    y = rwkv_scan(hd(r), hd(v), hd(kk), lws, [hd(keys[0]), hd(keys[1])], rates, n_ctx)
    y = jnp.swapaxes(y, 1, 2)
    mean = jnp.mean(y, -1, keepdims=True)
    var = jnp.mean(jnp.square(y - mean), -1, keepdims=True)
    yn = ((y - mean) * lax.rsqrt(var + GN_EPS)).reshape(M, D) * ln_w + ln_b
    k_bonus = 0.5 * (keys[0] + keys[1])
    bonus = (jnp.sum((r * k_bonus).reshape(M, H, N) * r_k, -1, keepdims=True) * v.reshape(M, H, N)).reshape(M, D)
    out = mm(((yn + bonus) * gate).astype(BF16), wo.astype(BF16))
    return out.reshape(B, T, D), v_first


def kernel(x, c, ctx, c_ctx, ada_w, ada_b, norm_mix, norm_ffn, hy_w_in, hy_w_out, mla_qa_norm, mla_w_qb, mla_kva_norm, mla_w_kvb, mla_q_norm, mla_k_norm, gdn_conv, gdn_a_log, gdn_dt_bias, gdn_out_norm, rk_mu, rk_wr, rk_wk, rk_wv, rk_wo, rk_w0, rk_w1, rk_w2, rk_a0, rk_a1, rk_a2, rk_g1, rk_g2, rk_kk, rk_ka, rk_rk, rk_ln_w, rk_ln_b, rk_v0, rk_v1, rk_v2, moe_w_group, moe_b_group, moe_w_expert, moe_b_expert, moe_w1, moe_w3, moe_w2):
    B, S, D = x.shape
    L = ctx.shape[1]
    T = L + S
    depth = ada_w.shape[0]
    rope = _rope_tables(S, L)
    n_rows = -(-(B + 1) // 8) * 8
    sc = jnp.concatenate([jax.nn.silu(c), jax.nn.silu(c_ctx)[None], jnp.zeros((n_rows - B - 1, D), F32)], 0)
    h = jnp.concatenate([ctx, x], axis=1)
    seg = math.gcd(L, S)
    v_first = None
    for l in range(depth):
        m = mm(sc, ada_w[l], hi=True) + ada_b[l]
        m_lat = jnp.broadcast_to(m[:B].reshape(B, 1, 6, D), (B, S // seg, 6, D))
        m_ctx = jnp.broadcast_to(m[B].reshape(1, 1, 6, D), (B, L // seg, 6, D))
        m_seg = jnp.concatenate([m_ctx, m_lat], axis=1)

        def mod(i, m_seg=m_seg):
            rows = jnp.broadcast_to(m_seg[:, :, None, i, :], (B, T // seg, seg, D))
            return rows.reshape(B, T, D)

        u = _rms(h, norm_mix[l]) * (1 + mod(1)) + mod(0)
        j = l // 2
        if l % 2 == 0:
            y = attn_delta_mixer(u, L, rope, hy_w_in[j], hy_w_out[j], mla_qa_norm[j], mla_w_qb[j],
                                 mla_kva_norm[j], mla_w_kvb[j], mla_q_norm[j], mla_k_norm[j], gdn_conv[j],
                                 gdn_a_log[j], gdn_dt_bias[j], gdn_out_norm[j])
        else:
            vres = None if j == 0 else (rk_v0[j - 1], rk_v1[j - 1], rk_v2[j - 1])
            y, vf = rwkv7_mixer(u, L, v_first, rk_mu[j], rk_wr[j], rk_wk[j], rk_wv[j], rk_wo[j], rk_w0[j],
                                rk_w1[j], rk_w2[j], rk_a0[j], rk_a1[j], rk_a2[j], rk_g1[j], rk_g2[j],
                                rk_kk[j], rk_ka[j], rk_rk[j], rk_ln_w[j], rk_ln_b[j], vres)
            if j == 0:
                v_first = vf
        h = h + mod(2) * y
        f = _rms(h, norm_ffn[l]) * (1 + mod(4)) + mod(3)
        moe_out = hier_moe(f.reshape(B * T, D), moe_w_group[l], moe_b_group[l], moe_w_expert[l],
                           moe_b_expert[l], moe_w1[l].astype(BF16), moe_w3[l].astype(BF16),
                           moe_w2[l].astype(BF16))
        h = h + mod(5) * moe_out.reshape(B, T, D)
    return h[:, L:]
```

```python
import functools
import math

import jax
import jax.numpy as jnp
from jax import lax
from jax.experimental import pallas as pl
from jax.experimental.pallas import tpu as pltpu

F32 = jnp.float32
BF16 = jnp.bfloat16
HI = lax.Precision.HIGHEST

DEPTH = 4
GRID_W = 64
EPS = 1e-6

MLA_HEADS = 8
MLA_Q_LORA = 256
MLA_KV_LORA = 128
MLA_NOPE = 64
MLA_ROPE = 32
MLA_V = 64
MLA_QK = MLA_NOPE + MLA_ROPE
MLA_SCALE = MLA_QK ** -0.5
ROPE_BASE = 10000.0
MLA_PAD = 128

GDN_HEADS = 4
GDN_DK = 128
GDN_DV = 128
GDN_CONV = 5
GDN_CHUNK = 64

RWKV_HEAD = 64
RWKV_CHUNK = 64
GN_EPS = 64e-5

MOE_GROUPS = 4
MOE_PER_GROUP = 8
MOE_EXPERTS = MOE_GROUPS * MOE_PER_GROUP
MOE_TOPK = 2
MOE_BLOCK = 256

MLA_COLS = MLA_Q_LORA + MLA_KV_LORA + MLA_ROPE
GDN_QKV = GDN_HEADS * (2 * GDN_DK + GDN_DV)
GDN_Z = GDN_HEADS * GDN_DV
GDN_AB = 2 * 2 * GDN_HEADS

VMEM_LIMIT_BYTES = 48 * 1024 * 1024

GDN_PASSES = 1
RWKV_PASSES = 1
ATTN_BF16_EXP = False


def _cparams(*sem):
    return pltpu.CompilerParams(dimension_semantics=sem, vmem_limit_bytes=VMEM_LIMIT_BYTES)


def _pick(n, cands):
    for c in cands:
        if n % c == 0:
            return c
    return n


def _split(a):
    hi = a.astype(BF16)
    lo = (a - hi.astype(F32)).astype(BF16)
    return hi, lo


def _dg(a, b, dn, passes):
    if passes == 6:
        return lax.dot_general(a, b, dn, precision=HI, preferred_element_type=F32)
    if passes == 1:
        return lax.dot_general(a.astype(BF16), b.astype(BF16), dn, preferred_element_type=F32)
    ah, al = _split(a)
    bh, bl = _split(b)
    d = functools.partial(lax.dot_general, dimension_numbers=dn, preferred_element_type=F32)
    return d(ah, bh) + d(al, bh) + d(ah, bl)


_NN = (((1,), (0,)), ((), ()))
_NT = (((1,), (1,)), ((), ()))
_TN = (((0,), (0,)), ((), ()))
_BNN = (((2,), (1,)), ((0,), (0,)))
_BNT = (((2,), (2,)), ((0,), (0,)))
_BTN = (((1,), (1,)), ((0,), (0,)))


def _mm_body(x_ref, w_ref, o_ref, *, hi):
    if hi:
        o_ref[...] = jnp.dot(x_ref[...], w_ref[...], precision=HI, preferred_element_type=F32)
    else:
        o_ref[...] = jnp.dot(x_ref[...].astype(BF16), w_ref[...].astype(BF16),
                             preferred_element_type=F32)


def mm(x, w, hi=False):
    M, K = x.shape
    N = w.shape[1]
    tm = _pick(M, (512, 256, 128, 64, 32, 16, 8))
    tn = _pick(N, (512, 384, 256, 128))
    return pl.pallas_call(
        functools.partial(_mm_body, hi=hi),
        grid=(M // tm, N // tn),
        in_specs=[pl.BlockSpec((tm, K), lambda i, j: (i, 0)),
                  pl.BlockSpec((K, tn), lambda i, j: (0, j))],
        out_specs=pl.BlockSpec((tm, tn), lambda i, j: (i, j)),
        out_shape=jax.ShapeDtypeStruct((M, N), F32),
        compiler_params=_cparams("parallel", "parallel"),
        name="dense_mm",
    )(x, w)


def _attn_body(q_ref, k_ref, v_ref, o_ref, m_ref, acc_ref, *, c2, bf16_exp):
    ki = pl.program_id(3)

    @pl.when(ki == 0)
    def _():
        m_ref[...] = jnp.full(m_ref.shape, -1e30, F32)
        acc_ref[...] = jnp.zeros(acc_ref.shape, F32)

    for h in range(2):
        q = q_ref[0, :, h * MLA_PAD:(h + 1) * MLA_PAD]
        k = k_ref[0, :, h * MLA_PAD:(h + 1) * MLA_PAD]
        v = v_ref[0, :, h * MLA_PAD:(h + 1) * MLA_PAD]
        s = lax.dot_general(q, k, _NT, preferred_element_type=F32)
        m_prev = m_ref[h]
        m_new = jnp.maximum(m_prev, jnp.max(s, axis=-1, keepdims=True))
        alpha = jnp.exp2((m_prev - m_new) * c2)
        x = (s - m_new) * c2
        p = jnp.exp2(x.astype(BF16)) if bf16_exp else jnp.exp2(x).astype(BF16)
        acc_ref[h] = alpha * acc_ref[h] + jnp.dot(p, v, preferred_element_type=F32)
        m_ref[h] = m_new

    @pl.when(ki == pl.num_programs(3) - 1)
    def _():
        outs = []
        for h in range(2):
            a = acc_ref[h]
            outs.append(a[:, :MLA_V] / a[:, MLA_V:MLA_V + 1])
        o_ref[0] = jnp.concatenate(outs, axis=-1)


def attention(q, k, v):
    B, Sq, _ = q.shape
    Sk = k.shape[1]
    tq = _pick(Sq, (512, 256, 128))
    tk = _pick(Sk, (768, 512, 384, 256, 128))
    return pl.pallas_call(
        functools.partial(_attn_body, c2=MLA_SCALE * math.log2(math.e), bf16_exp=ATTN_BF16_EXP),
        grid=(B, MLA_HEADS // 2, Sq // tq, Sk // tk),
        in_specs=[pl.BlockSpec((1, tq, 2 * MLA_PAD), lambda b, p, i, j: (b, i, p)),
                  pl.BlockSpec((1, tk, 2 * MLA_PAD), lambda b, p, i, j: (b, j, p)),
                  pl.BlockSpec((1, tk, 2 * MLA_PAD), lambda b, p, i, j: (b, j, p))],
        out_specs=pl.BlockSpec((1, tq, 2 * MLA_V), lambda b, p, i, j: (b, i, p)),
        out_shape=jax.ShapeDtypeStruct((B, Sq, MLA_HEADS * MLA_V), F32),
        scratch_shapes=[pltpu.VMEM((2, tq, 1), F32), pltpu.VMEM((2, tq, MLA_PAD), F32)],
        compiler_params=_cparams("parallel", "parallel", "parallel", "arbitrary"),
        name="mla_attention",
    )(q, k, v)


def _tri_masks(C, rev):
    row = lax.broadcasted_iota(jnp.int32, (C, C), 0)
    col = lax.broadcasted_iota(jnp.int32, (C, C), 1)
    if rev:
        return row <= col, row < col
    return row >= col, row > col


def _neumann_inverse(nil, dn, passes):
    C = nil.shape[-1]
    eye = (lax.broadcasted_iota(jnp.int32, (C, C), 0) ==
           lax.broadcasted_iota(jnp.int32, (C, C), 1)).astype(F32)
    x = eye + nil
    p = nil
    for _ in range(int(math.log2(C)) - 1):
        p = _dg(p, p, dn, passes)
        x = x + _dg(x, p, dn, passes)
    return x


def _gdn_body(qf, kf, vf, gcf, bcf, grf, qb, kb, vb, gcb, bcb, grb, of_ref, ob_ref, s_ref, *, passes):
    C = GDN_CHUNK
    H = GDN_HEADS

    @pl.when(pl.program_id(1) == 0)
    def _():
        s_ref[...] = jnp.zeros(s_ref.shape, F32)

    dirs = ((qf, kf, vf, gcf, bcf, grf, of_ref), (qb, kb, vb, gcb, bcb, grb, ob_ref))
    for d, (q_ref, k_ref, v_ref, gc_ref, bc_ref, gr_ref, o_ref) in enumerate(dirs):
        rev = d == 1
        incl, strict = _tri_masks(C, rev)
        tri = incl.astype(F32)
        gcum_col = _dg(tri, gc_ref[0], _NN, 6)
        gcum_row = _dg(gr_ref[0, 0], tri, _NT, 6)
        beta_all = bc_ref[0]
        for h in range(H):
            idx = d * H + h
            gc = gcum_col[:, idx:idx + 1]
            gr = gcum_row[idx:idx + 1, :]
            beta = beta_all[:, idx:idx + 1]
            q = q_ref[0, :, h * GDN_DK:(h + 1) * GDN_DK]
            k = k_ref[0, :, h * GDN_DK:(h + 1) * GDN_DK]
            v = v_ref[0, :, h * GDN_DV:(h + 1) * GDN_DV]
            decay = jnp.exp(jnp.where(incl, gc - gr, -1e30))
            kbeta = k * beta
            lower = jnp.where(strict, _dg(kbeta, k, _NT, passes) * decay, 0.0)
            tinv = _neumann_inverse(-lower, _NN, passes)
            eg = jnp.exp(gc)
            rhs = jnp.concatenate([v * beta, kbeta * eg], axis=-1)
            sol = _dg(tinv, rhs, _NN, passes)
            u = sol[:, :GDN_DV]
            w = sol[:, GDN_DV:]
            aqk = jnp.where(incl, _dg(q, k, _NT, passes) * decay, 0.0)
            s = s_ref[d, h]
            v_new = u - _dg(w, s, _NN, passes)
            o = _dg(q * eg, s, _NN, passes) + _dg(aqk, v_new, _NN, passes)
            g_last = gc[0:1, :] if rev else gc[C - 1:C, :]
            s_ref[d, h] = s * jnp.exp(g_last) + _dg(k * jnp.exp(g_last - gc), v_new, _TN, passes)
            o_ref[0, :, h * GDN_DV:(h + 1) * GDN_DV] = o


def _rev_chunk(i, ncc, nc):
    return jnp.where(i < ncc, ncc - 1 - i, nc - 1 + ncc - i)


def gdn_scan(q, k, v, g, beta, n_ctx):
    B, T, _ = q.shape
    C = GDN_CHUNK
    nc = T // C
    ncc = n_ctx // C
    gcol = g.reshape(B, T, 2 * GDN_HEADS)
    bcol = beta.reshape(B, T, 2 * GDN_HEADS)
    grow = jnp.swapaxes(gcol.reshape(B, nc, C, 2 * GDN_HEADS), 2, 3)
    fwd = lambda b, i: (b, i, 0)
    bwd = lambda b, i: (b, _rev_chunk(i, ncc, nc), 0)
    fwd4 = lambda b, i: (b, i, 0, 0)
    bwd4 = lambda b, i: (b, _rev_chunk(i, ncc, nc), 0, 0)
    wide = q.shape[-1]
    wv = v.shape[-1]

    def specs(m3, m4):
        return [pl.BlockSpec((1, C, wide), m3), pl.BlockSpec((1, C, wide), m3), pl.BlockSpec((1, C, wv), m3),
                pl.BlockSpec((1, C, 2 * GDN_HEADS), m3), pl.BlockSpec((1, C, 2 * GDN_HEADS), m3),
                pl.BlockSpec((1, 1, 2 * GDN_HEADS, C), m4)]

    of, ob = pl.pallas_call(
        functools.partial(_gdn_body, passes=GDN_PASSES),
        grid=(B, nc),
        in_specs=specs(fwd, fwd4) + specs(bwd, bwd4),
        out_specs=[pl.BlockSpec((1, C, wv), fwd), pl.BlockSpec((1, C, wv), bwd)],
        out_shape=[jax.ShapeDtypeStruct((B, T, wv), F32)] * 2,
        scratch_shapes=[pltpu.VMEM((2, GDN_HEADS, GDN_DK, GDN_DV), F32)],
        compiler_params=_cparams("parallel", "arbitrary"),
        name="gdn_scan",
    )(q, k, v, gcol, bcol, grow, q, k, v, gcol, bcol, grow)
    return of + ob


def _rwkv_dir(r, lw, k, v, kk, rate, s, rev, passes):
    H, C, N = r.shape
    incl, strict = _tri_masks(C, rev)
    tri = jnp.broadcast_to(incl.astype(F32), (H, C, C))
    linc = _dg(tri, lw, _BNN, 6)
    lexc = linc - lw
    ltot = linc[:, 0:1, :] if rev else linc[:, C - 1:C, :]
    b = kk * rate
    at = -kk * jnp.exp(lexc)
    rt = r * jnp.exp(linc)
    einv = jnp.exp(-linc)
    bt = b * einv
    kt = k * einv
    etail = jnp.exp(ltot - linc)
    lab = jnp.where(strict, _dg(at, bt, _BNT, passes), 0.0)
    lak = jnp.where(strict, _dg(at, kt, _BNT, passes), 0.0)
    mrb = jnp.where(incl, _dg(rt, bt, _BNT, passes), 0.0)
    mrk = jnp.where(incl, _dg(rt, kt, _BNT, passes), 0.0)
    tinv = _neumann_inverse(lab, _BNN, passes)
    a_s = _dg(at, s, _BNT, passes)
    r_s = _dg(rt, s, _BNT, passes)
    u = _dg(tinv, a_s + _dg(lak, v, _BNN, passes), _BNN, passes)
    y = r_s + _dg(mrb, u, _BNN, passes) + _dg(mrk, v, _BNN, passes)
    s_new = (s * jnp.exp(ltot) + _dg(u, b * etail, _BTN, passes) + _dg(v, k * etail, _BTN, passes))
    return y, s_new


def _rwkv_body(rf, vf, kkf, lwf, kf, af, rb, vb, kkb, lwb, kb, ab, yf_ref, yb_ref, s_ref, *, passes):
    @pl.when(pl.program_id(1) == 0)
    def _():
        s_ref[...] = jnp.zeros(s_ref.shape, F32)

    N = RWKV_HEAD
    H = rf.shape[-1] // N

    def heads(ref):
        x = ref[0]
        return jnp.stack([x[:, h * N:(h + 1) * N] for h in range(H)], axis=0)

    dirs = ((rf, vf, kkf, lwf, kf, af, yf_ref), (rb, vb, kkb, lwb, kb, ab, yb_ref))
    for d, (r_ref, v_ref, kk_ref, lw_ref, k_ref, a_ref, y_ref) in enumerate(dirs):
        y, s_new = _rwkv_dir(heads(r_ref), heads(lw_ref), heads(k_ref), heads(v_ref), heads(kk_ref),
                             heads(a_ref), s_ref[d], d == 1, passes)
        s_ref[d] = s_new
        y_ref[0] = jnp.concatenate([y[h] for h in range(H)], axis=-1)


def rwkv_scan(r, v, kk, lw, key, rate, n_ctx):
    B, T, D = r.shape
    N = RWKV_HEAD
    C = RWKV_CHUNK
    nc = T // C
    ncc = n_ctx // C
    fwd = lambda b, i: (b, i, 0)
    bwd = lambda b, i: (b, _rev_chunk(i, ncc, nc), 0)
    blk = (1, C, D)
    return pl.pallas_call(
        functools.partial(_rwkv_body, passes=RWKV_PASSES),
        grid=(B, nc),
        in_specs=[pl.BlockSpec(blk, fwd)] * 6 + [pl.BlockSpec(blk, bwd)] * 6,
        out_specs=[pl.BlockSpec(blk, fwd), pl.BlockSpec(blk, bwd)],
        out_shape=[jax.ShapeDtypeStruct((B, T, D), F32)] * 2,
        scratch_shapes=[pltpu.VMEM((2, D // N, N, N), F32)],
        compiler_params=_cparams("parallel", "arbitrary"),
        name="rwkv7_scan",
    )(r, v, kk, lw[0], key[0], rate[0], r, v, kk, lw[1], key[1], rate[1])


def _moe_body(be_ref, x_ref, w1_ref, w3_ref, w2_ref, ws_ref, o_ref):
    del be_ref
    x = x_ref[...]
    h1 = jnp.dot(x, w1_ref[0], preferred_element_type=F32)
    h3 = jnp.dot(x, w3_ref[0], preferred_element_type=F32)
    hid = (h1 * jax.nn.sigmoid(h1)) * h3
    y = jnp.dot(hid.astype(BF16), w2_ref[0], preferred_element_type=F32)
    o_ref[...] = y * ws_ref[...]


def moe_experts(xs, slot_w, blk_e, w1, w3, w2):
    n_slots, D = xs.shape
    hid = w1.shape[-1]
    n_blocks = n_slots // MOE_BLOCK
    return pl.pallas_call(
        _moe_body,
        grid_spec=pltpu.PrefetchScalarGridSpec(
            num_scalar_prefetch=1,
            grid=(n_blocks,),
            in_specs=[pl.BlockSpec((MOE_BLOCK, D), lambda i, be: (i, 0)),
                      pl.BlockSpec((1, D, hid), lambda i, be: (be[i], 0, 0)),
                      pl.BlockSpec((1, D, hid), lambda i, be: (be[i], 0, 0)),
                      pl.BlockSpec((1, hid, D), lambda i, be: (be[i], 0, 0)),
                      pl.BlockSpec((MOE_BLOCK, 1), lambda i, be: (i, 0))],
            out_specs=pl.BlockSpec((MOE_BLOCK, D), lambda i, be: (i, 0)),
        ),
        out_shape=jax.ShapeDtypeStruct((n_slots, D), F32),
        compiler_params=_cparams("arbitrary"),
        name="moe_experts",
    )(blk_e, xs, w1, w3, w2, slot_w)


def hier_moe(tokens, logits, b_group, b_expert, w1, w3, w2):
    N, D = tokens.shape
    logits = logits[:, :MOE_GROUPS + MOE_EXPERTS]
    pg = jax.nn.softmax(logits[:, :MOE_GROUPS] + b_group, axis=-1)
    g_idx = jnp.argmax(pg, axis=-1)[:, None].astype(jnp.int32)
    pg_top = jnp.max(pg, axis=-1, keepdims=True)
    le = logits[:, MOE_GROUPS:] + b_expert
    sel = g_idx * MOE_PER_GROUP + jnp.arange(MOE_PER_GROUP)[None, :]
    pe = jax.nn.softmax(jnp.take_along_axis(le, sel, axis=1), axis=-1)
    lane = jnp.arange(MOE_PER_GROUP, dtype=jnp.int32)[None, :]
    e1 = jnp.argmax(pe, axis=-1)[:, None].astype(jnp.int32)
    rest = jnp.where(lane == e1, -jnp.inf, pe)
    e2 = jnp.argmax(rest, axis=-1)[:, None].astype(jnp.int32)
    pe_top = jnp.concatenate([jnp.max(pe, -1, keepdims=True), jnp.max(rest, -1, keepdims=True)], axis=1)
    e_loc = jnp.concatenate([e1, e2], axis=1)
    wts = pg_top * pe_top / jnp.sum(pe_top, -1, keepdims=True)
    eid = (g_idx * MOE_PER_GROUP + e_loc).reshape(-1)
    tok = jnp.repeat(jnp.arange(N, dtype=jnp.int32), MOE_TOPK)
    A = N * MOE_TOPK
    onehot = (eid[:, None] == jnp.arange(MOE_EXPERTS)[None, :]).astype(jnp.int32)
    rank = jnp.take_along_axis(jnp.cumsum(onehot, 0), eid[:, None], 1)[:, 0] - 1
    counts = jnp.sum(onehot, 0)
    padded = (counts + MOE_BLOCK - 1) // MOE_BLOCK * MOE_BLOCK
    pend = jnp.cumsum(padded)
    dest = (pend - padded)[eid] + rank
    n_blocks = -(-A // MOE_BLOCK) + MOE_EXPERTS
    n_slots = n_blocks * MOE_BLOCK
    slot_tok = jnp.zeros((n_slots,), jnp.int32).at[dest].set(tok)
    slot_w = jnp.zeros((n_slots,), F32).at[dest].set(wts.reshape(-1))
    blk_e = jnp.minimum(jnp.searchsorted(pend, jnp.arange(n_blocks, dtype=jnp.int32) * MOE_BLOCK, side='right'),
                        MOE_EXPERTS - 1).astype(jnp.int32)
    xs = tokens[slot_tok]
    ys = moe_experts(xs, slot_w[:, None], blk_e, w1, w3, w2)
    d2 = dest.reshape(N, MOE_TOPK)
    return ys[d2[:, 0]] + ys[d2[:, 1]]


def _rms(x, g):
    return x * lax.rsqrt(jnp.mean(x * x, -1, keepdims=True) + EPS) * g


def _l2(x):
    return x * lax.rsqrt(jnp.sum(x * x, -1, keepdims=True) + EPS)


def _rope_tables(n_lat, n_ctx):
    rows = n_lat // GRID_W
    row = jnp.repeat(jnp.arange(rows, dtype=F32), GRID_W)
    col = jnp.tile(jnp.arange(GRID_W, dtype=F32), rows)
    n_freq = MLA_ROPE // 4
    inv = ROPE_BASE ** (-jnp.arange(n_freq, dtype=F32) / n_freq)
    ang = jnp.stack([row[:, None] * inv, col[:, None] * inv], axis=1)
    cos = jnp.concatenate([jnp.ones((n_ctx, 2, n_freq), F32), jnp.cos(ang)], 0)
    sin = jnp.concatenate([jnp.zeros((n_ctx, 2, n_freq), F32), jnp.sin(ang)], 0)
    return cos, sin


def _rope(x, cos, sin):
    B, T, H, _ = x.shape
    xr = x.reshape(B, T, H, 2, 2, MLA_ROPE // 4)
    x1, x2 = xr[..., 0, :], xr[..., 1, :]
    c = cos[None, :, None]
    s = sin[None, :, None]
    return jnp.stack([x1 * c - x2 * s, x2 * c + x1 * s], axis=-2).reshape(B, T, H, MLA_ROPE)


def _per_segment(fn, x, n_ctx):
    return jnp.concatenate([fn(x[:, :n_ctx]), fn(x[:, n_ctx:])], axis=1)


def _mla_heads(t, gain, cos, sin):
    B, T, H, _ = t.shape
    t = _rms(t, gain)
    t = jnp.concatenate([t[..., :MLA_NOPE], _rope(t[..., MLA_NOPE:], cos, sin),
                         jnp.zeros((B, T, H, MLA_PAD - MLA_QK), F32)], -1)
    return t.reshape(B, T, H * MLA_PAD).astype(BF16)


def attn_delta_mixer(u, n_ctx, rope, w_in, w_out, qa_g, w_qb, kva_g, w_kvb, qn_g, kn_g,
                     conv_w, a_log, dt_bias, out_g):
    B, T, D = u.shape
    M = B * T
    H = MLA_HEADS
    cos, sin = rope
    ub = u.reshape(M, D).astype(BF16)
    wb = w_in.astype(BF16)
    c0, c1, c2 = MLA_COLS, MLA_COLS + GDN_QKV, MLA_COLS + GDN_QKV + GDN_Z
    um = mm(ub, wb[:, :c0])
    gq = mm(ub, wb[:, c0:c1]).reshape(B, T, GDN_QKV)
    z = mm(ub, wb[:, c1:c2]).reshape(B, T, GDN_HEADS, GDN_DV)
    ab = mm(ub, wb[:, c2:]).reshape(B, T, 2, 2, GDN_HEADS)

    c_q = _rms(um[:, :MLA_Q_LORA], qa_g).astype(BF16)
    c_kv = _rms(um[:, MLA_Q_LORA:MLA_Q_LORA + MLA_KV_LORA], kva_g).astype(BF16)
    k_pe = um[:, MLA_Q_LORA + MLA_KV_LORA:].reshape(B, T, 1, MLA_ROPE)
    q = mm(c_q, w_qb.astype(BF16)).reshape(B, T, H, MLA_QK)
    wkv = w_kvb.reshape(MLA_KV_LORA, H, MLA_NOPE + MLA_V)
    w_kv2 = jnp.concatenate([wkv[:, :, :MLA_NOPE].reshape(MLA_KV_LORA, H * MLA_NOPE),
                             wkv[:, :, MLA_NOPE:].reshape(MLA_KV_LORA, H * MLA_V)], axis=1)
    kv = mm(c_kv, w_kv2.astype(BF16))
    k_nope = kv[:, :H * MLA_NOPE].reshape(B, T, H, MLA_NOPE)
    val = kv[:, H * MLA_NOPE:].reshape(B, T, H, MLA_V)
    val = jnp.concatenate([val, jnp.ones((B, T, H, 1), F32), jnp.zeros((B, T, H, MLA_PAD - MLA_V - 1), F32)], -1)
    val = val.reshape(B, T, H * MLA_PAD).astype(BF16)
    k = jnp.concatenate([k_nope, jnp.broadcast_to(k_pe, (B, T, H, MLA_ROPE))], -1)
    qh = _mla_heads(q, qn_g, cos, sin)
    kh = _mla_heads(k, kn_g, cos, sin)
    a_lat = attention(qh[:, n_ctx:], kh, val)
    a_ctx = attention(qh[:, :n_ctx], kh[:, :n_ctx], val[:, :n_ctx])
    a = jnp.concatenate([a_ctx, a_lat], axis=1)

    def conv(x):
        t = x.shape[1]
        xp = jnp.pad(x, ((0, 0), (GDN_CONV // 2, GDN_CONV // 2), (0, 0)))
        return sum(xp[:, j:j + t] * conv_w[j] for j in range(GDN_CONV))

    qkv = jax.nn.silu(_per_segment(conv, gq, n_ctx))
    nk = GDN_HEADS * GDN_DK
    gq_ = (_l2(qkv[..., :nk].reshape(B, T, GDN_HEADS, GDN_DK)) * GDN_DK ** -0.5).reshape(B, T, nk)
    gk_ = _l2(qkv[..., nk:2 * nk].reshape(B, T, GDN_HEADS, GDN_DK)).reshape(B, T, nk)
    gv_ = qkv[..., 2 * nk:]
    g = -jnp.exp(a_log) * jax.nn.softplus(ab[:, :, :, 0] + dt_bias)
    beta = jax.nn.sigmoid(ab[:, :, :, 1])
    o = gdn_scan(gq_, gk_, gv_, g, beta, n_ctx).reshape(B, T, GDN_HEADS, GDN_DV)
    dl = (_rms(o, out_g) * jax.nn.silu(z)).reshape(B, T, GDN_HEADS * GDN_DV)

    return jnp.concatenate([a, dl], axis=-1).reshape(M, -1).astype(BF16)


def _const_spec(shape):
    return pl.BlockSpec(shape, lambda i: (0,) * len(shape), pipeline_mode=pl.Buffered(1))


def _normmod(x, gain, shift, scale):
    return x * lax.rsqrt(jnp.mean(x * x, -1, keepdims=True) + EPS) * gain * (1 + scale) + shift


def _softplus(x):
    return jnp.maximum(x, 0.0) + jnp.log1p(jnp.exp(-jnp.abs(x)))


def _head_indicator(D, N):
    e = (jnp.arange(D)[:, None] // N == jnp.arange(128)[None, :]).astype(BF16)
    return e, e.T


def _seg_dot(x, e):
    xh, xl = _split(x)
    return jnp.dot(xh, e, preferred_element_type=F32) + jnp.dot(xl, e, preferred_element_type=F32)


def _dotf(a, b):
    return jnp.dot(a, b, preferred_element_type=F32)


def _rwkv_pre_body(*refs, tm, blocks_per_batch, ctx_blocks, vres):
    (h_ref, hp_ref, hn_ref, m_ref, gain_ref, mu_ref, w0_ref, a0_ref, kk_ref, ka_ref, e_ref, et_ref,
     wr_ref, wk_ref, wv_ref, w1_ref, w2_ref, a1_ref, a2_ref, g1_ref, g2_ref) = refs[:21]
    rest = refs[21:]
    if vres:
        v0_ref, v1_ref, v2_ref, vf_ref = rest[:4]
        rest = rest[4:]
    r_o, v_o, kk_o, lw0_o, lw1_o, k0_o, k1_o, ra0_o, ra1_o, gate_o = rest

    tb = pl.program_id(0) % blocks_per_batch
    seg_start = (tb == 0) | (tb == ctx_blocks)
    seg_end = (tb == ctx_blocks - 1) | (tb == blocks_per_batch - 1)
    shift, scale, gain = m_ref[0, 0:1, :], m_ref[0, 1:2, :], gain_ref[...]
    u = _normmod(h_ref[...], gain, shift, scale)
    up = jnp.where(seg_start, 0.0, _normmod(hp_ref[7:8, :], gain, shift, scale))
    un = jnp.where(seg_end, 0.0, _normmod(hn_ref[0:1, :], gain, shift, scale))
    row = lax.broadcasted_iota(jnp.int32, (tm, 1), 0)
    u_prev = jnp.where(row == 0, up, pltpu.roll(u, 1, 0))
    u_next = jnp.where(row == tm - 1, un, pltpu.roll(u, tm - 1, 0))
    xx = 0.5 * (u_prev + u_next) - u
    xr, xw, xk, xv, xa, xg = [(u + xx * mu_ref[j:j + 1, :]).astype(BF16) for j in range(6)]

    r = _dotf(xr, wr_ref[...])
    k = _dotf(xk, wk_ref[...])
    v = _dotf(xv, wv_ref[...])
    if vres:
        lo = _dotf(xv, v1_ref[...]).astype(BF16)
        v = v + (vf_ref[...] - v) * jax.nn.sigmoid(v0_ref[...] + _dotf(lo, v2_ref[...]))
    tl = jnp.tanh(_dotf(xw, w1_ref[...])).astype(BF16)
    al = _dotf(xa, a1_ref[...]).astype(BF16)
    gl = jax.nn.sigmoid(_dotf(xg, g1_ref[...])).astype(BF16)
    gate_o[...] = _dotf(gl, g2_ref[...])
    kx = k * kk_ref[...]
    inv = lax.rsqrt(_seg_dot(kx * kx, e_ref[...]) + EPS)
    r_o[...] = r
    v_o[...] = v
    kk_o[...] = kx * _seg_dot(inv, et_ref[...])
    for d, (lw_o, k_o, ra_o) in enumerate(((lw0_o, k0_o, ra0_o), (lw1_o, k1_o, ra1_o))):
        w_log = -_softplus(-(w0_ref[d:d + 1, :] + _dotf(tl, w2_ref[d]))) - 0.5
        lw_o[...] = -jnp.exp(w_log)
        a = jax.nn.sigmoid(a0_ref[d:d + 1, :] + _dotf(al, a2_ref[d]))
        ra_o[...] = a
        k_o[...] = k * (1 + (a - 1) * ka_ref[...])


def _row_tile(seg):
    return _pick(seg, (256, 128, 64, 32, 16, 8))


def _pad_cols(w, n):
    return jnp.pad(w, ((0, 0), (0, n - w.shape[1])))


def _pad_rows(w, n):
    return jnp.pad(w, ((0, n - w.shape[0]), (0, 0)))


def rwkv_pre(h, m_seg, seg, T, n_ctx, gain, mu, wr, wk, wv, w0, w1, w2, a0, a1, a2, g1, g2, k_k, k_a, vres, v_first):
    M, D = h.shape
    tm = _row_tile(seg)
    lora = w1.shape[-1]
    e, et = _head_indicator(D, RWKV_HEAD)
    zero = jnp.zeros((lora, D), F32)
    w2p = jnp.stack([jnp.concatenate([w2[0], zero], 0), jnp.concatenate([zero, w2[1]], 0)]).astype(BF16)
    a2p = jnp.stack([jnp.concatenate([a2[0], zero], 0), jnp.concatenate([zero, a2[1]], 0)]).astype(BF16)
    gp = -(-g1.shape[1] // 128) * 128
    row = lambda a: a.reshape(1, D)
    consts = [row(gain), mu, w0, a0, row(k_k), row(k_a), e, et,
              wr.astype(BF16), wk.astype(BF16), wv.astype(BF16),
              jnp.concatenate([w1[0], w1[1]], 1).astype(BF16), w2p,
              jnp.concatenate([a1[0], a1[1]], 1).astype(BF16), a2p,
              _pad_cols(g1, gp).astype(BF16), _pad_rows(g2, gp).astype(BF16)]
    row_spec = pl.BlockSpec((tm, D), lambda i: (i, 0))
    last8 = M // 8 - 1
    in_specs = [row_spec,
                pl.BlockSpec((8, D), lambda i: (jnp.maximum(i * (tm // 8) - 1, 0), 0)),
                pl.BlockSpec((8, D), lambda i: (jnp.minimum((i + 1) * (tm // 8), last8), 0)),
                pl.BlockSpec((1, 6, D), lambda i: (i * tm // seg, 0, 0))]
    in_specs += [_const_spec(c.shape) for c in consts]
    args = [h, h, h, m_seg] + consts
    if vres is not None:
        v0, v1, v2 = vres
        extra = [row(v0), _pad_cols(v1, 128).astype(BF16), _pad_rows(v2, 128).astype(BF16)]
        in_specs += [_const_spec(c.shape) for c in extra] + [row_spec]
        args += extra + [v_first]
    return pl.pallas_call(
        functools.partial(_rwkv_pre_body, tm=tm, blocks_per_batch=T // tm, ctx_blocks=n_ctx // tm,
                          vres=vres is not None),
        grid=(M // tm,),
        in_specs=in_specs,
        out_specs=[row_spec] * 10,
        out_shape=[jax.ShapeDtypeStruct((M, D), F32)] * 10,
        compiler_params=_cparams("parallel"),
        name="rwkv7_pre",
    )(*args)


def _post_tail(xo, h_ref, m_ref, gain_ref, w_ref, wrt_ref, h_o, f_o, lg_o):
    h_new = h_ref[...] + m_ref[0, 2:3, :] * _dotf(xo, w_ref[...])
    h_o[...] = h_new
    f = _normmod(h_new, gain_ref[...], m_ref[0, 3:4, :], m_ref[0, 4:5, :])
    f_o[...] = f.astype(BF16)
    lg_o[...] = jnp.dot(f, wrt_ref[...], precision=HI, preferred_element_type=F32)


def _mix_post_body(xo_ref, h_ref, m_ref, gain_ref, w_ref, wrt_ref, h_o, f_o, lg_o):
    _post_tail(xo_ref[...], h_ref, m_ref, gain_ref, w_ref, wrt_ref, h_o, f_o, lg_o)


def _rwkv_post_body(yf_ref, yb_ref, r_ref, k0_ref, k1_ref, v_ref, gate_ref, lnw_ref, lnb_ref, rk_ref, e_ref, et_ref,
                    h_ref, m_ref, gain_ref, w_ref, wrt_ref, h_o, f_o, lg_o):
    e, et = e_ref[...], et_ref[...]
    inv_n = 1.0 / RWKV_HEAD
    y = yf_ref[...] + yb_ref[...]
    yc = y - _seg_dot(_seg_dot(y, e) * inv_n, et)
    var = _seg_dot(_seg_dot(yc * yc, e) * inv_n, et)
    yn = yc * lax.rsqrt(var + GN_EPS) * lnw_ref[...] + lnb_ref[...]
    k_bonus = 0.5 * (k0_ref[...] + k1_ref[...])
    bonus = _seg_dot(_seg_dot(r_ref[...] * k_bonus * rk_ref[...], e), et) * v_ref[...]
    xo = ((yn + bonus) * gate_ref[...]).astype(BF16)
    _post_tail(xo, h_ref, m_ref, gain_ref, w_ref, wrt_ref, h_o, f_o, lg_o)


def _post_call(body, name, row_args, consts, h, m_seg, seg, gain, w_out, w_router):
    M, D = h.shape
    tm = _row_tile(seg)
    row_spec = lambda a: pl.BlockSpec((tm, a.shape[1]), lambda i: (i, 0))
    tail = [gain.reshape(1, D), w_out.astype(BF16), w_router]
    in_specs = ([row_spec(a) for a in row_args] + [_const_spec(c.shape) for c in consts] +
                [row_spec(h), pl.BlockSpec((1, 6, D), lambda i: (i * tm // seg, 0, 0))] +
                [_const_spec(c.shape) for c in tail])
    nr = w_router.shape[1]
    return pl.pallas_call(
        body,
        grid=(M // tm,),
        in_specs=in_specs,
        out_specs=[pl.BlockSpec((tm, D), lambda i: (i, 0)), pl.BlockSpec((tm, D), lambda i: (i, 0)),
                   pl.BlockSpec((tm, nr), lambda i: (i, 0))],
        out_shape=[jax.ShapeDtypeStruct((M, D), F32), jax.ShapeDtypeStruct((M, D), BF16),
                   jax.ShapeDtypeStruct((M, nr), F32)],
        compiler_params=_cparams("parallel"),
        name=name,
    )(*row_args, *consts, h, m_seg, *tail)


def mix_post(xo, h, m_seg, seg, gain, w_out, w_router):
    return _post_call(_mix_post_body, "mix_post", [xo], [], h, m_seg, seg, gain, w_out, w_router)


def rwkv_post(yf, yb, r, k0, k1, v, gate, ln_w, ln_b, r_k, h, m_seg, seg, gain, wo, w_router):
    D = h.shape[1]
    e, et = _head_indicator(D, RWKV_HEAD)
    consts = [ln_w.reshape(1, D), ln_b.reshape(1, D), r_k.reshape(1, D), e, et]
    return _post_call(_rwkv_post_body, "rwkv7_post", [yf, yb, r, k0, k1, v, gate], consts, h, m_seg, seg,
                      gain, wo, w_router)


def kernel(x, c, ctx, c_ctx, ada_w, ada_b, norm_mix, norm_ffn, hy_w_in, hy_w_out, mla_qa_norm, mla_w_qb, mla_kva_norm, mla_w_kvb, mla_q_norm, mla_k_norm, gdn_conv, gdn_a_log, gdn_dt_bias, gdn_out_norm, rk_mu, rk_wr, rk_wk, rk_wv, rk_wo, rk_w0, rk_w1, rk_w2, rk_a0, rk_a1, rk_a2, rk_g1, rk_g2, rk_kk, rk_ka, rk_rk, rk_ln_w, rk_ln_b, rk_v0, rk_v1, rk_v2, moe_w_group, moe_b_group, moe_w_expert, moe_b_expert, moe_w1, moe_w3, moe_w2):
    B, S, D = x.shape
    L = ctx.shape[1]
    T = L + S
    depth = ada_w.shape[0]
    rope = _rope_tables(S, L)
    n_rows = -(-(B + 1) // 8) * 8
    sc = jnp.concatenate([jax.nn.silu(c), jax.nn.silu(c_ctx)[None], jnp.zeros((n_rows - B - 1, D), F32)], 0)
    M = B * T
    h = jnp.concatenate([ctx, x], axis=1).reshape(M, D)
    seg = math.gcd(L, S)
    nseg = T // seg
    v_first = None
    for l in range(depth):
        m = mm(sc, ada_w[l], hi=True) + ada_b[l]
        m_lat = jnp.broadcast_to(m[:B].reshape(B, 1, 6, D), (B, S // seg, 6, D))
        m_ctx = jnp.broadcast_to(m[B].reshape(1, 1, 6, D), (B, L // seg, 6, D))
        m_seg = jnp.concatenate([m_ctx, m_lat], axis=1).reshape(B * nseg, 6, D)

        def mod(i, m_seg=m_seg):
            return m_seg[:, None, i, :]

        router = _pad_cols(jnp.concatenate([moe_w_group[l], moe_w_expert[l]], axis=1), 128)
        j = l // 2
        if l % 2 == 0:
            h3 = h.reshape(B * nseg, seg, D)
            u = (_rms(h3, norm_mix[l]) * (1 + mod(1)) + mod(0)).reshape(B, T, D)
            mix = attn_delta_mixer(u, L, rope, hy_w_in[j], hy_w_out[j], mla_qa_norm[j], mla_w_qb[j],
                                   mla_kva_norm[j], mla_w_kvb[j], mla_q_norm[j], mla_k_norm[j], gdn_conv[j],
                                   gdn_a_log[j], gdn_dt_bias[j], gdn_out_norm[j])
            h, f, logits = mix_post(mix, h, m_seg, seg, norm_ffn[l], hy_w_out[j], router)
        else:
            vres = None if j == 0 else (rk_v0[j - 1], rk_v1[j - 1], rk_v2[j - 1])
            r, v, kk, lw0, lw1, k0, k1, ra0, ra1, gate = rwkv_pre(
                h, m_seg, seg, T, L, norm_mix[l], rk_mu[j], rk_wr[j], rk_wk[j], rk_wv[j], rk_w0[j], rk_w1[j],
                rk_w2[j], rk_a0[j], rk_a1[j], rk_a2[j], rk_g1[j], rk_g2[j], rk_kk[j], rk_ka[j], vres, v_first)
            if j == 0:
                v_first = v
            b3 = lambda a: a.reshape(B, T, D)
            yf, yb = rwkv_scan(b3(r), b3(v), b3(kk), [b3(lw0), b3(lw1)], [b3(k0), b3(k1)], [b3(ra0), b3(ra1)], L)
            h, f, logits = rwkv_post(yf.reshape(M, D), yb.reshape(M, D), r, k0, k1, v, gate, rk_ln_w[j], rk_ln_b[j],
                                     rk_rk[j], h, m_seg, seg, norm_ffn[l], rk_wo[j], router)
        moe_out = hier_moe(f, logits, moe_b_group[l], moe_b_expert[l], moe_w1[l].astype(BF16),
                           moe_w3[l].astype(BF16), moe_w2[l].astype(BF16))
        h = (h.reshape(B * nseg, seg, D) + mod(5) * moe_out.reshape(B * nseg, seg, D)).reshape(M, D)
    return h.reshape(B, T, D)[:, L:]
```

```python
import functools
import math

import jax
import jax.numpy as jnp
from jax import lax
from jax.experimental import pallas as pl
from jax.experimental.pallas import tpu as pltpu

F32 = jnp.float32
BF16 = jnp.bfloat16
HI = lax.Precision.HIGHEST

DEPTH = 4
GRID_W = 64
EPS = 1e-6

MLA_HEADS = 8
MLA_Q_LORA = 256
MLA_KV_LORA = 128
MLA_NOPE = 64
MLA_ROPE = 32
MLA_V = 64
MLA_QK = MLA_NOPE + MLA_ROPE
MLA_SCALE = MLA_QK ** -0.5
ROPE_BASE = 10000.0
MLA_PAD = 128

GDN_HEADS = 4
GDN_DK = 128
GDN_DV = 128
GDN_CONV = 5
GDN_CHUNK = 64

RWKV_HEAD = 64
RWKV_CHUNK = 64
GN_EPS = 64e-5

MOE_GROUPS = 4
MOE_PER_GROUP = 8
MOE_EXPERTS = MOE_GROUPS * MOE_PER_GROUP
MOE_TOPK = 2
MOE_BLOCK = 256

MLA_COLS = MLA_Q_LORA + MLA_KV_LORA + MLA_ROPE
GDN_QKV = GDN_HEADS * (2 * GDN_DK + GDN_DV)
GDN_Z = GDN_HEADS * GDN_DV
GDN_AB = 2 * 2 * GDN_HEADS

VMEM_LIMIT_BYTES = 48 * 1024 * 1024

GDN_PASSES = 1
RWKV_PASSES = 1
ATTN_BF16_EXP = False


def _cparams(*sem):
    return pltpu.CompilerParams(dimension_semantics=sem, vmem_limit_bytes=VMEM_LIMIT_BYTES)


def _pick(n, cands):
    for c in cands:
        if n % c == 0:
            return c
    return n


def _split(a):
    hi = a.astype(BF16)
    lo = (a - hi.astype(F32)).astype(BF16)
    return hi, lo


def _dg(a, b, dn, passes):
    if passes == 6:
        return lax.dot_general(a, b, dn, precision=HI, preferred_element_type=F32)
    if passes == 1:
        return lax.dot_general(a.astype(BF16), b.astype(BF16), dn, preferred_element_type=F32)
    ah, al = _split(a)
    bh, bl = _split(b)
    d = functools.partial(lax.dot_general, dimension_numbers=dn, preferred_element_type=F32)
    return d(ah, bh) + d(al, bh) + d(ah, bl)


_NN = (((1,), (0,)), ((), ()))
_NT = (((1,), (1,)), ((), ()))
_TN = (((0,), (0,)), ((), ()))
_BNN = (((2,), (1,)), ((0,), (0,)))
_BNT = (((2,), (2,)), ((0,), (0,)))
_BTN = (((1,), (1,)), ((0,), (0,)))


def _mm_body(x_ref, w_ref, o_ref, *, hi):
    if hi:
        o_ref[...] = jnp.dot(x_ref[...], w_ref[...], precision=HI, preferred_element_type=F32)
    else:
        o_ref[...] = jnp.dot(x_ref[...].astype(BF16), w_ref[...].astype(BF16),
                             preferred_element_type=F32)


def mm(x, w, hi=False):
    M, K = x.shape
    N = w.shape[1]
    tm = _pick(M, (512, 256, 128, 64, 32, 16, 8))
    tn = _pick(N, (512, 384, 256, 128))
    return pl.pallas_call(
        functools.partial(_mm_body, hi=hi),
        grid=(M // tm, N // tn),
        in_specs=[pl.BlockSpec((tm, K), lambda i, j: (i, 0)),
                  pl.BlockSpec((K, tn), lambda i, j: (0, j))],
        out_specs=pl.BlockSpec((tm, tn), lambda i, j: (i, j)),
        out_shape=jax.ShapeDtypeStruct((M, N), F32),
        compiler_params=_cparams("parallel", "parallel"),
        name="dense_mm",
    )(x, w)


def _attn_body(q_ref, k_ref, v_ref, o_ref, m_ref, acc_ref, *, c2, bf16_exp):
    ki = pl.program_id(3)

    @pl.when(ki == 0)
    def _():
        m_ref[...] = jnp.full(m_ref.shape, -1e30, F32)
        acc_ref[...] = jnp.zeros(acc_ref.shape, F32)

    for h in range(2):
        q = q_ref[0, :, h * MLA_PAD:(h + 1) * MLA_PAD]
        k = k_ref[0, :, h * MLA_PAD:(h + 1) * MLA_PAD]
        v = v_ref[0, :, h * MLA_PAD:(h + 1) * MLA_PAD]
        s = lax.dot_general(q, k, _NT, preferred_element_type=F32)
        m_prev = m_ref[h]
        m_new = jnp.maximum(m_prev, jnp.max(s, axis=-1, keepdims=True))
        alpha = jnp.exp2((m_prev - m_new) * c2)
        x = (s - m_new) * c2
        p = jnp.exp2(x.astype(BF16)) if bf16_exp else jnp.exp2(x).astype(BF16)
        acc_ref[h] = alpha * acc_ref[h] + jnp.dot(p, v, preferred_element_type=F32)
        m_ref[h] = m_new

    @pl.when(ki == pl.num_programs(3) - 1)
    def _():
        outs = []
        for h in range(2):
            a = acc_ref[h]
            outs.append(a[:, :MLA_V] / a[:, MLA_V:MLA_V + 1])
        o_ref[0] = jnp.concatenate(outs, axis=-1)


def attention(q, k, v):
    B, Sq, _ = q.shape
    Sk = k.shape[1]
    tq = _pick(Sq, (512, 256, 128))
    tk = _pick(Sk, (768, 512, 384, 256, 128))
    return pl.pallas_call(
        functools.partial(_attn_body, c2=MLA_SCALE * math.log2(math.e), bf16_exp=ATTN_BF16_EXP),
        grid=(B, MLA_HEADS // 2, Sq // tq, Sk // tk),
        in_specs=[pl.BlockSpec((1, tq, 2 * MLA_PAD), lambda b, p, i, j: (b, i, p)),
                  pl.BlockSpec((1, tk, 2 * MLA_PAD), lambda b, p, i, j: (b, j, p)),
                  pl.BlockSpec((1, tk, 2 * MLA_PAD), lambda b, p, i, j: (b, j, p))],
        out_specs=pl.BlockSpec((1, tq, 2 * MLA_V), lambda b, p, i, j: (b, i, p)),
        out_shape=jax.ShapeDtypeStruct((B, Sq, MLA_HEADS * MLA_V), F32),
        scratch_shapes=[pltpu.VMEM((2, tq, 1), F32), pltpu.VMEM((2, tq, MLA_PAD), F32)],
        compiler_params=_cparams("parallel", "parallel", "parallel", "arbitrary"),
        name="mla_attention",
    )(q, k, v)


def _tri_masks(C, rev):
    row = lax.broadcasted_iota(jnp.int32, (C, C), 0)
    col = lax.broadcasted_iota(jnp.int32, (C, C), 1)
    if rev:
        return row <= col, row < col
    return row >= col, row > col


def _neumann_inverse(nil, dn, passes):
    C = nil.shape[-1]
    eye = (lax.broadcasted_iota(jnp.int32, (C, C), 0) ==
           lax.broadcasted_iota(jnp.int32, (C, C), 1)).astype(F32)
    x = eye + nil
    p = nil
    for _ in range(int(math.log2(C)) - 1):
        p = _dg(p, p, dn, passes)
        x = x + _dg(x, p, dn, passes)
    return x


def _gdn_body(qf, kf, vf, gcf, bcf, grf, qb, kb, vb, gcb, bcb, grb, of_ref, ob_ref, s_ref, *, passes):
    C = GDN_CHUNK
    H = GDN_HEADS

    @pl.when(pl.program_id(1) == 0)
    def _():
        s_ref[...] = jnp.zeros(s_ref.shape, F32)

    dirs = ((qf, kf, vf, gcf, bcf, grf, of_ref), (qb, kb, vb, gcb, bcb, grb, ob_ref))
    for d, (q_ref, k_ref, v_ref, gc_ref, bc_ref, gr_ref, o_ref) in enumerate(dirs):
        rev = d == 1
        incl, strict = _tri_masks(C, rev)
        tri = incl.astype(F32)
        gcum_col = _dg(tri, gc_ref[0], _NN, 6)
        gcum_row = _dg(gr_ref[0, 0], tri, _NT, 6)
        beta_all = bc_ref[0]
        for h in range(H):
            idx = d * H + h
            gc = gcum_col[:, idx:idx + 1]
            gr = gcum_row[idx:idx + 1, :]
            beta = beta_all[:, idx:idx + 1]
            q = q_ref[0, :, h * GDN_DK:(h + 1) * GDN_DK]
            k = k_ref[0, :, h * GDN_DK:(h + 1) * GDN_DK]
            v = v_ref[0, :, h * GDN_DV:(h + 1) * GDN_DV]
            decay = jnp.exp(jnp.where(incl, gc - gr, -1e30))
            kbeta = k * beta
            lower = jnp.where(strict, _dg(kbeta, k, _NT, passes) * decay, 0.0)
            tinv = _neumann_inverse(-lower, _NN, passes)
            eg = jnp.exp(gc)
            rhs = jnp.concatenate([v * beta, kbeta * eg], axis=-1)
            sol = _dg(tinv, rhs, _NN, passes)
            u = sol[:, :GDN_DV]
            w = sol[:, GDN_DV:]
            aqk = jnp.where(incl, _dg(q, k, _NT, passes) * decay, 0.0)
            s = s_ref[d, h]
            v_new = u - _dg(w, s, _NN, passes)
            o = _dg(q * eg, s, _NN, passes) + _dg(aqk, v_new, _NN, passes)
            g_last = gc[0:1, :] if rev else gc[C - 1:C, :]
            s_ref[d, h] = s * jnp.exp(g_last) + _dg(k * jnp.exp(g_last - gc), v_new, _TN, passes)
            o_ref[0, :, h * GDN_DV:(h + 1) * GDN_DV] = o


def _rev_chunk(i, ncc, nc):
    return jnp.where(i < ncc, ncc - 1 - i, nc - 1 + ncc - i)


def gdn_scan(q, k, v, g, beta, n_ctx):
    B, T, _ = q.shape
    C = GDN_CHUNK
    nc = T // C
    ncc = n_ctx // C
    gcol = g.reshape(B, T, 2 * GDN_HEADS)
    bcol = beta.reshape(B, T, 2 * GDN_HEADS)
    grow = jnp.swapaxes(gcol.reshape(B, nc, C, 2 * GDN_HEADS), 2, 3)
    fwd = lambda b, i: (b, i, 0)
    bwd = lambda b, i: (b, _rev_chunk(i, ncc, nc), 0)
    fwd4 = lambda b, i: (b, i, 0, 0)
    bwd4 = lambda b, i: (b, _rev_chunk(i, ncc, nc), 0, 0)
    wide = q.shape[-1]
    wv = v.shape[-1]

    def specs(m3, m4):
        return [pl.BlockSpec((1, C, wide), m3), pl.BlockSpec((1, C, wide), m3), pl.BlockSpec((1, C, wv), m3),
                pl.BlockSpec((1, C, 2 * GDN_HEADS), m3), pl.BlockSpec((1, C, 2 * GDN_HEADS), m3),
                pl.BlockSpec((1, 1, 2 * GDN_HEADS, C), m4)]

    of, ob = pl.pallas_call(
        functools.partial(_gdn_body, passes=GDN_PASSES),
        grid=(B, nc),
        in_specs=specs(fwd, fwd4) + specs(bwd, bwd4),
        out_specs=[pl.BlockSpec((1, C, wv), fwd), pl.BlockSpec((1, C, wv), bwd)],
        out_shape=[jax.ShapeDtypeStruct((B, T, wv), F32)] * 2,
        scratch_shapes=[pltpu.VMEM((2, GDN_HEADS, GDN_DK, GDN_DV), F32)],
        compiler_params=_cparams("parallel", "arbitrary"),
        name="gdn_scan",
    )(q, k, v, gcol, bcol, grow, q, k, v, gcol, bcol, grow)
    return of, ob


def _rwkv_dir(r, lw, k, v, kk, rate, s, rev, passes):
    H, C, N = r.shape
    incl, strict = _tri_masks(C, rev)
    tri = jnp.broadcast_to(incl.astype(F32), (H, C, C))
    linc = _dg(tri, lw, _BNN, 6)
    lexc = linc - lw
    ltot = linc[:, 0:1, :] if rev else linc[:, C - 1:C, :]
    b = kk * rate
    at = -kk * jnp.exp(lexc)
    rt = r * jnp.exp(linc)
    einv = jnp.exp(-linc)
    bt = b * einv
    kt = k * einv
    etail = jnp.exp(ltot - linc)
    lab = jnp.where(strict, _dg(at, bt, _BNT, passes), 0.0)
    lak = jnp.where(strict, _dg(at, kt, _BNT, passes), 0.0)
    mrb = jnp.where(incl, _dg(rt, bt, _BNT, passes), 0.0)
    mrk = jnp.where(incl, _dg(rt, kt, _BNT, passes), 0.0)
    tinv = _neumann_inverse(lab, _BNN, passes)
    a_s = _dg(at, s, _BNT, passes)
    r_s = _dg(rt, s, _BNT, passes)
    u = _dg(tinv, a_s + _dg(lak, v, _BNN, passes), _BNN, passes)
    y = r_s + _dg(mrb, u, _BNN, passes) + _dg(mrk, v, _BNN, passes)
    s_new = (s * jnp.exp(ltot) + _dg(u, b * etail, _BTN, passes) + _dg(v, k * etail, _BTN, passes))
    return y, s_new


def _rwkv_body(rf, vf, kkf, lwf, kf, af, rb, vb, kkb, lwb, kb, ab, yf_ref, yb_ref, s_ref, *, passes):
    @pl.when(pl.program_id(1) == 0)
    def _():
        s_ref[...] = jnp.zeros(s_ref.shape, F32)

    N = RWKV_HEAD
    H = rf.shape[-1] // N

    def heads(ref):
        x = ref[0]
        return jnp.stack([x[:, h * N:(h + 1) * N] for h in range(H)], axis=0)

    dirs = ((rf, vf, kkf, lwf, kf, af, yf_ref), (rb, vb, kkb, lwb, kb, ab, yb_ref))
    for d, (r_ref, v_ref, kk_ref, lw_ref, k_ref, a_ref, y_ref) in enumerate(dirs):
        y, s_new = _rwkv_dir(heads(r_ref), heads(lw_ref), heads(k_ref), heads(v_ref), heads(kk_ref),
                             heads(a_ref), s_ref[d], d == 1, passes)
        s_ref[d] = s_new
        y_ref[0] = jnp.concatenate([y[h] for h in range(H)], axis=-1)


def rwkv_scan(r, v, kk, lw, key, rate, n_ctx):
    B, T, D = r.shape
    N = RWKV_HEAD
    C = RWKV_CHUNK
    nc = T // C
    ncc = n_ctx // C
    fwd = lambda b, i: (b, i, 0)
    bwd = lambda b, i: (b, _rev_chunk(i, ncc, nc), 0)
    blk = (1, C, D)
    return pl.pallas_call(
        functools.partial(_rwkv_body, passes=RWKV_PASSES),
        grid=(B, nc),
        in_specs=[pl.BlockSpec(blk, fwd)] * 6 + [pl.BlockSpec(blk, bwd)] * 6,
        out_specs=[pl.BlockSpec(blk, fwd), pl.BlockSpec(blk, bwd)],
        out_shape=[jax.ShapeDtypeStruct((B, T, D), F32)] * 2,
        scratch_shapes=[pltpu.VMEM((2, D // N, N, N), F32)],
        compiler_params=_cparams("parallel", "arbitrary"),
        name="rwkv7_scan",
    )(r, v, kk, lw[0], key[0], rate[0], r, v, kk, lw[1], key[1], rate[1])


def _moe_body(be_ref, x_ref, w1_ref, w3_ref, w2_ref, ws_ref, o_ref, w1b, w3b, w2b):
    i = pl.program_id(0)
    prev = be_ref[jnp.maximum(i - 1, 0)]

    @pl.when((i == 0) | (be_ref[i] != prev))
    def _():
        w1b[...] = w1_ref[0].astype(BF16)
        w3b[...] = w3_ref[0].astype(BF16)
        w2b[...] = w2_ref[0].astype(BF16)

    x = x_ref[...]
    h1 = jnp.dot(x, w1b[...], preferred_element_type=F32)
    h3 = jnp.dot(x, w3b[...], preferred_element_type=F32)
    hid = (h1 * jax.nn.sigmoid(h1)) * h3
    y = jnp.dot(hid.astype(BF16), w2b[...], preferred_element_type=F32)
    o_ref[...] = y * ws_ref[...]


def moe_experts(xs, slot_w, blk_e, w1, w3, w2):
    n_slots, D = xs.shape
    hid = w1.shape[-1]
    n_blocks = n_slots // MOE_BLOCK
    return pl.pallas_call(
        _moe_body,
        grid_spec=pltpu.PrefetchScalarGridSpec(
            num_scalar_prefetch=1,
            grid=(n_blocks,),
            in_specs=[pl.BlockSpec((MOE_BLOCK, D), lambda i, be: (i, 0)),
                      pl.BlockSpec((1, D, hid), lambda i, be: (be[i], 0, 0)),
                      pl.BlockSpec((1, D, hid), lambda i, be: (be[i], 0, 0)),
                      pl.BlockSpec((1, hid, D), lambda i, be: (be[i], 0, 0)),
                      pl.BlockSpec((MOE_BLOCK, 1), lambda i, be: (i, 0))],
            out_specs=pl.BlockSpec((MOE_BLOCK, D), lambda i, be: (i, 0)),
            scratch_shapes=[pltpu.VMEM((D, hid), BF16), pltpu.VMEM((D, hid), BF16), pltpu.VMEM((hid, D), BF16)],
        ),
        out_shape=jax.ShapeDtypeStruct((n_slots, D), F32),
        compiler_params=_cparams("arbitrary"),
        name="moe_experts",
    )(blk_e, xs, w1, w3, w2, slot_w)


def hier_moe(tokens, logits, b_group, b_expert, w1, w3, w2):
    N, D = tokens.shape
    logits = logits[:, :MOE_GROUPS + MOE_EXPERTS]
    pg = jax.nn.softmax(logits[:, :MOE_GROUPS] + b_group, axis=-1)
    g_idx = jnp.argmax(pg, axis=-1)[:, None].astype(jnp.int32)
    pg_top = jnp.max(pg, axis=-1, keepdims=True)
    le = logits[:, MOE_GROUPS:] + b_expert
    sel = g_idx * MOE_PER_GROUP + jnp.arange(MOE_PER_GROUP)[None, :]
    pe = jax.nn.softmax(jnp.take_along_axis(le, sel, axis=1), axis=-1)
    lane = jnp.arange(MOE_PER_GROUP, dtype=jnp.int32)[None, :]
    e1 = jnp.argmax(pe, axis=-1)[:, None].astype(jnp.int32)
    rest = jnp.where(lane == e1, -jnp.inf, pe)
    e2 = jnp.argmax(rest, axis=-1)[:, None].astype(jnp.int32)
    pe_top = jnp.concatenate([jnp.max(pe, -1, keepdims=True), jnp.max(rest, -1, keepdims=True)], axis=1)
    e_loc = jnp.concatenate([e1, e2], axis=1)
    wts = pg_top * pe_top / jnp.sum(pe_top, -1, keepdims=True)
    eid = (g_idx * MOE_PER_GROUP + e_loc).reshape(-1)
    tok = jnp.repeat(jnp.arange(N, dtype=jnp.int32), MOE_TOPK)
    A = N * MOE_TOPK
    onehot = (eid[:, None] == jnp.arange(MOE_EXPERTS)[None, :]).astype(jnp.int32)
    rank = jnp.take_along_axis(jnp.cumsum(onehot, 0), eid[:, None], 1)[:, 0] - 1
    counts = jnp.sum(onehot, 0)
    padded = (counts + MOE_BLOCK - 1) // MOE_BLOCK * MOE_BLOCK
    pend = jnp.cumsum(padded)
    dest = (pend - padded)[eid] + rank
    n_blocks = -(-A // MOE_BLOCK) + MOE_EXPERTS
    n_slots = n_blocks * MOE_BLOCK
    slot_src = jnp.full((n_slots,), -1, jnp.int32).at[dest].set(jnp.arange(A, dtype=jnp.int32))
    src = jnp.maximum(slot_src, 0)
    slot_tok = tok[src] * (slot_src >= 0)
    slot_w = jnp.where(slot_src >= 0, wts.reshape(-1)[src], 0.0)
    starts = jnp.arange(n_blocks, dtype=jnp.int32)[:, None] * MOE_BLOCK
    blk_e = jnp.minimum(jnp.sum((pend[None, :] <= starts).astype(jnp.int32), axis=1), MOE_EXPERTS - 1)
    xs = tokens[slot_tok]
    ys = moe_experts(xs, slot_w[:, None], blk_e, w1, w3, w2)
    d2 = dest.reshape(N, MOE_TOPK)
    return ys[d2[:, 0]] + ys[d2[:, 1]]


def _rope_tables(n_lat, n_ctx):
    rows = n_lat // GRID_W
    row = jnp.repeat(jnp.arange(rows, dtype=F32), GRID_W)
    col = jnp.tile(jnp.arange(GRID_W, dtype=F32), rows)
    n_freq = MLA_ROPE // 4
    inv = ROPE_BASE ** (-jnp.arange(n_freq, dtype=F32) / n_freq)
    ang = jnp.stack([row[:, None] * inv, col[:, None] * inv], axis=1)
    cos, sin = jnp.cos(ang), jnp.sin(ang)
    zf = jnp.zeros((n_lat, n_freq), F32)
    lat = lambda parts, fill: jnp.concatenate(
        [jnp.full((n_lat, MLA_NOPE), fill, F32)] + parts + [jnp.full((n_lat, MLA_PAD - MLA_QK), fill, F32)], axis=1)
    c = lat([cos[:, 0], cos[:, 0], cos[:, 1], cos[:, 1]], 1.0)
    s_lo = lat([-sin[:, 0], zf, -sin[:, 1], zf], 0.0)
    s_hi = lat([zf, sin[:, 0], zf, sin[:, 1]], 0.0)
    ctx = lambda fill: jnp.full((n_ctx, MLA_PAD), fill, F32)
    return (jnp.concatenate([ctx(1.0), c], 0), jnp.concatenate([ctx(0.0), s_lo], 0),
            jnp.concatenate([ctx(0.0), s_hi], 0))


def _const_spec(shape):
    return pl.BlockSpec(shape, lambda i: (0,) * len(shape), pipeline_mode=pl.Buffered(1))


def _normmod(x, gain, shift, scale):
    return x * lax.rsqrt(jnp.mean(x * x, -1, keepdims=True) + EPS) * gain * (1 + scale) + shift


def _softplus(x):
    return jnp.maximum(x, 0.0) + jnp.log1p(jnp.exp(-jnp.abs(x)))


def _head_indicator(D, N):
    e = (jnp.arange(D)[:, None] // N == jnp.arange(128)[None, :]).astype(BF16)
    return e, e.T


def _seg_dot(x, e):
    xh, xl = _split(x)
    return jnp.dot(xh, e, preferred_element_type=F32) + jnp.dot(xl, e, preferred_element_type=F32)


def _dotf(a, b):
    return jnp.dot(a, b, preferred_element_type=F32)


def _rwkv_pre_body(*refs, tm, blocks_per_batch, ctx_blocks, vres):
    (h_ref, hp_ref, hn_ref, m_ref, gain_ref, mu_ref, w0_ref, a0_ref, kk_ref, ka_ref, e_ref, et_ref,
     wr_ref, wk_ref, wv_ref, w1_ref, w2_ref, a1_ref, a2_ref, g1_ref, g2_ref) = refs[:21]
    rest = refs[21:]
    if vres:
        v0_ref, v1_ref, v2_ref, vf_ref = rest[:4]
        rest = rest[4:]
    r_o, v_o, kk_o, lw0_o, lw1_o, k0_o, k1_o, ra0_o, ra1_o, gate_o = rest

    tb = pl.program_id(0) % blocks_per_batch
    seg_start = (tb == 0) | (tb == ctx_blocks)
    seg_end = (tb == ctx_blocks - 1) | (tb == blocks_per_batch - 1)
    shift, scale, gain = m_ref[0, 0:1, :], m_ref[0, 1:2, :], gain_ref[...]
    u = _normmod(h_ref[...], gain, shift, scale)
    up = jnp.where(seg_start, 0.0, _normmod(hp_ref[7:8, :], gain, shift, scale))
    un = jnp.where(seg_end, 0.0, _normmod(hn_ref[0:1, :], gain, shift, scale))
    row = lax.broadcasted_iota(jnp.int32, (tm, 1), 0)
    u_prev = jnp.where(row == 0, up, pltpu.roll(u, 1, 0))
    u_next = jnp.where(row == tm - 1, un, pltpu.roll(u, tm - 1, 0))
    xx = 0.5 * (u_prev + u_next) - u
    xr, xw, xk, xv, xa, xg = [(u + xx * mu_ref[j:j + 1, :]).astype(BF16) for j in range(6)]

    r = _dotf(xr, wr_ref[...])
    k = _dotf(xk, wk_ref[...])
    v = _dotf(xv, wv_ref[...])
    if vres:
        lo = _dotf(xv, v1_ref[...]).astype(BF16)
        v = v + (vf_ref[...] - v) * jax.nn.sigmoid(v0_ref[...] + _dotf(lo, v2_ref[...]))
    tl = jnp.tanh(_dotf(xw, w1_ref[...])).astype(BF16)
    al = _dotf(xa, a1_ref[...]).astype(BF16)
    gl = jax.nn.sigmoid(_dotf(xg, g1_ref[...])).astype(BF16)
    gate_o[...] = _dotf(gl, g2_ref[...])
    kx = k * kk_ref[...]
    inv = lax.rsqrt(_seg_dot(kx * kx, e_ref[...]) + EPS)
    r_o[...] = r
    v_o[...] = v
    kk_o[...] = kx * _seg_dot(inv, et_ref[...])
    for d, (lw_o, k_o, ra_o) in enumerate(((lw0_o, k0_o, ra0_o), (lw1_o, k1_o, ra1_o))):
        w_log = -_softplus(-(w0_ref[d:d + 1, :] + _dotf(tl, w2_ref[d]))) - 0.5
        lw_o[...] = -jnp.exp(w_log)
        a = jax.nn.sigmoid(a0_ref[d:d + 1, :] + _dotf(al, a2_ref[d]))
        ra_o[...] = a
        k_o[...] = k * (1 + (a - 1) * ka_ref[...])


def _row_tile(seg):
    return _pick(seg, (256, 128, 64, 32, 16, 8))


def _pad_cols(w, n):
    return jnp.pad(w, ((0, 0), (0, n - w.shape[1])))


def _pad_rows(w, n):
    return jnp.pad(w, ((0, n - w.shape[0]), (0, 0)))


def rwkv_pre(h, m_seg, seg, T, n_ctx, gain, mu, wr, wk, wv, w0, w1, w2, a0, a1, a2, g1, g2, k_k, k_a, vres, v_first):
    M, D = h.shape
    tm = _row_tile(seg)
    lora = w1.shape[-1]
    e, et = _head_indicator(D, RWKV_HEAD)
    zero = jnp.zeros((lora, D), F32)
    w2p = jnp.stack([jnp.concatenate([w2[0], zero], 0), jnp.concatenate([zero, w2[1]], 0)]).astype(BF16)
    a2p = jnp.stack([jnp.concatenate([a2[0], zero], 0), jnp.concatenate([zero, a2[1]], 0)]).astype(BF16)
    gp = -(-g1.shape[1] // 128) * 128
    row = lambda a: a.reshape(1, D)
    consts = [row(gain), mu, w0, a0, row(k_k), row(k_a), e, et,
              wr.astype(BF16), wk.astype(BF16), wv.astype(BF16),
              jnp.concatenate([w1[0], w1[1]], 1).astype(BF16), w2p,
              jnp.concatenate([a1[0], a1[1]], 1).astype(BF16), a2p,
              _pad_cols(g1, gp).astype(BF16), _pad_rows(g2, gp).astype(BF16)]
    row_spec = pl.BlockSpec((tm, D), lambda i: (i, 0))
    last8 = M // 8 - 1
    in_specs = [row_spec,
                pl.BlockSpec((8, D), lambda i: (jnp.maximum(i * (tm // 8) - 1, 0), 0)),
                pl.BlockSpec((8, D), lambda i: (jnp.minimum((i + 1) * (tm // 8), last8), 0)),
                pl.BlockSpec((1, 6, D), lambda i: (i * tm // seg, 0, 0))]
    in_specs += [_const_spec(c.shape) for c in consts]
    args = [h, h, h, m_seg] + consts
    if vres is not None:
        v0, v1, v2 = vres
        extra = [row(v0), _pad_cols(v1, 128).astype(BF16), _pad_rows(v2, 128).astype(BF16)]
        in_specs += [_const_spec(c.shape) for c in extra] + [row_spec]
        args += extra + [v_first]
    return pl.pallas_call(
        functools.partial(_rwkv_pre_body, tm=tm, blocks_per_batch=T // tm, ctx_blocks=n_ctx // tm,
                          vres=vres is not None),
        grid=(M // tm,),
        in_specs=in_specs,
        out_specs=[row_spec] * 10,
        out_shape=[jax.ShapeDtypeStruct((M, D), F32)] * 10,
        compiler_params=_cparams("parallel"),
        name="rwkv7_pre",
    )(*args)


def _post_tail(xo, h_ref, m_ref, gain_ref, w_ref, wrt_ref, h_o, f_o, lg_o):
    h_new = h_ref[...] + m_ref[0, 2:3, :] * _dotf(xo, w_ref[...])
    h_o[...] = h_new
    f = _normmod(h_new, gain_ref[...], m_ref[0, 3:4, :], m_ref[0, 4:5, :])
    f_o[...] = f.astype(BF16)
    lg_o[...] = jnp.dot(f, wrt_ref[...], precision=HI, preferred_element_type=F32)


def _rwkv_post_body(yf_ref, yb_ref, r_ref, k0_ref, k1_ref, v_ref, gate_ref, lnw_ref, lnb_ref, rk_ref, e_ref, et_ref,
                    h_ref, m_ref, gain_ref, w_ref, wrt_ref, h_o, f_o, lg_o):
    e, et = e_ref[...], et_ref[...]
    inv_n = 1.0 / RWKV_HEAD
    y = yf_ref[...] + yb_ref[...]
    yc = y - _seg_dot(_seg_dot(y, e) * inv_n, et)
    var = _seg_dot(_seg_dot(yc * yc, e) * inv_n, et)
    yn = yc * lax.rsqrt(var + GN_EPS) * lnw_ref[...] + lnb_ref[...]
    k_bonus = 0.5 * (k0_ref[...] + k1_ref[...])
    bonus = _seg_dot(_seg_dot(r_ref[...] * k_bonus * rk_ref[...], e), et) * v_ref[...]
    xo = ((yn + bonus) * gate_ref[...]).astype(BF16)
    _post_tail(xo, h_ref, m_ref, gain_ref, w_ref, wrt_ref, h_o, f_o, lg_o)


def _post_call(body, name, row_args, consts, h, m_seg, seg, gain, w_out, w_router):
    M, D = h.shape
    tm = _row_tile(seg)
    row_spec = lambda a: pl.BlockSpec((tm, a.shape[1]), lambda i: (i, 0))
    tail = [gain.reshape(1, D), w_out.astype(BF16), w_router]
    in_specs = ([row_spec(a) for a in row_args] + [_const_spec(c.shape) for c in consts] +
                [row_spec(h), pl.BlockSpec((1, 6, D), lambda i: (i * tm // seg, 0, 0))] +
                [_const_spec(c.shape) for c in tail])
    nr = w_router.shape[1]
    return pl.pallas_call(
        body,
        grid=(M // tm,),
        in_specs=in_specs,
        out_specs=[pl.BlockSpec((tm, D), lambda i: (i, 0)), pl.BlockSpec((tm, D), lambda i: (i, 0)),
                   pl.BlockSpec((tm, nr), lambda i: (i, 0))],
        out_shape=[jax.ShapeDtypeStruct((M, D), F32), jax.ShapeDtypeStruct((M, D), BF16),
                   jax.ShapeDtypeStruct((M, nr), F32)],
        compiler_params=_cparams("parallel"),
        name=name,
    )(*row_args, *consts, h, m_seg, *tail)


def _hy_post_body(a_ref, of_ref, ob_ref, z_ref, og_ref, h_ref, m_ref, gain_ref, w_ref, wrt_ref, h_o, f_o, lg_o):
    o = of_ref[...] + ob_ref[...]
    z = z_ref[...]
    parts = [a_ref[...]]
    for hd in range(GDN_HEADS):
        sl = slice(hd * GDN_DV, (hd + 1) * GDN_DV)
        oh, zh = o[:, sl], z[:, sl]
        on = oh * lax.rsqrt(jnp.mean(oh * oh, -1, keepdims=True) + EPS) * og_ref[...]
        parts.append(on * (zh * jax.nn.sigmoid(zh)))
    xo = jnp.concatenate(parts, axis=-1).astype(BF16)
    _post_tail(xo, h_ref, m_ref, gain_ref, w_ref, wrt_ref, h_o, f_o, lg_o)


def hy_post(a, of, ob, z, out_g, h, m_seg, seg, gain, w_out, w_router):
    return _post_call(_hy_post_body, "hybrid_post", [a, of, ob, z], [out_g.reshape(1, -1)], h, m_seg, seg,
                      gain, w_out, w_router)


def _rope_lanes(x, c, s_lo, s_hi):
    return x * c + pltpu.roll(x, MLA_PAD - 8, 1) * s_lo + pltpu.roll(x, 8, 1) * s_hi


def _hy_pre_body(h_ref, m_ref, gain_ref, c_ref, slo_ref, shi_ref, wq1, wkv1, wpe, wgq, wz, wab, qag, kvag,
                 wqb, wkn, wv, qng, kng, q_o, k_o, v_o, gq_o, z_o, ab_o):
    u = _normmod(h_ref[...], gain_ref[...], m_ref[0, 0:1, :], m_ref[0, 1:2, :]).astype(BF16)
    gq_o[...] = _dotf(u, wgq[...])
    z_o[...] = _dotf(u, wz[...])
    ab_o[...] = _dotf(u, wab[...])
    cq = _dotf(u, wq1[...])
    ckv = _dotf(u, wkv1[...])
    pe = _dotf(u, wpe[...])
    cq = (cq * lax.rsqrt(jnp.mean(cq * cq, -1, keepdims=True) + EPS) * qag[...]).astype(BF16)
    ckv = (ckv * lax.rsqrt(jnp.mean(ckv * ckv, -1, keepdims=True) + EPS) * kvag[...]).astype(BF16)
    q = _dotf(cq, wqb[...])
    kn = _dotf(ckv, wkn[...])
    vv = _dotf(ckv, wv[...])
    c, s_lo, s_hi = c_ref[...], slo_ref[...], shi_ref[...]
    one_col = (lax.broadcasted_iota(jnp.int32, (1, MLA_PAD), 1) == MLA_V).astype(F32)
    inv_d = 1.0 / MLA_QK

    def head_norm(t, g):
        return t * lax.rsqrt(jnp.sum(t * t, -1, keepdims=True) * inv_d + EPS) * g

    for hd in range(MLA_HEADS):
        sl = slice(hd * MLA_PAD, (hd + 1) * MLA_PAD)
        q_o[:, sl] = _rope_lanes(head_norm(q[:, sl], qng[...]), c, s_lo, s_hi).astype(BF16)
        k_o[:, sl] = _rope_lanes(head_norm(kn[:, sl] + pe, kng[...]), c, s_lo, s_hi).astype(BF16)
        v_o[:, sl] = (vv[:, sl] + one_col).astype(BF16)


def _pad_heads(w, heads, width, to):
    K = w.shape[0]
    return jnp.pad(w.reshape(K, heads, width), ((0, 0), (0, 0), (0, to - width))).reshape(K, heads * to)


def hy_pre(h, m_seg, seg, T, gain, rope, w_in, qa_g, w_qb, kva_g, w_kvb, qn_g, kn_g):
    M, D = h.shape
    tm = _row_tile(seg)
    H = MLA_HEADS
    c0, c1, c2 = MLA_COLS, MLA_COLS + GDN_QKV, MLA_COLS + GDN_QKV + GDN_Z
    kvl = MLA_Q_LORA + MLA_KV_LORA
    wb = w_in.astype(BF16)
    wpe = jnp.pad(wb[:, kvl:c0], ((0, 0), (MLA_NOPE, MLA_PAD - MLA_QK)))
    wkv = w_kvb.reshape(MLA_KV_LORA, H, MLA_NOPE + MLA_V)
    pad1 = lambda g: jnp.pad(g, (0, MLA_PAD - MLA_QK)).reshape(1, MLA_PAD)
    consts = [wb[:, :MLA_Q_LORA], wb[:, MLA_Q_LORA:kvl], wpe, wb[:, c0:c1], wb[:, c1:c2],
              _pad_cols(wb[:, c2:], 128), qa_g.reshape(1, -1), kva_g.reshape(1, -1),
              _pad_heads(w_qb, H, MLA_QK, MLA_PAD).astype(BF16),
              _pad_heads(wkv[:, :, :MLA_NOPE].reshape(MLA_KV_LORA, -1), H, MLA_NOPE, MLA_PAD).astype(BF16),
              _pad_heads(wkv[:, :, MLA_NOPE:].reshape(MLA_KV_LORA, -1), H, MLA_V, MLA_PAD).astype(BF16),
              pad1(qn_g), pad1(kn_g)]
    bpb = T // tm
    row = lambda n: pl.BlockSpec((tm, n), lambda i: (i, 0))
    tab = pl.BlockSpec((tm, MLA_PAD), lambda i: (i % bpb, 0))
    in_specs = ([row(D), pl.BlockSpec((1, 6, D), lambda i: (i * tm // seg, 0, 0)), _const_spec((1, D)), tab, tab, tab]
                + [_const_spec(c.shape) for c in consts])
    wide = H * MLA_PAD
    return pl.pallas_call(
        _hy_pre_body,
        grid=(M // tm,),
        in_specs=in_specs,
        out_specs=[row(wide), row(wide), row(wide), row(GDN_QKV), row(GDN_Z), row(128)],
        out_shape=[jax.ShapeDtypeStruct((M, wide), BF16)] * 3 + [jax.ShapeDtypeStruct((M, GDN_QKV), F32),
                   jax.ShapeDtypeStruct((M, GDN_Z), F32), jax.ShapeDtypeStruct((M, 128), F32)],
        compiler_params=_cparams("parallel"),
        name="hybrid_pre",
    )(h, m_seg, gain.reshape(1, D), *rope, *consts)


def _gdn_prep_body(x_ref, xp_ref, xn_ref, w_ref, q_o, k_o, v_o, *, tm, blocks_per_batch, ctx_blocks):
    tb = pl.program_id(0) % blocks_per_batch
    seg_start = (tb == 0) | (tb == ctx_blocks)
    seg_end = (tb == ctx_blocks - 1) | (tb == blocks_per_batch - 1)
    x = x_ref[...]
    xp = jnp.where(seg_start, 0.0, xp_ref[...])
    xn = jnp.where(seg_end, 0.0, xn_ref[...])
    row = lax.broadcasted_iota(jnp.int32, (tm, 1), 0)
    half = GDN_CONV // 2
    acc = x * w_ref[half:half + 1, :]
    for s in range(1, half + 1):
        before = pltpu.roll(x, s, 0)
        after = pltpu.roll(x, tm - s, 0)
        for r in range(s):
            before = jnp.where(row == r, xp[8 - s + r:8 - s + r + 1, :], before)
            after = jnp.where(row == tm - s + r, xn[r:r + 1, :], after)
        acc = acc + before * w_ref[half - s:half - s + 1, :] + after * w_ref[half + s:half + s + 1, :]
    y = acc * jax.nn.sigmoid(acc)
    nk = GDN_HEADS * GDN_DK
    for hd in range(GDN_HEADS):
        sl = slice(hd * GDN_DK, (hd + 1) * GDN_DK)
        qh = y[:, sl]
        kh = y[:, nk + hd * GDN_DK:nk + (hd + 1) * GDN_DK]
        q_o[:, sl] = qh * lax.rsqrt(jnp.sum(qh * qh, -1, keepdims=True) + EPS) * GDN_DK ** -0.5
        k_o[:, sl] = kh * lax.rsqrt(jnp.sum(kh * kh, -1, keepdims=True) + EPS)
    v_o[...] = y[:, 2 * nk:]


def gdn_prep(gq, conv_w, seg, T, n_ctx):
    M, W = gq.shape
    tm = _row_tile(seg)
    last8 = M // 8 - 1
    nk = GDN_HEADS * GDN_DK
    row = lambda n: pl.BlockSpec((tm, n), lambda i: (i, 0))
    return pl.pallas_call(
        functools.partial(_gdn_prep_body, tm=tm, blocks_per_batch=T // tm, ctx_blocks=n_ctx // tm),
        grid=(M // tm,),
        in_specs=[row(W),
                  pl.BlockSpec((8, W), lambda i: (jnp.maximum(i * (tm // 8) - 1, 0), 0)),
                  pl.BlockSpec((8, W), lambda i: (jnp.minimum((i + 1) * (tm // 8), last8), 0)),
                  _const_spec(conv_w.shape)],
        out_specs=[row(nk), row(nk), row(W - 2 * nk)],
        out_shape=[jax.ShapeDtypeStruct((M, nk), F32), jax.ShapeDtypeStruct((M, nk), F32),
                   jax.ShapeDtypeStruct((M, W - 2 * nk), F32)],
        compiler_params=_cparams("parallel"),
        name="gdn_prep",
    )(gq, gq, gq, conv_w)


def rwkv_post(yf, yb, r, k0, k1, v, gate, ln_w, ln_b, r_k, h, m_seg, seg, gain, wo, w_router):
    D = h.shape[1]
    e, et = _head_indicator(D, RWKV_HEAD)
    consts = [ln_w.reshape(1, D), ln_b.reshape(1, D), r_k.reshape(1, D), e, et]
    return _post_call(_rwkv_post_body, "rwkv7_post", [yf, yb, r, k0, k1, v, gate], consts, h, m_seg, seg,
                      gain, wo, w_router)


def kernel(x, c, ctx, c_ctx, ada_w, ada_b, norm_mix, norm_ffn, hy_w_in, hy_w_out, mla_qa_norm, mla_w_qb, mla_kva_norm, mla_w_kvb, mla_q_norm, mla_k_norm, gdn_conv, gdn_a_log, gdn_dt_bias, gdn_out_norm, rk_mu, rk_wr, rk_wk, rk_wv, rk_wo, rk_w0, rk_w1, rk_w2, rk_a0, rk_a1, rk_a2, rk_g1, rk_g2, rk_kk, rk_ka, rk_rk, rk_ln_w, rk_ln_b, rk_v0, rk_v1, rk_v2, moe_w_group, moe_b_group, moe_w_expert, moe_b_expert, moe_w1, moe_w3, moe_w2):
    B, S, D = x.shape
    L = ctx.shape[1]
    T = L + S
    depth = ada_w.shape[0]
    rope = _rope_tables(S, L)
    n_rows = -(-(B + 1) // 8) * 8
    sc = jnp.concatenate([jax.nn.silu(c), jax.nn.silu(c_ctx)[None], jnp.zeros((n_rows - B - 1, D), F32)], 0)
    M = B * T
    h = jnp.concatenate([ctx, x], axis=1).reshape(M, D)
    seg = math.gcd(L, S)
    nseg = T // seg
    v_first = None
    for l in range(depth):
        m = mm(sc, ada_w[l], hi=True) + ada_b[l]
        m_lat = jnp.broadcast_to(m[:B].reshape(B, 1, 6, D), (B, S // seg, 6, D))
        m_ctx = jnp.broadcast_to(m[B].reshape(1, 1, 6, D), (B, L // seg, 6, D))
        m_seg = jnp.concatenate([m_ctx, m_lat], axis=1).reshape(B * nseg, 6, D)

        def mod(i, m_seg=m_seg):
            return m_seg[:, None, i, :]

        router = _pad_cols(jnp.concatenate([moe_w_group[l], moe_w_expert[l]], axis=1), 128)
        j = l // 2
        b3 = lambda a: a.reshape(B, T, a.shape[-1])
        if l % 2 == 0:
            q, k, v, gq, z, ab = hy_pre(h, m_seg, seg, T, norm_mix[l], rope, hy_w_in[j], mla_qa_norm[j],
                                        mla_w_qb[j], mla_kva_norm[j], mla_w_kvb[j], mla_q_norm[j], mla_k_norm[j])
            q, k, v = b3(q), b3(k), b3(v)
            a_lat = attention(q[:, L:], k, v)
            a_ctx = attention(q[:, :L], k[:, :L], v[:, :L])
            a = jnp.concatenate([a_ctx, a_lat], axis=1).reshape(M, -1)
            gq_, gk_, gv_ = gdn_prep(gq, gdn_conv[j], seg, T, L)
            ab = ab[:, :GDN_AB].reshape(B, T, 2, 2, GDN_HEADS)
            g = -jnp.exp(gdn_a_log[j]) * jax.nn.softplus(ab[:, :, :, 0] + gdn_dt_bias[j])
            beta = jax.nn.sigmoid(ab[:, :, :, 1])
            of, ob = gdn_scan(b3(gq_), b3(gk_), b3(gv_), g, beta, L)
            h, f, logits = hy_post(a, of.reshape(M, -1), ob.reshape(M, -1), z, gdn_out_norm[j], h, m_seg, seg,
                                   norm_ffn[l], hy_w_out[j], router)
        else:
            vres = None if j == 0 else (rk_v0[j - 1], rk_v1[j - 1], rk_v2[j - 1])
            r, v, kk, lw0, lw1, k0, k1, ra0, ra1, gate = rwkv_pre(
                h, m_seg, seg, T, L, norm_mix[l], rk_mu[j], rk_wr[j], rk_wk[j], rk_wv[j], rk_w0[j], rk_w1[j],
                rk_w2[j], rk_a0[j], rk_a1[j], rk_a2[j], rk_g1[j], rk_g2[j], rk_kk[j], rk_ka[j], vres, v_first)
            if j == 0:
                v_first = v
            b3 = lambda a: a.reshape(B, T, D)
            yf, yb = rwkv_scan(b3(r), b3(v), b3(kk), [b3(lw0), b3(lw1)], [b3(k0), b3(k1)], [b3(ra0), b3(ra1)], L)
            h, f, logits = rwkv_post(yf.reshape(M, D), yb.reshape(M, D), r, k0, k1, v, gate, rk_ln_w[j], rk_ln_b[j],
                                     rk_rk[j], h, m_seg, seg, norm_ffn[l], rk_wo[j], router)
        moe_out = hier_moe(f, logits, moe_b_group[l], moe_b_expert[l], moe_w1[l], moe_w3[l], moe_w2[l])
        h = (h.reshape(B * nseg, seg, D) + mod(5) * moe_out.reshape(B * nseg, seg, D)).reshape(M, D)
    return h.reshape(B, T, D)[:, L:]
```

```python
import functools
import math

import jax
import jax.numpy as jnp
from jax import lax
from jax.experimental import pallas as pl
from jax.experimental.pallas import tpu as pltpu

F32 = jnp.float32
BF16 = jnp.bfloat16
HI = lax.Precision.HIGHEST

DEPTH = 4
GRID_W = 64
EPS = 1e-6

MLA_HEADS = 8
MLA_Q_LORA = 256
MLA_KV_LORA = 128
MLA_NOPE = 64
MLA_ROPE = 32
MLA_V = 64
MLA_QK = MLA_NOPE + MLA_ROPE
MLA_SCALE = MLA_QK ** -0.5
ROPE_BASE = 10000.0
MLA_PAD = 128

GDN_HEADS = 4
GDN_DK = 128
GDN_DV = 128
GDN_CONV = 5
GDN_CHUNK = 64

RWKV_HEAD = 64
RWKV_CHUNK = 64
GN_EPS = 64e-5

MOE_GROUPS = 4
MOE_PER_GROUP = 8
MOE_EXPERTS = MOE_GROUPS * MOE_PER_GROUP
MOE_TOPK = 2
MOE_BLOCK = 256

MLA_COLS = MLA_Q_LORA + MLA_KV_LORA + MLA_ROPE
GDN_QKV = GDN_HEADS * (2 * GDN_DK + GDN_DV)
GDN_Z = GDN_HEADS * GDN_DV
GDN_AB = 2 * 2 * GDN_HEADS

VMEM_LIMIT_BYTES = 48 * 1024 * 1024

GDN_PASSES = 1
RWKV_PASSES = 1


def _cparams(*sem):
    return pltpu.CompilerParams(dimension_semantics=sem, vmem_limit_bytes=VMEM_LIMIT_BYTES)


def _pick(n, cands):
    for c in cands:
        if n % c == 0:
            return c
    return n


def _split(a):
    hi = a.astype(BF16)
    lo = (a - hi.astype(F32)).astype(BF16)
    return hi, lo


def _dg(a, b, dn, passes):
    if passes == 6:
        return lax.dot_general(a, b, dn, precision=HI, preferred_element_type=F32)
    if passes == 1:
        return lax.dot_general(a.astype(BF16), b.astype(BF16), dn, preferred_element_type=F32)
    ah, al = _split(a)
    bh, bl = _split(b)
    d = functools.partial(lax.dot_general, dimension_numbers=dn, preferred_element_type=F32)
    return d(ah, bh) + d(al, bh) + d(ah, bl)


_NN = (((1,), (0,)), ((), ()))
_NT = (((1,), (1,)), ((), ()))
_TN = (((0,), (0,)), ((), ()))
_BNN = (((2,), (1,)), ((0,), (0,)))
_BNT = (((2,), (2,)), ((0,), (0,)))
_BTN = (((1,), (1,)), ((0,), (0,)))


def _mm_body(x_ref, w_ref, o_ref, *, hi):
    if hi:
        o_ref[...] = jnp.dot(x_ref[...], w_ref[...], precision=HI, preferred_element_type=F32)
    else:
        o_ref[...] = jnp.dot(x_ref[...].astype(BF16), w_ref[...].astype(BF16),
                             preferred_element_type=F32)


def mm(x, w, hi=False):
    M, K = x.shape
    N = w.shape[1]
    tm = _pick(M, (512, 256, 128, 64, 32, 16, 8))
    tn = _pick(N, (512, 384, 256, 128))
    return pl.pallas_call(
        functools.partial(_mm_body, hi=hi),
        grid=(M // tm, N // tn),
        in_specs=[pl.BlockSpec((tm, K), lambda i, j: (i, 0)),
                  pl.BlockSpec((K, tn), lambda i, j: (0, j))],
        out_specs=pl.BlockSpec((tm, tn), lambda i, j: (i, j)),
        out_shape=jax.ShapeDtypeStruct((M, N), F32),
        compiler_params=_cparams("parallel", "parallel"),
        name="dense_mm",
    )(x, w)


def _attn_body(q_ref, k_ref, v_ref, o_ref, m_ref, acc_ref, *, c2):
    ki = pl.program_id(3)

    @pl.when(ki == 0)
    def _():
        m_ref[...] = jnp.full(m_ref.shape, -1e30, F32)
        acc_ref[...] = jnp.zeros(acc_ref.shape, F32)

    heads = range(2)
    sl = [slice(h * MLA_PAD, (h + 1) * MLA_PAD) for h in heads]
    m_prev = [m_ref[h] for h in heads]
    acc_prev = [acc_ref[h] for h in heads]
    s = [lax.dot_general(q_ref[0, :, sl[h]], k_ref[0, :, sl[h]], _NT, preferred_element_type=F32) for h in heads]
    m_new, alpha, p = [], [], []
    reps = s[0].shape[1] // MLA_PAD
    for h in heads:
        m_new.append(jnp.maximum(m_prev[h], jnp.max(s[h], axis=-1, keepdims=True)))
        alpha.append(jnp.exp2((m_prev[h] - m_new[h]) * c2))
        x = (s[h] - jnp.tile(m_new[h], (1, reps))) * c2
        p.append(jnp.exp2(x).astype(BF16))
    pv = [jnp.dot(p[h], v_ref[0, :, sl[h]], preferred_element_type=F32) for h in heads]
    for h in heads:
        acc_ref[h] = alpha[h] * acc_prev[h] + pv[h]
        m_ref[h] = m_new[h]

    @pl.when(ki == pl.num_programs(3) - 1)
    def _():
        outs = []
        for h in range(2):
            a = acc_ref[h]
            outs.append(a[:, :MLA_V] / a[:, MLA_V:MLA_V + 1])
        o_ref[0] = jnp.concatenate(outs, axis=-1)


def attention(q, k, v):
    B, Sq, _ = q.shape
    Sk = k.shape[1]
    tq = _pick(Sq, (1024, 512, 256, 128))
    tk = _pick(Sk, (768, 512, 384, 256, 128))
    return pl.pallas_call(
        functools.partial(_attn_body, c2=MLA_SCALE * math.log2(math.e)),
        grid=(B, MLA_HEADS // 2, Sq // tq, Sk // tk),
        in_specs=[pl.BlockSpec((1, tq, 2 * MLA_PAD), lambda b, p, i, j: (b, i, p)),
                  pl.BlockSpec((1, tk, 2 * MLA_PAD), lambda b, p, i, j: (b, j, p)),
                  pl.BlockSpec((1, tk, 2 * MLA_PAD), lambda b, p, i, j: (b, j, p))],
        out_specs=pl.BlockSpec((1, tq, 2 * MLA_V), lambda b, p, i, j: (b, i, p)),
        out_shape=jax.ShapeDtypeStruct((B, Sq, MLA_HEADS * MLA_V), F32),
        scratch_shapes=[pltpu.VMEM((2, tq, MLA_PAD), F32), pltpu.VMEM((2, tq, MLA_PAD), F32)],
        compiler_params=_cparams("parallel", "parallel", "parallel", "arbitrary"),
        name="mla_attention",
    )(q, k, v)


def _tri_masks(C, rev):
    row = lax.broadcasted_iota(jnp.int32, (C, C), 0)
    col = lax.broadcasted_iota(jnp.int32, (C, C), 1)
    if rev:
        return row <= col, row < col
    return row >= col, row > col


def _neumann_inverse(nil, dn, passes):
    C = nil.shape[-1]
    eye = (lax.broadcasted_iota(jnp.int32, (C, C), 0) ==
           lax.broadcasted_iota(jnp.int32, (C, C), 1)).astype(F32)
    x = eye + nil
    p = nil
    for _ in range(int(math.log2(C)) - 1):
        p = _dg(p, p, dn, passes)
        x = x + _dg(x, p, dn, passes)
    return x


def _gdn_body(qf, kf, vf, gcf, bcf, grf, qb, kb, vb, gcb, bcb, grb, of_ref, ob_ref, s_ref, *, passes):
    C = GDN_CHUNK
    H = GDN_HEADS

    @pl.when(pl.program_id(1) == 0)
    def _():
        s_ref[...] = jnp.zeros(s_ref.shape, F32)

    dirs = ((qf, kf, vf, gcf, bcf, grf), (qb, kb, vb, gcb, bcb, grb))
    qs, ks, vs, gcs, grs, betas, glast = [], [], [], [], [], [], []
    for d, (q_ref, k_ref, v_ref, gc_ref, bc_ref, gr_ref) in enumerate(dirs):
        rev = d == 1
        tri = _tri_masks(C, rev)[0].astype(F32)
        gcum_col = _dg(tri, gc_ref[0], _NN, 6)
        gcum_row = _dg(gr_ref[0, 0], tri, _NT, 6)
        beta_all = bc_ref[0]
        t_last = 0 if rev else C - 1
        for h in range(H):
            idx = d * H + h
            gcs.append(gcum_col[:, idx:idx + 1])
            grs.append(gcum_row[idx:idx + 1, :])
            glast.append(gcum_row[idx:idx + 1, t_last:t_last + 1])
            betas.append(beta_all[:, idx:idx + 1])
            qs.append(q_ref[0, :, h * GDN_DK:(h + 1) * GDN_DK])
            ks.append(k_ref[0, :, h * GDN_DK:(h + 1) * GDN_DK])
            vs.append(v_ref[0, :, h * GDN_DV:(h + 1) * GDN_DV])
    q, k, v = jnp.stack(qs), jnp.stack(ks), jnp.stack(vs)
    gc, gr, beta, g_last = jnp.stack(gcs), jnp.stack(grs), jnp.stack(betas), jnp.stack(glast)
    n = 2 * H
    unit = lax.broadcasted_iota(jnp.int32, (n, C, C), 0)
    ahead = (lax.broadcasted_iota(jnp.int32, (n, C, C), 1) - lax.broadcasted_iota(jnp.int32, (n, C, C), 2))
    ahead = jnp.where(unit < H, ahead, -ahead)
    incl = ahead >= 0
    strict = ahead > 0

    decay = jnp.exp(jnp.where(incl, gc - gr, -1e30))
    kbeta = k * beta
    lower = jnp.where(strict, _dg(kbeta, k, _BNT, passes) * decay, 0.0)
    tinv = _neumann_inverse(-lower, _BNN, passes)
    eg = jnp.exp(gc)
    u = _dg(tinv, v * beta, _BNN, passes)
    w = _dg(tinv, kbeta * eg, _BNN, passes)
    aqk = jnp.where(incl, _dg(q, k, _BNT, passes) * decay, 0.0)
    s = s_ref[...]
    v_new = u - _dg(w, s, _BNN, passes)
    o = _dg(q * eg, s, _BNN, passes) + _dg(aqk, v_new, _BNN, passes)
    s_ref[...] = s * jnp.exp(g_last) + _dg(k * jnp.exp(g_last - gc), v_new, _BTN, passes)
    for h in range(H):
        of_ref[0, :, h * GDN_DV:(h + 1) * GDN_DV] = o[h]
        ob_ref[0, :, h * GDN_DV:(h + 1) * GDN_DV] = o[H + h]


def _rev_chunk(i, ncc, nc):
    return jnp.where(i < ncc, ncc - 1 - i, nc - 1 + ncc - i)


def gdn_scan(q, k, v, g, beta, n_ctx):
    B, T, _ = q.shape
    C = GDN_CHUNK
    nc = T // C
    ncc = n_ctx // C
    gcol = g.reshape(B, T, 2 * GDN_HEADS)
    bcol = beta.reshape(B, T, 2 * GDN_HEADS)
    grow = jnp.swapaxes(gcol.reshape(B, nc, C, 2 * GDN_HEADS), 2, 3)
    fwd = lambda b, i: (b, i, 0)
    bwd = lambda b, i: (b, _rev_chunk(i, ncc, nc), 0)
    fwd4 = lambda b, i: (b, i, 0, 0)
    bwd4 = lambda b, i: (b, _rev_chunk(i, ncc, nc), 0, 0)
    wide = q.shape[-1]
    wv = v.shape[-1]

    def specs(m3, m4):
        return [pl.BlockSpec((1, C, wide), m3), pl.BlockSpec((1, C, wide), m3), pl.BlockSpec((1, C, wv), m3),
                pl.BlockSpec((1, C, 2 * GDN_HEADS), m3), pl.BlockSpec((1, C, 2 * GDN_HEADS), m3),
                pl.BlockSpec((1, 1, 2 * GDN_HEADS, C), m4)]

    of, ob = pl.pallas_call(
        functools.partial(_gdn_body, passes=GDN_PASSES),
        grid=(B, nc),
        in_specs=specs(fwd, fwd4) + specs(bwd, bwd4),
        out_specs=[pl.BlockSpec((1, C, wv), fwd), pl.BlockSpec((1, C, wv), bwd)],
        out_shape=[jax.ShapeDtypeStruct((B, T, wv), F32)] * 2,
        scratch_shapes=[pltpu.VMEM((2 * GDN_HEADS, GDN_DK, GDN_DV), F32)],
        compiler_params=_cparams("parallel", "arbitrary"),
        name="gdn_scan",
    )(q, k, v, gcol, bcol, grow, q, k, v, gcol, bcol, grow)
    return of, ob


def _rwkv_dir(r, lw, k, v, kk, rate, s, rev, passes):
    H, C, N = r.shape
    incl, strict = _tri_masks(C, rev)
    tri = jnp.broadcast_to(incl.astype(F32), (H, C, C))
    linc = _dg(tri, lw, _BNN, 6)
    lexc = linc - lw
    ltot = linc[:, 0:1, :] if rev else linc[:, C - 1:C, :]
    b = kk * rate
    at = -kk * jnp.exp(lexc)
    rt = r * jnp.exp(linc)
    einv = jnp.exp(-linc)
    bt = b * einv
    kt = k * einv
    etail = jnp.exp(ltot - linc)
    lab = jnp.where(strict, _dg(at, bt, _BNT, passes), 0.0)
    lak = jnp.where(strict, _dg(at, kt, _BNT, passes), 0.0)
    mrb = jnp.where(incl, _dg(rt, bt, _BNT, passes), 0.0)
    mrk = jnp.where(incl, _dg(rt, kt, _BNT, passes), 0.0)
    tinv = _neumann_inverse(lab, _BNN, passes)
    a_s = _dg(at, s, _BNT, passes)
    r_s = _dg(rt, s, _BNT, passes)
    u = _dg(tinv, a_s + _dg(lak, v, _BNN, passes), _BNN, passes)
    y = r_s + _dg(mrb, u, _BNN, passes) + _dg(mrk, v, _BNN, passes)
    s_new = (s * jnp.exp(ltot) + _dg(u, b * etail, _BTN, passes) + _dg(v, k * etail, _BTN, passes))
    return y, s_new


def _rwkv_body(rf, vf, kkf, lwf, kf, af, rb, vb, kkb, lwb, kb, ab, yf_ref, yb_ref, s_ref, *, passes):
    @pl.when(pl.program_id(1) == 0)
    def _():
        s_ref[...] = jnp.zeros(s_ref.shape, F32)

    N = RWKV_HEAD
    H = rf.shape[-1] // N

    def heads(ref):
        x = ref[0]
        return jnp.stack([x[:, h * N:(h + 1) * N] for h in range(H)], axis=0)

    dirs = ((rf, vf, kkf, lwf, kf, af, yf_ref), (rb, vb, kkb, lwb, kb, ab, yb_ref))
    for d, (r_ref, v_ref, kk_ref, lw_ref, k_ref, a_ref, y_ref) in enumerate(dirs):
        y, s_new = _rwkv_dir(heads(r_ref), heads(lw_ref), heads(k_ref), heads(v_ref), heads(kk_ref),
                             heads(a_ref), s_ref[d], d == 1, passes)
        s_ref[d] = s_new
        y_ref[0] = jnp.concatenate([y[h] for h in range(H)], axis=-1)


def rwkv_scan(r, v, kk, lw, key, rate, n_ctx):
    B, T, D = r.shape
    N = RWKV_HEAD
    C = RWKV_CHUNK
    nc = T // C
    ncc = n_ctx // C
    fwd = lambda b, i: (b, i, 0)
    bwd = lambda b, i: (b, _rev_chunk(i, ncc, nc), 0)
    blk = (1, C, D)
    return pl.pallas_call(
        functools.partial(_rwkv_body, passes=RWKV_PASSES),
        grid=(B, nc),
        in_specs=[pl.BlockSpec(blk, fwd)] * 6 + [pl.BlockSpec(blk, bwd)] * 6,
        out_specs=[pl.BlockSpec(blk, fwd), pl.BlockSpec(blk, bwd)],
        out_shape=[jax.ShapeDtypeStruct((B, T, D), F32)] * 2,
        scratch_shapes=[pltpu.VMEM((2, D // N, N, N), F32)],
        compiler_params=_cparams("parallel", "arbitrary"),
        name="rwkv7_scan",
    )(r, v, kk, lw[0], key[0], rate[0], r, v, kk, lw[1], key[1], rate[1])


def _moe_body(be_ref, x_ref, w1_ref, w3_ref, w2_ref, ws_ref, o_ref, w1b, w3b, w2b):
    i = pl.program_id(0)
    prev = be_ref[jnp.maximum(i - 1, 0)]

    @pl.when((i == 0) | (be_ref[i] != prev))
    def _():
        w1b[...] = w1_ref[0].astype(BF16)
        w3b[...] = w3_ref[0].astype(BF16)
        w2b[...] = w2_ref[0].astype(BF16)

    x = x_ref[...]
    h1 = jnp.dot(x, w1b[...], preferred_element_type=F32)
    h3 = jnp.dot(x, w3b[...], preferred_element_type=F32)
    hid = (h1 * jax.nn.sigmoid(h1)) * h3
    y = jnp.dot(hid.astype(BF16), w2b[...], preferred_element_type=F32)
    o_ref[...] = y * ws_ref[...]


def moe_experts(xs, slot_w, blk_e, w1, w3, w2):
    n_slots, D = xs.shape
    hid = w1.shape[-1]
    n_blocks = n_slots // MOE_BLOCK
    return pl.pallas_call(
        _moe_body,
        grid_spec=pltpu.PrefetchScalarGridSpec(
            num_scalar_prefetch=1,
            grid=(n_blocks,),
            in_specs=[pl.BlockSpec((MOE_BLOCK, D), lambda i, be: (i, 0)),
                      pl.BlockSpec((1, D, hid), lambda i, be: (be[i], 0, 0)),
                      pl.BlockSpec((1, D, hid), lambda i, be: (be[i], 0, 0)),
                      pl.BlockSpec((1, hid, D), lambda i, be: (be[i], 0, 0)),
                      pl.BlockSpec((MOE_BLOCK, 1), lambda i, be: (i, 0))],
            out_specs=pl.BlockSpec((MOE_BLOCK, D), lambda i, be: (i, 0)),
            scratch_shapes=[pltpu.VMEM((D, hid), BF16), pltpu.VMEM((D, hid), BF16), pltpu.VMEM((hid, D), BF16)],
        ),
        out_shape=jax.ShapeDtypeStruct((n_slots, D), F32),
        compiler_params=_cparams("arbitrary"),
        name="moe_experts",
    )(blk_e, xs, w1, w3, w2, slot_w)


def hier_moe(tokens, logits, b_group, b_expert, w1, w3, w2):
    N, D = tokens.shape
    logits = logits[:, :MOE_GROUPS + MOE_EXPERTS]
    pg = jax.nn.softmax(logits[:, :MOE_GROUPS] + b_group, axis=-1)
    g_idx = jnp.argmax(pg, axis=-1)[:, None].astype(jnp.int32)
    pg_top = jnp.max(pg, axis=-1, keepdims=True)
    le = logits[:, MOE_GROUPS:] + b_expert
    sel = g_idx * MOE_PER_GROUP + jnp.arange(MOE_PER_GROUP)[None, :]
    pe = jax.nn.softmax(jnp.take_along_axis(le, sel, axis=1), axis=-1)
    lane = jnp.arange(MOE_PER_GROUP, dtype=jnp.int32)[None, :]
    e1 = jnp.argmax(pe, axis=-1)[:, None].astype(jnp.int32)
    rest = jnp.where(lane == e1, -jnp.inf, pe)
    e2 = jnp.argmax(rest, axis=-1)[:, None].astype(jnp.int32)
    pe_top = jnp.concatenate([jnp.max(pe, -1, keepdims=True), jnp.max(rest, -1, keepdims=True)], axis=1)
    e_loc = jnp.concatenate([e1, e2], axis=1)
    wts = pg_top * pe_top / jnp.sum(pe_top, -1, keepdims=True)
    eid = (g_idx * MOE_PER_GROUP + e_loc).reshape(-1)
    tok = jnp.repeat(jnp.arange(N, dtype=jnp.int32), MOE_TOPK)
    A = N * MOE_TOPK
    onehot = (eid[:, None] == jnp.arange(MOE_EXPERTS)[None, :]).astype(jnp.int32)
    rank = jnp.take_along_axis(jnp.cumsum(onehot, 0), eid[:, None], 1)[:, 0] - 1
    counts = jnp.sum(onehot, 0)
    padded = (counts + MOE_BLOCK - 1) // MOE_BLOCK * MOE_BLOCK
    pend = jnp.cumsum(padded)
    dest = (pend - padded)[eid] + rank
    n_blocks = -(-A // MOE_BLOCK) + MOE_EXPERTS
    n_slots = n_blocks * MOE_BLOCK
    slot_src = jnp.full((n_slots,), -1, jnp.int32).at[dest].set(jnp.arange(A, dtype=jnp.int32))
    src = jnp.maximum(slot_src, 0)
    slot_tok = tok[src] * (slot_src >= 0)
    slot_w = jnp.where(slot_src >= 0, wts.reshape(-1)[src], 0.0)
    starts = jnp.arange(n_blocks, dtype=jnp.int32)[:, None] * MOE_BLOCK
    blk_e = jnp.minimum(jnp.sum((pend[None, :] <= starts).astype(jnp.int32), axis=1), MOE_EXPERTS - 1)
    xs = tokens[slot_tok]
    ys = moe_experts(xs, slot_w[:, None], blk_e, w1, w3, w2)
    d2 = dest.reshape(N, MOE_TOPK)
    return ys[d2[:, 0]] + ys[d2[:, 1]]


def _rope_tables(n_lat, n_ctx):
    rows = n_lat // GRID_W
    row = jnp.repeat(jnp.arange(rows, dtype=F32), GRID_W)
    col = jnp.tile(jnp.arange(GRID_W, dtype=F32), rows)
    n_freq = MLA_ROPE // 4
    inv = ROPE_BASE ** (-jnp.arange(n_freq, dtype=F32) / n_freq)
    ang = jnp.stack([row[:, None] * inv, col[:, None] * inv], axis=1)
    cos, sin = jnp.cos(ang), jnp.sin(ang)
    zf = jnp.zeros((n_lat, n_freq), F32)
    lat = lambda parts, fill: jnp.concatenate(
        [jnp.full((n_lat, MLA_NOPE), fill, F32)] + parts + [jnp.full((n_lat, MLA_PAD - MLA_QK), fill, F32)], axis=1)
    c = lat([cos[:, 0], cos[:, 0], cos[:, 1], cos[:, 1]], 1.0)
    s_lo = lat([-sin[:, 0], zf, -sin[:, 1], zf], 0.0)
    s_hi = lat([zf, sin[:, 0], zf, sin[:, 1]], 0.0)
    ctx = lambda fill: jnp.full((n_ctx, MLA_PAD), fill, F32)
    return (jnp.concatenate([ctx(1.0), c], 0), jnp.concatenate([ctx(0.0), s_lo], 0),
            jnp.concatenate([ctx(0.0), s_hi], 0))


def _const_spec(shape):
    return pl.BlockSpec(shape, lambda i: (0,) * len(shape), pipeline_mode=pl.Buffered(1))


def _normmod(x, gain, shift, scale):
    return x * lax.rsqrt(jnp.mean(x * x, -1, keepdims=True) + EPS) * gain * (1 + scale) + shift


def _softplus(x):
    return jnp.maximum(x, 0.0) + jnp.log1p(jnp.exp(-jnp.abs(x)))


def _head_indicator(D, N):
    e = (jnp.arange(D)[:, None] // N == jnp.arange(128)[None, :]).astype(BF16)
    return e, e.T


def _seg_dot(x, e):
    xh, xl = _split(x)
    return jnp.dot(xh, e, preferred_element_type=F32) + jnp.dot(xl, e, preferred_element_type=F32)


def _dotf(a, b):
    return jnp.dot(a, b, preferred_element_type=F32)


def _rwkv_pre_body(*refs, tm, blocks_per_batch, ctx_blocks, vres):
    (h_ref, hp_ref, hn_ref, m_ref, gain_ref, mu_ref, w0_ref, a0_ref, kk_ref, ka_ref, e_ref, et_ref,
     wr_ref, wk_ref, wv_ref, w1_ref, w2_ref, a1_ref, a2_ref, g1_ref, g2_ref) = refs[:21]
    rest = refs[21:]
    if vres:
        v0_ref, v1_ref, v2_ref, vf_ref = rest[:4]
        rest = rest[4:]
    r_o, v_o, kk_o, lw0_o, lw1_o, k0_o, k1_o, ra0_o, ra1_o, gate_o = rest

    tb = pl.program_id(0) % blocks_per_batch
    seg_start = (tb == 0) | (tb == ctx_blocks)
    seg_end = (tb == ctx_blocks - 1) | (tb == blocks_per_batch - 1)
    shift, scale, gain = m_ref[0, 0:1, :], m_ref[0, 1:2, :], gain_ref[...]
    u = _normmod(h_ref[...], gain, shift, scale)
    up = jnp.where(seg_start, 0.0, _normmod(hp_ref[7:8, :], gain, shift, scale))
    un = jnp.where(seg_end, 0.0, _normmod(hn_ref[0:1, :], gain, shift, scale))
    row = lax.broadcasted_iota(jnp.int32, (tm, 1), 0)
    u_prev = jnp.where(row == 0, up, pltpu.roll(u, 1, 0))
    u_next = jnp.where(row == tm - 1, un, pltpu.roll(u, tm - 1, 0))
    xx = 0.5 * (u_prev + u_next) - u
    xr, xw, xk, xv, xa, xg = [(u + xx * mu_ref[j:j + 1, :]).astype(BF16) for j in range(6)]

    r = _dotf(xr, wr_ref[...])
    k = _dotf(xk, wk_ref[...])
    v = _dotf(xv, wv_ref[...])
    if vres:
        lo = _dotf(xv, v1_ref[...]).astype(BF16)
        v = v + (vf_ref[...] - v) * jax.nn.sigmoid(v0_ref[...] + _dotf(lo, v2_ref[...]))
    tl = jnp.tanh(_dotf(xw, w1_ref[...])).astype(BF16)
    al = _dotf(xa, a1_ref[...]).astype(BF16)
    gl = jax.nn.sigmoid(_dotf(xg, g1_ref[...])).astype(BF16)
    gate_o[...] = _dotf(gl, g2_ref[...])
    kx = k * kk_ref[...]
    inv = lax.rsqrt(_seg_dot(kx * kx, e_ref[...]) + EPS)
    r_o[...] = r
    v_o[...] = v
    kk_o[...] = kx * _seg_dot(inv, et_ref[...])
    for d, (lw_o, k_o, ra_o) in enumerate(((lw0_o, k0_o, ra0_o), (lw1_o, k1_o, ra1_o))):
        w_log = -_softplus(-(w0_ref[d:d + 1, :] + _dotf(tl, w2_ref[d]))) - 0.5
        lw_o[...] = -jnp.exp(w_log)
        a = jax.nn.sigmoid(a0_ref[d:d + 1, :] + _dotf(al, a2_ref[d]))
        ra_o[...] = a
        k_o[...] = k * (1 + (a - 1) * ka_ref[...])


def _row_tile(seg):
    return _pick(seg, (256, 128, 64, 32, 16, 8))


def _pad_cols(w, n):
    return jnp.pad(w, ((0, 0), (0, n - w.shape[1])))


def _pad_rows(w, n):
    return jnp.pad(w, ((0, n - w.shape[0]), (0, 0)))


def rwkv_pre(h, m_seg, seg, T, n_ctx, gain, mu, wr, wk, wv, w0, w1, w2, a0, a1, a2, g1, g2, k_k, k_a, vres, v_first):
    M, D = h.shape
    tm = _row_tile(seg)
    lora = w1.shape[-1]
    e, et = _head_indicator(D, RWKV_HEAD)
    zero = jnp.zeros((lora, D), F32)
    w2p = jnp.stack([jnp.concatenate([w2[0], zero], 0), jnp.concatenate([zero, w2[1]], 0)]).astype(BF16)
    a2p = jnp.stack([jnp.concatenate([a2[0], zero], 0), jnp.concatenate([zero, a2[1]], 0)]).astype(BF16)
    gp = -(-g1.shape[1] // 128) * 128
    row = lambda a: a.reshape(1, D)
    consts = [row(gain), mu, w0, a0, row(k_k), row(k_a), e, et,
              wr.astype(BF16), wk.astype(BF16), wv.astype(BF16),
              jnp.concatenate([w1[0], w1[1]], 1).astype(BF16), w2p,
              jnp.concatenate([a1[0], a1[1]], 1).astype(BF16), a2p,
              _pad_cols(g1, gp).astype(BF16), _pad_rows(g2, gp).astype(BF16)]
    row_spec = pl.BlockSpec((tm, D), lambda i: (i, 0))
    last8 = M // 8 - 1
    in_specs = [row_spec,
                pl.BlockSpec((8, D), lambda i: (jnp.maximum(i * (tm // 8) - 1, 0), 0)),
                pl.BlockSpec((8, D), lambda i: (jnp.minimum((i + 1) * (tm // 8), last8), 0)),
                pl.BlockSpec((1, 6, D), lambda i: (i * tm // seg, 0, 0))]
    in_specs += [_const_spec(c.shape) for c in consts]
    args = [h, h, h, m_seg] + consts
    if vres is not None:
        v0, v1, v2 = vres
        extra = [row(v0), _pad_cols(v1, 128).astype(BF16), _pad_rows(v2, 128).astype(BF16)]
        in_specs += [_const_spec(c.shape) for c in extra] + [row_spec]
        args += extra + [v_first]
    return pl.pallas_call(
        functools.partial(_rwkv_pre_body, tm=tm, blocks_per_batch=T // tm, ctx_blocks=n_ctx // tm,
                          vres=vres is not None),
        grid=(M // tm,),
        in_specs=in_specs,
        out_specs=[row_spec] * 10,
        out_shape=[jax.ShapeDtypeStruct((M, D), F32)] * 10,
        compiler_params=_cparams("parallel"),
        name="rwkv7_pre",
    )(*args)


def _post_tail(xo, h_ref, m_ref, gain_ref, w_ref, wrt_ref, h_o, f_o, lg_o):
    h_new = h_ref[...] + m_ref[0, 2:3, :] * _dotf(xo, w_ref[...])
    h_o[...] = h_new
    f = _normmod(h_new, gain_ref[...], m_ref[0, 3:4, :], m_ref[0, 4:5, :])
    f_o[...] = f.astype(BF16)
    lg_o[...] = jnp.dot(f, wrt_ref[...], precision=HI, preferred_element_type=F32)


def _rwkv_post_body(yf_ref, yb_ref, r_ref, k0_ref, k1_ref, v_ref, gate_ref, lnw_ref, lnb_ref, rk_ref, e_ref, et_ref,
                    h_ref, m_ref, gain_ref, w_ref, wrt_ref, h_o, f_o, lg_o):
    e, et = e_ref[...], et_ref[...]
    inv_n = 1.0 / RWKV_HEAD
    y = yf_ref[...] + yb_ref[...]
    yc = y - _seg_dot(_seg_dot(y, e) * inv_n, et)
    var = _seg_dot(_seg_dot(yc * yc, e) * inv_n, et)
    yn = yc * lax.rsqrt(var + GN_EPS) * lnw_ref[...] + lnb_ref[...]
    k_bonus = 0.5 * (k0_ref[...] + k1_ref[...])
    bonus = _seg_dot(_seg_dot(r_ref[...] * k_bonus * rk_ref[...], e), et) * v_ref[...]
    xo = ((yn + bonus) * gate_ref[...]).astype(BF16)
    _post_tail(xo, h_ref, m_ref, gain_ref, w_ref, wrt_ref, h_o, f_o, lg_o)


def _post_call(body, name, row_args, consts, h, m_seg, seg, gain, w_out, w_router):
    M, D = h.shape
    tm = _row_tile(seg)
    row_spec = lambda a: pl.BlockSpec((tm, a.shape[1]), lambda i: (i, 0))
    tail = [gain.reshape(1, D), w_out.astype(BF16), w_router]
    in_specs = ([row_spec(a) for a in row_args] + [_const_spec(c.shape) for c in consts] +
                [row_spec(h), pl.BlockSpec((1, 6, D), lambda i: (i * tm // seg, 0, 0))] +
                [_const_spec(c.shape) for c in tail])
    nr = w_router.shape[1]
    return pl.pallas_call(
        body,
        grid=(M // tm,),
        in_specs=in_specs,
        out_specs=[pl.BlockSpec((tm, D), lambda i: (i, 0)), pl.BlockSpec((tm, D), lambda i: (i, 0)),
                   pl.BlockSpec((tm, nr), lambda i: (i, 0))],
        out_shape=[jax.ShapeDtypeStruct((M, D), F32), jax.ShapeDtypeStruct((M, D), BF16),
                   jax.ShapeDtypeStruct((M, nr), F32)],
        compiler_params=_cparams("parallel"),
        name=name,
    )(*row_args, *consts, h, m_seg, *tail)


def _hy_post_body(a_ref, of_ref, ob_ref, z_ref, og_ref, h_ref, m_ref, gain_ref, w_ref, wrt_ref, h_o, f_o, lg_o):
    o = of_ref[...] + ob_ref[...]
    z = z_ref[...]
    parts = [a_ref[...]]
    for hd in range(GDN_HEADS):
        sl = slice(hd * GDN_DV, (hd + 1) * GDN_DV)
        oh, zh = o[:, sl], z[:, sl]
        on = oh * lax.rsqrt(jnp.mean(oh * oh, -1, keepdims=True) + EPS) * og_ref[...]
        parts.append(on * (zh * jax.nn.sigmoid(zh)))
    xo = jnp.concatenate(parts, axis=-1).astype(BF16)
    _post_tail(xo, h_ref, m_ref, gain_ref, w_ref, wrt_ref, h_o, f_o, lg_o)


def hy_post(a, of, ob, z, out_g, h, m_seg, seg, gain, w_out, w_router):
    return _post_call(_hy_post_body, "hybrid_post", [a, of, ob, z], [out_g.reshape(1, -1)], h, m_seg, seg,
                      gain, w_out, w_router)


def _rope_lanes(x, c, s_lo, s_hi):
    return x * c + pltpu.roll(x, MLA_PAD - 8, 1) * s_lo + pltpu.roll(x, 8, 1) * s_hi


def _hy_pre_body(h_ref, m_ref, gain_ref, c_ref, slo_ref, shi_ref, wq1, wkv1, wpe, wgq, wz, wab, qag, kvag,
                 wqb, wkn, wv, qng, kng, q_o, k_o, v_o, gq_o, z_o, ab_o):
    u = _normmod(h_ref[...], gain_ref[...], m_ref[0, 0:1, :], m_ref[0, 1:2, :]).astype(BF16)
    gq_o[...] = _dotf(u, wgq[...])
    z_o[...] = _dotf(u, wz[...])
    ab_o[...] = _dotf(u, wab[...])
    cq = _dotf(u, wq1[...])
    ckv = _dotf(u, wkv1[...])
    pe = _dotf(u, wpe[...])
    cq = (cq * lax.rsqrt(jnp.mean(cq * cq, -1, keepdims=True) + EPS) * qag[...]).astype(BF16)
    ckv = (ckv * lax.rsqrt(jnp.mean(ckv * ckv, -1, keepdims=True) + EPS) * kvag[...]).astype(BF16)
    q = _dotf(cq, wqb[...])
    kn = _dotf(ckv, wkn[...])
    vv = _dotf(ckv, wv[...])
    c, s_lo, s_hi = c_ref[...], slo_ref[...], shi_ref[...]
    one_col = (lax.broadcasted_iota(jnp.int32, (1, MLA_PAD), 1) == MLA_V).astype(F32)
    inv_d = 1.0 / MLA_QK

    def head_norm(t, g):
        return t * lax.rsqrt(jnp.sum(t * t, -1, keepdims=True) * inv_d + EPS) * g

    for hd in range(MLA_HEADS):
        sl = slice(hd * MLA_PAD, (hd + 1) * MLA_PAD)
        q_o[:, sl] = _rope_lanes(head_norm(q[:, sl], qng[...]), c, s_lo, s_hi).astype(BF16)
        k_o[:, sl] = _rope_lanes(head_norm(kn[:, sl] + pe, kng[...]), c, s_lo, s_hi).astype(BF16)
        v_o[:, sl] = (vv[:, sl] + one_col).astype(BF16)


def _pad_heads(w, heads, width, to):
    K = w.shape[0]
    return jnp.pad(w.reshape(K, heads, width), ((0, 0), (0, 0), (0, to - width))).reshape(K, heads * to)


def hy_pre(h, m_seg, seg, T, gain, rope, w_in, qa_g, w_qb, kva_g, w_kvb, qn_g, kn_g):
    M, D = h.shape
    tm = _row_tile(seg)
    H = MLA_HEADS
    c0, c1, c2 = MLA_COLS, MLA_COLS + GDN_QKV, MLA_COLS + GDN_QKV + GDN_Z
    kvl = MLA_Q_LORA + MLA_KV_LORA
    wb = w_in.astype(BF16)
    wpe = jnp.pad(wb[:, kvl:c0], ((0, 0), (MLA_NOPE, MLA_PAD - MLA_QK)))
    wkv = w_kvb.reshape(MLA_KV_LORA, H, MLA_NOPE + MLA_V)
    pad1 = lambda g: jnp.pad(g, (0, MLA_PAD - MLA_QK)).reshape(1, MLA_PAD)
    consts = [wb[:, :MLA_Q_LORA], wb[:, MLA_Q_LORA:kvl], wpe, wb[:, c0:c1], wb[:, c1:c2],
              _pad_cols(wb[:, c2:], 128), qa_g.reshape(1, -1), kva_g.reshape(1, -1),
              _pad_heads(w_qb, H, MLA_QK, MLA_PAD).astype(BF16),
              _pad_heads(wkv[:, :, :MLA_NOPE].reshape(MLA_KV_LORA, -1), H, MLA_NOPE, MLA_PAD).astype(BF16),
              _pad_heads(wkv[:, :, MLA_NOPE:].reshape(MLA_KV_LORA, -1), H, MLA_V, MLA_PAD).astype(BF16),
              pad1(qn_g), pad1(kn_g)]
    bpb = T // tm
    row = lambda n: pl.BlockSpec((tm, n), lambda i: (i, 0))
    tab = pl.BlockSpec((tm, MLA_PAD), lambda i: (i % bpb, 0))
    in_specs = ([row(D), pl.BlockSpec((1, 6, D), lambda i: (i * tm // seg, 0, 0)), _const_spec((1, D)), tab, tab, tab]
                + [_const_spec(c.shape) for c in consts])
    wide = H * MLA_PAD
    return pl.pallas_call(
        _hy_pre_body,
        grid=(M // tm,),
        in_specs=in_specs,
        out_specs=[row(wide), row(wide), row(wide), row(GDN_QKV), row(GDN_Z), row(128)],
        out_shape=[jax.ShapeDtypeStruct((M, wide), BF16)] * 3 + [jax.ShapeDtypeStruct((M, GDN_QKV), F32),
                   jax.ShapeDtypeStruct((M, GDN_Z), F32), jax.ShapeDtypeStruct((M, 128), F32)],
        compiler_params=_cparams("parallel"),
        name="hybrid_pre",
    )(h, m_seg, gain.reshape(1, D), *rope, *consts)


def _gdn_prep_body(x_ref, xp_ref, xn_ref, w_ref, q_o, k_o, v_o, *, tm, blocks_per_batch, ctx_blocks):
    tb = pl.program_id(0) % blocks_per_batch
    seg_start = (tb == 0) | (tb == ctx_blocks)
    seg_end = (tb == ctx_blocks - 1) | (tb == blocks_per_batch - 1)
    x = x_ref[...]
    xp = jnp.where(seg_start, 0.0, xp_ref[...])
    xn = jnp.where(seg_end, 0.0, xn_ref[...])
    row = lax.broadcasted_iota(jnp.int32, (tm, 1), 0)
    half = GDN_CONV // 2
    acc = x * w_ref[half:half + 1, :]
    for s in range(1, half + 1):
        before = pltpu.roll(x, s, 0)
        after = pltpu.roll(x, tm - s, 0)
        for r in range(s):
            before = jnp.where(row == r, xp[8 - s + r:8 - s + r + 1, :], before)
            after = jnp.where(row == tm - s + r, xn[r:r + 1, :], after)
        acc = acc + before * w_ref[half - s:half - s + 1, :] + after * w_ref[half + s:half + s + 1, :]
    y = acc * jax.nn.sigmoid(acc)
    nk = GDN_HEADS * GDN_DK
    for hd in range(GDN_HEADS):
        sl = slice(hd * GDN_DK, (hd + 1) * GDN_DK)
        qh = y[:, sl]
        kh = y[:, nk + hd * GDN_DK:nk + (hd + 1) * GDN_DK]
        q_o[:, sl] = qh * lax.rsqrt(jnp.sum(qh * qh, -1, keepdims=True) + EPS) * GDN_DK ** -0.5
        k_o[:, sl] = kh * lax.rsqrt(jnp.sum(kh * kh, -1, keepdims=True) + EPS)
    v_o[...] = y[:, 2 * nk:]


def gdn_prep(gq, conv_w, seg, T, n_ctx):
    M, W = gq.shape
    tm = _row_tile(seg)
    last8 = M // 8 - 1
    nk = GDN_HEADS * GDN_DK
    row = lambda n: pl.BlockSpec((tm, n), lambda i: (i, 0))
    return pl.pallas_call(
        functools.partial(_gdn_prep_body, tm=tm, blocks_per_batch=T // tm, ctx_blocks=n_ctx // tm),
        grid=(M // tm,),
        in_specs=[row(W),
                  pl.BlockSpec((8, W), lambda i: (jnp.maximum(i * (tm // 8) - 1, 0), 0)),
                  pl.BlockSpec((8, W), lambda i: (jnp.minimum((i + 1) * (tm // 8), last8), 0)),
                  _const_spec(conv_w.shape)],
        out_specs=[row(nk), row(nk), row(W - 2 * nk)],
        out_shape=[jax.ShapeDtypeStruct((M, nk), F32), jax.ShapeDtypeStruct((M, nk), F32),
                   jax.ShapeDtypeStruct((M, W - 2 * nk), F32)],
        compiler_params=_cparams("parallel"),
        name="gdn_prep",
    )(gq, gq, gq, conv_w)


def rwkv_post(yf, yb, r, k0, k1, v, gate, ln_w, ln_b, r_k, h, m_seg, seg, gain, wo, w_router):
    D = h.shape[1]
    e, et = _head_indicator(D, RWKV_HEAD)
    consts = [ln_w.reshape(1, D), ln_b.reshape(1, D), r_k.reshape(1, D), e, et]
    return _post_call(_rwkv_post_body, "rwkv7_post", [yf, yb, r, k0, k1, v, gate], consts, h, m_seg, seg,
                      gain, wo, w_router)


def kernel(x, c, ctx, c_ctx, ada_w, ada_b, norm_mix, norm_ffn, hy_w_in, hy_w_out, mla_qa_norm, mla_w_qb, mla_kva_norm, mla_w_kvb, mla_q_norm, mla_k_norm, gdn_conv, gdn_a_log, gdn_dt_bias, gdn_out_norm, rk_mu, rk_wr, rk_wk, rk_wv, rk_wo, rk_w0, rk_w1, rk_w2, rk_a0, rk_a1, rk_a2, rk_g1, rk_g2, rk_kk, rk_ka, rk_rk, rk_ln_w, rk_ln_b, rk_v0, rk_v1, rk_v2, moe_w_group, moe_b_group, moe_w_expert, moe_b_expert, moe_w1, moe_w3, moe_w2):
    B, S, D = x.shape
    L = ctx.shape[1]
    T = L + S
    depth = ada_w.shape[0]
    rope = _rope_tables(S, L)
    n_rows = -(-(B + 1) // 8) * 8
    sc = jnp.concatenate([jax.nn.silu(c), jax.nn.silu(c_ctx)[None], jnp.zeros((n_rows - B - 1, D), F32)], 0)
    M = B * T
    h = jnp.concatenate([ctx, x], axis=1).reshape(M, D)
    seg = math.gcd(L, S)
    nseg = T // seg
    v_first = None
    for l in range(depth):
        m = mm(sc, ada_w[l], hi=True) + ada_b[l]
        m_lat = jnp.broadcast_to(m[:B].reshape(B, 1, 6, D), (B, S // seg, 6, D))
        m_ctx = jnp.broadcast_to(m[B].reshape(1, 1, 6, D), (B, L // seg, 6, D))
        m_seg = jnp.concatenate([m_ctx, m_lat], axis=1).reshape(B * nseg, 6, D)

        def mod(i, m_seg=m_seg):
            return m_seg[:, None, i, :]

        router = _pad_cols(jnp.concatenate([moe_w_group[l], moe_w_expert[l]], axis=1), 128)
        j = l // 2
        b3 = lambda a: a.reshape(B, T, a.shape[-1])
        if l % 2 == 0:
            q, k, v, gq, z, ab = hy_pre(h, m_seg, seg, T, norm_mix[l], rope, hy_w_in[j], mla_qa_norm[j],
                                        mla_w_qb[j], mla_kva_norm[j], mla_w_kvb[j], mla_q_norm[j], mla_k_norm[j])
            q, k, v = b3(q), b3(k), b3(v)
            a_lat = attention(q[:, L:], k, v)
            a_ctx = attention(q[:, :L], k[:, :L], v[:, :L])
            a = jnp.concatenate([a_ctx, a_lat], axis=1).reshape(M, -1)
            gq_, gk_, gv_ = gdn_prep(gq, gdn_conv[j], seg, T, L)
            ab = ab[:, :GDN_AB].reshape(B, T, 2, 2, GDN_HEADS)
            g = -jnp.exp(gdn_a_log[j]) * jax.nn.softplus(ab[:, :, :, 0] + gdn_dt_bias[j])
            beta = jax.nn.sigmoid(ab[:, :, :, 1])
            of, ob = gdn_scan(b3(gq_), b3(gk_), b3(gv_), g, beta, L)
            h, f, logits = hy_post(a, of.reshape(M, -1), ob.reshape(M, -1), z, gdn_out_norm[j], h, m_seg, seg,
                                   norm_ffn[l], hy_w_out[j], router)
        else:
            vres = None if j == 0 else (rk_v0[j - 1], rk_v1[j - 1], rk_v2[j - 1])
            r, v, kk, lw0, lw1, k0, k1, ra0, ra1, gate = rwkv_pre(
                h, m_seg, seg, T, L, norm_mix[l], rk_mu[j], rk_wr[j], rk_wk[j], rk_wv[j], rk_w0[j], rk_w1[j],
                rk_w2[j], rk_a0[j], rk_a1[j], rk_a2[j], rk_g1[j], rk_g2[j], rk_kk[j], rk_ka[j], vres, v_first)
            if j == 0:
                v_first = v
            b3 = lambda a: a.reshape(B, T, D)
            yf, yb = rwkv_scan(b3(r), b3(v), b3(kk), [b3(lw0), b3(lw1)], [b3(k0), b3(k1)], [b3(ra0), b3(ra1)], L)
            h, f, logits = rwkv_post(yf.reshape(M, D), yb.reshape(M, D), r, k0, k1, v, gate, rk_ln_w[j], rk_ln_b[j],
                                     rk_rk[j], h, m_seg, seg, norm_ffn[l], rk_wo[j], router)
        moe_out = hier_moe(f, logits, moe_b_group[l], moe_b_expert[l], moe_w1[l], moe_w3[l], moe_w2[l])
        h = (h.reshape(B * nseg, seg, D) + mod(5) * moe_out.reshape(B * nseg, seg, D)).reshape(M, D)
    return h.reshape(B, T, D)[:, L:]
```

```python
import functools
import math

import jax
import jax.numpy as jnp
from jax import lax
from jax.experimental import pallas as pl
from jax.experimental.pallas import tpu as pltpu

F32 = jnp.float32
BF16 = jnp.bfloat16
HI = lax.Precision.HIGHEST

DEPTH = 4
GRID_W = 64
EPS = 1e-6

MLA_HEADS = 8
MLA_Q_LORA = 256
MLA_KV_LORA = 128
MLA_NOPE = 64
MLA_ROPE = 32
MLA_V = 64
MLA_QK = MLA_NOPE + MLA_ROPE
MLA_SCALE = MLA_QK ** -0.5
ROPE_BASE = 10000.0
MLA_PAD = 128

GDN_HEADS = 4
GDN_DK = 128
GDN_DV = 128
GDN_CONV = 5
GDN_CHUNK = 64

RWKV_HEAD = 64
RWKV_CHUNK = 64
GN_EPS = 64e-5

MOE_GROUPS = 4
MOE_PER_GROUP = 8
MOE_EXPERTS = MOE_GROUPS * MOE_PER_GROUP
MOE_TOPK = 2
MOE_BLOCK = 256

MLA_COLS = MLA_Q_LORA + MLA_KV_LORA + MLA_ROPE
GDN_QKV = GDN_HEADS * (2 * GDN_DK + GDN_DV)
GDN_Z = GDN_HEADS * GDN_DV
GDN_AB = 2 * 2 * GDN_HEADS

VMEM_LIMIT_BYTES = 48 * 1024 * 1024

GDN_PASSES = 1
RWKV_PASSES = 1


def _cparams(*sem):
    return pltpu.CompilerParams(dimension_semantics=sem, vmem_limit_bytes=VMEM_LIMIT_BYTES)


def _pick(n, cands):
    for c in cands:
        if n % c == 0:
            return c
    return n


def _split(a):
    hi = a.astype(BF16)
    lo = (a - hi.astype(F32)).astype(BF16)
    return hi, lo


def _dg(a, b, dn, passes):
    if passes == 6:
        return lax.dot_general(a, b, dn, precision=HI, preferred_element_type=F32)
    if passes == 1:
        return lax.dot_general(a.astype(BF16), b.astype(BF16), dn, preferred_element_type=F32)
    ah, al = _split(a)
    bh, bl = _split(b)
    d = functools.partial(lax.dot_general, dimension_numbers=dn, preferred_element_type=F32)
    return d(ah, bh) + d(al, bh) + d(ah, bl)


_NN = (((1,), (0,)), ((), ()))
_NT = (((1,), (1,)), ((), ()))
_TN = (((0,), (0,)), ((), ()))
_BNN = (((2,), (1,)), ((0,), (0,)))
_BNT = (((2,), (2,)), ((0,), (0,)))
_BTN = (((1,), (1,)), ((0,), (0,)))


def _mm_body(x_ref, w_ref, o_ref, *, hi):
    if hi:
        o_ref[...] = jnp.dot(x_ref[...], w_ref[...], precision=HI, preferred_element_type=F32)
    else:
        o_ref[...] = jnp.dot(x_ref[...].astype(BF16), w_ref[...].astype(BF16),
                             preferred_element_type=F32)


def mm(x, w, hi=False):
    M, K = x.shape
    N = w.shape[1]
    tm = _pick(M, (512, 256, 128, 64, 32, 16, 8))
    tn = _pick(N, (512, 384, 256, 128))
    return pl.pallas_call(
        functools.partial(_mm_body, hi=hi),
        grid=(M // tm, N // tn),
        in_specs=[pl.BlockSpec((tm, K), lambda i, j: (i, 0)),
                  pl.BlockSpec((K, tn), lambda i, j: (0, j))],
        out_specs=pl.BlockSpec((tm, tn), lambda i, j: (i, j)),
        out_shape=jax.ShapeDtypeStruct((M, N), F32),
        compiler_params=_cparams("parallel", "parallel"),
        name="dense_mm",
    )(x, w)


def _attn_body(q_ref, k_ref, v_ref, o_ref, m_ref, acc_ref, *, c2):
    ki = pl.program_id(3)

    @pl.when(ki == 0)
    def _():
        m_ref[...] = jnp.full(m_ref.shape, -1e30, F32)
        acc_ref[...] = jnp.zeros(acc_ref.shape, F32)

    heads = range(2)
    sl = [slice(h * MLA_PAD, (h + 1) * MLA_PAD) for h in heads]
    m_prev = [m_ref[h] for h in heads]
    acc_prev = [acc_ref[h] for h in heads]
    s = [lax.dot_general(q_ref[0, :, sl[h]], k_ref[0, :, sl[h]], _NT, preferred_element_type=F32) for h in heads]
    m_new, alpha, p = [], [], []
    reps = s[0].shape[1] // MLA_PAD
    for h in heads:
        m_new.append(jnp.maximum(m_prev[h], jnp.max(s[h], axis=-1, keepdims=True)))
        alpha.append(jnp.exp2((m_prev[h] - m_new[h]) * c2))
        x = (s[h] - jnp.tile(m_new[h], (1, reps))) * c2
        p.append(jnp.exp2(x).astype(BF16))
    pv = [jnp.dot(p[h], v_ref[0, :, sl[h]], preferred_element_type=F32) for h in heads]
    for h in heads:
        acc_ref[h] = alpha[h] * acc_prev[h] + pv[h]
        m_ref[h] = m_new[h]

    @pl.when(ki == pl.num_programs(3) - 1)
    def _():
        outs = []
        for h in range(2):
            a = acc_ref[h]
            outs.append(a[:, :MLA_V] / a[:, MLA_V:MLA_V + 1])
        o_ref[0] = jnp.concatenate(outs, axis=-1)


def attention(q, k, v):
    B, Sq, _ = q.shape
    Sk = k.shape[1]
    tq = _pick(Sq, (1024, 512, 256, 128))
    tk = _pick(Sk, (1408, 768, 512, 384, 256, 128))
    return pl.pallas_call(
        functools.partial(_attn_body, c2=MLA_SCALE * math.log2(math.e)),
        grid=(B, MLA_HEADS // 2, Sq // tq, Sk // tk),
        in_specs=[pl.BlockSpec((1, tq, 2 * MLA_PAD), lambda b, p, i, j: (b, i, p)),
                  pl.BlockSpec((1, tk, 2 * MLA_PAD), lambda b, p, i, j: (b, j, p)),
                  pl.BlockSpec((1, tk, 2 * MLA_PAD), lambda b, p, i, j: (b, j, p))],
        out_specs=pl.BlockSpec((1, tq, 2 * MLA_V), lambda b, p, i, j: (b, i, p)),
        out_shape=jax.ShapeDtypeStruct((B, Sq, MLA_HEADS * MLA_V), F32),
        scratch_shapes=[pltpu.VMEM((2, tq, MLA_PAD), F32), pltpu.VMEM((2, tq, MLA_PAD), F32)],
        compiler_params=_cparams("parallel", "parallel", "parallel", "arbitrary"),
        name="mla_attention",
    )(q, k, v)


def _tri_masks(C, rev):
    row = lax.broadcasted_iota(jnp.int32, (C, C), 0)
    col = lax.broadcasted_iota(jnp.int32, (C, C), 1)
    if rev:
        return row <= col, row < col
    return row >= col, row > col


def _neumann_inverse(nil, dn, passes):
    C = nil.shape[-1]
    eye = (lax.broadcasted_iota(jnp.int32, (C, C), 0) ==
           lax.broadcasted_iota(jnp.int32, (C, C), 1)).astype(F32)
    x = eye + nil
    p = nil
    for _ in range(int(math.log2(C)) - 1):
        p = _dg(p, p, dn, passes)
        x = x + _dg(x, p, dn, passes)
    return x


def _gdn_body(qf, kf, vf, gcf, bcf, grf, qb, kb, vb, gcb, bcb, grb, of_ref, ob_ref, s_ref, *, passes):
    C = GDN_CHUNK
    H = GDN_HEADS

    @pl.when(pl.program_id(1) == 0)
    def _():
        s_ref[...] = jnp.zeros(s_ref.shape, F32)

    dirs = ((qf, kf, vf, gcf, bcf, grf), (qb, kb, vb, gcb, bcb, grb))
    qs, ks, vs, gcs, grs, betas, glast = [], [], [], [], [], [], []
    for d, (q_ref, k_ref, v_ref, gc_ref, bc_ref, gr_ref) in enumerate(dirs):
        rev = d == 1
        tri = _tri_masks(C, rev)[0].astype(F32)
        gcum_col = _dg(tri, gc_ref[0], _NN, 6)
        gcum_row = _dg(gr_ref[0, 0], tri, _NT, 6)
        beta_all = bc_ref[0]
        t_last = 0 if rev else C - 1
        for h in range(H):
            idx = d * H + h
            gcs.append(gcum_col[:, idx:idx + 1])
            grs.append(gcum_row[idx:idx + 1, :])
            glast.append(gcum_row[idx:idx + 1, t_last:t_last + 1])
            betas.append(beta_all[:, idx:idx + 1])
            qs.append(q_ref[0, :, h * GDN_DK:(h + 1) * GDN_DK])
            ks.append(k_ref[0, :, h * GDN_DK:(h + 1) * GDN_DK])
            vs.append(v_ref[0, :, h * GDN_DV:(h + 1) * GDN_DV])
    q, k, v = jnp.stack(qs), jnp.stack(ks), jnp.stack(vs)
    gc, gr, beta, g_last = jnp.stack(gcs), jnp.stack(grs), jnp.stack(betas), jnp.stack(glast)
    n = 2 * H
    unit = lax.broadcasted_iota(jnp.int32, (n, C, C), 0)
    ahead = (lax.broadcasted_iota(jnp.int32, (n, C, C), 1) - lax.broadcasted_iota(jnp.int32, (n, C, C), 2))
    ahead = jnp.where(unit < H, ahead, -ahead)
    incl = ahead >= 0
    strict = ahead > 0

    decay = jnp.exp(jnp.where(incl, gc - gr, -1e30))
    kbeta = k * beta
    lower = jnp.where(strict, _dg(kbeta, k, _BNT, passes) * decay, 0.0)
    tinv = _neumann_inverse(-lower, _BNN, passes)
    eg = jnp.exp(gc)
    u = _dg(tinv, v * beta, _BNN, passes)
    w = _dg(tinv, kbeta * eg, _BNN, passes)
    aqk = jnp.where(incl, _dg(q, k, _BNT, passes) * decay, 0.0)
    s = s_ref[...]
    v_new = u - _dg(w, s, _BNN, passes)
    o = _dg(q * eg, s, _BNN, passes) + _dg(aqk, v_new, _BNN, passes)
    s_ref[...] = s * jnp.exp(g_last) + _dg(k * jnp.exp(g_last - gc), v_new, _BTN, passes)
    for h in range(H):
        of_ref[0, :, h * GDN_DV:(h + 1) * GDN_DV] = o[h]
        ob_ref[0, :, h * GDN_DV:(h + 1) * GDN_DV] = o[H + h]


def _rev_chunk(i, ncc, nc):
    return jnp.where(i < ncc, ncc - 1 - i, nc - 1 + ncc - i)


def gdn_scan(q, k, v, g, beta, n_ctx):
    B, T, _ = q.shape
    C = GDN_CHUNK
    nc = T // C
    ncc = n_ctx // C
    gcol = g.reshape(B, T, 2 * GDN_HEADS)
    bcol = beta.reshape(B, T, 2 * GDN_HEADS)
    grow = jnp.swapaxes(gcol.reshape(B, nc, C, 2 * GDN_HEADS), 2, 3)
    fwd = lambda b, i: (b, i, 0)
    bwd = lambda b, i: (b, _rev_chunk(i, ncc, nc), 0)
    fwd4 = lambda b, i: (b, i, 0, 0)
    bwd4 = lambda b, i: (b, _rev_chunk(i, ncc, nc), 0, 0)
    wide = q.shape[-1]
    wv = v.shape[-1]

    def specs(m3, m4):
        return [pl.BlockSpec((1, C, wide), m3), pl.BlockSpec((1, C, wide), m3), pl.BlockSpec((1, C, wv), m3),
                pl.BlockSpec((1, C, 2 * GDN_HEADS), m3), pl.BlockSpec((1, C, 2 * GDN_HEADS), m3),
                pl.BlockSpec((1, 1, 2 * GDN_HEADS, C), m4)]

    of, ob = pl.pallas_call(
        functools.partial(_gdn_body, passes=GDN_PASSES),
        grid=(B, nc),
        in_specs=specs(fwd, fwd4) + specs(bwd, bwd4),
        out_specs=[pl.BlockSpec((1, C, wv), fwd), pl.BlockSpec((1, C, wv), bwd)],
        out_shape=[jax.ShapeDtypeStruct((B, T, wv), F32)] * 2,
        scratch_shapes=[pltpu.VMEM((2 * GDN_HEADS, GDN_DK, GDN_DV), F32)],
        compiler_params=_cparams("parallel", "arbitrary"),
        name="gdn_scan",
    )(q, k, v, gcol, bcol, grow, q, k, v, gcol, bcol, grow)
    return of, ob


def _rwkv_dir(r, lw, k, v, kk, rate, s, rev, passes):
    C, D = r.shape
    N = RWKV_HEAD
    H = D // N
    incl, _ = _tri_masks(C, rev)
    tri = incl.astype(BF16)
    l1 = lw.astype(BF16)
    rem = lw - l1.astype(F32)
    l2 = rem.astype(BF16)
    l3 = (rem - l2.astype(F32)).astype(BF16)
    linc = _dotf(tri, l1) + _dotf(tri, l2) + _dotf(tri, l3)
    lexc = linc - lw
    ltot = linc[0:1, :] if rev else linc[C - 1:C, :]
    b = kk * rate
    einv = jnp.exp(-linc)
    etail = jnp.exp(ltot - linc)

    def hs(x):
        return jnp.stack([x[:, h * N:(h + 1) * N] for h in range(H)], axis=0)

    vh = hs(v)
    lhs = jnp.concatenate([hs(-kk * jnp.exp(lexc)), hs(r * jnp.exp(linc))], axis=1)
    rhs = jnp.concatenate([hs(b * einv), hs(k * einv)], axis=1)
    tail = jnp.concatenate([hs(b * etail), hs(k * etail)], axis=1)
    col = lax.broadcasted_iota(jnp.int32, (C, 2 * C), 1)
    ahead = lax.broadcasted_iota(jnp.int32, (C, 2 * C), 0) - jnp.where(col >= C, col - C, col)
    ahead = -ahead if rev else ahead
    sc = _dg(lhs, rhs, _BNT, passes)
    top = jnp.where(ahead > 0, sc[:, :C, :], 0.0)
    bot = jnp.where(ahead >= 0, sc[:, C:, :], 0.0)
    tinv = _neumann_inverse(top[:, :, :C], _BNN, passes)
    ars = _dg(lhs, s, _BNT, passes)
    zero_v = jnp.concatenate([jnp.zeros_like(vh), vh], axis=1)
    u = _dg(tinv, ars[:, :C, :] + _dg(top, zero_v, _BNN, passes), _BNN, passes)
    uv = jnp.concatenate([u, vh], axis=1)
    y = ars[:, C:, :] + _dg(bot, uv, _BNN, passes)
    s_new = s * jnp.exp(hs(ltot)) + _dg(uv, tail, _BTN, passes)
    return jnp.concatenate([y[h] for h in range(H)], axis=-1), s_new


def _rwkv_body(rf, vf, kkf, lwf, kf, af, rb, vb, kkb, lwb, kb, ab, yf_ref, yb_ref, s_ref, *, passes):
    @pl.when(pl.program_id(1) == 0)
    def _():
        s_ref[...] = jnp.zeros(s_ref.shape, F32)

    dirs = ((rf, vf, kkf, lwf, kf, af, yf_ref), (rb, vb, kkb, lwb, kb, ab, yb_ref))
    for d, (r_ref, v_ref, kk_ref, lw_ref, k_ref, a_ref, y_ref) in enumerate(dirs):
        y, s_new = _rwkv_dir(r_ref[0], lw_ref[0], k_ref[0], v_ref[0], kk_ref[0], a_ref[0], s_ref[d], d == 1, passes)
        s_ref[d] = s_new
        y_ref[0] = y


def rwkv_scan(r, v, kk, lw, key, rate, n_ctx):
    B, T, D = r.shape
    N = RWKV_HEAD
    C = RWKV_CHUNK
    nc = T // C
    ncc = n_ctx // C
    fwd = lambda b, i: (b, i, 0)
    bwd = lambda b, i: (b, _rev_chunk(i, ncc, nc), 0)
    blk = (1, C, D)
    return pl.pallas_call(
        functools.partial(_rwkv_body, passes=RWKV_PASSES),
        grid=(B, nc),
        in_specs=[pl.BlockSpec(blk, fwd)] * 6 + [pl.BlockSpec(blk, bwd)] * 6,
        out_specs=[pl.BlockSpec(blk, fwd), pl.BlockSpec(blk, bwd)],
        out_shape=[jax.ShapeDtypeStruct((B, T, D), F32)] * 2,
        scratch_shapes=[pltpu.VMEM((2, D // N, N, N), F32)],
        compiler_params=_cparams("parallel", "arbitrary"),
        name="rwkv7_scan",
    )(r, v, kk, lw[0], key[0], rate[0], r, v, kk, lw[1], key[1], rate[1])


def _moe_body(be_ref, x_ref, w1_ref, w3_ref, w2_ref, ws_ref, o_ref, w1b, w3b, w2b):
    i = pl.program_id(0)
    prev = be_ref[jnp.maximum(i - 1, 0)]

    @pl.when((i == 0) | (be_ref[i] != prev))
    def _():
        w1b[...] = w1_ref[0].astype(BF16)
        w3b[...] = w3_ref[0].astype(BF16)
        w2b[...] = w2_ref[0].astype(BF16)

    x = x_ref[...]
    h1 = jnp.dot(x, w1b[...], preferred_element_type=F32)
    h3 = jnp.dot(x, w3b[...], preferred_element_type=F32)
    hid = (h1 * jax.nn.sigmoid(h1)) * h3
    y = jnp.dot(hid.astype(BF16), w2b[...], preferred_element_type=F32)
    o_ref[...] = y * ws_ref[...]


def moe_experts(xs, slot_w, blk_e, w1, w3, w2):
    n_slots, D = xs.shape
    hid = w1.shape[-1]
    n_blocks = n_slots // MOE_BLOCK
    return pl.pallas_call(
        _moe_body,
        grid_spec=pltpu.PrefetchScalarGridSpec(
            num_scalar_prefetch=1,
            grid=(n_blocks,),
            in_specs=[pl.BlockSpec((MOE_BLOCK, D), lambda i, be: (i, 0)),
                      pl.BlockSpec((1, D, hid), lambda i, be: (be[i], 0, 0)),
                      pl.BlockSpec((1, D, hid), lambda i, be: (be[i], 0, 0)),
                      pl.BlockSpec((1, hid, D), lambda i, be: (be[i], 0, 0)),
                      pl.BlockSpec((MOE_BLOCK, 1), lambda i, be: (i, 0))],
            out_specs=pl.BlockSpec((MOE_BLOCK, D), lambda i, be: (i, 0)),
            scratch_shapes=[pltpu.VMEM((D, hid), BF16), pltpu.VMEM((D, hid), BF16), pltpu.VMEM((hid, D), BF16)],
        ),
        out_shape=jax.ShapeDtypeStruct((n_slots, D), F32),
        compiler_params=_cparams("arbitrary"),
        name="moe_experts",
    )(blk_e, xs, w1, w3, w2, slot_w)


def _route_body(lg_ref, bias_ref, out_ref, cnt_ref, run_ref, *, tm):
    @pl.when(pl.program_id(0) == 0)
    def _():
        run_ref[...] = jnp.zeros(run_ref.shape, F32)

    x = lg_ref[...] + bias_ref[...]
    lane = lax.broadcasted_iota(jnp.int32, x.shape, 1)
    far = 1 << 20

    def first_lane(hit):
        return jnp.min(jnp.where(hit, lane, far), axis=-1, keepdims=True)

    def masked_softmax(mask):
        xm = jnp.where(mask, x, -1e30)
        e = jnp.where(mask, jnp.exp(xm - jnp.max(xm, axis=-1, keepdims=True)), 0.0)
        return e / jnp.sum(e, axis=-1, keepdims=True)

    is_group = lane < MOE_GROUPS
    pg = masked_softmax(is_group)
    pg_top = jnp.max(pg, axis=-1, keepdims=True)
    g_idx = first_lane(is_group & (pg == pg_top))
    lo = MOE_GROUPS + MOE_PER_GROUP * g_idx
    in_group = (lane >= lo) & (lane < lo + MOE_PER_GROUP)
    pe = masked_softmax(in_group)
    p1 = jnp.max(pe, axis=-1, keepdims=True)
    l1 = first_lane(in_group & (pe == p1))
    rest_ok = in_group & (lane != l1)
    rest = jnp.where(rest_ok, pe, -1.0)
    p2 = jnp.max(rest, axis=-1, keepdims=True)
    l2 = first_lane(rest_ok & (rest == p2))
    psum = p1 + p2
    w1 = pg_top * p1 / psum
    w2 = pg_top * p2 / psum

    oh1 = (lane == l1).astype(F32)
    oh2 = (lane == l2).astype(F32)
    both = oh1 + oh2
    earlier = (lax.broadcasted_iota(jnp.int32, (tm, tm), 0) > lax.broadcasted_iota(jnp.int32, (tm, tm), 1))
    base = _dotf(earlier.astype(BF16), both.astype(BF16)) + run_ref[...]
    r1 = jnp.sum(base * oh1, axis=-1, keepdims=True)
    r2 = jnp.sum(base * oh2, axis=-1, keepdims=True)
    run_ref[...] = run_ref[...] + jnp.sum(both, axis=0, keepdims=True)
    cnt_ref[...] = run_ref[...]
    cols = ((l1 - MOE_GROUPS).astype(F32), (l2 - MOE_GROUPS).astype(F32), r1, r2, w1, w2)
    out = jnp.zeros(x.shape, F32)
    for j, c in enumerate(cols):
        out = jnp.where(lane == j, c, out)
    out_ref[...] = out


def moe_route(logits, b_group, b_expert):
    N, W = logits.shape
    tm = _pick(N, (512, 256, 128, 64, 32, 16, 8))
    bias = _pad_cols(jnp.concatenate([b_group, b_expert])[None, :], W)
    return pl.pallas_call(
        functools.partial(_route_body, tm=tm),
        grid=(N // tm,),
        in_specs=[pl.BlockSpec((tm, W), lambda i: (i, 0)), pl.BlockSpec((1, W), lambda i: (0, 0))],
        out_specs=[pl.BlockSpec((tm, W), lambda i: (i, 0)), pl.BlockSpec((1, W), lambda i: (0, 0))],
        out_shape=[jax.ShapeDtypeStruct((N, W), F32), jax.ShapeDtypeStruct((1, W), F32)],
        scratch_shapes=[pltpu.VMEM((1, W), F32)],
        compiler_params=_cparams("arbitrary"),
        name="moe_route",
    )(logits, bias)


def hier_moe(tokens, logits, b_group, b_expert, w1, w3, w2):
    N, D = tokens.shape
    route, cnt = moe_route(logits, b_group, b_expert)
    eid = route[:, 0:MOE_TOPK].astype(jnp.int32).reshape(-1)
    rank = route[:, MOE_TOPK:2 * MOE_TOPK].astype(jnp.int32).reshape(-1)
    wts = route[:, 2 * MOE_TOPK:3 * MOE_TOPK]
    counts = cnt[0, MOE_GROUPS:MOE_GROUPS + MOE_EXPERTS].astype(jnp.int32)
    tok = jnp.repeat(jnp.arange(N, dtype=jnp.int32), MOE_TOPK)
    A = N * MOE_TOPK
    padded = (counts + MOE_BLOCK - 1) // MOE_BLOCK * MOE_BLOCK
    pend = jnp.cumsum(padded)
    dest = (pend - padded)[eid] + rank
    n_blocks = -(-A // MOE_BLOCK) + MOE_EXPERTS
    n_slots = n_blocks * MOE_BLOCK
    slot_src = jnp.full((n_slots,), -1, jnp.int32).at[dest].set(jnp.arange(A, dtype=jnp.int32))
    src = jnp.maximum(slot_src, 0)
    slot_tok = tok[src] * (slot_src >= 0)
    slot_w = jnp.where(slot_src >= 0, wts.reshape(-1)[src], 0.0)
    starts = jnp.arange(n_blocks, dtype=jnp.int32)[:, None] * MOE_BLOCK
    blk_e = jnp.minimum(jnp.sum((pend[None, :] <= starts).astype(jnp.int32), axis=1), MOE_EXPERTS - 1)
    xs = tokens[slot_tok]
    ys = moe_experts(xs, slot_w[:, None], blk_e, w1, w3, w2)
    d2 = dest.reshape(N, MOE_TOPK)
    return ys[d2[:, 0]] + ys[d2[:, 1]]


def _rope_tables(n_lat, n_ctx):
    rows = n_lat // GRID_W
    row = jnp.repeat(jnp.arange(rows, dtype=F32), GRID_W)
    col = jnp.tile(jnp.arange(GRID_W, dtype=F32), rows)
    n_freq = MLA_ROPE // 4
    inv = ROPE_BASE ** (-jnp.arange(n_freq, dtype=F32) / n_freq)
    ang = jnp.stack([row[:, None] * inv, col[:, None] * inv], axis=1)
    cos, sin = jnp.cos(ang), jnp.sin(ang)
    zf = jnp.zeros((n_lat, n_freq), F32)
    lat = lambda parts, fill: jnp.concatenate(
        [jnp.full((n_lat, MLA_NOPE), fill, F32)] + parts + [jnp.full((n_lat, MLA_PAD - MLA_QK), fill, F32)], axis=1)
    c = lat([cos[:, 0], cos[:, 0], cos[:, 1], cos[:, 1]], 1.0)
    s_lo = lat([-sin[:, 0], zf, -sin[:, 1], zf], 0.0)
    s_hi = lat([zf, sin[:, 0], zf, sin[:, 1]], 0.0)
    ctx = lambda fill: jnp.full((n_ctx, MLA_PAD), fill, F32)
    return (jnp.concatenate([ctx(1.0), c], 0), jnp.concatenate([ctx(0.0), s_lo], 0),
            jnp.concatenate([ctx(0.0), s_hi], 0))


def _const_spec(shape):
    return pl.BlockSpec(shape, lambda i: (0,) * len(shape), pipeline_mode=pl.Buffered(1))


def _normmod(x, gain, shift, scale):
    return x * lax.rsqrt(jnp.mean(x * x, -1, keepdims=True) + EPS) * gain * (1 + scale) + shift


def _softplus(x):
    return jnp.maximum(x, 0.0) + jnp.log1p(jnp.exp(-jnp.abs(x)))


def _head_indicator(D, N):
    e = (jnp.arange(D)[:, None] // N == jnp.arange(128)[None, :]).astype(BF16)
    return e, e.T


def _seg_dot(x, e):
    xh, xl = _split(x)
    return jnp.dot(xh, e, preferred_element_type=F32) + jnp.dot(xl, e, preferred_element_type=F32)


def _dotf(a, b):
    return jnp.dot(a, b, preferred_element_type=F32)


def _rwkv_pre_body(*refs, tm, blocks_per_batch, ctx_blocks, vres):
    (h_ref, hp_ref, hn_ref, m_ref, gain_ref, mu_ref, w0_ref, a0_ref, kk_ref, ka_ref, e_ref, et_ref,
     wr_ref, wk_ref, wv_ref, w1_ref, w2_ref, a1_ref, a2_ref, g1_ref, g2_ref) = refs[:21]
    rest = refs[21:]
    if vres:
        v0_ref, v1_ref, v2_ref, vf_ref = rest[:4]
        rest = rest[4:]
    r_o, v_o, kk_o, lw0_o, lw1_o, k0_o, k1_o, ra0_o, ra1_o, gate_o = rest

    tb = pl.program_id(0) % blocks_per_batch
    seg_start = (tb == 0) | (tb == ctx_blocks)
    seg_end = (tb == ctx_blocks - 1) | (tb == blocks_per_batch - 1)
    shift, scale, gain = m_ref[0, 0:1, :], m_ref[0, 1:2, :], gain_ref[...]
    u = _normmod(h_ref[...], gain, shift, scale)
    up = jnp.where(seg_start, 0.0, _normmod(hp_ref[7:8, :], gain, shift, scale))
    un = jnp.where(seg_end, 0.0, _normmod(hn_ref[0:1, :], gain, shift, scale))
    row = lax.broadcasted_iota(jnp.int32, (tm, 1), 0)
    u_prev = jnp.where(row == 0, up, pltpu.roll(u, 1, 0))
    u_next = jnp.where(row == tm - 1, un, pltpu.roll(u, tm - 1, 0))
    xx = 0.5 * (u_prev + u_next) - u
    xr, xw, xk, xv, xa, xg = [(u + xx * mu_ref[j:j + 1, :]).astype(BF16) for j in range(6)]

    r = _dotf(xr, wr_ref[...])
    k = _dotf(xk, wk_ref[...])
    v = _dotf(xv, wv_ref[...])
    if vres:
        lo = _dotf(xv, v1_ref[...]).astype(BF16)
        v = v + (vf_ref[...] - v) * jax.nn.sigmoid(v0_ref[...] + _dotf(lo, v2_ref[...]))
    tl = jnp.tanh(_dotf(xw, w1_ref[...])).astype(BF16)
    al = _dotf(xa, a1_ref[...]).astype(BF16)
    gl = jax.nn.sigmoid(_dotf(xg, g1_ref[...])).astype(BF16)
    gate_o[...] = _dotf(gl, g2_ref[...])
    kx = k * kk_ref[...]
    inv = lax.rsqrt(_seg_dot(kx * kx, e_ref[...]) + EPS)
    r_o[...] = r
    v_o[...] = v
    kk_o[...] = kx * _seg_dot(inv, et_ref[...])
    for d, (lw_o, k_o, ra_o) in enumerate(((lw0_o, k0_o, ra0_o), (lw1_o, k1_o, ra1_o))):
        w_log = -_softplus(-(w0_ref[d:d + 1, :] + _dotf(tl, w2_ref[d]))) - 0.5
        lw_o[...] = -jnp.exp(w_log)
        a = jax.nn.sigmoid(a0_ref[d:d + 1, :] + _dotf(al, a2_ref[d]))
        ra_o[...] = a
        k_o[...] = k * (1 + (a - 1) * ka_ref[...])


def _row_tile(seg):
    return _pick(seg, (256, 128, 64, 32, 16, 8))


def _pad_cols(w, n):
    return jnp.pad(w, ((0, 0), (0, n - w.shape[1])))


def _pad_rows(w, n):
    return jnp.pad(w, ((0, n - w.shape[0]), (0, 0)))


def rwkv_pre(h, m_seg, seg, T, n_ctx, gain, mu, wr, wk, wv, w0, w1, w2, a0, a1, a2, g1, g2, k_k, k_a, vres, v_first):
    M, D = h.shape
    tm = _row_tile(seg)
    lora = w1.shape[-1]
    e, et = _head_indicator(D, RWKV_HEAD)
    zero = jnp.zeros((lora, D), F32)
    w2p = jnp.stack([jnp.concatenate([w2[0], zero], 0), jnp.concatenate([zero, w2[1]], 0)]).astype(BF16)
    a2p = jnp.stack([jnp.concatenate([a2[0], zero], 0), jnp.concatenate([zero, a2[1]], 0)]).astype(BF16)
    gp = -(-g1.shape[1] // 128) * 128
    row = lambda a: a.reshape(1, D)
    consts = [row(gain), mu, w0, a0, row(k_k), row(k_a), e, et,
              wr.astype(BF16), wk.astype(BF16), wv.astype(BF16),
              jnp.concatenate([w1[0], w1[1]], 1).astype(BF16), w2p,
              jnp.concatenate([a1[0], a1[1]], 1).astype(BF16), a2p,
              _pad_cols(g1, gp).astype(BF16), _pad_rows(g2, gp).astype(BF16)]
    row_spec = pl.BlockSpec((tm, D), lambda i: (i, 0))
    last8 = M // 8 - 1
    in_specs = [row_spec,
                pl.BlockSpec((8, D), lambda i: (jnp.maximum(i * (tm // 8) - 1, 0), 0)),
                pl.BlockSpec((8, D), lambda i: (jnp.minimum((i + 1) * (tm // 8), last8), 0)),
                pl.BlockSpec((1, 6, D), lambda i: (i * tm // seg, 0, 0))]
    in_specs += [_const_spec(c.shape) for c in consts]
    args = [h, h, h, m_seg] + consts
    if vres is not None:
        v0, v1, v2 = vres
        extra = [row(v0), _pad_cols(v1, 128).astype(BF16), _pad_rows(v2, 128).astype(BF16)]
        in_specs += [_const_spec(c.shape) for c in extra] + [row_spec]
        args += extra + [v_first]
    return pl.pallas_call(
        functools.partial(_rwkv_pre_body, tm=tm, blocks_per_batch=T // tm, ctx_blocks=n_ctx // tm,
                          vres=vres is not None),
        grid=(M // tm,),
        in_specs=in_specs,
        out_specs=[row_spec] * 10,
        out_shape=[jax.ShapeDtypeStruct((M, D), F32)] * 10,
        compiler_params=_cparams("parallel"),
        name="rwkv7_pre",
    )(*args)


def _post_tail(xo, h_ref, m_ref, gain_ref, w_ref, wrt_ref, h_o, f_o, lg_o):
    h_new = h_ref[...] + m_ref[0, 2:3, :] * _dotf(xo, w_ref[...])
    h_o[...] = h_new
    f = _normmod(h_new, gain_ref[...], m_ref[0, 3:4, :], m_ref[0, 4:5, :])
    f_o[...] = f.astype(BF16)
    lg_o[...] = jnp.dot(f, wrt_ref[...], precision=HI, preferred_element_type=F32)


def _rwkv_post_body(yf_ref, yb_ref, r_ref, k0_ref, k1_ref, v_ref, gate_ref, lnw_ref, lnb_ref, rk_ref, e_ref, et_ref,
                    h_ref, m_ref, gain_ref, w_ref, wrt_ref, h_o, f_o, lg_o):
    e, et = e_ref[...], et_ref[...]
    inv_n = 1.0 / RWKV_HEAD
    y = yf_ref[...] + yb_ref[...]
    yc = y - _seg_dot(_seg_dot(y, e) * inv_n, et)
    var = _seg_dot(_seg_dot(yc * yc, e) * inv_n, et)
    yn = yc * lax.rsqrt(var + GN_EPS) * lnw_ref[...] + lnb_ref[...]
    k_bonus = 0.5 * (k0_ref[...] + k1_ref[...])
    bonus = _seg_dot(_seg_dot(r_ref[...] * k_bonus * rk_ref[...], e), et) * v_ref[...]
    xo = ((yn + bonus) * gate_ref[...]).astype(BF16)
    _post_tail(xo, h_ref, m_ref, gain_ref, w_ref, wrt_ref, h_o, f_o, lg_o)


def _post_call(body, name, row_args, consts, h, m_seg, seg, gain, w_out, w_router):
    M, D = h.shape
    tm = _row_tile(seg)
    row_spec = lambda a: pl.BlockSpec((tm, a.shape[1]), lambda i: (i, 0))
    tail = [gain.reshape(1, D), w_out.astype(BF16), w_router]
    in_specs = ([row_spec(a) for a in row_args] + [_const_spec(c.shape) for c in consts] +
                [row_spec(h), pl.BlockSpec((1, 6, D), lambda i: (i * tm // seg, 0, 0))] +
                [_const_spec(c.shape) for c in tail])
    nr = w_router.shape[1]
    return pl.pallas_call(
        body,
        grid=(M // tm,),
        in_specs=in_specs,
        out_specs=[pl.BlockSpec((tm, D), lambda i: (i, 0)), pl.BlockSpec((tm, D), lambda i: (i, 0)),
                   pl.BlockSpec((tm, nr), lambda i: (i, 0))],
        out_shape=[jax.ShapeDtypeStruct((M, D), F32), jax.ShapeDtypeStruct((M, D), BF16),
                   jax.ShapeDtypeStruct((M, nr), F32)],
        compiler_params=_cparams("parallel"),
        name=name,
    )(*row_args, *consts, h, m_seg, *tail)


def _hy_post_body(a_ref, of_ref, ob_ref, z_ref, og_ref, h_ref, m_ref, gain_ref, w_ref, wrt_ref, h_o, f_o, lg_o):
    o = of_ref[...] + ob_ref[...]
    z = z_ref[...]
    parts = [a_ref[...]]
    for hd in range(GDN_HEADS):
        sl = slice(hd * GDN_DV, (hd + 1) * GDN_DV)
        oh, zh = o[:, sl], z[:, sl]
        on = oh * lax.rsqrt(jnp.mean(oh * oh, -1, keepdims=True) + EPS) * og_ref[...]
        parts.append(on * (zh * jax.nn.sigmoid(zh)))
    xo = jnp.concatenate(parts, axis=-1).astype(BF16)
    _post_tail(xo, h_ref, m_ref, gain_ref, w_ref, wrt_ref, h_o, f_o, lg_o)


def hy_post(a, of, ob, z, out_g, h, m_seg, seg, gain, w_out, w_router):
    return _post_call(_hy_post_body, "hybrid_post", [a, of, ob, z], [out_g.reshape(1, -1)], h, m_seg, seg,
                      gain, w_out, w_router)


def _rope_lanes(x, c, s_lo, s_hi):
    return x * c + pltpu.roll(x, MLA_PAD - 8, 1) * s_lo + pltpu.roll(x, 8, 1) * s_hi


def _hy_pre_body(h_ref, m_ref, gain_ref, c_ref, slo_ref, shi_ref, wq1, wkv1, wpe, wgq, wz, wab, qag, kvag,
                 wqb, wkn, wv, qng, kng, q_o, k_o, v_o, gq_o, z_o, ab_o):
    u = _normmod(h_ref[...], gain_ref[...], m_ref[0, 0:1, :], m_ref[0, 1:2, :]).astype(BF16)
    gq_o[...] = _dotf(u, wgq[...])
    z_o[...] = _dotf(u, wz[...])
    ab_o[...] = _dotf(u, wab[...])
    cq = _dotf(u, wq1[...])
    ckv = _dotf(u, wkv1[...])
    pe = _dotf(u, wpe[...])
    cq = (cq * lax.rsqrt(jnp.mean(cq * cq, -1, keepdims=True) + EPS) * qag[...]).astype(BF16)
    ckv = (ckv * lax.rsqrt(jnp.mean(ckv * ckv, -1, keepdims=True) + EPS) * kvag[...]).astype(BF16)
    q = _dotf(cq, wqb[...])
    kn = _dotf(ckv, wkn[...])
    vv = _dotf(ckv, wv[...])
    c, s_lo, s_hi = c_ref[...], slo_ref[...], shi_ref[...]
    one_col = (lax.broadcasted_iota(jnp.int32, (1, MLA_PAD), 1) == MLA_V).astype(F32)
    inv_d = 1.0 / MLA_QK

    def head_norm(t, g):
        return t * lax.rsqrt(jnp.sum(t * t, -1, keepdims=True) * inv_d + EPS) * g

    for hd in range(MLA_HEADS):
        sl = slice(hd * MLA_PAD, (hd + 1) * MLA_PAD)
        q_o[:, sl] = _rope_lanes(head_norm(q[:, sl], qng[...]), c, s_lo, s_hi).astype(BF16)
        k_o[:, sl] = _rope_lanes(head_norm(kn[:, sl] + pe, kng[...]), c, s_lo, s_hi).astype(BF16)
        v_o[:, sl] = (vv[:, sl] + one_col).astype(BF16)


def _pad_heads(w, heads, width, to):
    K = w.shape[0]
    return jnp.pad(w.reshape(K, heads, width), ((0, 0), (0, 0), (0, to - width))).reshape(K, heads * to)


def hy_pre(h, m_seg, seg, T, gain, rope, w_in, qa_g, w_qb, kva_g, w_kvb, qn_g, kn_g):
    M, D = h.shape
    tm = _row_tile(seg)
    H = MLA_HEADS
    c0, c1, c2 = MLA_COLS, MLA_COLS + GDN_QKV, MLA_COLS + GDN_QKV + GDN_Z
    kvl = MLA_Q_LORA + MLA_KV_LORA
    wb = w_in.astype(BF16)
    wpe = jnp.pad(wb[:, kvl:c0], ((0, 0), (MLA_NOPE, MLA_PAD - MLA_QK)))
    wkv = w_kvb.reshape(MLA_KV_LORA, H, MLA_NOPE + MLA_V)
    pad1 = lambda g: jnp.pad(g, (0, MLA_PAD - MLA_QK)).reshape(1, MLA_PAD)
    consts = [wb[:, :MLA_Q_LORA], wb[:, MLA_Q_LORA:kvl], wpe, wb[:, c0:c1], wb[:, c1:c2],
              _pad_cols(wb[:, c2:], 128), qa_g.reshape(1, -1), kva_g.reshape(1, -1),
              _pad_heads(w_qb, H, MLA_QK, MLA_PAD).astype(BF16),
              _pad_heads(wkv[:, :, :MLA_NOPE].reshape(MLA_KV_LORA, -1), H, MLA_NOPE, MLA_PAD).astype(BF16),
              _pad_heads(wkv[:, :, MLA_NOPE:].reshape(MLA_KV_LORA, -1), H, MLA_V, MLA_PAD).astype(BF16),
              pad1(qn_g), pad1(kn_g)]
    bpb = T // tm
    row = lambda n: pl.BlockSpec((tm, n), lambda i: (i, 0))
    tab = pl.BlockSpec((tm, MLA_PAD), lambda i: (i % bpb, 0))
    in_specs = ([row(D), pl.BlockSpec((1, 6, D), lambda i: (i * tm // seg, 0, 0)), _const_spec((1, D)), tab, tab, tab]
                + [_const_spec(c.shape) for c in consts])
    wide = H * MLA_PAD
    return pl.pallas_call(
        _hy_pre_body,
        grid=(M // tm,),
        in_specs=in_specs,
        out_specs=[row(wide), row(wide), row(wide), row(GDN_QKV), row(GDN_Z), row(128)],
        out_shape=[jax.ShapeDtypeStruct((M, wide), BF16)] * 3 + [jax.ShapeDtypeStruct((M, GDN_QKV), F32),
                   jax.ShapeDtypeStruct((M, GDN_Z), F32), jax.ShapeDtypeStruct((M, 128), F32)],
        compiler_params=_cparams("parallel"),
        name="hybrid_pre",
    )(h, m_seg, gain.reshape(1, D), *rope, *consts)


def _gdn_prep_body(x_ref, xp_ref, xn_ref, w_ref, q_o, k_o, v_o, *, tm, blocks_per_batch, ctx_blocks):
    tb = pl.program_id(0) % blocks_per_batch
    seg_start = (tb == 0) | (tb == ctx_blocks)
    seg_end = (tb == ctx_blocks - 1) | (tb == blocks_per_batch - 1)
    x = x_ref[...]
    xp = jnp.where(seg_start, 0.0, xp_ref[...])
    xn = jnp.where(seg_end, 0.0, xn_ref[...])
    row = lax.broadcasted_iota(jnp.int32, (tm, 1), 0)
    half = GDN_CONV // 2
    acc = x * w_ref[half:half + 1, :]
    for s in range(1, half + 1):
        before = pltpu.roll(x, s, 0)
        after = pltpu.roll(x, tm - s, 0)
        for r in range(s):
            before = jnp.where(row == r, xp[8 - s + r:8 - s + r + 1, :], before)
            after = jnp.where(row == tm - s + r, xn[r:r + 1, :], after)
        acc = acc + before * w_ref[half - s:half - s + 1, :] + after * w_ref[half + s:half + s + 1, :]
    y = acc * jax.nn.sigmoid(acc)
    nk = GDN_HEADS * GDN_DK
    for hd in range(GDN_HEADS):
        sl = slice(hd * GDN_DK, (hd + 1) * GDN_DK)
        qh = y[:, sl]
        kh = y[:, nk + hd * GDN_DK:nk + (hd + 1) * GDN_DK]
        q_o[:, sl] = qh * lax.rsqrt(jnp.sum(qh * qh, -1, keepdims=True) + EPS) * GDN_DK ** -0.5
        k_o[:, sl] = kh * lax.rsqrt(jnp.sum(kh * kh, -1, keepdims=True) + EPS)
    v_o[...] = y[:, 2 * nk:]


def gdn_prep(gq, conv_w, seg, T, n_ctx):
    M, W = gq.shape
    tm = _row_tile(seg)
    last8 = M // 8 - 1
    nk = GDN_HEADS * GDN_DK
    row = lambda n: pl.BlockSpec((tm, n), lambda i: (i, 0))
    return pl.pallas_call(
        functools.partial(_gdn_prep_body, tm=tm, blocks_per_batch=T // tm, ctx_blocks=n_ctx // tm),
        grid=(M // tm,),
        in_specs=[row(W),
                  pl.BlockSpec((8, W), lambda i: (jnp.maximum(i * (tm // 8) - 1, 0), 0)),
                  pl.BlockSpec((8, W), lambda i: (jnp.minimum((i + 1) * (tm // 8), last8), 0)),
                  _const_spec(conv_w.shape)],
        out_specs=[row(nk), row(nk), row(W - 2 * nk)],
        out_shape=[jax.ShapeDtypeStruct((M, nk), F32), jax.ShapeDtypeStruct((M, nk), F32),
                   jax.ShapeDtypeStruct((M, W - 2 * nk), F32)],
        compiler_params=_cparams("parallel"),
        name="gdn_prep",
    )(gq, gq, gq, conv_w)


def rwkv_post(yf, yb, r, k0, k1, v, gate, ln_w, ln_b, r_k, h, m_seg, seg, gain, wo, w_router):
    D = h.shape[1]
    e, et = _head_indicator(D, RWKV_HEAD)
    consts = [ln_w.reshape(1, D), ln_b.reshape(1, D), r_k.reshape(1, D), e, et]
    return _post_call(_rwkv_post_body, "rwkv7_post", [yf, yb, r, k0, k1, v, gate], consts, h, m_seg, seg,
                      gain, wo, w_router)


def kernel(x, c, ctx, c_ctx, ada_w, ada_b, norm_mix, norm_ffn, hy_w_in, hy_w_out, mla_qa_norm, mla_w_qb, mla_kva_norm, mla_w_kvb, mla_q_norm, mla_k_norm, gdn_conv, gdn_a_log, gdn_dt_bias, gdn_out_norm, rk_mu, rk_wr, rk_wk, rk_wv, rk_wo, rk_w0, rk_w1, rk_w2, rk_a0, rk_a1, rk_a2, rk_g1, rk_g2, rk_kk, rk_ka, rk_rk, rk_ln_w, rk_ln_b, rk_v0, rk_v1, rk_v2, moe_w_group, moe_b_group, moe_w_expert, moe_b_expert, moe_w1, moe_w3, moe_w2):
    B, S, D = x.shape
    L = ctx.shape[1]
    T = L + S
    depth = ada_w.shape[0]
    rope = _rope_tables(S, L)
    n_rows = -(-(B + 1) // 8) * 8
    sc = jnp.concatenate([jax.nn.silu(c), jax.nn.silu(c_ctx)[None], jnp.zeros((n_rows - B - 1, D), F32)], 0)
    M = B * T
    h = jnp.concatenate([ctx, x], axis=1).reshape(M, D)
    seg = math.gcd(L, S)
    nseg = T // seg
    v_first = None
    for l in range(depth):
        m = mm(sc, ada_w[l], hi=True) + ada_b[l]
        m_lat = jnp.broadcast_to(m[:B].reshape(B, 1, 6, D), (B, S // seg, 6, D))
        m_ctx = jnp.broadcast_to(m[B].reshape(1, 1, 6, D), (B, L // seg, 6, D))
        m_seg = jnp.concatenate([m_ctx, m_lat], axis=1).reshape(B * nseg, 6, D)

        def mod(i, m_seg=m_seg):
            return m_seg[:, None, i, :]

        router = _pad_cols(jnp.concatenate([moe_w_group[l], moe_w_expert[l]], axis=1), 128)
        j = l // 2
        b3 = lambda a: a.reshape(B, T, a.shape[-1])
        if l % 2 == 0:
            q, k, v, gq, z, ab = hy_pre(h, m_seg, seg, T, norm_mix[l], rope, hy_w_in[j], mla_qa_norm[j],
                                        mla_w_qb[j], mla_kva_norm[j], mla_w_kvb[j], mla_q_norm[j], mla_k_norm[j])
            q, k, v = b3(q), b3(k), b3(v)
            a_lat = attention(q[:, L:], k, v)
            a_ctx = attention(q[:, :L], k[:, :L], v[:, :L])
            a = jnp.concatenate([a_ctx, a_lat], axis=1).reshape(M, -1)
            gq_, gk_, gv_ = gdn_prep(gq, gdn_conv[j], seg, T, L)
            ab = ab[:, :GDN_AB].reshape(B, T, 2, 2, GDN_HEADS)
            g = -jnp.exp(gdn_a_log[j]) * jax.nn.softplus(ab[:, :, :, 0] + gdn_dt_bias[j])
            beta = jax.nn.sigmoid(ab[:, :, :, 1])
            of, ob = gdn_scan(b3(gq_), b3(gk_), b3(gv_), g, beta, L)
            h, f, logits = hy_post(a, of.reshape(M, -1), ob.reshape(M, -1), z, gdn_out_norm[j], h, m_seg, seg,
                                   norm_ffn[l], hy_w_out[j], router)
        else:
            vres = None if j == 0 else (rk_v0[j - 1], rk_v1[j - 1], rk_v2[j - 1])
            r, v, kk, lw0, lw1, k0, k1, ra0, ra1, gate = rwkv_pre(
                h, m_seg, seg, T, L, norm_mix[l], rk_mu[j], rk_wr[j], rk_wk[j], rk_wv[j], rk_w0[j], rk_w1[j],
                rk_w2[j], rk_a0[j], rk_a1[j], rk_a2[j], rk_g1[j], rk_g2[j], rk_kk[j], rk_ka[j], vres, v_first)
            if j == 0:
                v_first = v
            b3 = lambda a: a.reshape(B, T, D)
            yf, yb = rwkv_scan(b3(r), b3(v), b3(kk), [b3(lw0), b3(lw1)], [b3(k0), b3(k1)], [b3(ra0), b3(ra1)], L)
            h, f, logits = rwkv_post(yf.reshape(M, D), yb.reshape(M, D), r, k0, k1, v, gate, rk_ln_w[j], rk_ln_b[j],
                                     rk_rk[j], h, m_seg, seg, norm_ffn[l], rk_wo[j], router)
        moe_out = hier_moe(f, logits, moe_b_group[l], moe_b_expert[l], moe_w1[l], moe_w3[l], moe_w2[l])
        h = (h.reshape(B * nseg, seg, D) + mod(5) * moe_out.reshape(B * nseg, seg, D)).reshape(M, D)
    return h.reshape(B, T, D)[:, L:]
```

```python
import functools
import math

import jax
import jax.numpy as jnp
from jax import lax
from jax.experimental import pallas as pl
from jax.experimental.pallas import tpu as pltpu

F32 = jnp.float32
BF16 = jnp.bfloat16
HI = lax.Precision.HIGHEST

DEPTH = 4
GRID_W = 64
EPS = 1e-6

MLA_HEADS = 8
MLA_Q_LORA = 256
MLA_KV_LORA = 128
MLA_NOPE = 64
MLA_ROPE = 32
MLA_V = 64
MLA_QK = MLA_NOPE + MLA_ROPE
MLA_SCALE = MLA_QK ** -0.5
ROPE_BASE = 10000.0
MLA_PAD = 128

GDN_HEADS = 4
GDN_DK = 128
GDN_DV = 128
GDN_CONV = 5
GDN_CHUNK = 64

RWKV_HEAD = 64
RWKV_CHUNK = 64
GN_EPS = 64e-5

MOE_GROUPS = 4
MOE_PER_GROUP = 8
MOE_EXPERTS = MOE_GROUPS * MOE_PER_GROUP
MOE_TOPK = 2
MOE_BLOCK = 256

MLA_COLS = MLA_Q_LORA + MLA_KV_LORA + MLA_ROPE
GDN_QKV = GDN_HEADS * (2 * GDN_DK + GDN_DV)
GDN_Z = GDN_HEADS * GDN_DV
GDN_AB = 2 * 2 * GDN_HEADS

VMEM_LIMIT_BYTES = 48 * 1024 * 1024

GDN_PASSES = 1
RWKV_PASSES = 1


def _cparams(*sem):
    return pltpu.CompilerParams(dimension_semantics=sem, vmem_limit_bytes=VMEM_LIMIT_BYTES)


def _pick(n, cands):
    for c in cands:
        if n % c == 0:
            return c
    return n


def _split(a):
    hi = a.astype(BF16)
    lo = (a - hi.astype(F32)).astype(BF16)
    return hi, lo


def _dg(a, b, dn, passes):
    if passes == 6:
        return lax.dot_general(a, b, dn, precision=HI, preferred_element_type=F32)
    if passes == 1:
        return lax.dot_general(a.astype(BF16), b.astype(BF16), dn, preferred_element_type=F32)
    ah, al = _split(a)
    bh, bl = _split(b)
    d = functools.partial(lax.dot_general, dimension_numbers=dn, preferred_element_type=F32)
    return d(ah, bh) + d(al, bh) + d(ah, bl)


_NN = (((1,), (0,)), ((), ()))
_NT = (((1,), (1,)), ((), ()))
_TN = (((0,), (0,)), ((), ()))
_BNN = (((2,), (1,)), ((0,), (0,)))
_BNT = (((2,), (2,)), ((0,), (0,)))
_BTN = (((1,), (1,)), ((0,), (0,)))


def _mm_body(x_ref, w_ref, o_ref, *, hi):
    if hi:
        o_ref[...] = jnp.dot(x_ref[...], w_ref[...], precision=HI, preferred_element_type=F32)
    else:
        o_ref[...] = jnp.dot(x_ref[...].astype(BF16), w_ref[...].astype(BF16),
                             preferred_element_type=F32)


def mm(x, w, hi=False):
    M, K = x.shape
    N = w.shape[1]
    tm = _pick(M, (512, 256, 128, 64, 32, 16, 8))
    tn = _pick(N, (512, 384, 256, 128))
    return pl.pallas_call(
        functools.partial(_mm_body, hi=hi),
        grid=(M // tm, N // tn),
        in_specs=[pl.BlockSpec((tm, K), lambda i, j: (i, 0)),
                  pl.BlockSpec((K, tn), lambda i, j: (0, j))],
        out_specs=pl.BlockSpec((tm, tn), lambda i, j: (i, j)),
        out_shape=jax.ShapeDtypeStruct((M, N), F32),
        compiler_params=_cparams("parallel", "parallel"),
        name="dense_mm",
    )(x, w)


def _attn_body(q_ref, k_ref, v_ref, o_ref, m_ref, acc_ref, *, c2):
    ki = pl.program_id(3)

    @pl.when(ki == 0)
    def _():
        m_ref[...] = jnp.full(m_ref.shape, -1e30, F32)
        acc_ref[...] = jnp.zeros(acc_ref.shape, F32)

    heads = range(2)
    sl = [slice(h * MLA_PAD, (h + 1) * MLA_PAD) for h in heads]
    m_prev = [m_ref[h] for h in heads]
    acc_prev = [acc_ref[h] for h in heads]
    s = [lax.dot_general(q_ref[0, :, sl[h]], k_ref[0, :, sl[h]], _NT, preferred_element_type=F32) for h in heads]
    m_new, alpha, p = [], [], []
    reps = s[0].shape[1] // MLA_PAD
    for h in heads:
        m_new.append(jnp.maximum(m_prev[h], jnp.max(s[h], axis=-1, keepdims=True)))
        alpha.append(jnp.exp2((m_prev[h] - m_new[h]) * c2))
        x = (s[h] - jnp.tile(m_new[h], (1, reps))) * c2
        p.append(jnp.exp2(x).astype(BF16))
    pv = [jnp.dot(p[h], v_ref[0, :, sl[h]], preferred_element_type=F32) for h in heads]
    for h in heads:
        acc_ref[h] = alpha[h] * acc_prev[h] + pv[h]
        m_ref[h] = m_new[h]

    @pl.when(ki == pl.num_programs(3) - 1)
    def _():
        outs = []
        for h in range(2):
            a = acc_ref[h]
            outs.append(a[:, :MLA_V] / a[:, MLA_V:MLA_V + 1])
        o_ref[0] = jnp.concatenate(outs, axis=-1)


def attention(q, k, v):
    B, Sq, _ = q.shape
    Sk = k.shape[1]
    tq = _pick(Sq, (1024, 512, 256, 128))
    tk = _pick(Sk, (1408, 768, 512, 384, 256, 128))
    return pl.pallas_call(
        functools.partial(_attn_body, c2=MLA_SCALE * math.log2(math.e)),
        grid=(B, MLA_HEADS // 2, Sq // tq, Sk // tk),
        in_specs=[pl.BlockSpec((1, tq, 2 * MLA_PAD), lambda b, p, i, j: (b, i, p)),
                  pl.BlockSpec((1, tk, 2 * MLA_PAD), lambda b, p, i, j: (b, j, p)),
                  pl.BlockSpec((1, tk, 2 * MLA_PAD), lambda b, p, i, j: (b, j, p))],
        out_specs=pl.BlockSpec((1, tq, 2 * MLA_V), lambda b, p, i, j: (b, i, p)),
        out_shape=jax.ShapeDtypeStruct((B, Sq, MLA_HEADS * MLA_V), F32),
        scratch_shapes=[pltpu.VMEM((2, tq, MLA_PAD), F32), pltpu.VMEM((2, tq, MLA_PAD), F32)],
        compiler_params=_cparams("parallel", "parallel", "parallel", "arbitrary"),
        name="mla_attention",
    )(q, k, v)


def _tri_masks(C, rev):
    row = lax.broadcasted_iota(jnp.int32, (C, C), 0)
    col = lax.broadcasted_iota(jnp.int32, (C, C), 1)
    if rev:
        return row <= col, row < col
    return row >= col, row > col


def _neumann_inverse(nil, dn, passes):
    C = nil.shape[-1]
    eye = (lax.broadcasted_iota(jnp.int32, (C, C), 0) ==
           lax.broadcasted_iota(jnp.int32, (C, C), 1)).astype(F32)
    x = eye + nil
    p = nil
    for _ in range(int(math.log2(C)) - 1):
        p = _dg(p, p, dn, passes)
        x = x + _dg(x, p, dn, passes)
    return x


def _gdn_body(qf, kf, vf, gcf, bcf, grf, qb, kb, vb, gcb, bcb, grb, of_ref, ob_ref, s_ref, *, passes):
    C = GDN_CHUNK
    H = GDN_HEADS

    @pl.when(pl.program_id(1) == 0)
    def _():
        s_ref[...] = jnp.zeros(s_ref.shape, F32)

    dirs = ((qf, kf, vf, gcf, bcf, grf), (qb, kb, vb, gcb, bcb, grb))
    qs, ks, vs, gcs, grs, betas, glast = [], [], [], [], [], [], []
    for d, (q_ref, k_ref, v_ref, gc_ref, bc_ref, gr_ref) in enumerate(dirs):
        rev = d == 1
        tri = _tri_masks(C, rev)[0].astype(F32)
        gcum_col = _dg(tri, gc_ref[0], _NN, 6)
        gcum_row = _dg(gr_ref[0, 0], tri, _NT, 6)
        beta_all = bc_ref[0]
        t_last = 0 if rev else C - 1
        for h in range(H):
            idx = d * H + h
            gcs.append(gcum_col[:, idx:idx + 1])
            grs.append(gcum_row[idx:idx + 1, :])
            glast.append(gcum_row[idx:idx + 1, t_last:t_last + 1])
            betas.append(beta_all[:, idx:idx + 1])
            qs.append(q_ref[0, :, h * GDN_DK:(h + 1) * GDN_DK])
            ks.append(k_ref[0, :, h * GDN_DK:(h + 1) * GDN_DK])
            vs.append(v_ref[0, :, h * GDN_DV:(h + 1) * GDN_DV])
    q, k, v = jnp.stack(qs), jnp.stack(ks), jnp.stack(vs)
    gc, gr, beta, g_last = jnp.stack(gcs), jnp.stack(grs), jnp.stack(betas), jnp.stack(glast)
    n = 2 * H
    unit = lax.broadcasted_iota(jnp.int32, (n, C, C), 0)
    ahead = (lax.broadcasted_iota(jnp.int32, (n, C, C), 1) - lax.broadcasted_iota(jnp.int32, (n, C, C), 2))
    ahead = jnp.where(unit < H, ahead, -ahead)
    incl = ahead >= 0
    strict = ahead > 0

    decay = jnp.exp(jnp.where(incl, gc - gr, -1e30))
    kbeta = k * beta
    lower = jnp.where(strict, _dg(kbeta, k, _BNT, passes) * decay, 0.0)
    tinv = _neumann_inverse(-lower, _BNN, passes)
    eg = jnp.exp(gc)
    u = _dg(tinv, v * beta, _BNN, passes)
    w = _dg(tinv, kbeta * eg, _BNN, passes)
    aqk = jnp.where(incl, _dg(q, k, _BNT, passes) * decay, 0.0)
    s = s_ref[...]
    v_new = u - _dg(w, s, _BNN, passes)
    o = _dg(q * eg, s, _BNN, passes) + _dg(aqk, v_new, _BNN, passes)
    s_ref[...] = s * jnp.exp(g_last) + _dg(k * jnp.exp(g_last - gc), v_new, _BTN, passes)
    for h in range(H):
        of_ref[0, :, h * GDN_DV:(h + 1) * GDN_DV] = o[h]
        ob_ref[0, :, h * GDN_DV:(h + 1) * GDN_DV] = o[H + h]


def _rev_chunk(i, ncc, nc):
    return jnp.where(i < ncc, ncc - 1 - i, nc - 1 + ncc - i)


def gdn_scan(q, k, v, g, beta, n_ctx):
    B, T, _ = q.shape
    C = GDN_CHUNK
    nc = T // C
    ncc = n_ctx // C
    gcol = g.reshape(B, T, 2 * GDN_HEADS)
    bcol = beta.reshape(B, T, 2 * GDN_HEADS)
    grow = jnp.swapaxes(gcol.reshape(B, nc, C, 2 * GDN_HEADS), 2, 3)
    fwd = lambda b, i: (b, i, 0)
    bwd = lambda b, i: (b, _rev_chunk(i, ncc, nc), 0)
    fwd4 = lambda b, i: (b, i, 0, 0)
    bwd4 = lambda b, i: (b, _rev_chunk(i, ncc, nc), 0, 0)
    wide = q.shape[-1]
    wv = v.shape[-1]

    def specs(m3, m4):
        return [pl.BlockSpec((1, C, wide), m3), pl.BlockSpec((1, C, wide), m3), pl.BlockSpec((1, C, wv), m3),
                pl.BlockSpec((1, C, 2 * GDN_HEADS), m3), pl.BlockSpec((1, C, 2 * GDN_HEADS), m3),
                pl.BlockSpec((1, 1, 2 * GDN_HEADS, C), m4)]

    of, ob = pl.pallas_call(
        functools.partial(_gdn_body, passes=GDN_PASSES),
        grid=(B, nc),
        in_specs=specs(fwd, fwd4) + specs(bwd, bwd4),
        out_specs=[pl.BlockSpec((1, C, wv), fwd), pl.BlockSpec((1, C, wv), bwd)],
        out_shape=[jax.ShapeDtypeStruct((B, T, wv), F32)] * 2,
        scratch_shapes=[pltpu.VMEM((2 * GDN_HEADS, GDN_DK, GDN_DV), F32)],
        compiler_params=_cparams("parallel", "arbitrary"),
        name="gdn_scan",
    )(q, k, v, gcol, bcol, grow, q, k, v, gcol, bcol, grow)
    return of, ob


def _rwkv_prep(r, lw, k, v, kk, rate, rev):
    C, D = r.shape
    N = RWKV_HEAD
    H = D // N
    incl, _ = _tri_masks(C, rev)
    tri = incl.astype(BF16)
    l1 = lw.astype(BF16)
    rem = lw - l1.astype(F32)
    l2 = rem.astype(BF16)
    l3 = (rem - l2.astype(F32)).astype(BF16)
    linc = _dotf(tri, l1) + _dotf(tri, l2) + _dotf(tri, l3)
    lexc = linc - lw
    ltot = linc[0:1, :] if rev else linc[C - 1:C, :]
    b = kk * rate
    einv = jnp.exp(-linc)
    etail = jnp.exp(ltot - linc)

    def hs(x):
        return jnp.stack([x[:, h * N:(h + 1) * N] for h in range(H)], axis=0)

    lhs = jnp.concatenate([hs(-kk * jnp.exp(lexc)), hs(r * jnp.exp(linc))], axis=1)
    rhs = jnp.concatenate([hs(b * einv), hs(k * einv)], axis=1)
    tail = jnp.concatenate([hs(b * etail), hs(k * etail)], axis=1)
    return lhs, rhs, tail, hs(v), jnp.exp(hs(ltot))


def _rwkv_body(rf, vf, kkf, lwf, kf, af, rb, vb, kkb, lwb, kb, ab, yf_ref, yb_ref, s_ref, *, passes):
    @pl.when(pl.program_id(1) == 0)
    def _():
        s_ref[...] = jnp.zeros(s_ref.shape, F32)

    C = rf.shape[1]
    H = rf.shape[2] // RWKV_HEAD
    dirs = ((rf, vf, kkf, lwf, kf, af), (rb, vb, kkb, lwb, kb, ab))
    parts = [_rwkv_prep(r_ref[0], lw_ref[0], k_ref[0], v_ref[0], kk_ref[0], a_ref[0], d == 1)
             for d, (r_ref, v_ref, kk_ref, lw_ref, k_ref, a_ref) in enumerate(dirs)]
    lhs, rhs, tail, vh, ptot = [jnp.concatenate([parts[0][j], parts[1][j]], axis=0) for j in range(5)]
    s = s_ref[...]
    shape = (2 * H, C, 2 * C)
    col = lax.broadcasted_iota(jnp.int32, shape, 2)
    ahead = lax.broadcasted_iota(jnp.int32, shape, 1) - jnp.where(col >= C, col - C, col)
    ahead = jnp.where(lax.broadcasted_iota(jnp.int32, shape, 0) < H, ahead, -ahead)
    sc = _dg(lhs, rhs, _BNT, passes)
    top = jnp.where(ahead > 0, sc[:, :C, :], 0.0)
    bot = jnp.where(ahead >= 0, sc[:, C:, :], 0.0)
    tinv = _neumann_inverse(top[:, :, :C], _BNN, passes)
    ars = _dg(lhs, s, _BNT, passes)
    zero_v = jnp.concatenate([jnp.zeros_like(vh), vh], axis=1)
    u = _dg(tinv, ars[:, :C, :] + _dg(top, zero_v, _BNN, passes), _BNN, passes)
    uv = jnp.concatenate([u, vh], axis=1)
    y = ars[:, C:, :] + _dg(bot, uv, _BNN, passes)
    s_ref[...] = s * ptot + _dg(uv, tail, _BTN, passes)
    yf_ref[0] = jnp.concatenate([y[h] for h in range(H)], axis=-1)
    yb_ref[0] = jnp.concatenate([y[H + h] for h in range(H)], axis=-1)


def rwkv_scan(r, v, kk, lw, key, rate, n_ctx):
    B, T, D = r.shape
    N = RWKV_HEAD
    C = RWKV_CHUNK
    nc = T // C
    ncc = n_ctx // C
    fwd = lambda b, i: (b, i, 0)
    bwd = lambda b, i: (b, _rev_chunk(i, ncc, nc), 0)
    blk = (1, C, D)
    return pl.pallas_call(
        functools.partial(_rwkv_body, passes=RWKV_PASSES),
        grid=(B, nc),
        in_specs=[pl.BlockSpec(blk, fwd)] * 6 + [pl.BlockSpec(blk, bwd)] * 6,
        out_specs=[pl.BlockSpec(blk, fwd), pl.BlockSpec(blk, bwd)],
        out_shape=[jax.ShapeDtypeStruct((B, T, D), F32)] * 2,
        scratch_shapes=[pltpu.VMEM((2 * (D // N), N, N), F32)],
        compiler_params=_cparams("parallel", "arbitrary"),
        name="rwkv7_scan",
    )(r, v, kk, lw[0], key[0], rate[0], r, v, kk, lw[1], key[1], rate[1])


def _moe_body(be_ref, x_ref, w1_ref, w3_ref, w2_ref, ws_ref, o_ref, w1b, w3b, w2b):
    i = pl.program_id(0)
    prev = be_ref[jnp.maximum(i - 1, 0)]

    @pl.when((i == 0) | (be_ref[i] != prev))
    def _():
        w1b[...] = w1_ref[0].astype(BF16)
        w3b[...] = w3_ref[0].astype(BF16)
        w2b[...] = w2_ref[0].astype(BF16)

    x = x_ref[...]
    h1 = jnp.dot(x, w1b[...], preferred_element_type=F32)
    h3 = jnp.dot(x, w3b[...], preferred_element_type=F32)
    hid = (h1 * jax.nn.sigmoid(h1)) * h3
    y = jnp.dot(hid.astype(BF16), w2b[...], preferred_element_type=F32)
    o_ref[...] = y * ws_ref[...]


def moe_experts(xs, slot_w, blk_e, w1, w3, w2):
    n_slots, D = xs.shape
    hid = w1.shape[-1]
    n_blocks = n_slots // MOE_BLOCK
    return pl.pallas_call(
        _moe_body,
        grid_spec=pltpu.PrefetchScalarGridSpec(
            num_scalar_prefetch=1,
            grid=(n_blocks,),
            in_specs=[pl.BlockSpec((MOE_BLOCK, D), lambda i, be: (i, 0)),
                      pl.BlockSpec((1, D, hid), lambda i, be: (be[i], 0, 0)),
                      pl.BlockSpec((1, D, hid), lambda i, be: (be[i], 0, 0)),
                      pl.BlockSpec((1, hid, D), lambda i, be: (be[i], 0, 0)),
                      pl.BlockSpec((MOE_BLOCK, 1), lambda i, be: (i, 0))],
            out_specs=pl.BlockSpec((MOE_BLOCK, D), lambda i, be: (i, 0)),
            scratch_shapes=[pltpu.VMEM((D, hid), BF16), pltpu.VMEM((D, hid), BF16), pltpu.VMEM((hid, D), BF16)],
        ),
        out_shape=jax.ShapeDtypeStruct((n_slots, D), F32),
        compiler_params=_cparams("arbitrary"),
        name="moe_experts",
    )(blk_e, xs, w1, w3, w2, slot_w)


def _route_body(lg_ref, bias_ref, out_ref, cnt_ref, run_ref, *, tm):
    @pl.when(pl.program_id(0) == 0)
    def _():
        run_ref[...] = jnp.zeros(run_ref.shape, F32)

    x = lg_ref[...] + bias_ref[...]
    lane = lax.broadcasted_iota(jnp.int32, x.shape, 1)
    far = 1 << 20

    def first_lane(hit):
        return jnp.min(jnp.where(hit, lane, far), axis=-1, keepdims=True)

    def masked_softmax(mask):
        xm = jnp.where(mask, x, -1e30)
        e = jnp.where(mask, jnp.exp(xm - jnp.max(xm, axis=-1, keepdims=True)), 0.0)
        return e / jnp.sum(e, axis=-1, keepdims=True)

    is_group = lane < MOE_GROUPS
    pg = masked_softmax(is_group)
    pg_top = jnp.max(pg, axis=-1, keepdims=True)
    g_idx = first_lane(is_group & (pg == pg_top))
    lo = MOE_GROUPS + MOE_PER_GROUP * g_idx
    in_group = (lane >= lo) & (lane < lo + MOE_PER_GROUP)
    pe = masked_softmax(in_group)
    p1 = jnp.max(pe, axis=-1, keepdims=True)
    l1 = first_lane(in_group & (pe == p1))
    rest_ok = in_group & (lane != l1)
    rest = jnp.where(rest_ok, pe, -1.0)
    p2 = jnp.max(rest, axis=-1, keepdims=True)
    l2 = first_lane(rest_ok & (rest == p2))
    psum = p1 + p2
    w1 = pg_top * p1 / psum
    w2 = pg_top * p2 / psum

    oh1 = (lane == l1).astype(F32)
    oh2 = (lane == l2).astype(F32)
    both = oh1 + oh2
    earlier = (lax.broadcasted_iota(jnp.int32, (tm, tm), 0) > lax.broadcasted_iota(jnp.int32, (tm, tm), 1))
    base = _dotf(earlier.astype(BF16), both.astype(BF16)) + run_ref[...]
    r1 = jnp.sum(base * oh1, axis=-1, keepdims=True)
    r2 = jnp.sum(base * oh2, axis=-1, keepdims=True)
    run_ref[...] = run_ref[...] + jnp.sum(both, axis=0, keepdims=True)
    cnt_ref[...] = run_ref[...]
    cols = ((l1 - MOE_GROUPS).astype(F32), (l2 - MOE_GROUPS).astype(F32), r1, r2, w1, w2)
    out = jnp.zeros(x.shape, F32)
    for j, c in enumerate(cols):
        out = jnp.where(lane == j, c, out)
    out_ref[...] = out


def moe_route(logits, b_group, b_expert):
    N, W = logits.shape
    tm = _pick(N, (512, 256, 128, 64, 32, 16, 8))
    bias = _pad_cols(jnp.concatenate([b_group, b_expert])[None, :], W)
    return pl.pallas_call(
        functools.partial(_route_body, tm=tm),
        grid=(N // tm,),
        in_specs=[pl.BlockSpec((tm, W), lambda i: (i, 0)), pl.BlockSpec((1, W), lambda i: (0, 0))],
        out_specs=[pl.BlockSpec((tm, W), lambda i: (i, 0)), pl.BlockSpec((1, W), lambda i: (0, 0))],
        out_shape=[jax.ShapeDtypeStruct((N, W), F32), jax.ShapeDtypeStruct((1, W), F32)],
        scratch_shapes=[pltpu.VMEM((1, W), F32)],
        compiler_params=_cparams("arbitrary"),
        name="moe_route",
    )(logits, bias)


def hier_moe(tokens, logits, b_group, b_expert, w1, w3, w2):
    N, D = tokens.shape
    route, cnt = moe_route(logits, b_group, b_expert)
    eid = route[:, 0:MOE_TOPK].astype(jnp.int32).reshape(-1)
    rank = route[:, MOE_TOPK:2 * MOE_TOPK].astype(jnp.int32).reshape(-1)
    wts = route[:, 2 * MOE_TOPK:3 * MOE_TOPK]
    counts = cnt[0, MOE_GROUPS:MOE_GROUPS + MOE_EXPERTS].astype(jnp.int32)
    tok = jnp.repeat(jnp.arange(N, dtype=jnp.int32), MOE_TOPK)
    A = N * MOE_TOPK
    padded = (counts + MOE_BLOCK - 1) // MOE_BLOCK * MOE_BLOCK
    pend = jnp.cumsum(padded)
    dest = (pend - padded)[eid] + rank
    n_blocks = -(-A // MOE_BLOCK) + MOE_EXPERTS
    n_slots = n_blocks * MOE_BLOCK
    slot_src = jnp.full((n_slots,), -1, jnp.int32).at[dest].set(jnp.arange(A, dtype=jnp.int32))
    src = jnp.maximum(slot_src, 0)
    slot_tok = tok[src] * (slot_src >= 0)
    slot_w = jnp.where(slot_src >= 0, wts.reshape(-1)[src], 0.0)
    starts = jnp.arange(n_blocks, dtype=jnp.int32)[:, None] * MOE_BLOCK
    blk_e = jnp.minimum(jnp.sum((pend[None, :] <= starts).astype(jnp.int32), axis=1), MOE_EXPERTS - 1)
    xs = tokens[slot_tok]
    ys = moe_experts(xs, slot_w[:, None], blk_e, w1, w3, w2)
    d2 = dest.reshape(N, MOE_TOPK)
    return ys[d2[:, 0]] + ys[d2[:, 1]]


def _rope_tables(n_lat, n_ctx):
    rows = n_lat // GRID_W
    row = jnp.repeat(jnp.arange(rows, dtype=F32), GRID_W)
    col = jnp.tile(jnp.arange(GRID_W, dtype=F32), rows)
    n_freq = MLA_ROPE // 4
    inv = ROPE_BASE ** (-jnp.arange(n_freq, dtype=F32) / n_freq)
    ang = jnp.stack([row[:, None] * inv, col[:, None] * inv], axis=1)
    cos, sin = jnp.cos(ang), jnp.sin(ang)
    zf = jnp.zeros((n_lat, n_freq), F32)
    lat = lambda parts, fill: jnp.concatenate(
        [jnp.full((n_lat, MLA_NOPE), fill, F32)] + parts + [jnp.full((n_lat, MLA_PAD - MLA_QK), fill, F32)], axis=1)
    c = lat([cos[:, 0], cos[:, 0], cos[:, 1], cos[:, 1]], 1.0)
    s_lo = lat([-sin[:, 0], zf, -sin[:, 1], zf], 0.0)
    s_hi = lat([zf, sin[:, 0], zf, sin[:, 1]], 0.0)
    ctx = lambda fill: jnp.full((n_ctx, MLA_PAD), fill, F32)
    return jnp.concatenate([ctx(1.0), c], 0), jnp.concatenate([ctx(0.0), s_lo + s_hi], 0)


def _const_spec(shape):
    return pl.BlockSpec(shape, lambda i: (0,) * len(shape), pipeline_mode=pl.Buffered(1))


def _normmod(x, gain, shift, scale):
    return x * lax.rsqrt(jnp.mean(x * x, -1, keepdims=True) + EPS) * gain * (1 + scale) + shift


def _head_indicator(D, N):
    e = (jnp.arange(D)[:, None] // N == jnp.arange(128)[None, :]).astype(BF16)
    return e, e.T


def _seg_dot(x, e):
    xh, xl = _split(x)
    return jnp.dot(xh, e, preferred_element_type=F32) + jnp.dot(xl, e, preferred_element_type=F32)


def _dotf(a, b):
    return jnp.dot(a, b, preferred_element_type=F32)


def _rwkv_pre_body(*refs, tm, blocks_per_batch, ctx_blocks, vres):
    (h_ref, hp_ref, hn_ref, m_ref, gain_ref, mu_ref, w0_ref, a0_ref, kk_ref, ka_ref, e_ref, et_ref,
     wr_ref, wk_ref, wv_ref, w1_ref, w2_ref, a1_ref, a2_ref, g1_ref, g2_ref) = refs[:21]
    rest = refs[21:]
    if vres:
        v0_ref, v1_ref, v2_ref, vf_ref = rest[:4]
        rest = rest[4:]
    r_o, v_o, kk_o, lw0_o, lw1_o, k0_o, k1_o, ra0_o, ra1_o, gate_o = rest

    tb = pl.program_id(0) % blocks_per_batch
    seg_start = (tb == 0) | (tb == ctx_blocks)
    seg_end = (tb == ctx_blocks - 1) | (tb == blocks_per_batch - 1)
    shift, scale, gain = m_ref[0, 0:1, :], m_ref[0, 1:2, :], gain_ref[...]
    u = _normmod(h_ref[...], gain, shift, scale)
    up = jnp.where(seg_start, 0.0, _normmod(hp_ref[7:8, :], gain, shift, scale))
    un = jnp.where(seg_end, 0.0, _normmod(hn_ref[0:1, :], gain, shift, scale))
    row = lax.broadcasted_iota(jnp.int32, (tm, 1), 0)
    u_prev = jnp.where(row == 0, up, pltpu.roll(u, 1, 0))
    u_next = jnp.where(row == tm - 1, un, pltpu.roll(u, tm - 1, 0))
    xx = 0.5 * (u_prev + u_next) - u
    xr, xw, xk, xv, xa, xg = [(u + xx * mu_ref[j:j + 1, :]).astype(BF16) for j in range(6)]

    r = _dotf(xr, wr_ref[...])
    k = _dotf(xk, wk_ref[...])
    v = _dotf(xv, wv_ref[...])
    if vres:
        lo = _dotf(xv, v1_ref[...]).astype(BF16)
        v = v + (vf_ref[...] - v) * jax.nn.sigmoid(v0_ref[...] + _dotf(lo, v2_ref[...]))
    tl = jnp.tanh(_dotf(xw, w1_ref[...])).astype(BF16)
    al = _dotf(xa, a1_ref[...]).astype(BF16)
    gl = jax.nn.sigmoid(_dotf(xg, g1_ref[...])).astype(BF16)
    gate_o[...] = _dotf(gl, g2_ref[...])
    kx = k * kk_ref[...]
    inv = lax.rsqrt(_seg_dot(kx * kx, e_ref[...]) + EPS)
    r_o[...] = r
    v_o[...] = v
    kk_o[...] = kx * _seg_dot(inv, et_ref[...])
    for d, (lw_o, k_o, ra_o) in enumerate(((lw0_o, k0_o, ra0_o), (lw1_o, k1_o, ra1_o))):
        z = w0_ref[d:d + 1, :] + _dotf(tl, w2_ref[d])
        lw_o[...] = -math.exp(-0.5) * jax.nn.sigmoid(z)
        a = jax.nn.sigmoid(a0_ref[d:d + 1, :] + _dotf(al, a2_ref[d]))
        ra_o[...] = a
        k_o[...] = k * (1 + (a - 1) * ka_ref[...])


def _row_tile(seg):
    return _pick(seg, (256, 128, 64, 32, 16, 8))


def _pad_cols(w, n):
    return jnp.pad(w, ((0, 0), (0, n - w.shape[1])))


def _pad_rows(w, n):
    return jnp.pad(w, ((0, n - w.shape[0]), (0, 0)))


def rwkv_pre(h, m_seg, seg, T, n_ctx, gain, mu, wr, wk, wv, w0, w1, w2, a0, a1, a2, g1, g2, k_k, k_a, vres, v_first):
    M, D = h.shape
    tm = _row_tile(seg)
    lora = w1.shape[-1]
    e, et = _head_indicator(D, RWKV_HEAD)
    zero = jnp.zeros((lora, D), F32)
    w2p = jnp.stack([jnp.concatenate([w2[0], zero], 0), jnp.concatenate([zero, w2[1]], 0)]).astype(BF16)
    a2p = jnp.stack([jnp.concatenate([a2[0], zero], 0), jnp.concatenate([zero, a2[1]], 0)]).astype(BF16)
    gp = -(-g1.shape[1] // 128) * 128
    row = lambda a: a.reshape(1, D)
    consts = [row(gain), mu, w0, a0, row(k_k), row(k_a), e, et,
              wr.astype(BF16), wk.astype(BF16), wv.astype(BF16),
              jnp.concatenate([w1[0], w1[1]], 1).astype(BF16), w2p,
              jnp.concatenate([a1[0], a1[1]], 1).astype(BF16), a2p,
              _pad_cols(g1, gp).astype(BF16), _pad_rows(g2, gp).astype(BF16)]
    row_spec = pl.BlockSpec((tm, D), lambda i: (i, 0))
    last8 = M // 8 - 1
    in_specs = [row_spec,
                pl.BlockSpec((8, D), lambda i: (jnp.maximum(i * (tm // 8) - 1, 0), 0)),
                pl.BlockSpec((8, D), lambda i: (jnp.minimum((i + 1) * (tm // 8), last8), 0)),
                pl.BlockSpec((1, 6, D), lambda i: (i * tm // seg, 0, 0))]
    in_specs += [_const_spec(c.shape) for c in consts]
    args = [h, h, h, m_seg] + consts
    if vres is not None:
        v0, v1, v2 = vres
        extra = [row(v0), _pad_cols(v1, 128).astype(BF16), _pad_rows(v2, 128).astype(BF16)]
        in_specs += [_const_spec(c.shape) for c in extra] + [row_spec]
        args += extra + [v_first]
    return pl.pallas_call(
        functools.partial(_rwkv_pre_body, tm=tm, blocks_per_batch=T // tm, ctx_blocks=n_ctx // tm,
                          vres=vres is not None),
        grid=(M // tm,),
        in_specs=in_specs,
        out_specs=[row_spec] * 10,
        out_shape=[jax.ShapeDtypeStruct((M, D), F32)] * 10,
        compiler_params=_cparams("parallel"),
        name="rwkv7_pre",
    )(*args)


def _post_tail(xo, h_ref, m_ref, gain_ref, w_ref, wrt_ref, h_o, f_o, lg_o):
    h_new = h_ref[...] + m_ref[0, 2:3, :] * _dotf(xo, w_ref[...])
    h_o[...] = h_new
    f = _normmod(h_new, gain_ref[...], m_ref[0, 3:4, :], m_ref[0, 4:5, :])
    f_o[...] = f.astype(BF16)
    lg_o[...] = _dg(f, wrt_ref[...], _NN, 3)


def _rwkv_post_body(yf_ref, yb_ref, r_ref, k0_ref, k1_ref, v_ref, gate_ref, lnw_ref, lnb_ref, rk_ref, e_ref, et_ref,
                    h_ref, m_ref, gain_ref, w_ref, wrt_ref, h_o, f_o, lg_o):
    e, et = e_ref[...], et_ref[...]
    inv_n = 1.0 / RWKV_HEAD
    y = yf_ref[...] + yb_ref[...]
    yc = y - _seg_dot(_seg_dot(y, e) * inv_n, et)
    var = _seg_dot(_seg_dot(yc * yc, e) * inv_n, et)
    yn = yc * lax.rsqrt(var + GN_EPS) * lnw_ref[...] + lnb_ref[...]
    k_bonus = 0.5 * (k0_ref[...] + k1_ref[...])
    bonus = _seg_dot(_seg_dot(r_ref[...] * k_bonus * rk_ref[...], e), et) * v_ref[...]
    xo = ((yn + bonus) * gate_ref[...]).astype(BF16)
    _post_tail(xo, h_ref, m_ref, gain_ref, w_ref, wrt_ref, h_o, f_o, lg_o)


def _post_call(body, name, row_args, consts, h, m_seg, seg, gain, w_out, w_router):
    M, D = h.shape
    tm = _row_tile(seg)
    row_spec = lambda a: pl.BlockSpec((tm, a.shape[1]), lambda i: (i, 0))
    tail = [gain.reshape(1, D), w_out.astype(BF16), w_router]
    in_specs = ([row_spec(a) for a in row_args] + [_const_spec(c.shape) for c in consts] +
                [row_spec(h), pl.BlockSpec((1, 6, D), lambda i: (i * tm // seg, 0, 0))] +
                [_const_spec(c.shape) for c in tail])
    nr = w_router.shape[1]
    return pl.pallas_call(
        body,
        grid=(M // tm,),
        in_specs=in_specs,
        out_specs=[pl.BlockSpec((tm, D), lambda i: (i, 0)), pl.BlockSpec((tm, D), lambda i: (i, 0)),
                   pl.BlockSpec((tm, nr), lambda i: (i, 0))],
        out_shape=[jax.ShapeDtypeStruct((M, D), F32), jax.ShapeDtypeStruct((M, D), BF16),
                   jax.ShapeDtypeStruct((M, nr), F32)],
        compiler_params=_cparams("parallel"),
        name=name,
    )(*row_args, *consts, h, m_seg, *tail)


def _hy_post_body(a_ref, of_ref, ob_ref, z_ref, og_ref, h_ref, m_ref, gain_ref, w_ref, wrt_ref, h_o, f_o, lg_o):
    o = of_ref[...] + ob_ref[...]
    z = z_ref[...]
    parts = [a_ref[...]]
    for hd in range(GDN_HEADS):
        sl = slice(hd * GDN_DV, (hd + 1) * GDN_DV)
        oh, zh = o[:, sl], z[:, sl]
        on = oh * lax.rsqrt(jnp.mean(oh * oh, -1, keepdims=True) + EPS) * og_ref[...]
        parts.append(on * (zh * jax.nn.sigmoid(zh)))
    xo = jnp.concatenate(parts, axis=-1).astype(BF16)
    _post_tail(xo, h_ref, m_ref, gain_ref, w_ref, wrt_ref, h_o, f_o, lg_o)


def hy_post(a, of, ob, z, out_g, h, m_seg, seg, gain, w_out, w_router):
    return _post_call(_hy_post_body, "hybrid_post", [a, of, ob, z], [out_g.reshape(1, -1)], h, m_seg, seg,
                      gain, w_out, w_router)


def _hy_pre_body(h_ref, m_ref, gain_ref, c_ref, s_ref, wq1, wkv1, wpe, wpe2, wgq, wz, wab, qag, kvag,
                 wqb, wqb2, wkn, wv, qng, qng2, kng, kng2, q_o, k_o, v_o, gq_o, z_o, ab_o):
    u = _normmod(h_ref[...], gain_ref[...], m_ref[0, 0:1, :], m_ref[0, 1:2, :]).astype(BF16)
    gq_o[...] = _dotf(u, wgq[...])
    z_o[...] = _dotf(u, wz[...])
    ab_o[...] = _dotf(u, wab[...])
    cq = _dotf(u, wq1[...])
    ckv = _dotf(u, wkv1[...])
    pe = _dotf(u, wpe[...])
    pe2 = _dotf(u, wpe2[...])
    cq = (cq * lax.rsqrt(jnp.mean(cq * cq, -1, keepdims=True) + EPS) * qag[...]).astype(BF16)
    ckv = (ckv * lax.rsqrt(jnp.mean(ckv * ckv, -1, keepdims=True) + EPS) * kvag[...]).astype(BF16)
    q = _dotf(cq, wqb[...])
    q2 = _dotf(cq, wqb2[...])
    kn = _dotf(ckv, wkn[...])
    vv = _dotf(ckv, wv[...])
    c, s = c_ref[...], s_ref[...]
    qc, qs = qng[...] * c, qng2[...] * s
    kc, ks = kng[...] * c, kng2[...] * s
    k_rot = pe2 * ks
    one_col = (lax.broadcasted_iota(jnp.int32, (1, MLA_PAD), 1) == MLA_V).astype(F32)
    inv_d = 1.0 / MLA_QK

    def inv_rms(t):
        return lax.rsqrt(jnp.sum(t * t, -1, keepdims=True) * inv_d + EPS)

    for hd in range(MLA_HEADS):
        sl = slice(hd * MLA_PAD, (hd + 1) * MLA_PAD)
        qh = q[:, sl]
        kh = kn[:, sl] + pe
        q_o[:, sl] = (inv_rms(qh) * (qh * qc + q2[:, sl] * qs)).astype(BF16)
        k_o[:, sl] = (inv_rms(kh) * (kh * kc + k_rot)).astype(BF16)
        v_o[:, sl] = (vv[:, sl] + one_col).astype(BF16)


def _pad_heads(w, heads, width, to):
    K = w.shape[0]
    return jnp.pad(w.reshape(K, heads, width), ((0, 0), (0, 0), (0, to - width))).reshape(K, heads * to)


def hy_pre(h, m_seg, seg, T, gain, rope, w_in, qa_g, w_qb, kva_g, w_kvb, qn_g, kn_g):
    M, D = h.shape
    tm = _row_tile(seg)
    H = MLA_HEADS
    c0, c1, c2 = MLA_COLS, MLA_COLS + GDN_QKV, MLA_COLS + GDN_QKV + GDN_Z
    kvl = MLA_Q_LORA + MLA_KV_LORA
    wb = w_in.astype(BF16)
    wpe = jnp.pad(wb[:, kvl:c0], ((0, 0), (MLA_NOPE, MLA_PAD - MLA_QK)))
    wkv = w_kvb.reshape(MLA_KV_LORA, H, MLA_NOPE + MLA_V)
    pad1 = lambda g: jnp.pad(g, (0, MLA_PAD - MLA_QK)).reshape(1, MLA_PAD)
    lane = jnp.arange(MLA_PAD)
    rot = (lane >= MLA_NOPE) & (lane < MLA_QK)
    n_freq = MLA_ROPE // 4
    partner = jnp.where(rot, jnp.where(((lane - MLA_NOPE) // n_freq) % 2 == 0, lane + n_freq, lane - n_freq), lane)
    wqb_pad = _pad_heads(w_qb, H, MLA_QK, MLA_PAD).astype(BF16)
    wqb2 = wqb_pad.reshape(MLA_Q_LORA, H, MLA_PAD)[:, :, partner].reshape(MLA_Q_LORA, H * MLA_PAD)
    consts = [wb[:, :MLA_Q_LORA], wb[:, MLA_Q_LORA:kvl], wpe, wpe[:, partner], wb[:, c0:c1], wb[:, c1:c2],
              _pad_cols(wb[:, c2:], 128), qa_g.reshape(1, -1), kva_g.reshape(1, -1),
              wqb_pad, wqb2,
              _pad_heads(wkv[:, :, :MLA_NOPE].reshape(MLA_KV_LORA, -1), H, MLA_NOPE, MLA_PAD).astype(BF16),
              _pad_heads(wkv[:, :, MLA_NOPE:].reshape(MLA_KV_LORA, -1), H, MLA_V, MLA_PAD).astype(BF16),
              pad1(qn_g), pad1(qn_g)[:, partner], pad1(kn_g), pad1(kn_g)[:, partner]]
    bpb = T // tm
    row = lambda n: pl.BlockSpec((tm, n), lambda i: (i, 0))
    tab = pl.BlockSpec((tm, MLA_PAD), lambda i: (i % bpb, 0))
    in_specs = ([row(D), pl.BlockSpec((1, 6, D), lambda i: (i * tm // seg, 0, 0)), _const_spec((1, D)), tab, tab]
                + [_const_spec(c.shape) for c in consts])
    wide = H * MLA_PAD
    return pl.pallas_call(
        _hy_pre_body,
        grid=(M // tm,),
        in_specs=in_specs,
        out_specs=[row(wide), row(wide), row(wide), row(GDN_QKV), row(GDN_Z), row(128)],
        out_shape=[jax.ShapeDtypeStruct((M, wide), BF16)] * 3 + [jax.ShapeDtypeStruct((M, GDN_QKV), F32),
                   jax.ShapeDtypeStruct((M, GDN_Z), F32), jax.ShapeDtypeStruct((M, 128), F32)],
        compiler_params=_cparams("parallel"),
        name="hybrid_pre",
    )(h, m_seg, gain.reshape(1, D), *rope, *consts)


def _gdn_prep_body(x_ref, xp_ref, xn_ref, w_ref, q_o, k_o, v_o, *, tm, blocks_per_batch, ctx_blocks):
    tb = pl.program_id(0) % blocks_per_batch
    seg_start = (tb == 0) | (tb == ctx_blocks)
    seg_end = (tb == ctx_blocks - 1) | (tb == blocks_per_batch - 1)
    x = x_ref[...]
    xp = jnp.where(seg_start, 0.0, xp_ref[...])
    xn = jnp.where(seg_end, 0.0, xn_ref[...])
    row = lax.broadcasted_iota(jnp.int32, (tm, 1), 0)
    half = GDN_CONV // 2
    acc = x * w_ref[half:half + 1, :]
    for s in range(1, half + 1):
        before = pltpu.roll(x, s, 0)
        after = pltpu.roll(x, tm - s, 0)
        for r in range(s):
            before = jnp.where(row == r, xp[8 - s + r:8 - s + r + 1, :], before)
            after = jnp.where(row == tm - s + r, xn[r:r + 1, :], after)
        acc = acc + before * w_ref[half - s:half - s + 1, :] + after * w_ref[half + s:half + s + 1, :]
    y = acc * jax.nn.sigmoid(acc)
    nk = GDN_HEADS * GDN_DK
    for hd in range(GDN_HEADS):
        sl = slice(hd * GDN_DK, (hd + 1) * GDN_DK)
        qh = y[:, sl]
        kh = y[:, nk + hd * GDN_DK:nk + (hd + 1) * GDN_DK]
        q_o[:, sl] = qh * lax.rsqrt(jnp.sum(qh * qh, -1, keepdims=True) + EPS) * GDN_DK ** -0.5
        k_o[:, sl] = kh * lax.rsqrt(jnp.sum(kh * kh, -1, keepdims=True) + EPS)
    v_o[...] = y[:, 2 * nk:]


def gdn_prep(gq, conv_w, seg, T, n_ctx):
    M, W = gq.shape
    tm = _row_tile(seg)
    last8 = M // 8 - 1
    nk = GDN_HEADS * GDN_DK
    row = lambda n: pl.BlockSpec((tm, n), lambda i: (i, 0))
    return pl.pallas_call(
        functools.partial(_gdn_prep_body, tm=tm, blocks_per_batch=T // tm, ctx_blocks=n_ctx // tm),
        grid=(M // tm,),
        in_specs=[row(W),
                  pl.BlockSpec((8, W), lambda i: (jnp.maximum(i * (tm // 8) - 1, 0), 0)),
                  pl.BlockSpec((8, W), lambda i: (jnp.minimum((i + 1) * (tm // 8), last8), 0)),
                  _const_spec(conv_w.shape)],
        out_specs=[row(nk), row(nk), row(W - 2 * nk)],
        out_shape=[jax.ShapeDtypeStruct((M, nk), F32), jax.ShapeDtypeStruct((M, nk), F32),
                   jax.ShapeDtypeStruct((M, W - 2 * nk), F32)],
        compiler_params=_cparams("parallel"),
        name="gdn_prep",
    )(gq, gq, gq, conv_w)


def rwkv_post(yf, yb, r, k0, k1, v, gate, ln_w, ln_b, r_k, h, m_seg, seg, gain, wo, w_router):
    D = h.shape[1]
    e, et = _head_indicator(D, RWKV_HEAD)
    consts = [ln_w.reshape(1, D), ln_b.reshape(1, D), r_k.reshape(1, D), e, et]
    return _post_call(_rwkv_post_body, "rwkv7_post", [yf, yb, r, k0, k1, v, gate], consts, h, m_seg, seg,
                      gain, wo, w_router)


def kernel(x, c, ctx, c_ctx, ada_w, ada_b, norm_mix, norm_ffn, hy_w_in, hy_w_out, mla_qa_norm, mla_w_qb, mla_kva_norm, mla_w_kvb, mla_q_norm, mla_k_norm, gdn_conv, gdn_a_log, gdn_dt_bias, gdn_out_norm, rk_mu, rk_wr, rk_wk, rk_wv, rk_wo, rk_w0, rk_w1, rk_w2, rk_a0, rk_a1, rk_a2, rk_g1, rk_g2, rk_kk, rk_ka, rk_rk, rk_ln_w, rk_ln_b, rk_v0, rk_v1, rk_v2, moe_w_group, moe_b_group, moe_w_expert, moe_b_expert, moe_w1, moe_w3, moe_w2):
    B, S, D = x.shape
    L = ctx.shape[1]
    T = L + S
    depth = ada_w.shape[0]
    rope = _rope_tables(S, L)
    n_rows = -(-(B + 1) // 8) * 8
    sc = jnp.concatenate([jax.nn.silu(c), jax.nn.silu(c_ctx)[None], jnp.zeros((n_rows - B - 1, D), F32)], 0)
    M = B * T
    h = jnp.concatenate([ctx, x], axis=1).reshape(M, D)
    seg = math.gcd(L, S)
    nseg = T // seg
    v_first = None
    for l in range(depth):
        m = mm(sc, ada_w[l], hi=True) + ada_b[l]
        m_lat = jnp.broadcast_to(m[:B].reshape(B, 1, 6, D), (B, S // seg, 6, D))
        m_ctx = jnp.broadcast_to(m[B].reshape(1, 1, 6, D), (B, L // seg, 6, D))
        m_seg = jnp.concatenate([m_ctx, m_lat], axis=1).reshape(B * nseg, 6, D)

        def mod(i, m_seg=m_seg):
            return m_seg[:, None, i, :]

        router = _pad_cols(jnp.concatenate([moe_w_group[l], moe_w_expert[l]], axis=1), 128)
        j = l // 2
        b3 = lambda a: a.reshape(B, T, a.shape[-1])
        if l % 2 == 0:
            q, k, v, gq, z, ab = hy_pre(h, m_seg, seg, T, norm_mix[l], rope, hy_w_in[j], mla_qa_norm[j],
                                        mla_w_qb[j], mla_kva_norm[j], mla_w_kvb[j], mla_q_norm[j], mla_k_norm[j])
            q, k, v = b3(q), b3(k), b3(v)
            a_lat = attention(q[:, L:], k, v)
            a_ctx = attention(q[:, :L], k[:, :L], v[:, :L])
            a = jnp.concatenate([a_ctx, a_lat], axis=1).reshape(M, -1)
            gq_, gk_, gv_ = gdn_prep(gq, gdn_conv[j], seg, T, L)
            ab = ab[:, :GDN_AB].reshape(B, T, 2, 2, GDN_HEADS)
            g = -jnp.exp(gdn_a_log[j]) * jax.nn.softplus(ab[:, :, :, 0] + gdn_dt_bias[j])
            beta = jax.nn.sigmoid(ab[:, :, :, 1])
            of, ob = gdn_scan(b3(gq_), b3(gk_), b3(gv_), g, beta, L)
            h, f, logits = hy_post(a, of.reshape(M, -1), ob.reshape(M, -1), z, gdn_out_norm[j], h, m_seg, seg,
                                   norm_ffn[l], hy_w_out[j], router)
        else:
            vres = None if j == 0 else (rk_v0[j - 1], rk_v1[j - 1], rk_v2[j - 1])
            r, v, kk, lw0, lw1, k0, k1, ra0, ra1, gate = rwkv_pre(
                h, m_seg, seg, T, L, norm_mix[l], rk_mu[j], rk_wr[j], rk_wk[j], rk_wv[j], rk_w0[j], rk_w1[j],
                rk_w2[j], rk_a0[j], rk_a1[j], rk_a2[j], rk_g1[j], rk_g2[j], rk_kk[j], rk_ka[j], vres, v_first)
            if j == 0:
                v_first = v
            b3 = lambda a: a.reshape(B, T, D)
            yf, yb = rwkv_scan(b3(r), b3(v), b3(kk), [b3(lw0), b3(lw1)], [b3(k0), b3(k1)], [b3(ra0), b3(ra1)], L)
            h, f, logits = rwkv_post(yf.reshape(M, D), yb.reshape(M, D), r, k0, k1, v, gate, rk_ln_w[j], rk_ln_b[j],
                                     rk_rk[j], h, m_seg, seg, norm_ffn[l], rk_wo[j], router)
        moe_out = hier_moe(f, logits, moe_b_group[l], moe_b_expert[l], moe_w1[l], moe_w3[l], moe_w2[l])
        h = (h.reshape(B * nseg, seg, D) + mod(5) * moe_out.reshape(B * nseg, seg, D)).reshape(M, D)
    return h.reshape(B, T, D)[:, L:]
```

```python
import functools
import math

import jax
import jax.numpy as jnp
from jax import lax
from jax.experimental import pallas as pl
from jax.experimental.pallas import tpu as pltpu

F32 = jnp.float32
BF16 = jnp.bfloat16
HI = lax.Precision.HIGHEST

DEPTH = 4
GRID_W = 64
EPS = 1e-6

MLA_HEADS = 8
MLA_Q_LORA = 256
MLA_KV_LORA = 128
MLA_NOPE = 64
MLA_ROPE = 32
MLA_V = 64
MLA_QK = MLA_NOPE + MLA_ROPE
MLA_SCALE = MLA_QK ** -0.5
ROPE_BASE = 10000.0
MLA_PAD = 128

GDN_HEADS = 4
GDN_DK = 128
GDN_DV = 128
GDN_CONV = 5
GDN_CHUNK = 64

RWKV_HEAD = 64
RWKV_CHUNK = 64
GN_EPS = 64e-5

MOE_GROUPS = 4
MOE_PER_GROUP = 8
MOE_EXPERTS = MOE_GROUPS * MOE_PER_GROUP
MOE_TOPK = 2
MOE_BLOCK = 256

MLA_COLS = MLA_Q_LORA + MLA_KV_LORA + MLA_ROPE
GDN_QKV = GDN_HEADS * (2 * GDN_DK + GDN_DV)
GDN_Z = GDN_HEADS * GDN_DV
GDN_AB = 2 * 2 * GDN_HEADS

VMEM_LIMIT_BYTES = 48 * 1024 * 1024

GDN_PASSES = 1
RWKV_PASSES = 1


def _cparams(*sem):
    return pltpu.CompilerParams(dimension_semantics=sem, vmem_limit_bytes=VMEM_LIMIT_BYTES)


def _pick(n, cands):
    for c in cands:
        if n % c == 0:
            return c
    return n


def _split(a):
    hi = a.astype(BF16)
    lo = (a - hi.astype(F32)).astype(BF16)
    return hi, lo


def _dg(a, b, dn, passes):
    if passes == 6:
        return lax.dot_general(a, b, dn, precision=HI, preferred_element_type=F32)
    if passes == 1:
        return lax.dot_general(a.astype(BF16), b.astype(BF16), dn, preferred_element_type=F32)
    ah, al = _split(a)
    bh, bl = _split(b)
    d = functools.partial(lax.dot_general, dimension_numbers=dn, preferred_element_type=F32)
    return d(ah, bh) + d(al, bh) + d(ah, bl)


_NN = (((1,), (0,)), ((), ()))
_NT = (((1,), (1,)), ((), ()))
_TN = (((0,), (0,)), ((), ()))
_BNN = (((2,), (1,)), ((0,), (0,)))
_BNT = (((2,), (2,)), ((0,), (0,)))
_BTN = (((1,), (1,)), ((0,), (0,)))


def _mm_body(x_ref, w_ref, o_ref, *, hi):
    if hi:
        o_ref[...] = jnp.dot(x_ref[...], w_ref[...], precision=HI, preferred_element_type=F32)
    else:
        o_ref[...] = jnp.dot(x_ref[...].astype(BF16), w_ref[...].astype(BF16),
                             preferred_element_type=F32)


def mm(x, w, hi=False):
    M, K = x.shape
    N = w.shape[1]
    tm = _pick(M, (512, 256, 128, 64, 32, 16, 8))
    tn = _pick(N, (512, 384, 256, 128))
    return pl.pallas_call(
        functools.partial(_mm_body, hi=hi),
        grid=(M // tm, N // tn),
        in_specs=[pl.BlockSpec((tm, K), lambda i, j: (i, 0)),
                  pl.BlockSpec((K, tn), lambda i, j: (0, j))],
        out_specs=pl.BlockSpec((tm, tn), lambda i, j: (i, j)),
        out_shape=jax.ShapeDtypeStruct((M, N), F32),
        compiler_params=_cparams("parallel", "parallel"),
        name="dense_mm",
    )(x, w)


def _attn_body(q_ref, k_ref, v_ref, o_ref, m_ref, acc_ref, *, c2):
    ki = pl.program_id(3)

    @pl.when(ki == 0)
    def _():
        m_ref[...] = jnp.full(m_ref.shape, -1e30, F32)
        acc_ref[...] = jnp.zeros(acc_ref.shape, F32)

    heads = range(2)
    sl = [slice(h * MLA_PAD, (h + 1) * MLA_PAD) for h in heads]
    m_prev = [m_ref[h] for h in heads]
    acc_prev = [acc_ref[h] for h in heads]
    s = [lax.dot_general(q_ref[0, :, sl[h]], k_ref[0, :, sl[h]], _NT, preferred_element_type=F32) for h in heads]
    m_new, alpha, p = [], [], []
    reps = s[0].shape[1] // MLA_PAD
    for h in heads:
        m_new.append(jnp.maximum(m_prev[h], jnp.max(s[h], axis=-1, keepdims=True)))
        alpha.append(jnp.exp2((m_prev[h] - m_new[h]) * c2))
        x = (s[h] - jnp.tile(m_new[h], (1, reps))) * c2
        p.append(jnp.exp2(x).astype(BF16))
    pv = [jnp.dot(p[h], v_ref[0, :, sl[h]], preferred_element_type=F32) for h in heads]
    for h in heads:
        acc_ref[h] = alpha[h] * acc_prev[h] + pv[h]
        m_ref[h] = m_new[h]

    @pl.when(ki == pl.num_programs(3) - 1)
    def _():
        outs = []
        for h in range(2):
            a = acc_ref[h]
            outs.append(a[:, :MLA_V] / a[:, MLA_V:MLA_V + 1])
        o_ref[0] = jnp.concatenate(outs, axis=-1)


def attention(q, k, v):
    B, Sq, _ = q.shape
    Sk = k.shape[1]
    tq = _pick(Sq, (1024, 512, 256, 128))
    tk = _pick(Sk, (1408, 768, 512, 384, 256, 128))
    return pl.pallas_call(
        functools.partial(_attn_body, c2=MLA_SCALE * math.log2(math.e)),
        grid=(B, MLA_HEADS // 2, Sq // tq, Sk // tk),
        in_specs=[pl.BlockSpec((1, tq, 2 * MLA_PAD), lambda b, p, i, j: (b, i, p)),
                  pl.BlockSpec((1, tk, 2 * MLA_PAD), lambda b, p, i, j: (b, j, p)),
                  pl.BlockSpec((1, tk, 2 * MLA_PAD), lambda b, p, i, j: (b, j, p))],
        out_specs=pl.BlockSpec((1, tq, 2 * MLA_V), lambda b, p, i, j: (b, i, p)),
        out_shape=jax.ShapeDtypeStruct((B, Sq, MLA_HEADS * MLA_V), F32),
        scratch_shapes=[pltpu.VMEM((2, tq, MLA_PAD), F32), pltpu.VMEM((2, tq, MLA_PAD), F32)],
        compiler_params=_cparams("parallel", "parallel", "parallel", "arbitrary"),
        name="mla_attention",
    )(q, k, v)


def _tri_masks(C, rev):
    row = lax.broadcasted_iota(jnp.int32, (C, C), 0)
    col = lax.broadcasted_iota(jnp.int32, (C, C), 1)
    if rev:
        return row <= col, row < col
    return row >= col, row > col


def _neumann_inverse(nil, dn, passes):
    C = nil.shape[-1]
    eye = (lax.broadcasted_iota(jnp.int32, (C, C), 0) ==
           lax.broadcasted_iota(jnp.int32, (C, C), 1)).astype(F32)
    x = eye + nil
    p = nil
    for _ in range(int(math.log2(C)) - 1):
        p = _dg(p, p, dn, passes)
        x = x + _dg(x, p, dn, passes)
    return x


def _gdn_body(qf, kf, vf, gcf, bcf, grf, qb, kb, vb, gcb, bcb, grb, of_ref, ob_ref, s_ref, *, passes):
    C = GDN_CHUNK
    H = GDN_HEADS

    @pl.when(pl.program_id(1) == 0)
    def _():
        s_ref[...] = jnp.zeros(s_ref.shape, F32)

    dirs = ((qf, kf, vf, gcf, bcf, grf), (qb, kb, vb, gcb, bcb, grb))
    qs, ks, vs, gcs, grs, betas, glast = [], [], [], [], [], [], []
    for d, (q_ref, k_ref, v_ref, gc_ref, bc_ref, gr_ref) in enumerate(dirs):
        rev = d == 1
        tri = _tri_masks(C, rev)[0].astype(F32)
        gcum_col = _dg(tri, gc_ref[0], _NN, 6)
        gcum_row = _dg(gr_ref[0, 0], tri, _NT, 6)
        beta_all = bc_ref[0]
        t_last = 0 if rev else C - 1
        for h in range(H):
            idx = d * H + h
            gcs.append(gcum_col[:, idx:idx + 1])
            grs.append(gcum_row[idx:idx + 1, :])
            glast.append(gcum_row[idx:idx + 1, t_last:t_last + 1])
            betas.append(beta_all[:, idx:idx + 1])
            qs.append(q_ref[0, :, h * GDN_DK:(h + 1) * GDN_DK])
            ks.append(k_ref[0, :, h * GDN_DK:(h + 1) * GDN_DK])
            vs.append(v_ref[0, :, h * GDN_DV:(h + 1) * GDN_DV])
    q, k, v = jnp.stack(qs), jnp.stack(ks), jnp.stack(vs)
    gc, gr, beta, g_last = jnp.stack(gcs), jnp.stack(grs), jnp.stack(betas), jnp.stack(glast)
    n = 2 * H
    unit = lax.broadcasted_iota(jnp.int32, (n, C, C), 0)
    ahead = (lax.broadcasted_iota(jnp.int32, (n, C, C), 1) - lax.broadcasted_iota(jnp.int32, (n, C, C), 2))
    ahead = jnp.where(unit < H, ahead, -ahead)
    incl = ahead >= 0
    strict = ahead > 0

    decay = jnp.exp(jnp.where(incl, gc - gr, -1e30))
    kbeta = k * beta
    lower = jnp.where(strict, _dg(kbeta, k, _BNT, passes) * decay, 0.0)
    tinv = _neumann_inverse(-lower, _BNN, passes)
    eg = jnp.exp(gc)
    u = _dg(tinv, v * beta, _BNN, passes)
    w = _dg(tinv, kbeta * eg, _BNN, passes)
    aqk = jnp.where(incl, _dg(q, k, _BNT, passes) * decay, 0.0)
    s = s_ref[...]
    v_new = u - _dg(w, s, _BNN, passes)
    o = _dg(q * eg, s, _BNN, passes) + _dg(aqk, v_new, _BNN, passes)
    s_ref[...] = s * jnp.exp(g_last) + _dg(k * jnp.exp(g_last - gc), v_new, _BTN, passes)
    for h in range(H):
        of_ref[0, :, h * GDN_DV:(h + 1) * GDN_DV] = o[h]
        ob_ref[0, :, h * GDN_DV:(h + 1) * GDN_DV] = o[H + h]


def _rev_chunk(i, ncc, nc):
    return jnp.where(i < ncc, ncc - 1 - i, nc - 1 + ncc - i)


def gdn_scan(q, k, v, g, beta, n_ctx):
    B, T, _ = q.shape
    C = GDN_CHUNK
    nc = T // C
    ncc = n_ctx // C
    gcol = g.reshape(B, T, 2 * GDN_HEADS)
    bcol = beta.reshape(B, T, 2 * GDN_HEADS)
    grow = jnp.swapaxes(gcol.reshape(B, nc, C, 2 * GDN_HEADS), 2, 3)
    fwd = lambda b, i: (b, i, 0)
    bwd = lambda b, i: (b, _rev_chunk(i, ncc, nc), 0)
    fwd4 = lambda b, i: (b, i, 0, 0)
    bwd4 = lambda b, i: (b, _rev_chunk(i, ncc, nc), 0, 0)
    wide = q.shape[-1]
    wv = v.shape[-1]

    def specs(m3, m4):
        return [pl.BlockSpec((1, C, wide), m3), pl.BlockSpec((1, C, wide), m3), pl.BlockSpec((1, C, wv), m3),
                pl.BlockSpec((1, C, 2 * GDN_HEADS), m3), pl.BlockSpec((1, C, 2 * GDN_HEADS), m3),
                pl.BlockSpec((1, 1, 2 * GDN_HEADS, C), m4)]

    of, ob = pl.pallas_call(
        functools.partial(_gdn_body, passes=GDN_PASSES),
        grid=(B, nc),
        in_specs=specs(fwd, fwd4) + specs(bwd, bwd4),
        out_specs=[pl.BlockSpec((1, C, wv), fwd), pl.BlockSpec((1, C, wv), bwd)],
        out_shape=[jax.ShapeDtypeStruct((B, T, wv), F32)] * 2,
        scratch_shapes=[pltpu.VMEM((2 * GDN_HEADS, GDN_DK, GDN_DV), F32)],
        compiler_params=_cparams("parallel", "arbitrary"),
        name="gdn_scan",
    )(q, k, v, gcol, bcol, grow, q, k, v, gcol, bcol, grow)
    return of, ob


def _rwkv_prep(r, lw, k, v, kk, rate, rev):
    C, D = r.shape
    N = RWKV_HEAD
    H = D // N
    incl, _ = _tri_masks(C, rev)
    tri = incl.astype(BF16)
    l1 = lw.astype(BF16)
    rem = lw - l1.astype(F32)
    l2 = rem.astype(BF16)
    l3 = (rem - l2.astype(F32)).astype(BF16)
    linc = _dotf(tri, l1) + _dotf(tri, l2) + _dotf(tri, l3)
    lexc = linc - lw
    ltot = linc[0:1, :] if rev else linc[C - 1:C, :]
    b = kk * rate
    einv = jnp.exp(-linc)
    etail = jnp.exp(ltot - linc)

    def hs(x):
        return jnp.stack([x[:, h * N:(h + 1) * N] for h in range(H)], axis=0)

    lhs = jnp.concatenate([hs(-kk * jnp.exp(lexc)), hs(r * jnp.exp(linc))], axis=1)
    rhs = jnp.concatenate([hs(b * einv), hs(k * einv)], axis=1)
    tail = jnp.concatenate([hs(b * etail), hs(k * etail)], axis=1)
    return lhs, rhs, tail, hs(v), jnp.exp(hs(ltot))


def _rwkv_body(rf, vf, kkf, lwf, kf, af, rb, vb, kkb, lwb, kb, ab, yf_ref, yb_ref, s_ref, *, passes):
    @pl.when(pl.program_id(1) == 0)
    def _():
        s_ref[...] = jnp.zeros(s_ref.shape, F32)

    C = rf.shape[1]
    H = rf.shape[2] // RWKV_HEAD
    dirs = ((rf, vf, kkf, lwf, kf, af), (rb, vb, kkb, lwb, kb, ab))
    parts = [_rwkv_prep(r_ref[0], lw_ref[0], k_ref[0], v_ref[0], kk_ref[0], a_ref[0], d == 1)
             for d, (r_ref, v_ref, kk_ref, lw_ref, k_ref, a_ref) in enumerate(dirs)]
    lhs, rhs, tail, vh, ptot = [jnp.concatenate([parts[0][j], parts[1][j]], axis=0) for j in range(5)]
    s = s_ref[...]
    shape = (2 * H, C, 2 * C)
    col = lax.broadcasted_iota(jnp.int32, shape, 2)
    ahead = lax.broadcasted_iota(jnp.int32, shape, 1) - jnp.where(col >= C, col - C, col)
    ahead = jnp.where(lax.broadcasted_iota(jnp.int32, shape, 0) < H, ahead, -ahead)
    sc = _dg(lhs, rhs, _BNT, passes)
    top = jnp.where(ahead > 0, sc[:, :C, :], 0.0)
    bot = jnp.where(ahead >= 0, sc[:, C:, :], 0.0)
    tinv = _neumann_inverse(top[:, :, :C], _BNN, passes)
    ars = _dg(lhs, s, _BNT, passes)
    zero_v = jnp.concatenate([jnp.zeros_like(vh), vh], axis=1)
    u = _dg(tinv, ars[:, :C, :] + _dg(top, zero_v, _BNN, passes), _BNN, passes)
    uv = jnp.concatenate([u, vh], axis=1)
    y = ars[:, C:, :] + _dg(bot, uv, _BNN, passes)
    s_ref[...] = s * ptot + _dg(uv, tail, _BTN, passes)
    yf_ref[0] = jnp.concatenate([y[h] for h in range(H)], axis=-1)
    yb_ref[0] = jnp.concatenate([y[H + h] for h in range(H)], axis=-1)


def rwkv_scan(r, v, kk, lw, key, rate, n_ctx):
    B, T, D = r.shape
    N = RWKV_HEAD
    C = RWKV_CHUNK
    nc = T // C
    ncc = n_ctx // C
    fwd = lambda b, i: (b, i, 0)
    bwd = lambda b, i: (b, _rev_chunk(i, ncc, nc), 0)
    blk = (1, C, D)
    return pl.pallas_call(
        functools.partial(_rwkv_body, passes=RWKV_PASSES),
        grid=(B, nc),
        in_specs=[pl.BlockSpec(blk, fwd)] * 6 + [pl.BlockSpec(blk, bwd)] * 6,
        out_specs=[pl.BlockSpec(blk, fwd), pl.BlockSpec(blk, bwd)],
        out_shape=[jax.ShapeDtypeStruct((B, T, D), F32)] * 2,
        scratch_shapes=[pltpu.VMEM((2 * (D // N), N, N), F32)],
        compiler_params=_cparams("parallel", "arbitrary"),
        name="rwkv7_scan",
    )(r, v, kk, lw[0], key[0], rate[0], r, v, kk, lw[1], key[1], rate[1])


def _moe_body(be_ref, x_ref, w1_ref, w3_ref, w2_ref, o_ref, w1b, w3b, w2b):
    i = pl.program_id(0)
    prev = be_ref[jnp.maximum(i - 1, 0)]

    @pl.when((i == 0) | (be_ref[i] != prev))
    def _():
        w1b[...] = w1_ref[0, 0].astype(BF16)
        w3b[...] = w3_ref[0, 0].astype(BF16)
        w2b[...] = w2_ref[0, 0].astype(BF16)

    x = x_ref[...]
    h1 = jnp.dot(x, w1b[...], preferred_element_type=F32)
    h3 = jnp.dot(x, w3b[...], preferred_element_type=F32)
    hid = (h1 * jax.nn.sigmoid(h1)) * h3
    o_ref[...] = jnp.dot(hid.astype(BF16), w2b[...], preferred_element_type=F32)


def moe_experts(xs, blk_e, w1, w3, w2, layer):
    n_slots, D = xs.shape
    hid = w1.shape[-1]
    n_blocks = n_slots // MOE_BLOCK
    return pl.pallas_call(
        _moe_body,
        grid_spec=pltpu.PrefetchScalarGridSpec(
            num_scalar_prefetch=1,
            grid=(n_blocks,),
            in_specs=[pl.BlockSpec((MOE_BLOCK, D), lambda i, be: (i, 0)),
                      pl.BlockSpec((1, 1, D, hid), lambda i, be: (layer, be[i], 0, 0)),
                      pl.BlockSpec((1, 1, D, hid), lambda i, be: (layer, be[i], 0, 0)),
                      pl.BlockSpec((1, 1, hid, D), lambda i, be: (layer, be[i], 0, 0))],
            out_specs=pl.BlockSpec((MOE_BLOCK, D), lambda i, be: (i, 0)),
            scratch_shapes=[pltpu.VMEM((D, hid), BF16), pltpu.VMEM((D, hid), BF16), pltpu.VMEM((hid, D), BF16)],
        ),
        out_shape=jax.ShapeDtypeStruct((n_slots, D), F32),
        compiler_params=_cparams("arbitrary"),
        name="moe_experts",
    )(blk_e, xs, w1, w3, w2)


def _route_body(lg_ref, bias_ref, out_ref, cnt_ref, run_ref, *, tm):
    @pl.when(pl.program_id(0) == 0)
    def _():
        run_ref[...] = jnp.zeros(run_ref.shape, F32)

    x = lg_ref[...] + bias_ref[...]
    lane = lax.broadcasted_iota(jnp.int32, x.shape, 1)
    far = 1 << 20

    def first_lane(hit):
        return jnp.min(jnp.where(hit, lane, far), axis=-1, keepdims=True)

    def masked_softmax(mask):
        xm = jnp.where(mask, x, -1e30)
        e = jnp.where(mask, jnp.exp(xm - jnp.max(xm, axis=-1, keepdims=True)), 0.0)
        return e / jnp.sum(e, axis=-1, keepdims=True)

    is_group = lane < MOE_GROUPS
    pg = masked_softmax(is_group)
    pg_top = jnp.max(pg, axis=-1, keepdims=True)
    g_idx = first_lane(is_group & (pg == pg_top))
    lo = MOE_GROUPS + MOE_PER_GROUP * g_idx
    in_group = (lane >= lo) & (lane < lo + MOE_PER_GROUP)
    pe = masked_softmax(in_group)
    p1 = jnp.max(pe, axis=-1, keepdims=True)
    l1 = first_lane(in_group & (pe == p1))
    rest_ok = in_group & (lane != l1)
    rest = jnp.where(rest_ok, pe, -1.0)
    p2 = jnp.max(rest, axis=-1, keepdims=True)
    l2 = first_lane(rest_ok & (rest == p2))
    psum = p1 + p2
    w1 = pg_top * p1 / psum
    w2 = pg_top * p2 / psum

    oh1 = (lane == l1).astype(F32)
    oh2 = (lane == l2).astype(F32)
    both = oh1 + oh2
    earlier = (lax.broadcasted_iota(jnp.int32, (tm, tm), 0) > lax.broadcasted_iota(jnp.int32, (tm, tm), 1))
    base = _dotf(earlier.astype(BF16), both.astype(BF16)) + run_ref[...]
    r1 = jnp.sum(base * oh1, axis=-1, keepdims=True)
    r2 = jnp.sum(base * oh2, axis=-1, keepdims=True)
    run_ref[...] = run_ref[...] + jnp.sum(both, axis=0, keepdims=True)
    cnt_ref[...] = run_ref[...]
    cols = ((l1 - MOE_GROUPS).astype(F32), (l2 - MOE_GROUPS).astype(F32), r1, r2, w1, w2)
    out = jnp.zeros(x.shape, F32)
    for j, c in enumerate(cols):
        out = jnp.where(lane == j, c, out)
    out_ref[...] = out


def moe_route(logits, b_group, b_expert):
    N, W = logits.shape
    tm = _pick(N, (512, 256, 128, 64, 32, 16, 8))
    bias = _pad_cols(jnp.concatenate([b_group, b_expert])[None, :], W)
    return pl.pallas_call(
        functools.partial(_route_body, tm=tm),
        grid=(N // tm,),
        in_specs=[pl.BlockSpec((tm, W), lambda i: (i, 0)), pl.BlockSpec((1, W), lambda i: (0, 0))],
        out_specs=[pl.BlockSpec((tm, W), lambda i: (i, 0)), pl.BlockSpec((1, W), lambda i: (0, 0))],
        out_shape=[jax.ShapeDtypeStruct((N, W), F32), jax.ShapeDtypeStruct((1, W), F32)],
        scratch_shapes=[pltpu.VMEM((1, W), F32)],
        compiler_params=_cparams("arbitrary"),
        name="moe_route",
    )(logits, bias)


def hier_moe(tokens, logits, b_group, b_expert, w1, w3, w2, layer):
    N, D = tokens.shape
    route, cnt = moe_route(logits, b_group, b_expert)
    eid = route[:, 0:MOE_TOPK].astype(jnp.int32).reshape(-1)
    rank = route[:, MOE_TOPK:2 * MOE_TOPK].astype(jnp.int32).reshape(-1)
    wts = route[:, 2 * MOE_TOPK:3 * MOE_TOPK]
    counts = cnt[0, MOE_GROUPS:MOE_GROUPS + MOE_EXPERTS].astype(jnp.int32)
    tok = jnp.repeat(jnp.arange(N, dtype=jnp.int32), MOE_TOPK)
    A = N * MOE_TOPK
    padded = (counts + MOE_BLOCK - 1) // MOE_BLOCK * MOE_BLOCK
    pend = jnp.cumsum(padded)
    dest = (pend - padded)[eid] + rank
    n_blocks = -(-A // MOE_BLOCK) + MOE_EXPERTS
    n_slots = n_blocks * MOE_BLOCK
    slot_tok = jnp.zeros((n_slots,), jnp.int32).at[dest].set(tok)
    starts = jnp.arange(n_blocks, dtype=jnp.int32)[:, None] * MOE_BLOCK
    blk_e = jnp.minimum(jnp.sum((pend[None, :] <= starts).astype(jnp.int32), axis=1), MOE_EXPERTS - 1)
    xs = tokens[slot_tok]
    ys = moe_experts(xs, blk_e, w1, w3, w2, layer)
    d2 = dest.reshape(N, MOE_TOPK)
    return ys[d2[:, 0]] * wts[:, 0:1] + ys[d2[:, 1]] * wts[:, 1:2]


def _rope_tables(n_lat, n_ctx):
    rows = n_lat // GRID_W
    row = jnp.repeat(jnp.arange(rows, dtype=F32), GRID_W)
    col = jnp.tile(jnp.arange(GRID_W, dtype=F32), rows)
    n_freq = MLA_ROPE // 4
    inv = ROPE_BASE ** (-jnp.arange(n_freq, dtype=F32) / n_freq)
    ang = jnp.stack([row[:, None] * inv, col[:, None] * inv], axis=1)
    cos, sin = jnp.cos(ang), jnp.sin(ang)
    zf = jnp.zeros((n_lat, n_freq), F32)
    lat = lambda parts, fill: jnp.concatenate(
        [jnp.full((n_lat, MLA_NOPE), fill, F32)] + parts + [jnp.full((n_lat, MLA_PAD - MLA_QK), fill, F32)], axis=1)
    c = lat([cos[:, 0], cos[:, 0], cos[:, 1], cos[:, 1]], 1.0)
    s_lo = lat([-sin[:, 0], zf, -sin[:, 1], zf], 0.0)
    s_hi = lat([zf, sin[:, 0], zf, sin[:, 1]], 0.0)
    ctx = lambda fill: jnp.full((n_ctx, MLA_PAD), fill, F32)
    return jnp.concatenate([ctx(1.0), c], 0), jnp.concatenate([ctx(0.0), s_lo + s_hi], 0)


def _const_spec(shape):
    return pl.BlockSpec(shape, lambda i: (0,) * len(shape), pipeline_mode=pl.Buffered(1))


def _normmod(x, gain, shift, scale):
    return x * lax.rsqrt(jnp.mean(x * x, -1, keepdims=True) + EPS) * gain * (1 + scale) + shift


def _head_indicator(D, N):
    e = (jnp.arange(D)[:, None] // N == jnp.arange(128)[None, :]).astype(BF16)
    return e, e.T


def _seg_dot(x, e):
    xh, xl = _split(x)
    return jnp.dot(xh, e, preferred_element_type=F32) + jnp.dot(xl, e, preferred_element_type=F32)


def _dotf(a, b):
    return jnp.dot(a, b, preferred_element_type=F32)


def _rwkv_pre_body(*refs, tm, blocks_per_batch, ctx_blocks, vres):
    (h_ref, hp_ref, hn_ref, m_ref, gain_ref, mu_ref, w0_ref, a0_ref, kk_ref, ka_ref, e_ref, et_ref,
     wr_ref, wk_ref, wv_ref, w1_ref, w2_ref, a1_ref, a2_ref, g1_ref, g2_ref) = refs[:21]
    rest = refs[21:]
    if vres:
        v0_ref, v1_ref, v2_ref, vf_ref = rest[:4]
        rest = rest[4:]
    r_o, v_o, kk_o, lw0_o, lw1_o, k0_o, k1_o, ra0_o, ra1_o, gate_o = rest

    tb = pl.program_id(0) % blocks_per_batch
    seg_start = (tb == 0) | (tb == ctx_blocks)
    seg_end = (tb == ctx_blocks - 1) | (tb == blocks_per_batch - 1)
    shift, scale, gain = m_ref[0, 0:1, :], m_ref[0, 1:2, :], gain_ref[...]
    u = _normmod(h_ref[...], gain, shift, scale)
    up = jnp.where(seg_start, 0.0, _normmod(hp_ref[7:8, :], gain, shift, scale))
    un = jnp.where(seg_end, 0.0, _normmod(hn_ref[0:1, :], gain, shift, scale))
    row = lax.broadcasted_iota(jnp.int32, (tm, 1), 0)
    u_prev = jnp.where(row == 0, up, pltpu.roll(u, 1, 0))
    u_next = jnp.where(row == tm - 1, un, pltpu.roll(u, tm - 1, 0))
    xx = 0.5 * (u_prev + u_next) - u
    xr, xw, xk, xv, xa, xg = [(u + xx * mu_ref[j:j + 1, :]).astype(BF16) for j in range(6)]

    r = _dotf(xr, wr_ref[...])
    k = _dotf(xk, wk_ref[...])
    v = _dotf(xv, wv_ref[...])
    if vres:
        lo = _dotf(xv, v1_ref[...]).astype(BF16)
        v = v + (vf_ref[...] - v) * jax.nn.sigmoid(v0_ref[...] + _dotf(lo, v2_ref[...]))
    tl = jnp.tanh(_dotf(xw, w1_ref[...])).astype(BF16)
    al = _dotf(xa, a1_ref[...]).astype(BF16)
    gl = jax.nn.sigmoid(_dotf(xg, g1_ref[...])).astype(BF16)
    gate_o[...] = _dotf(gl, g2_ref[...])
    kx = k * kk_ref[...]
    inv = lax.rsqrt(_seg_dot(kx * kx, e_ref[...]) + EPS)
    r_o[...] = r
    v_o[...] = v
    kk_o[...] = kx * _seg_dot(inv, et_ref[...])
    for d, (lw_o, k_o, ra_o) in enumerate(((lw0_o, k0_o, ra0_o), (lw1_o, k1_o, ra1_o))):
        z = w0_ref[d:d + 1, :] + _dotf(tl, w2_ref[d])
        lw_o[...] = -math.exp(-0.5) * jax.nn.sigmoid(z)
        a = jax.nn.sigmoid(a0_ref[d:d + 1, :] + _dotf(al, a2_ref[d]))
        ra_o[...] = a
        k_o[...] = k * (1 + (a - 1) * ka_ref[...])


def _row_tile(seg):
    return _pick(seg, (256, 128, 64, 32, 16, 8))


def _pad_cols(w, n):
    return jnp.pad(w, ((0, 0), (0, n - w.shape[1])))


def _pad_rows(w, n):
    return jnp.pad(w, ((0, n - w.shape[0]), (0, 0)))


def rwkv_pre(h, m_seg, seg, T, n_ctx, gain, mu, wr, wk, wv, w0, w1, w2, a0, a1, a2, g1, g2, k_k, k_a, vres, v_first):
    M, D = h.shape
    tm = _row_tile(seg)
    lora = w1.shape[-1]
    e, et = _head_indicator(D, RWKV_HEAD)
    zero = jnp.zeros((lora, D), F32)
    w2p = jnp.stack([jnp.concatenate([w2[0], zero], 0), jnp.concatenate([zero, w2[1]], 0)]).astype(BF16)
    a2p = jnp.stack([jnp.concatenate([a2[0], zero], 0), jnp.concatenate([zero, a2[1]], 0)]).astype(BF16)
    gp = -(-g1.shape[1] // 128) * 128
    row = lambda a: a.reshape(1, D)
    consts = [row(gain), mu, w0, a0, row(k_k), row(k_a), e, et,
              wr.astype(BF16), wk.astype(BF16), wv.astype(BF16),
              jnp.concatenate([w1[0], w1[1]], 1).astype(BF16), w2p,
              jnp.concatenate([a1[0], a1[1]], 1).astype(BF16), a2p,
              _pad_cols(g1, gp).astype(BF16), _pad_rows(g2, gp).astype(BF16)]
    row_spec = pl.BlockSpec((tm, D), lambda i: (i, 0))
    last8 = M // 8 - 1
    in_specs = [row_spec,
                pl.BlockSpec((8, D), lambda i: (jnp.maximum(i * (tm // 8) - 1, 0), 0)),
                pl.BlockSpec((8, D), lambda i: (jnp.minimum((i + 1) * (tm // 8), last8), 0)),
                pl.BlockSpec((1, 6, D), lambda i: (i * tm // seg, 0, 0))]
    in_specs += [_const_spec(c.shape) for c in consts]
    args = [h, h, h, m_seg] + consts
    if vres is not None:
        v0, v1, v2 = vres
        extra = [row(v0), _pad_cols(v1, 128).astype(BF16), _pad_rows(v2, 128).astype(BF16)]
        in_specs += [_const_spec(c.shape) for c in extra] + [row_spec]
        args += extra + [v_first]
    return pl.pallas_call(
        functools.partial(_rwkv_pre_body, tm=tm, blocks_per_batch=T // tm, ctx_blocks=n_ctx // tm,
                          vres=vres is not None),
        grid=(M // tm,),
        in_specs=in_specs,
        out_specs=[row_spec] * 10,
        out_shape=[jax.ShapeDtypeStruct((M, D), F32)] * 10,
        compiler_params=_cparams("parallel"),
        name="rwkv7_pre",
    )(*args)


def _post_tail(xo, h_ref, m_ref, gain_ref, w_ref, wrt_ref, h_o, f_o, lg_o):
    h_new = h_ref[...] + m_ref[0, 2:3, :] * _dotf(xo, w_ref[...])
    h_o[...] = h_new
    f = _normmod(h_new, gain_ref[...], m_ref[0, 3:4, :], m_ref[0, 4:5, :])
    f_o[...] = f.astype(BF16)
    lg_o[...] = _dg(f, wrt_ref[...], _NN, 3)


def _rwkv_post_body(yf_ref, yb_ref, r_ref, k0_ref, k1_ref, v_ref, gate_ref, lnw_ref, lnb_ref, rk_ref, e_ref, et_ref,
                    h_ref, m_ref, gain_ref, w_ref, wrt_ref, h_o, f_o, lg_o):
    e, et = e_ref[...], et_ref[...]
    inv_n = 1.0 / RWKV_HEAD
    y = yf_ref[...] + yb_ref[...]
    yc = y - _seg_dot(_seg_dot(y, e) * inv_n, et)
    var = _seg_dot(_seg_dot(yc * yc, e) * inv_n, et)
    yn = yc * lax.rsqrt(var + GN_EPS) * lnw_ref[...] + lnb_ref[...]
    k_bonus = 0.5 * (k0_ref[...] + k1_ref[...])
    bonus = _seg_dot(_seg_dot(r_ref[...] * k_bonus * rk_ref[...], e), et) * v_ref[...]
    xo = ((yn + bonus) * gate_ref[...]).astype(BF16)
    _post_tail(xo, h_ref, m_ref, gain_ref, w_ref, wrt_ref, h_o, f_o, lg_o)


def _post_call(body, name, row_args, consts, h, m_seg, seg, gain, w_out, w_router):
    M, D = h.shape
    tm = _row_tile(seg)
    row_spec = lambda a: pl.BlockSpec((tm, a.shape[1]), lambda i: (i, 0))
    tail = [gain.reshape(1, D), w_out.astype(BF16), w_router]
    in_specs = ([row_spec(a) for a in row_args] + [_const_spec(c.shape) for c in consts] +
                [row_spec(h), pl.BlockSpec((1, 6, D), lambda i: (i * tm // seg, 0, 0))] +
                [_const_spec(c.shape) for c in tail])
    nr = w_router.shape[1]
    return pl.pallas_call(
        body,
        grid=(M // tm,),
        in_specs=in_specs,
        out_specs=[pl.BlockSpec((tm, D), lambda i: (i, 0)), pl.BlockSpec((tm, D), lambda i: (i, 0)),
                   pl.BlockSpec((tm, nr), lambda i: (i, 0))],
        out_shape=[jax.ShapeDtypeStruct((M, D), F32), jax.ShapeDtypeStruct((M, D), BF16),
                   jax.ShapeDtypeStruct((M, nr), F32)],
        compiler_params=_cparams("parallel"),
        name=name,
    )(*row_args, *consts, h, m_seg, *tail)


def _hy_post_body(a_ref, of_ref, ob_ref, z_ref, og_ref, h_ref, m_ref, gain_ref, w_ref, wrt_ref, h_o, f_o, lg_o):
    o = of_ref[...] + ob_ref[...]
    z = z_ref[...]
    parts = [a_ref[...]]
    for hd in range(GDN_HEADS):
        sl = slice(hd * GDN_DV, (hd + 1) * GDN_DV)
        oh, zh = o[:, sl], z[:, sl]
        on = oh * lax.rsqrt(jnp.mean(oh * oh, -1, keepdims=True) + EPS) * og_ref[...]
        parts.append(on * (zh * jax.nn.sigmoid(zh)))
    xo = jnp.concatenate(parts, axis=-1).astype(BF16)
    _post_tail(xo, h_ref, m_ref, gain_ref, w_ref, wrt_ref, h_o, f_o, lg_o)


def hy_post(a, of, ob, z, out_g, h, m_seg, seg, gain, w_out, w_router):
    return _post_call(_hy_post_body, "hybrid_post", [a, of, ob, z], [out_g.reshape(1, -1)], h, m_seg, seg,
                      gain, w_out, w_router)


def _hy_pre_body(h_ref, m_ref, gain_ref, c_ref, s_ref, wq1, wkv1, wpe, wpe2, wgq, wz, wab, qag, kvag,
                 wqb, wqb2, wkn, wv, qng, qng2, kng, kng2, q_o, k_o, v_o, gq_o, z_o, ab_o):
    u = _normmod(h_ref[...], gain_ref[...], m_ref[0, 0:1, :], m_ref[0, 1:2, :]).astype(BF16)
    gq_o[...] = _dotf(u, wgq[...])
    z_o[...] = _dotf(u, wz[...])
    ab_o[...] = _dotf(u, wab[...])
    cq = _dotf(u, wq1[...])
    ckv = _dotf(u, wkv1[...])
    pe = _dotf(u, wpe[...])
    pe2 = _dotf(u, wpe2[...])
    cq = (cq * lax.rsqrt(jnp.mean(cq * cq, -1, keepdims=True) + EPS) * qag[...]).astype(BF16)
    ckv = (ckv * lax.rsqrt(jnp.mean(ckv * ckv, -1, keepdims=True) + EPS) * kvag[...]).astype(BF16)
    q = _dotf(cq, wqb[...])
    q2 = _dotf(cq, wqb2[...])
    kn = _dotf(ckv, wkn[...])
    vv = _dotf(ckv, wv[...])
    c, s = c_ref[...], s_ref[...]
    qc, qs = qng[...] * c, qng2[...] * s
    kc, ks = kng[...] * c, kng2[...] * s
    k_rot = pe2 * ks
    one_col = (lax.broadcasted_iota(jnp.int32, (1, MLA_PAD), 1) == MLA_V).astype(F32)
    inv_d = 1.0 / MLA_QK

    def inv_rms(t):
        return lax.rsqrt(jnp.sum(t * t, -1, keepdims=True) * inv_d + EPS)

    for hd in range(MLA_HEADS):
        sl = slice(hd * MLA_PAD, (hd + 1) * MLA_PAD)
        qh = q[:, sl]
        kh = kn[:, sl] + pe
        q_o[:, sl] = (inv_rms(qh) * (qh * qc + q2[:, sl] * qs)).astype(BF16)
        k_o[:, sl] = (inv_rms(kh) * (kh * kc + k_rot)).astype(BF16)
        v_o[:, sl] = (vv[:, sl] + one_col).astype(BF16)


def _pad_heads(w, heads, width, to):
    K = w.shape[0]
    return jnp.pad(w.reshape(K, heads, width), ((0, 0), (0, 0), (0, to - width))).reshape(K, heads * to)


def hy_pre(h, m_seg, seg, T, gain, rope, w_in, qa_g, w_qb, kva_g, w_kvb, qn_g, kn_g):
    M, D = h.shape
    tm = _row_tile(seg)
    H = MLA_HEADS
    c0, c1, c2 = MLA_COLS, MLA_COLS + GDN_QKV, MLA_COLS + GDN_QKV + GDN_Z
    kvl = MLA_Q_LORA + MLA_KV_LORA
    wb = w_in.astype(BF16)
    wpe = jnp.pad(wb[:, kvl:c0], ((0, 0), (MLA_NOPE, MLA_PAD - MLA_QK)))
    wkv = w_kvb.reshape(MLA_KV_LORA, H, MLA_NOPE + MLA_V)
    pad1 = lambda g: jnp.pad(g, (0, MLA_PAD - MLA_QK)).reshape(1, MLA_PAD)
    lane = jnp.arange(MLA_PAD)
    rot = (lane >= MLA_NOPE) & (lane < MLA_QK)
    n_freq = MLA_ROPE // 4
    partner = jnp.where(rot, jnp.where(((lane - MLA_NOPE) // n_freq) % 2 == 0, lane + n_freq, lane - n_freq), lane)
    wqb_pad = _pad_heads(w_qb, H, MLA_QK, MLA_PAD).astype(BF16)
    wqb2 = wqb_pad.reshape(MLA_Q_LORA, H, MLA_PAD)[:, :, partner].reshape(MLA_Q_LORA, H * MLA_PAD)
    consts = [wb[:, :MLA_Q_LORA], wb[:, MLA_Q_LORA:kvl], wpe, wpe[:, partner], wb[:, c0:c1], wb[:, c1:c2],
              _pad_cols(wb[:, c2:], 128), qa_g.reshape(1, -1), kva_g.reshape(1, -1),
              wqb_pad, wqb2,
              _pad_heads(wkv[:, :, :MLA_NOPE].reshape(MLA_KV_LORA, -1), H, MLA_NOPE, MLA_PAD).astype(BF16),
              _pad_heads(wkv[:, :, MLA_NOPE:].reshape(MLA_KV_LORA, -1), H, MLA_V, MLA_PAD).astype(BF16),
              pad1(qn_g), pad1(qn_g)[:, partner], pad1(kn_g), pad1(kn_g)[:, partner]]
    bpb = T // tm
    row = lambda n: pl.BlockSpec((tm, n), lambda i: (i, 0))
    tab = pl.BlockSpec((tm, MLA_PAD), lambda i: (i % bpb, 0))
    in_specs = ([row(D), pl.BlockSpec((1, 6, D), lambda i: (i * tm // seg, 0, 0)), _const_spec((1, D)), tab, tab]
                + [_const_spec(c.shape) for c in consts])
    wide = H * MLA_PAD
    return pl.pallas_call(
        _hy_pre_body,
        grid=(M // tm,),
        in_specs=in_specs,
        out_specs=[row(wide), row(wide), row(wide), row(GDN_QKV), row(GDN_Z), row(128)],
        out_shape=[jax.ShapeDtypeStruct((M, wide), BF16)] * 3 + [jax.ShapeDtypeStruct((M, GDN_QKV), F32),
                   jax.ShapeDtypeStruct((M, GDN_Z), F32), jax.ShapeDtypeStruct((M, 128), F32)],
        compiler_params=_cparams("parallel"),
        name="hybrid_pre",
    )(h, m_seg, gain.reshape(1, D), *rope, *consts)


def _gdn_prep_body(x_ref, xp_ref, xn_ref, w_ref, q_o, k_o, v_o, *, tm, blocks_per_batch, ctx_blocks):
    tb = pl.program_id(0) % blocks_per_batch
    seg_start = (tb == 0) | (tb == ctx_blocks)
    seg_end = (tb == ctx_blocks - 1) | (tb == blocks_per_batch - 1)
    x = x_ref[...]
    xp = jnp.where(seg_start, 0.0, xp_ref[...])
    xn = jnp.where(seg_end, 0.0, xn_ref[...])
    row = lax.broadcasted_iota(jnp.int32, (tm, 1), 0)
    half = GDN_CONV // 2
    acc = x * w_ref[half:half + 1, :]
    for s in range(1, half + 1):
        before = pltpu.roll(x, s, 0)
        after = pltpu.roll(x, tm - s, 0)
        for r in range(s):
            before = jnp.where(row == r, xp[8 - s + r:8 - s + r + 1, :], before)
            after = jnp.where(row == tm - s + r, xn[r:r + 1, :], after)
        acc = acc + before * w_ref[half - s:half - s + 1, :] + after * w_ref[half + s:half + s + 1, :]
    y = acc * jax.nn.sigmoid(acc)
    nk = GDN_HEADS * GDN_DK
    for hd in range(GDN_HEADS):
        sl = slice(hd * GDN_DK, (hd + 1) * GDN_DK)
        qh = y[:, sl]
        kh = y[:, nk + hd * GDN_DK:nk + (hd + 1) * GDN_DK]
        q_o[:, sl] = qh * lax.rsqrt(jnp.sum(qh * qh, -1, keepdims=True) + EPS) * GDN_DK ** -0.5
        k_o[:, sl] = kh * lax.rsqrt(jnp.sum(kh * kh, -1, keepdims=True) + EPS)
    v_o[...] = y[:, 2 * nk:]


def gdn_prep(gq, conv_w, seg, T, n_ctx):
    M, W = gq.shape
    tm = _row_tile(seg)
    last8 = M // 8 - 1
    nk = GDN_HEADS * GDN_DK
    row = lambda n: pl.BlockSpec((tm, n), lambda i: (i, 0))
    return pl.pallas_call(
        functools.partial(_gdn_prep_body, tm=tm, blocks_per_batch=T // tm, ctx_blocks=n_ctx // tm),
        grid=(M // tm,),
        in_specs=[row(W),
                  pl.BlockSpec((8, W), lambda i: (jnp.maximum(i * (tm // 8) - 1, 0), 0)),
                  pl.BlockSpec((8, W), lambda i: (jnp.minimum((i + 1) * (tm // 8), last8), 0)),
                  _const_spec(conv_w.shape)],
        out_specs=[row(nk), row(nk), row(W - 2 * nk)],
        out_shape=[jax.ShapeDtypeStruct((M, nk), F32), jax.ShapeDtypeStruct((M, nk), F32),
                   jax.ShapeDtypeStruct((M, W - 2 * nk), F32)],
        compiler_params=_cparams("parallel"),
        name="gdn_prep",
    )(gq, gq, gq, conv_w)


def rwkv_post(yf, yb, r, k0, k1, v, gate, ln_w, ln_b, r_k, h, m_seg, seg, gain, wo, w_router):
    D = h.shape[1]
    e, et = _head_indicator(D, RWKV_HEAD)
    consts = [ln_w.reshape(1, D), ln_b.reshape(1, D), r_k.reshape(1, D), e, et]
    return _post_call(_rwkv_post_body, "rwkv7_post", [yf, yb, r, k0, k1, v, gate], consts, h, m_seg, seg,
                      gain, wo, w_router)


def kernel(x, c, ctx, c_ctx, ada_w, ada_b, norm_mix, norm_ffn, hy_w_in, hy_w_out, mla_qa_norm, mla_w_qb, mla_kva_norm, mla_w_kvb, mla_q_norm, mla_k_norm, gdn_conv, gdn_a_log, gdn_dt_bias, gdn_out_norm, rk_mu, rk_wr, rk_wk, rk_wv, rk_wo, rk_w0, rk_w1, rk_w2, rk_a0, rk_a1, rk_a2, rk_g1, rk_g2, rk_kk, rk_ka, rk_rk, rk_ln_w, rk_ln_b, rk_v0, rk_v1, rk_v2, moe_w_group, moe_b_group, moe_w_expert, moe_b_expert, moe_w1, moe_w3, moe_w2):
    B, S, D = x.shape
    L = ctx.shape[1]
    T = L + S
    depth = ada_w.shape[0]
    rope = _rope_tables(S, L)
    n_rows = -(-(B + 1) // 8) * 8
    sc = jnp.concatenate([jax.nn.silu(c), jax.nn.silu(c_ctx)[None], jnp.zeros((n_rows - B - 1, D), F32)], 0)
    M = B * T
    h = jnp.concatenate([ctx, x], axis=1).reshape(M, D)
    seg = math.gcd(L, S)
    nseg = T // seg
    v_first = None
    for l in range(depth):
        m = mm(sc, ada_w[l], hi=True) + ada_b[l]
        m_lat = jnp.broadcast_to(m[:B].reshape(B, 1, 6, D), (B, S // seg, 6, D))
        m_ctx = jnp.broadcast_to(m[B].reshape(1, 1, 6, D), (B, L // seg, 6, D))
        m_seg = jnp.concatenate([m_ctx, m_lat], axis=1).reshape(B * nseg, 6, D)

        def mod(i, m_seg=m_seg):
            return m_seg[:, None, i, :]

        router = _pad_cols(jnp.concatenate([moe_w_group[l], moe_w_expert[l]], axis=1), 128)
        j = l // 2
        b3 = lambda a: a.reshape(B, T, a.shape[-1])
        if l % 2 == 0:
            q, k, v, gq, z, ab = hy_pre(h, m_seg, seg, T, norm_mix[l], rope, hy_w_in[j], mla_qa_norm[j],
                                        mla_w_qb[j], mla_kva_norm[j], mla_w_kvb[j], mla_q_norm[j], mla_k_norm[j])
            q, k, v = b3(q), b3(k), b3(v)
            a_lat = attention(q[:, L:], k, v)
            a_ctx = attention(q[:, :L], k[:, :L], v[:, :L])
            a = jnp.concatenate([a_ctx, a_lat], axis=1).reshape(M, -1)
            gq_, gk_, gv_ = gdn_prep(gq, gdn_conv[j], seg, T, L)
            ab = ab[:, :GDN_AB].reshape(B, T, 2, 2, GDN_HEADS)
            g = -jnp.exp(gdn_a_log[j]) * jax.nn.softplus(ab[:, :, :, 0] + gdn_dt_bias[j])
            beta = jax.nn.sigmoid(ab[:, :, :, 1])
            of, ob = gdn_scan(b3(gq_), b3(gk_), b3(gv_), g, beta, L)
            h, f, logits = hy_post(a, of.reshape(M, -1), ob.reshape(M, -1), z, gdn_out_norm[j], h, m_seg, seg,
                                   norm_ffn[l], hy_w_out[j], router)
        else:
            vres = None if j == 0 else (rk_v0[j - 1], rk_v1[j - 1], rk_v2[j - 1])
            r, v, kk, lw0, lw1, k0, k1, ra0, ra1, gate = rwkv_pre(
                h, m_seg, seg, T, L, norm_mix[l], rk_mu[j], rk_wr[j], rk_wk[j], rk_wv[j], rk_w0[j], rk_w1[j],
                rk_w2[j], rk_a0[j], rk_a1[j], rk_a2[j], rk_g1[j], rk_g2[j], rk_kk[j], rk_ka[j], vres, v_first)
            if j == 0:
                v_first = v
            b3 = lambda a: a.reshape(B, T, D)
            yf, yb = rwkv_scan(b3(r), b3(v), b3(kk), [b3(lw0), b3(lw1)], [b3(k0), b3(k1)], [b3(ra0), b3(ra1)], L)
            h, f, logits = rwkv_post(yf.reshape(M, D), yb.reshape(M, D), r, k0, k1, v, gate, rk_ln_w[j], rk_ln_b[j],
                                     rk_rk[j], h, m_seg, seg, norm_ffn[l], rk_wo[j], router)
        moe_out = hier_moe(f, logits, moe_b_group[l], moe_b_expert[l], moe_w1, moe_w3, moe_w2, l)
        h = (h.reshape(B * nseg, seg, D) + mod(5) * moe_out.reshape(B * nseg, seg, D)).reshape(M, D)
    return h.reshape(B, T, D)[:, L:]
```

```python
import functools
import math

import jax
import jax.numpy as jnp
from jax import lax
from jax.experimental import pallas as pl
from jax.experimental.pallas import tpu as pltpu

F32 = jnp.float32
BF16 = jnp.bfloat16
HI = lax.Precision.HIGHEST

DEPTH = 4
GRID_W = 64
EPS = 1e-6

MLA_HEADS = 8
MLA_Q_LORA = 256
MLA_KV_LORA = 128
MLA_NOPE = 64
MLA_ROPE = 32
MLA_V = 64
MLA_QK = MLA_NOPE + MLA_ROPE
MLA_SCALE = MLA_QK ** -0.5
ROPE_BASE = 10000.0
MLA_PAD = 128

GDN_HEADS = 4
GDN_DK = 128
GDN_DV = 128
GDN_CONV = 5
GDN_CHUNK = 64

RWKV_HEAD = 64
RWKV_CHUNK = 64
GN_EPS = 64e-5

MOE_GROUPS = 4
MOE_PER_GROUP = 8
MOE_EXPERTS = MOE_GROUPS * MOE_PER_GROUP
MOE_TOPK = 2
MOE_BLOCK = 256

MLA_COLS = MLA_Q_LORA + MLA_KV_LORA + MLA_ROPE
GDN_QKV = GDN_HEADS * (2 * GDN_DK + GDN_DV)
GDN_Z = GDN_HEADS * GDN_DV
GDN_AB = 2 * 2 * GDN_HEADS

VMEM_LIMIT_BYTES = 48 * 1024 * 1024

GDN_PASSES = 1
RWKV_PASSES = 1


def _cparams(*sem):
    return pltpu.CompilerParams(dimension_semantics=sem, vmem_limit_bytes=VMEM_LIMIT_BYTES)


def _pick(n, cands):
    for c in cands:
        if n % c == 0:
            return c
    return n


def _split(a):
    hi = a.astype(BF16)
    lo = (a - hi.astype(F32)).astype(BF16)
    return hi, lo


def _dg(a, b, dn, passes):
    if passes == 6:
        return lax.dot_general(a, b, dn, precision=HI, preferred_element_type=F32)
    if passes == 1:
        return lax.dot_general(a.astype(BF16), b.astype(BF16), dn, preferred_element_type=F32)
    ah, al = _split(a)
    bh, bl = _split(b)
    d = functools.partial(lax.dot_general, dimension_numbers=dn, preferred_element_type=F32)
    return d(ah, bh) + d(al, bh) + d(ah, bl)


_NN = (((1,), (0,)), ((), ()))
_NT = (((1,), (1,)), ((), ()))
_TN = (((0,), (0,)), ((), ()))
_BNN = (((2,), (1,)), ((0,), (0,)))
_BNT = (((2,), (2,)), ((0,), (0,)))
_BTN = (((1,), (1,)), ((0,), (0,)))


def _mm_body(x_ref, w_ref, o_ref, *, hi):
    if hi:
        o_ref[...] = jnp.dot(x_ref[...], w_ref[...], precision=HI, preferred_element_type=F32)
    else:
        o_ref[...] = jnp.dot(x_ref[...].astype(BF16), w_ref[...].astype(BF16),
                             preferred_element_type=F32)


def mm(x, w, hi=False):
    M, K = x.shape
    N = w.shape[1]
    tm = _pick(M, (512, 256, 128, 64, 32, 16, 8))
    tn = _pick(N, (512, 384, 256, 128))
    return pl.pallas_call(
        functools.partial(_mm_body, hi=hi),
        grid=(M // tm, N // tn),
        in_specs=[pl.BlockSpec((tm, K), lambda i, j: (i, 0)),
                  pl.BlockSpec((K, tn), lambda i, j: (0, j))],
        out_specs=pl.BlockSpec((tm, tn), lambda i, j: (i, j)),
        out_shape=jax.ShapeDtypeStruct((M, N), F32),
        compiler_params=_cparams("parallel", "parallel"),
        name="dense_mm",
    )(x, w)


def _attn_body(q_ref, k_ref, v_ref, o_ref, m_ref, acc_ref, *, c2):
    ki = pl.program_id(3)

    @pl.when(ki == 0)
    def _():
        m_ref[...] = jnp.full(m_ref.shape, -1e30, F32)
        acc_ref[...] = jnp.zeros(acc_ref.shape, F32)

    heads = range(2)
    sl = [slice(h * MLA_PAD, (h + 1) * MLA_PAD) for h in heads]
    m_prev = [m_ref[h] for h in heads]
    acc_prev = [acc_ref[h] for h in heads]
    s = [lax.dot_general(q_ref[0, :, sl[h]], k_ref[0, :, sl[h]], _NT, preferred_element_type=F32) for h in heads]
    m_new, alpha, p = [], [], []
    reps = s[0].shape[1] // MLA_PAD
    for h in heads:
        m_new.append(jnp.maximum(m_prev[h], jnp.max(s[h], axis=-1, keepdims=True)))
        alpha.append(jnp.exp2((m_prev[h] - m_new[h]) * c2))
        x = (s[h] - jnp.tile(m_new[h], (1, reps))) * c2
        p.append(jnp.exp2(x).astype(BF16))
    pv = [jnp.dot(p[h], v_ref[0, :, sl[h]], preferred_element_type=F32) for h in heads]
    for h in heads:
        acc_ref[h] = alpha[h] * acc_prev[h] + pv[h]
        m_ref[h] = m_new[h]

    @pl.when(ki == pl.num_programs(3) - 1)
    def _():
        outs = []
        for h in range(2):
            a = acc_ref[h]
            outs.append(a[:, :MLA_V] / a[:, MLA_V:MLA_V + 1])
        o_ref[0] = jnp.concatenate(outs, axis=-1)


def attention(q, k, v):
    B, Sq, _ = q.shape
    Sk = k.shape[1]
    tq = _pick(Sq, (1024, 512, 256, 128))
    tk = _pick(Sk, (1408, 768, 512, 384, 256, 128))
    return pl.pallas_call(
        functools.partial(_attn_body, c2=MLA_SCALE * math.log2(math.e)),
        grid=(B, MLA_HEADS // 2, Sq // tq, Sk // tk),
        in_specs=[pl.BlockSpec((1, tq, 2 * MLA_PAD), lambda b, p, i, j: (b, i, p)),
                  pl.BlockSpec((1, tk, 2 * MLA_PAD), lambda b, p, i, j: (b, j, p)),
                  pl.BlockSpec((1, tk, 2 * MLA_PAD), lambda b, p, i, j: (b, j, p))],
        out_specs=pl.BlockSpec((1, tq, 2 * MLA_V), lambda b, p, i, j: (b, i, p)),
        out_shape=jax.ShapeDtypeStruct((B, Sq, MLA_HEADS * MLA_V), F32),
        scratch_shapes=[pltpu.VMEM((2, tq, MLA_PAD), F32), pltpu.VMEM((2, tq, MLA_PAD), F32)],
        compiler_params=_cparams("parallel", "parallel", "parallel", "arbitrary"),
        name="mla_attention",
    )(q, k, v)


def _tri_masks(C, rev):
    row = lax.broadcasted_iota(jnp.int32, (C, C), 0)
    col = lax.broadcasted_iota(jnp.int32, (C, C), 1)
    if rev:
        return row <= col, row < col
    return row >= col, row > col


def _neumann_inverse(nil, dn, passes):
    C = nil.shape[-1]
    eye = (lax.broadcasted_iota(jnp.int32, (C, C), 0) ==
           lax.broadcasted_iota(jnp.int32, (C, C), 1)).astype(F32)
    x = eye + nil
    p = nil
    for _ in range(int(math.log2(C)) - 1):
        p = _dg(p, p, dn, passes)
        x = x + _dg(x, p, dn, passes)
    return x


def _gdn_body(qf, kf, vf, gcf, bcf, grf, qb, kb, vb, gcb, bcb, grb, of_ref, ob_ref, s_ref, *, passes):
    C = GDN_CHUNK
    H = GDN_HEADS

    @pl.when(pl.program_id(1) == 0)
    def _():
        s_ref[...] = jnp.zeros(s_ref.shape, F32)

    dirs = ((qf, kf, vf, gcf, bcf, grf), (qb, kb, vb, gcb, bcb, grb))
    qs, ks, vs, gcs, grs, betas, glast = [], [], [], [], [], [], []
    for d, (q_ref, k_ref, v_ref, gc_ref, bc_ref, gr_ref) in enumerate(dirs):
        rev = d == 1
        tri = _tri_masks(C, rev)[0].astype(F32)
        gcum_col = _dg(tri, gc_ref[0], _NN, 6)
        gcum_row = _dg(gr_ref[0, 0], tri, _NT, 6)
        beta_all = bc_ref[0]
        t_last = 0 if rev else C - 1
        for h in range(H):
            idx = d * H + h
            gcs.append(gcum_col[:, idx:idx + 1])
            grs.append(gcum_row[idx:idx + 1, :])
            glast.append(gcum_row[idx:idx + 1, t_last:t_last + 1])
            betas.append(beta_all[:, idx:idx + 1])
            qs.append(q_ref[0, :, h * GDN_DK:(h + 1) * GDN_DK])
            ks.append(k_ref[0, :, h * GDN_DK:(h + 1) * GDN_DK])
            vs.append(v_ref[0, :, h * GDN_DV:(h + 1) * GDN_DV])
    q, k, v = jnp.stack(qs), jnp.stack(ks), jnp.stack(vs)
    gc, gr, beta, g_last = jnp.stack(gcs), jnp.stack(grs), jnp.stack(betas), jnp.stack(glast)
    n = 2 * H
    unit = lax.broadcasted_iota(jnp.int32, (n, C, C), 0)
    ahead = (lax.broadcasted_iota(jnp.int32, (n, C, C), 1) - lax.broadcasted_iota(jnp.int32, (n, C, C), 2))
    ahead = jnp.where(unit < H, ahead, -ahead)
    incl = ahead >= 0
    strict = ahead > 0

    decay = jnp.exp(jnp.where(incl, gc - gr, -1e30))
    kbeta = k * beta
    lower = jnp.where(strict, _dg(kbeta, k, _BNT, passes) * decay, 0.0)
    tinv = _neumann_inverse(-lower, _BNN, passes)
    eg = jnp.exp(gc)
    u = _dg(tinv, v * beta, _BNN, passes)
    w = _dg(tinv, kbeta * eg, _BNN, passes)
    aqk = jnp.where(incl, _dg(q, k, _BNT, passes) * decay, 0.0)
    s = s_ref[...]
    v_new = u - _dg(w, s, _BNN, passes)
    o = _dg(q * eg, s, _BNN, passes) + _dg(aqk, v_new, _BNN, passes)
    s_ref[...] = s * jnp.exp(g_last) + _dg(k * jnp.exp(g_last - gc), v_new, _BTN, passes)
    for h in range(H):
        of_ref[0, :, h * GDN_DV:(h + 1) * GDN_DV] = o[h]
        ob_ref[0, :, h * GDN_DV:(h + 1) * GDN_DV] = o[H + h]


def _rev_chunk(i, ncc, nc):
    return jnp.where(i < ncc, ncc - 1 - i, nc - 1 + ncc - i)


def gdn_scan(q, k, v, g, beta, n_ctx):
    B, T, _ = q.shape
    C = GDN_CHUNK
    nc = T // C
    ncc = n_ctx // C
    gcol = g.reshape(B, T, 2 * GDN_HEADS)
    bcol = beta.reshape(B, T, 2 * GDN_HEADS)
    grow = jnp.swapaxes(gcol.reshape(B, nc, C, 2 * GDN_HEADS), 2, 3)
    fwd = lambda b, i: (b, i, 0)
    bwd = lambda b, i: (b, _rev_chunk(i, ncc, nc), 0)
    fwd4 = lambda b, i: (b, i, 0, 0)
    bwd4 = lambda b, i: (b, _rev_chunk(i, ncc, nc), 0, 0)
    wide = q.shape[-1]
    wv = v.shape[-1]

    def specs(m3, m4):
        return [pl.BlockSpec((1, C, wide), m3), pl.BlockSpec((1, C, wide), m3), pl.BlockSpec((1, C, wv), m3),
                pl.BlockSpec((1, C, 2 * GDN_HEADS), m3), pl.BlockSpec((1, C, 2 * GDN_HEADS), m3),
                pl.BlockSpec((1, 1, 2 * GDN_HEADS, C), m4)]

    of, ob = pl.pallas_call(
        functools.partial(_gdn_body, passes=GDN_PASSES),
        grid=(B, nc),
        in_specs=specs(fwd, fwd4) + specs(bwd, bwd4),
        out_specs=[pl.BlockSpec((1, C, wv), fwd), pl.BlockSpec((1, C, wv), bwd)],
        out_shape=[jax.ShapeDtypeStruct((B, T, wv), F32)] * 2,
        scratch_shapes=[pltpu.VMEM((2 * GDN_HEADS, GDN_DK, GDN_DV), F32)],
        compiler_params=_cparams("parallel", "arbitrary"),
        name="gdn_scan",
    )(q, k, v, gcol, bcol, grow, q, k, v, gcol, bcol, grow)
    return of, ob


def _rwkv_prep(r, lw, k, v, kk, rate, rev):
    C, D = r.shape
    N = RWKV_HEAD
    H = D // N
    incl, _ = _tri_masks(C, rev)
    tri = incl.astype(BF16)
    l1 = lw.astype(BF16)
    rem = lw - l1.astype(F32)
    l2 = rem.astype(BF16)
    l3 = (rem - l2.astype(F32)).astype(BF16)
    linc = _dotf(tri, l1) + _dotf(tri, l2) + _dotf(tri, l3)
    lexc = linc - lw
    ltot = linc[0:1, :] if rev else linc[C - 1:C, :]
    b = kk * rate
    einv = jnp.exp(-linc)
    etail = jnp.exp(ltot - linc)

    def hs(x):
        return jnp.stack([x[:, h * N:(h + 1) * N] for h in range(H)], axis=0)

    lhs = jnp.concatenate([hs(-kk * jnp.exp(lexc)), hs(r * jnp.exp(linc))], axis=1)
    rhs = jnp.concatenate([hs(b * einv), hs(k * einv)], axis=1)
    tail = jnp.concatenate([hs(b * etail), hs(k * etail)], axis=1)
    return lhs, rhs, tail, hs(v), jnp.exp(hs(ltot))


def _rwkv_body(rf, vf, kkf, lwf, kf, af, rb, vb, kkb, lwb, kb, ab, yf_ref, yb_ref, s_ref, *, passes):
    @pl.when(pl.program_id(1) == 0)
    def _():
        s_ref[...] = jnp.zeros(s_ref.shape, F32)

    C = rf.shape[1]
    H = rf.shape[2] // RWKV_HEAD
    dirs = ((rf, vf, kkf, lwf, kf, af), (rb, vb, kkb, lwb, kb, ab))
    parts = [_rwkv_prep(r_ref[0], lw_ref[0], k_ref[0], v_ref[0], kk_ref[0], a_ref[0], d == 1)
             for d, (r_ref, v_ref, kk_ref, lw_ref, k_ref, a_ref) in enumerate(dirs)]
    lhs, rhs, tail, vh, ptot = [jnp.concatenate([parts[0][j], parts[1][j]], axis=0) for j in range(5)]
    s = s_ref[...]
    shape = (2 * H, C, 2 * C)
    col = lax.broadcasted_iota(jnp.int32, shape, 2)
    ahead = lax.broadcasted_iota(jnp.int32, shape, 1) - jnp.where(col >= C, col - C, col)
    ahead = jnp.where(lax.broadcasted_iota(jnp.int32, shape, 0) < H, ahead, -ahead)
    sc = _dg(lhs, rhs, _BNT, passes)
    top = jnp.where(ahead > 0, sc[:, :C, :], 0.0)
    bot = jnp.where(ahead >= 0, sc[:, C:, :], 0.0)
    tinv = _neumann_inverse(top[:, :, :C], _BNN, passes)
    ars = _dg(lhs, s, _BNT, passes)
    zero_v = jnp.concatenate([jnp.zeros_like(vh), vh], axis=1)
    u = _dg(tinv, ars[:, :C, :] + _dg(top, zero_v, _BNN, passes), _BNN, passes)
    uv = jnp.concatenate([u, vh], axis=1)
    y = ars[:, C:, :] + _dg(bot, uv, _BNN, passes)
    s_ref[...] = s * ptot + _dg(uv, tail, _BTN, passes)
    yf_ref[0] = jnp.concatenate([y[h] for h in range(H)], axis=-1)
    yb_ref[0] = jnp.concatenate([y[H + h] for h in range(H)], axis=-1)


def rwkv_scan(r, v, kk, lw, key, rate, n_ctx):
    B, T, D = r.shape
    N = RWKV_HEAD
    C = RWKV_CHUNK
    nc = T // C
    ncc = n_ctx // C
    fwd = lambda b, i: (b, i, 0)
    bwd = lambda b, i: (b, _rev_chunk(i, ncc, nc), 0)
    blk = (1, C, D)
    return pl.pallas_call(
        functools.partial(_rwkv_body, passes=RWKV_PASSES),
        grid=(B, nc),
        in_specs=[pl.BlockSpec(blk, fwd)] * 6 + [pl.BlockSpec(blk, bwd)] * 6,
        out_specs=[pl.BlockSpec(blk, fwd), pl.BlockSpec(blk, bwd)],
        out_shape=[jax.ShapeDtypeStruct((B, T, D), F32)] * 2,
        scratch_shapes=[pltpu.VMEM((2 * (D // N), N, N), F32)],
        compiler_params=_cparams("parallel", "arbitrary"),
        name="rwkv7_scan",
    )(r, v, kk, lw[0], key[0], rate[0], r, v, kk, lw[1], key[1], rate[1])


def _moe_body(be_ref, nu_ref, x_ref, w1_ref, w3_ref, w2_ref, o_ref, w1b, w3b, w2b):
    i = pl.program_id(0)
    prev = be_ref[jnp.maximum(i - 1, 0)]
    used = i < nu_ref[0]

    @pl.when(used & ((i == 0) | (be_ref[i] != prev)))
    def _():
        w1b[...] = w1_ref[0, 0].astype(BF16)
        w3b[...] = w3_ref[0, 0].astype(BF16)
        w2b[...] = w2_ref[0, 0].astype(BF16)

    @pl.when(used)
    def _():
        x = x_ref[...]
        h1 = jnp.dot(x, w1b[...], preferred_element_type=F32)
        h3 = jnp.dot(x, w3b[...], preferred_element_type=F32)
        hid = (h1 * jax.nn.sigmoid(h1)) * h3
        o_ref[...] = jnp.dot(hid.astype(BF16), w2b[...], preferred_element_type=F32)

    @pl.when(jnp.logical_not(used))
    def _():
        o_ref[...] = jnp.zeros(o_ref.shape, F32)


def moe_experts(xs, blk_e, n_used, w1, w3, w2, layer):
    n_slots, D = xs.shape
    hid = w1.shape[-1]
    n_blocks = n_slots // MOE_BLOCK
    return pl.pallas_call(
        _moe_body,
        grid_spec=pltpu.PrefetchScalarGridSpec(
            num_scalar_prefetch=2,
            grid=(n_blocks,),
            in_specs=[pl.BlockSpec((MOE_BLOCK, D), lambda i, be, nu: (i, 0)),
                      pl.BlockSpec((1, 1, D, hid), lambda i, be, nu: (layer, be[i], 0, 0)),
                      pl.BlockSpec((1, 1, D, hid), lambda i, be, nu: (layer, be[i], 0, 0)),
                      pl.BlockSpec((1, 1, hid, D), lambda i, be, nu: (layer, be[i], 0, 0))],
            out_specs=pl.BlockSpec((MOE_BLOCK, D), lambda i, be, nu: (i, 0)),
            scratch_shapes=[pltpu.VMEM((D, hid), BF16), pltpu.VMEM((D, hid), BF16), pltpu.VMEM((hid, D), BF16)],
        ),
        out_shape=jax.ShapeDtypeStruct((n_slots, D), F32),
        compiler_params=_cparams("arbitrary"),
        name="moe_experts",
    )(blk_e, n_used, xs, w1, w3, w2)


def _route_body(lg_ref, bias_ref, out_ref, cnt_ref, run_ref, *, tm):
    @pl.when(pl.program_id(0) == 0)
    def _():
        run_ref[...] = jnp.zeros(run_ref.shape, F32)

    x = lg_ref[...] + bias_ref[...]
    lane = lax.broadcasted_iota(jnp.int32, x.shape, 1)
    far = 1 << 20

    def first_lane(hit):
        return jnp.min(jnp.where(hit, lane, far), axis=-1, keepdims=True)

    def masked_softmax(mask):
        xm = jnp.where(mask, x, -1e30)
        e = jnp.where(mask, jnp.exp(xm - jnp.max(xm, axis=-1, keepdims=True)), 0.0)
        return e / jnp.sum(e, axis=-1, keepdims=True)

    is_group = lane < MOE_GROUPS
    pg = masked_softmax(is_group)
    pg_top = jnp.max(pg, axis=-1, keepdims=True)
    g_idx = first_lane(is_group & (pg == pg_top))
    lo = MOE_GROUPS + MOE_PER_GROUP * g_idx
    in_group = (lane >= lo) & (lane < lo + MOE_PER_GROUP)
    pe = masked_softmax(in_group)
    p1 = jnp.max(pe, axis=-1, keepdims=True)
    l1 = first_lane(in_group & (pe == p1))
    rest_ok = in_group & (lane != l1)
    rest = jnp.where(rest_ok, pe, -1.0)
    p2 = jnp.max(rest, axis=-1, keepdims=True)
    l2 = first_lane(rest_ok & (rest == p2))
    psum = p1 + p2
    w1 = pg_top * p1 / psum
    w2 = pg_top * p2 / psum

    oh1 = (lane == l1).astype(F32)
    oh2 = (lane == l2).astype(F32)
    both = oh1 + oh2
    earlier = (lax.broadcasted_iota(jnp.int32, (tm, tm), 0) > lax.broadcasted_iota(jnp.int32, (tm, tm), 1))
    base = _dotf(earlier.astype(BF16), both.astype(BF16)) + run_ref[...]
    r1 = jnp.sum(base * oh1, axis=-1, keepdims=True)
    r2 = jnp.sum(base * oh2, axis=-1, keepdims=True)
    run_ref[...] = run_ref[...] + jnp.sum(both, axis=0, keepdims=True)
    cnt_ref[...] = run_ref[...]
    cols = ((l1 - MOE_GROUPS).astype(F32), (l2 - MOE_GROUPS).astype(F32), r1, r2, w1, w2)
    out = jnp.zeros(x.shape, F32)
    for j, c in enumerate(cols):
        out = jnp.where(lane == j, c, out)
    out_ref[...] = out


def moe_route(logits, b_group, b_expert):
    N, W = logits.shape
    tm = _pick(N, (512, 256, 128, 64, 32, 16, 8))
    bias = _pad_cols(jnp.concatenate([b_group, b_expert])[None, :], W)
    return pl.pallas_call(
        functools.partial(_route_body, tm=tm),
        grid=(N // tm,),
        in_specs=[pl.BlockSpec((tm, W), lambda i: (i, 0)), pl.BlockSpec((1, W), lambda i: (0, 0))],
        out_specs=[pl.BlockSpec((tm, W), lambda i: (i, 0)), pl.BlockSpec((1, W), lambda i: (0, 0))],
        out_shape=[jax.ShapeDtypeStruct((N, W), F32), jax.ShapeDtypeStruct((1, W), F32)],
        scratch_shapes=[pltpu.VMEM((1, W), F32)],
        compiler_params=_cparams("arbitrary"),
        name="moe_route",
    )(logits, bias)


MOE_PARTS = 2


def hier_moe(tokens, logits, b_group, b_expert, w1, w3, w2, layer):
    N = tokens.shape[0]
    parts = MOE_PARTS if N % (MOE_PARTS * MOE_BLOCK) == 0 else 1
    n = N // parts
    outs = [_moe_part(tokens[p * n:(p + 1) * n], logits[p * n:(p + 1) * n], b_group, b_expert, w1, w3, w2, layer)
            for p in range(parts)]
    return jnp.concatenate(outs, axis=0) if parts > 1 else outs[0]


def _moe_part(tokens, logits, b_group, b_expert, w1, w3, w2, layer):
    N, D = tokens.shape
    route, cnt = moe_route(logits, b_group, b_expert)
    eid = route[:, 0:MOE_TOPK].astype(jnp.int32).reshape(-1)
    rank = route[:, MOE_TOPK:2 * MOE_TOPK].astype(jnp.int32).reshape(-1)
    wts = route[:, 2 * MOE_TOPK:3 * MOE_TOPK]
    counts = cnt[0, MOE_GROUPS:MOE_GROUPS + MOE_EXPERTS].astype(jnp.int32)
    tok = jnp.repeat(jnp.arange(N, dtype=jnp.int32), MOE_TOPK)
    A = N * MOE_TOPK
    padded = (counts + MOE_BLOCK - 1) // MOE_BLOCK * MOE_BLOCK
    pend = jnp.cumsum(padded)
    dest = (pend - padded)[eid] + rank
    n_blocks = -(-A // MOE_BLOCK) + MOE_EXPERTS
    n_slots = n_blocks * MOE_BLOCK
    slot_tok = jnp.zeros((n_slots,), jnp.int32).at[dest].set(tok)
    starts = jnp.arange(n_blocks, dtype=jnp.int32)[:, None] * MOE_BLOCK
    blk_e = jnp.minimum(jnp.sum((pend[None, :] <= starts).astype(jnp.int32), axis=1), MOE_EXPERTS - 1)
    xs = tokens[slot_tok]
    ys = moe_experts(xs, blk_e, (pend[-1:] // MOE_BLOCK).astype(jnp.int32), w1, w3, w2, layer)
    d2 = dest.reshape(N, MOE_TOPK)
    return ys[d2[:, 0]] * wts[:, 0:1] + ys[d2[:, 1]] * wts[:, 1:2]


def _rope_tables(n_lat, n_ctx):
    rows = n_lat // GRID_W
    row = jnp.repeat(jnp.arange(rows, dtype=F32), GRID_W)
    col = jnp.tile(jnp.arange(GRID_W, dtype=F32), rows)
    n_freq = MLA_ROPE // 4
    inv = ROPE_BASE ** (-jnp.arange(n_freq, dtype=F32) / n_freq)
    ang = jnp.stack([row[:, None] * inv, col[:, None] * inv], axis=1)
    cos, sin = jnp.cos(ang), jnp.sin(ang)
    zf = jnp.zeros((n_lat, n_freq), F32)
    lat = lambda parts, fill: jnp.concatenate(
        [jnp.full((n_lat, MLA_NOPE), fill, F32)] + parts + [jnp.full((n_lat, MLA_PAD - MLA_QK), fill, F32)], axis=1)
    c = lat([cos[:, 0], cos[:, 0], cos[:, 1], cos[:, 1]], 1.0)
    s_lo = lat([-sin[:, 0], zf, -sin[:, 1], zf], 0.0)
    s_hi = lat([zf, sin[:, 0], zf, sin[:, 1]], 0.0)
    ctx = lambda fill: jnp.full((n_ctx, MLA_PAD), fill, F32)
    return jnp.concatenate([ctx(1.0), c], 0), jnp.concatenate([ctx(0.0), s_lo + s_hi], 0)


def _const_spec(shape):
    return pl.BlockSpec(shape, lambda i: (0,) * len(shape), pipeline_mode=pl.Buffered(1))


def _normmod(x, gain, shift, scale):
    return x * lax.rsqrt(jnp.mean(x * x, -1, keepdims=True) + EPS) * gain * (1 + scale) + shift


def _head_indicator(D, N):
    e = (jnp.arange(D)[:, None] // N == jnp.arange(128)[None, :]).astype(BF16)
    return e, e.T


def _seg_dot(x, e):
    xh, xl = _split(x)
    return jnp.dot(xh, e, preferred_element_type=F32) + jnp.dot(xl, e, preferred_element_type=F32)


def _dotf(a, b):
    return jnp.dot(a, b, preferred_element_type=F32)


def _rwkv_pre_body(*refs, tm, blocks_per_batch, ctx_blocks, vres):
    (h_ref, hp_ref, hn_ref, m_ref, gain_ref, mu_ref, w0_ref, a0_ref, kk_ref, ka_ref, e_ref, et_ref,
     wr_ref, wk_ref, wv_ref, w1_ref, w2_ref, a1_ref, a2_ref, g1_ref, g2_ref) = refs[:21]
    rest = refs[21:]
    if vres:
        v0_ref, v1_ref, v2_ref, vf_ref = rest[:4]
        rest = rest[4:]
    r_o, v_o, kk_o, lw0_o, lw1_o, k0_o, k1_o, ra0_o, ra1_o, gate_o = rest

    tb = pl.program_id(0) % blocks_per_batch
    seg_start = (tb == 0) | (tb == ctx_blocks)
    seg_end = (tb == ctx_blocks - 1) | (tb == blocks_per_batch - 1)
    shift, scale, gain = m_ref[0, 0:1, :], m_ref[0, 1:2, :], gain_ref[...]
    u = _normmod(h_ref[...], gain, shift, scale)
    up = jnp.where(seg_start, 0.0, _normmod(hp_ref[7:8, :], gain, shift, scale))
    un = jnp.where(seg_end, 0.0, _normmod(hn_ref[0:1, :], gain, shift, scale))
    row = lax.broadcasted_iota(jnp.int32, (tm, 1), 0)
    u_prev = jnp.where(row == 0, up, pltpu.roll(u, 1, 0))
    u_next = jnp.where(row == tm - 1, un, pltpu.roll(u, tm - 1, 0))
    xx = 0.5 * (u_prev + u_next) - u
    xr, xw, xk, xv, xa, xg = [(u + xx * mu_ref[j:j + 1, :]).astype(BF16) for j in range(6)]

    r = _dotf(xr, wr_ref[...])
    k = _dotf(xk, wk_ref[...])
    v = _dotf(xv, wv_ref[...])
    if vres:
        lo = _dotf(xv, v1_ref[...]).astype(BF16)
        v = v + (vf_ref[...] - v) * jax.nn.sigmoid(v0_ref[...] + _dotf(lo, v2_ref[...]))
    tl = jnp.tanh(_dotf(xw, w1_ref[...])).astype(BF16)
    al = _dotf(xa, a1_ref[...]).astype(BF16)
    gl = jax.nn.sigmoid(_dotf(xg, g1_ref[...])).astype(BF16)
    gate_o[...] = _dotf(gl, g2_ref[...])
    kx = k * kk_ref[...]
    inv = lax.rsqrt(_seg_dot(kx * kx, e_ref[...]) + EPS)
    r_o[...] = r
    v_o[...] = v
    kk_o[...] = kx * _seg_dot(inv, et_ref[...])
    for d, (lw_o, k_o, ra_o) in enumerate(((lw0_o, k0_o, ra0_o), (lw1_o, k1_o, ra1_o))):
        z = w0_ref[d:d + 1, :] + _dotf(tl, w2_ref[d])
        lw_o[...] = -math.exp(-0.5) * jax.nn.sigmoid(z)
        a = jax.nn.sigmoid(a0_ref[d:d + 1, :] + _dotf(al, a2_ref[d]))
        ra_o[...] = a
        k_o[...] = k * (1 + (a - 1) * ka_ref[...])


def _row_tile(seg):
    return _pick(seg, (256, 128, 64, 32, 16, 8))


def _pad_cols(w, n):
    return jnp.pad(w, ((0, 0), (0, n - w.shape[1])))


def _pad_rows(w, n):
    return jnp.pad(w, ((0, n - w.shape[0]), (0, 0)))


def rwkv_pre(h, m_seg, seg, T, n_ctx, gain, mu, wr, wk, wv, w0, w1, w2, a0, a1, a2, g1, g2, k_k, k_a, vres, v_first):
    M, D = h.shape
    tm = _row_tile(seg)
    lora = w1.shape[-1]
    e, et = _head_indicator(D, RWKV_HEAD)
    zero = jnp.zeros((lora, D), F32)
    w2p = jnp.stack([jnp.concatenate([w2[0], zero], 0), jnp.concatenate([zero, w2[1]], 0)]).astype(BF16)
    a2p = jnp.stack([jnp.concatenate([a2[0], zero], 0), jnp.concatenate([zero, a2[1]], 0)]).astype(BF16)
    gp = -(-g1.shape[1] // 128) * 128
    row = lambda a: a.reshape(1, D)
    consts = [row(gain), mu, w0, a0, row(k_k), row(k_a), e, et,
              wr.astype(BF16), wk.astype(BF16), wv.astype(BF16),
              jnp.concatenate([w1[0], w1[1]], 1).astype(BF16), w2p,
              jnp.concatenate([a1[0], a1[1]], 1).astype(BF16), a2p,
              _pad_cols(g1, gp).astype(BF16), _pad_rows(g2, gp).astype(BF16)]
    row_spec = pl.BlockSpec((tm, D), lambda i: (i, 0))
    last8 = M // 8 - 1
    in_specs = [row_spec,
                pl.BlockSpec((8, D), lambda i: (jnp.maximum(i * (tm // 8) - 1, 0), 0)),
                pl.BlockSpec((8, D), lambda i: (jnp.minimum((i + 1) * (tm // 8), last8), 0)),
                pl.BlockSpec((1, 6, D), lambda i: (i * tm // seg, 0, 0))]
    in_specs += [_const_spec(c.shape) for c in consts]
    args = [h, h, h, m_seg] + consts
    if vres is not None:
        v0, v1, v2 = vres
        extra = [row(v0), _pad_cols(v1, 128).astype(BF16), _pad_rows(v2, 128).astype(BF16)]
        in_specs += [_const_spec(c.shape) for c in extra] + [row_spec]
        args += extra + [v_first]
    return pl.pallas_call(
        functools.partial(_rwkv_pre_body, tm=tm, blocks_per_batch=T // tm, ctx_blocks=n_ctx // tm,
                          vres=vres is not None),
        grid=(M // tm,),
        in_specs=in_specs,
        out_specs=[row_spec] * 10,
        out_shape=[jax.ShapeDtypeStruct((M, D), F32)] * 10,
        compiler_params=_cparams("parallel"),
        name="rwkv7_pre",
    )(*args)


def _post_tail(xo, h_ref, m_ref, gain_ref, w_ref, wrt_ref, h_o, f_o, lg_o):
    h_new = h_ref[...] + m_ref[0, 2:3, :] * _dotf(xo, w_ref[...])
    h_o[...] = h_new
    f = _normmod(h_new, gain_ref[...], m_ref[0, 3:4, :], m_ref[0, 4:5, :])
    f_o[...] = f.astype(BF16)
    lg_o[...] = _dg(f, wrt_ref[...], _NN, 3)


def _rwkv_post_body(yf_ref, yb_ref, r_ref, k0_ref, k1_ref, v_ref, gate_ref, lnw_ref, lnb_ref, rk_ref, e_ref, et_ref,
                    h_ref, m_ref, gain_ref, w_ref, wrt_ref, h_o, f_o, lg_o):
    e, et = e_ref[...], et_ref[...]
    inv_n = 1.0 / RWKV_HEAD
    y = yf_ref[...] + yb_ref[...]
    yc = y - _seg_dot(_seg_dot(y, e) * inv_n, et)
    var = _seg_dot(_seg_dot(yc * yc, e) * inv_n, et)
    yn = yc * lax.rsqrt(var + GN_EPS) * lnw_ref[...] + lnb_ref[...]
    k_bonus = 0.5 * (k0_ref[...] + k1_ref[...])
    bonus = _seg_dot(_seg_dot(r_ref[...] * k_bonus * rk_ref[...], e), et) * v_ref[...]
    xo = ((yn + bonus) * gate_ref[...]).astype(BF16)
    _post_tail(xo, h_ref, m_ref, gain_ref, w_ref, wrt_ref, h_o, f_o, lg_o)


def _post_call(body, name, row_args, consts, h, m_seg, seg, gain, w_out, w_router):
    M, D = h.shape
    tm = _row_tile(seg)
    row_spec = lambda a: pl.BlockSpec((tm, a.shape[1]), lambda i: (i, 0))
    tail = [gain.reshape(1, D), w_out.astype(BF16), w_router]
    in_specs = ([row_spec(a) for a in row_args] + [_const_spec(c.shape) for c in consts] +
                [row_spec(h), pl.BlockSpec((1, 6, D), lambda i: (i * tm // seg, 0, 0))] +
                [_const_spec(c.shape) for c in tail])
    nr = w_router.shape[1]
    return pl.pallas_call(
        body,
        grid=(M // tm,),
        in_specs=in_specs,
        out_specs=[pl.BlockSpec((tm, D), lambda i: (i, 0)), pl.BlockSpec((tm, D), lambda i: (i, 0)),
                   pl.BlockSpec((tm, nr), lambda i: (i, 0))],
        out_shape=[jax.ShapeDtypeStruct((M, D), F32), jax.ShapeDtypeStruct((M, D), BF16),
                   jax.ShapeDtypeStruct((M, nr), F32)],
        compiler_params=_cparams("parallel"),
        name=name,
    )(*row_args, *consts, h, m_seg, *tail)


def _hy_post_body(a_ref, of_ref, ob_ref, z_ref, og_ref, h_ref, m_ref, gain_ref, w_ref, wrt_ref, h_o, f_o, lg_o):
    o = of_ref[...] + ob_ref[...]
    z = z_ref[...]
    parts = [a_ref[...]]
    for hd in range(GDN_HEADS):
        sl = slice(hd * GDN_DV, (hd + 1) * GDN_DV)
        oh, zh = o[:, sl], z[:, sl]
        on = oh * lax.rsqrt(jnp.mean(oh * oh, -1, keepdims=True) + EPS) * og_ref[...]
        parts.append(on * (zh * jax.nn.sigmoid(zh)))
    xo = jnp.concatenate(parts, axis=-1).astype(BF16)
    _post_tail(xo, h_ref, m_ref, gain_ref, w_ref, wrt_ref, h_o, f_o, lg_o)


def hy_post(a, of, ob, z, out_g, h, m_seg, seg, gain, w_out, w_router):
    return _post_call(_hy_post_body, "hybrid_post", [a, of, ob, z], [out_g.reshape(1, -1)], h, m_seg, seg,
                      gain, w_out, w_router)


def _hy_pre_body(h_ref, m_ref, gain_ref, c_ref, s_ref, wq1, wkv1, wpe, wpe2, wgq, wz, wab, qag, kvag,
                 wqb, wqb2, wkn, wv, qng, qng2, kng, kng2, q_o, k_o, v_o, gq_o, z_o, ab_o):
    u = _normmod(h_ref[...], gain_ref[...], m_ref[0, 0:1, :], m_ref[0, 1:2, :]).astype(BF16)
    gq_o[...] = _dotf(u, wgq[...])
    z_o[...] = _dotf(u, wz[...])
    ab_o[...] = _dotf(u, wab[...])
    cq = _dotf(u, wq1[...])
    ckv = _dotf(u, wkv1[...])
    pe = _dotf(u, wpe[...])
    pe2 = _dotf(u, wpe2[...])
    cq = (cq * lax.rsqrt(jnp.mean(cq * cq, -1, keepdims=True) + EPS) * qag[...]).astype(BF16)
    ckv = (ckv * lax.rsqrt(jnp.mean(ckv * ckv, -1, keepdims=True) + EPS) * kvag[...]).astype(BF16)
    q = _dotf(cq, wqb[...])
    q2 = _dotf(cq, wqb2[...])
    kn = _dotf(ckv, wkn[...])
    vv = _dotf(ckv, wv[...])
    c, s = c_ref[...], s_ref[...]
    qc, qs = qng[...] * c, qng2[...] * s
    kc, ks = kng[...] * c, kng2[...] * s
    k_rot = pe2 * ks
    one_col = (lax.broadcasted_iota(jnp.int32, (1, MLA_PAD), 1) == MLA_V).astype(F32)
    inv_d = 1.0 / MLA_QK

    def inv_rms(t):
        return lax.rsqrt(jnp.sum(t * t, -1, keepdims=True) * inv_d + EPS)

    for hd in range(MLA_HEADS):
        sl = slice(hd * MLA_PAD, (hd + 1) * MLA_PAD)
        qh = q[:, sl]
        kh = kn[:, sl] + pe
        q_o[:, sl] = (inv_rms(qh) * (qh * qc + q2[:, sl] * qs)).astype(BF16)
        k_o[:, sl] = (inv_rms(kh) * (kh * kc + k_rot)).astype(BF16)
        v_o[:, sl] = (vv[:, sl] + one_col).astype(BF16)


def _pad_heads(w, heads, width, to):
    K = w.shape[0]
    return jnp.pad(w.reshape(K, heads, width), ((0, 0), (0, 0), (0, to - width))).reshape(K, heads * to)


def hy_pre(h, m_seg, seg, T, gain, rope, w_in, qa_g, w_qb, kva_g, w_kvb, qn_g, kn_g):
    M, D = h.shape
    tm = _row_tile(seg)
    H = MLA_HEADS
    c0, c1, c2 = MLA_COLS, MLA_COLS + GDN_QKV, MLA_COLS + GDN_QKV + GDN_Z
    kvl = MLA_Q_LORA + MLA_KV_LORA
    wb = w_in.astype(BF16)
    wpe = jnp.pad(wb[:, kvl:c0], ((0, 0), (MLA_NOPE, MLA_PAD - MLA_QK)))
    wkv = w_kvb.reshape(MLA_KV_LORA, H, MLA_NOPE + MLA_V)
    pad1 = lambda g: jnp.pad(g, (0, MLA_PAD - MLA_QK)).reshape(1, MLA_PAD)
    lane = jnp.arange(MLA_PAD)
    rot = (lane >= MLA_NOPE) & (lane < MLA_QK)
    n_freq = MLA_ROPE // 4
    partner = jnp.where(rot, jnp.where(((lane - MLA_NOPE) // n_freq) % 2 == 0, lane + n_freq, lane - n_freq), lane)
    wqb_pad = _pad_heads(w_qb, H, MLA_QK, MLA_PAD).astype(BF16)
    wqb2 = wqb_pad.reshape(MLA_Q_LORA, H, MLA_PAD)[:, :, partner].reshape(MLA_Q_LORA, H * MLA_PAD)
    consts = [wb[:, :MLA_Q_LORA], wb[:, MLA_Q_LORA:kvl], wpe, wpe[:, partner], wb[:, c0:c1], wb[:, c1:c2],
              _pad_cols(wb[:, c2:], 128), qa_g.reshape(1, -1), kva_g.reshape(1, -1),
              wqb_pad, wqb2,
              _pad_heads(wkv[:, :, :MLA_NOPE].reshape(MLA_KV_LORA, -1), H, MLA_NOPE, MLA_PAD).astype(BF16),
              _pad_heads(wkv[:, :, MLA_NOPE:].reshape(MLA_KV_LORA, -1), H, MLA_V, MLA_PAD).astype(BF16),
              pad1(qn_g), pad1(qn_g)[:, partner], pad1(kn_g), pad1(kn_g)[:, partner]]
    bpb = T // tm
    row = lambda n: pl.BlockSpec((tm, n), lambda i: (i, 0))
    tab = pl.BlockSpec((tm, MLA_PAD), lambda i: (i % bpb, 0))
    in_specs = ([row(D), pl.BlockSpec((1, 6, D), lambda i: (i * tm // seg, 0, 0)), _const_spec((1, D)), tab, tab]
                + [_const_spec(c.shape) for c in consts])
    wide = H * MLA_PAD
    return pl.pallas_call(
        _hy_pre_body,
        grid=(M // tm,),
        in_specs=in_specs,
        out_specs=[row(wide), row(wide), row(wide), row(GDN_QKV), row(GDN_Z), row(128)],
        out_shape=[jax.ShapeDtypeStruct((M, wide), BF16)] * 3 + [jax.ShapeDtypeStruct((M, GDN_QKV), F32),
                   jax.ShapeDtypeStruct((M, GDN_Z), F32), jax.ShapeDtypeStruct((M, 128), F32)],
        compiler_params=_cparams("parallel"),
        name="hybrid_pre",
    )(h, m_seg, gain.reshape(1, D), *rope, *consts)


def _gdn_prep_body(x_ref, xp_ref, xn_ref, w_ref, q_o, k_o, v_o, *, tm, blocks_per_batch, ctx_blocks):
    tb = pl.program_id(0) % blocks_per_batch
    seg_start = (tb == 0) | (tb == ctx_blocks)
    seg_end = (tb == ctx_blocks - 1) | (tb == blocks_per_batch - 1)
    x = x_ref[...]
    xp = jnp.where(seg_start, 0.0, xp_ref[...])
    xn = jnp.where(seg_end, 0.0, xn_ref[...])
    row = lax.broadcasted_iota(jnp.int32, (tm, 1), 0)
    half = GDN_CONV // 2
    acc = x * w_ref[half:half + 1, :]
    for s in range(1, half + 1):
        before = pltpu.roll(x, s, 0)
        after = pltpu.roll(x, tm - s, 0)
        for r in range(s):
            before = jnp.where(row == r, xp[8 - s + r:8 - s + r + 1, :], before)
            after = jnp.where(row == tm - s + r, xn[r:r + 1, :], after)
        acc = acc + before * w_ref[half - s:half - s + 1, :] + after * w_ref[half + s:half + s + 1, :]
    y = acc * jax.nn.sigmoid(acc)
    nk = GDN_HEADS * GDN_DK
    for hd in range(GDN_HEADS):
        sl = slice(hd * GDN_DK, (hd + 1) * GDN_DK)
        qh = y[:, sl]
        kh = y[:, nk + hd * GDN_DK:nk + (hd + 1) * GDN_DK]
        q_o[:, sl] = qh * lax.rsqrt(jnp.sum(qh * qh, -1, keepdims=True) + EPS) * GDN_DK ** -0.5
        k_o[:, sl] = kh * lax.rsqrt(jnp.sum(kh * kh, -1, keepdims=True) + EPS)
    v_o[...] = y[:, 2 * nk:]


def gdn_prep(gq, conv_w, seg, T, n_ctx):
    M, W = gq.shape
    tm = _row_tile(seg)
    last8 = M // 8 - 1
    nk = GDN_HEADS * GDN_DK
    row = lambda n: pl.BlockSpec((tm, n), lambda i: (i, 0))
    return pl.pallas_call(
        functools.partial(_gdn_prep_body, tm=tm, blocks_per_batch=T // tm, ctx_blocks=n_ctx // tm),
        grid=(M // tm,),
        in_specs=[row(W),
                  pl.BlockSpec((8, W), lambda i: (jnp.maximum(i * (tm // 8) - 1, 0), 0)),
                  pl.BlockSpec((8, W), lambda i: (jnp.minimum((i + 1) * (tm // 8), last8), 0)),
                  _const_spec(conv_w.shape)],
        out_specs=[row(nk), row(nk), row(W - 2 * nk)],
        out_shape=[jax.ShapeDtypeStruct((M, nk), F32), jax.ShapeDtypeStruct((M, nk), F32),
                   jax.ShapeDtypeStruct((M, W - 2 * nk), F32)],
        compiler_params=_cparams("parallel"),
        name="gdn_prep",
    )(gq, gq, gq, conv_w)


def rwkv_post(yf, yb, r, k0, k1, v, gate, ln_w, ln_b, r_k, h, m_seg, seg, gain, wo, w_router):
    D = h.shape[1]
    e, et = _head_indicator(D, RWKV_HEAD)
    consts = [ln_w.reshape(1, D), ln_b.reshape(1, D), r_k.reshape(1, D), e, et]
    return _post_call(_rwkv_post_body, "rwkv7_post", [yf, yb, r, k0, k1, v, gate], consts, h, m_seg, seg,
                      gain, wo, w_router)


def kernel(x, c, ctx, c_ctx, ada_w, ada_b, norm_mix, norm_ffn, hy_w_in, hy_w_out, mla_qa_norm, mla_w_qb, mla_kva_norm, mla_w_kvb, mla_q_norm, mla_k_norm, gdn_conv, gdn_a_log, gdn_dt_bias, gdn_out_norm, rk_mu, rk_wr, rk_wk, rk_wv, rk_wo, rk_w0, rk_w1, rk_w2, rk_a0, rk_a1, rk_a2, rk_g1, rk_g2, rk_kk, rk_ka, rk_rk, rk_ln_w, rk_ln_b, rk_v0, rk_v1, rk_v2, moe_w_group, moe_b_group, moe_w_expert, moe_b_expert, moe_w1, moe_w3, moe_w2):
    B, S, D = x.shape
    L = ctx.shape[1]
    T = L + S
    depth = ada_w.shape[0]
    rope = _rope_tables(S, L)
    n_rows = -(-(B + 1) // 8) * 8
    sc = jnp.concatenate([jax.nn.silu(c), jax.nn.silu(c_ctx)[None], jnp.zeros((n_rows - B - 1, D), F32)], 0)
    M = B * T
    h = jnp.concatenate([ctx, x], axis=1).reshape(M, D)
    seg = math.gcd(L, S)
    nseg = T // seg
    v_first = None
    for l in range(depth):
        m = mm(sc, ada_w[l], hi=True) + ada_b[l]
        m_lat = jnp.broadcast_to(m[:B].reshape(B, 1, 6, D), (B, S // seg, 6, D))
        m_ctx = jnp.broadcast_to(m[B].reshape(1, 1, 6, D), (B, L // seg, 6, D))
        m_seg = jnp.concatenate([m_ctx, m_lat], axis=1).reshape(B * nseg, 6, D)

        def mod(i, m_seg=m_seg):
            return m_seg[:, None, i, :]

        router = _pad_cols(jnp.concatenate([moe_w_group[l], moe_w_expert[l]], axis=1), 128)
        j = l // 2
        b3 = lambda a: a.reshape(B, T, a.shape[-1])
        if l % 2 == 0:
            q, k, v, gq, z, ab = hy_pre(h, m_seg, seg, T, norm_mix[l], rope, hy_w_in[j], mla_qa_norm[j],
                                        mla_w_qb[j], mla_kva_norm[j], mla_w_kvb[j], mla_q_norm[j], mla_k_norm[j])
            q, k, v = b3(q), b3(k), b3(v)
            a_lat = attention(q[:, L:], k, v)
            a_ctx = attention(q[:, :L], k[:, :L], v[:, :L])
            a = jnp.concatenate([a_ctx, a_lat], axis=1).reshape(M, -1)
            gq_, gk_, gv_ = gdn_prep(gq, gdn_conv[j], seg, T, L)
            ab = ab[:, :GDN_AB].reshape(B, T, 2, 2, GDN_HEADS)
            g = -jnp.exp(gdn_a_log[j]) * jax.nn.softplus(ab[:, :, :, 0] + gdn_dt_bias[j])
            beta = jax.nn.sigmoid(ab[:, :, :, 1])
            of, ob = gdn_scan(b3(gq_), b3(gk_), b3(gv_), g, beta, L)
            h, f, logits = hy_post(a, of.reshape(M, -1), ob.reshape(M, -1), z, gdn_out_norm[j], h, m_seg, seg,
                                   norm_ffn[l], hy_w_out[j], router)
        else:
            vres = None if j == 0 else (rk_v0[j - 1], rk_v1[j - 1], rk_v2[j - 1])
            r, v, kk, lw0, lw1, k0, k1, ra0, ra1, gate = rwkv_pre(
                h, m_seg, seg, T, L, norm_mix[l], rk_mu[j], rk_wr[j], rk_wk[j], rk_wv[j], rk_w0[j], rk_w1[j],
                rk_w2[j], rk_a0[j], rk_a1[j], rk_a2[j], rk_g1[j], rk_g2[j], rk_kk[j], rk_ka[j], vres, v_first)
            if j == 0:
                v_first = v
            b3 = lambda a: a.reshape(B, T, D)
            yf, yb = rwkv_scan(b3(r), b3(v), b3(kk), [b3(lw0), b3(lw1)], [b3(k0), b3(k1)], [b3(ra0), b3(ra1)], L)
            h, f, logits = rwkv_post(yf.reshape(M, D), yb.reshape(M, D), r, k0, k1, v, gate, rk_ln_w[j], rk_ln_b[j],
                                     rk_rk[j], h, m_seg, seg, norm_ffn[l], rk_wo[j], router)
        moe_out = hier_moe(f, logits, moe_b_group[l], moe_b_expert[l], moe_w1, moe_w3, moe_w2, l)
        h = (h.reshape(B * nseg, seg, D) + mod(5) * moe_out.reshape(B * nseg, seg, D)).reshape(M, D)
    return h.reshape(B, T, D)[:, L:]
```

```python
import functools
import math

import jax
import jax.numpy as jnp
from jax import lax
from jax.experimental import pallas as pl
from jax.experimental.pallas import tpu as pltpu

F32 = jnp.float32
BF16 = jnp.bfloat16
HI = lax.Precision.HIGHEST

DEPTH = 4
GRID_W = 64
EPS = 1e-6

MLA_HEADS = 8
MLA_Q_LORA = 256
MLA_KV_LORA = 128
MLA_NOPE = 64
MLA_ROPE = 32
MLA_V = 64
MLA_QK = MLA_NOPE + MLA_ROPE
MLA_SCALE = MLA_QK ** -0.5
ROPE_BASE = 10000.0
MLA_PAD = 128

GDN_HEADS = 4
GDN_DK = 128
GDN_DV = 128
GDN_CONV = 5
GDN_CHUNK = 64

RWKV_HEAD = 64
RWKV_CHUNK = 64
GN_EPS = 64e-5

MOE_GROUPS = 4
MOE_PER_GROUP = 8
MOE_EXPERTS = MOE_GROUPS * MOE_PER_GROUP
MOE_TOPK = 2
MOE_BLOCK = 256

MLA_COLS = MLA_Q_LORA + MLA_KV_LORA + MLA_ROPE
GDN_QKV = GDN_HEADS * (2 * GDN_DK + GDN_DV)
GDN_Z = GDN_HEADS * GDN_DV
GDN_AB = 2 * 2 * GDN_HEADS

VMEM_LIMIT_BYTES = 48 * 1024 * 1024

GDN_PASSES = 1
RWKV_PASSES = 1


def _cparams(*sem):
    return pltpu.CompilerParams(dimension_semantics=sem, vmem_limit_bytes=VMEM_LIMIT_BYTES)


def _pick(n, cands):
    for c in cands:
        if n % c == 0:
            return c
    return n


def _split(a):
    hi = a.astype(BF16)
    lo = (a - hi.astype(F32)).astype(BF16)
    return hi, lo


def _dg(a, b, dn, passes):
    if passes == 6:
        return lax.dot_general(a, b, dn, precision=HI, preferred_element_type=F32)
    if passes == 1:
        return lax.dot_general(a.astype(BF16), b.astype(BF16), dn, preferred_element_type=F32)
    ah, al = _split(a)
    bh, bl = _split(b)
    d = functools.partial(lax.dot_general, dimension_numbers=dn, preferred_element_type=F32)
    return d(ah, bh) + d(al, bh) + d(ah, bl)


_NN = (((1,), (0,)), ((), ()))
_NT = (((1,), (1,)), ((), ()))
_TN = (((0,), (0,)), ((), ()))
_BNN = (((2,), (1,)), ((0,), (0,)))
_BNT = (((2,), (2,)), ((0,), (0,)))
_BTN = (((1,), (1,)), ((0,), (0,)))


def _mm_body(x_ref, w_ref, o_ref, *, hi):
    if hi:
        o_ref[...] = jnp.dot(x_ref[...], w_ref[...], precision=HI, preferred_element_type=F32)
    else:
        o_ref[...] = jnp.dot(x_ref[...].astype(BF16), w_ref[...].astype(BF16),
                             preferred_element_type=F32)


def mm(x, w, hi=False):
    M, K = x.shape
    N = w.shape[1]
    tm = _pick(M, (512, 256, 128, 64, 32, 16, 8))
    tn = _pick(N, (512, 384, 256, 128))
    return pl.pallas_call(
        functools.partial(_mm_body, hi=hi),
        grid=(M // tm, N // tn),
        in_specs=[pl.BlockSpec((tm, K), lambda i, j: (i, 0)),
                  pl.BlockSpec((K, tn), lambda i, j: (0, j))],
        out_specs=pl.BlockSpec((tm, tn), lambda i, j: (i, j)),
        out_shape=jax.ShapeDtypeStruct((M, N), F32),
        compiler_params=_cparams("parallel", "parallel"),
        name="dense_mm",
    )(x, w)


def _attn_body(q_ref, k_ref, v_ref, o_ref, m_ref, acc_ref, *, c2):
    ki = pl.program_id(3)

    @pl.when(ki == 0)
    def _():
        m_ref[...] = jnp.full(m_ref.shape, -1e30, F32)
        acc_ref[...] = jnp.zeros(acc_ref.shape, F32)

    heads = range(2)
    sl = [slice(h * MLA_PAD, (h + 1) * MLA_PAD) for h in heads]
    m_prev = [m_ref[h] for h in heads]
    acc_prev = [acc_ref[h] for h in heads]
    s = [lax.dot_general(q_ref[0, :, sl[h]], k_ref[0, :, sl[h]], _NT, preferred_element_type=F32) for h in heads]
    m_new, alpha, p = [], [], []
    reps = s[0].shape[1] // MLA_PAD
    for h in heads:
        m_new.append(jnp.maximum(m_prev[h], jnp.max(s[h], axis=-1, keepdims=True)))
        alpha.append(jnp.exp2((m_prev[h] - m_new[h]) * c2))
        x = (s[h] - jnp.tile(m_new[h], (1, reps))) * c2
        p.append(jnp.exp2(x).astype(BF16))
    pv = [jnp.dot(p[h], v_ref[0, :, sl[h]], preferred_element_type=F32) for h in heads]
    for h in heads:
        acc_ref[h] = alpha[h] * acc_prev[h] + pv[h]
        m_ref[h] = m_new[h]

    @pl.when(ki == pl.num_programs(3) - 1)
    def _():
        outs = []
        for h in range(2):
            a = acc_ref[h]
            outs.append(a[:, :MLA_V] / a[:, MLA_V:MLA_V + 1])
        o_ref[0] = jnp.concatenate(outs, axis=-1)


def attention(q, k, v):
    B, Sq, _ = q.shape
    Sk = k.shape[1]
    tq = _pick(Sq, (1024, 512, 256, 128))
    tk = _pick(Sk, (1408, 768, 512, 384, 256, 128))
    return pl.pallas_call(
        functools.partial(_attn_body, c2=MLA_SCALE * math.log2(math.e)),
        grid=(B, MLA_HEADS // 2, Sq // tq, Sk // tk),
        in_specs=[pl.BlockSpec((1, tq, 2 * MLA_PAD), lambda b, p, i, j: (b, i, p)),
                  pl.BlockSpec((1, tk, 2 * MLA_PAD), lambda b, p, i, j: (b, j, p)),
                  pl.BlockSpec((1, tk, 2 * MLA_PAD), lambda b, p, i, j: (b, j, p))],
        out_specs=pl.BlockSpec((1, tq, 2 * MLA_V), lambda b, p, i, j: (b, i, p)),
        out_shape=jax.ShapeDtypeStruct((B, Sq, MLA_HEADS * MLA_V), F32),
        scratch_shapes=[pltpu.VMEM((2, tq, MLA_PAD), F32), pltpu.VMEM((2, tq, MLA_PAD), F32)],
        compiler_params=_cparams("parallel", "parallel", "parallel", "arbitrary"),
        name="mla_attention",
    )(q, k, v)


def _tri_masks(C, rev):
    row = lax.broadcasted_iota(jnp.int32, (C, C), 0)
    col = lax.broadcasted_iota(jnp.int32, (C, C), 1)
    if rev:
        return row <= col, row < col
    return row >= col, row > col


def _neumann_inverse(nil, dn, passes):
    C = nil.shape[-1]
    eye = (lax.broadcasted_iota(jnp.int32, (C, C), 0) ==
           lax.broadcasted_iota(jnp.int32, (C, C), 1)).astype(F32)
    x = eye + nil
    p = nil
    for _ in range(int(math.log2(C)) - 1):
        p = _dg(p, p, dn, passes)
        x = x + _dg(x, p, dn, passes)
    return x


def _gdn_body(qf, kf, vf, gcf, bcf, grf, qb, kb, vb, gcb, bcb, grb, of_ref, ob_ref, s_ref, *, passes):
    C = GDN_CHUNK
    H = GDN_HEADS

    @pl.when(pl.program_id(1) == 0)
    def _():
        s_ref[...] = jnp.zeros(s_ref.shape, F32)

    dirs = ((qf, kf, vf, gcf, bcf, grf), (qb, kb, vb, gcb, bcb, grb))
    per_step = qf.shape[1] // C
    n = 2 * H
    unit = lax.broadcasted_iota(jnp.int32, (n, C, C), 0)
    ahead = (lax.broadcasted_iota(jnp.int32, (n, C, C), 1) - lax.broadcasted_iota(jnp.int32, (n, C, C), 2))
    ahead = jnp.where(unit < H, ahead, -ahead)
    incl = ahead >= 0
    strict = ahead > 0
    s = s_ref[...]
    for j in range(per_step):
        sub = (j, per_step - 1 - j)
        qs, ks, vs, gcs, grs, betas, glast = [], [], [], [], [], [], []
        for d, (q_ref, k_ref, v_ref, gc_ref, bc_ref, gr_ref) in enumerate(dirs):
            rev = d == 1
            rows = slice(sub[d] * C, (sub[d] + 1) * C)
            tri = _tri_masks(C, rev)[0].astype(F32)
            gcum_col = _dg(tri, gc_ref[0, rows, :], _NN, 6)
            gcum_row = _dg(gr_ref[0, sub[d]], tri, _NT, 6)
            beta_all = bc_ref[0, rows, :]
            t_last = 0 if rev else C - 1
            for h in range(H):
                idx = d * H + h
                gcs.append(gcum_col[:, idx:idx + 1])
                grs.append(gcum_row[idx:idx + 1, :])
                glast.append(gcum_row[idx:idx + 1, t_last:t_last + 1])
                betas.append(beta_all[:, idx:idx + 1])
                qs.append(q_ref[0, rows, h * GDN_DK:(h + 1) * GDN_DK])
                ks.append(k_ref[0, rows, h * GDN_DK:(h + 1) * GDN_DK])
                vs.append(v_ref[0, rows, h * GDN_DV:(h + 1) * GDN_DV])
        q, k, v = jnp.stack(qs), jnp.stack(ks), jnp.stack(vs)
        gc, gr, beta, g_last = jnp.stack(gcs), jnp.stack(grs), jnp.stack(betas), jnp.stack(glast)

        decay = jnp.exp(jnp.where(incl, gc - gr, -1e30))
        kbeta = k * beta
        lower = jnp.where(strict, _dg(kbeta, k, _BNT, passes) * decay, 0.0)
        tinv = _neumann_inverse(-lower, _BNN, passes)
        eg = jnp.exp(gc)
        u = _dg(tinv, v * beta, _BNN, passes)
        w = _dg(tinv, kbeta * eg, _BNN, passes)
        aqk = jnp.where(incl, _dg(q, k, _BNT, passes) * decay, 0.0)
        v_new = u - _dg(w, s, _BNN, passes)
        o = _dg(q * eg, s, _BNN, passes) + _dg(aqk, v_new, _BNN, passes)
        s = s * jnp.exp(g_last) + _dg(k * jnp.exp(g_last - gc), v_new, _BTN, passes)
        for h in range(H):
            of_ref[0, sub[0] * C:(sub[0] + 1) * C, h * GDN_DV:(h + 1) * GDN_DV] = o[h]
            ob_ref[0, sub[1] * C:(sub[1] + 1) * C, h * GDN_DV:(h + 1) * GDN_DV] = o[H + h]
    s_ref[...] = s


SCAN_CHUNKS_PER_STEP = 2


def _scan_rows(chunk, T, n_ctx):
    rows = chunk * SCAN_CHUNKS_PER_STEP
    return rows if n_ctx % rows == 0 and (T - n_ctx) % rows == 0 else chunk


def _rev_chunk(i, ncc, nc):
    return jnp.where(i < ncc, ncc - 1 - i, nc - 1 + ncc - i)


def gdn_scan(q, k, v, g, beta, n_ctx):
    B, T, _ = q.shape
    C = GDN_CHUNK
    rows = _scan_rows(C, T, n_ctx)
    per_step = rows // C
    nc = T // rows
    ncc = n_ctx // rows
    gcol = g.reshape(B, T, 2 * GDN_HEADS)
    bcol = beta.reshape(B, T, 2 * GDN_HEADS)
    grow = jnp.swapaxes(gcol.reshape(B, T // C, C, 2 * GDN_HEADS), 2, 3)
    fwd = lambda b, i: (b, i, 0)
    bwd = lambda b, i: (b, _rev_chunk(i, ncc, nc), 0)
    fwd4 = lambda b, i: (b, i, 0, 0)
    bwd4 = lambda b, i: (b, _rev_chunk(i, ncc, nc), 0, 0)
    wide = q.shape[-1]
    wv = v.shape[-1]

    def specs(m3, m4):
        return [pl.BlockSpec((1, rows, wide), m3), pl.BlockSpec((1, rows, wide), m3), pl.BlockSpec((1, rows, wv), m3),
                pl.BlockSpec((1, rows, 2 * GDN_HEADS), m3), pl.BlockSpec((1, rows, 2 * GDN_HEADS), m3),
                pl.BlockSpec((1, per_step, 2 * GDN_HEADS, C), m4)]

    of, ob = pl.pallas_call(
        functools.partial(_gdn_body, passes=GDN_PASSES),
        grid=(B, nc),
        in_specs=specs(fwd, fwd4) + specs(bwd, bwd4),
        out_specs=[pl.BlockSpec((1, rows, wv), fwd), pl.BlockSpec((1, rows, wv), bwd)],
        out_shape=[jax.ShapeDtypeStruct((B, T, wv), F32)] * 2,
        scratch_shapes=[pltpu.VMEM((2 * GDN_HEADS, GDN_DK, GDN_DV), F32)],
        compiler_params=_cparams("parallel", "arbitrary"),
        name="gdn_scan",
    )(q, k, v, gcol, bcol, grow, q, k, v, gcol, bcol, grow)
    return of, ob


def _rwkv_prep(r, lw, k, v, kk, rate, rev):
    C, D = r.shape
    N = RWKV_HEAD
    H = D // N
    incl, _ = _tri_masks(C, rev)
    tri = incl.astype(BF16)
    l1 = lw.astype(BF16)
    rem = lw - l1.astype(F32)
    l2 = rem.astype(BF16)
    l3 = (rem - l2.astype(F32)).astype(BF16)
    linc = _dotf(tri, l1) + _dotf(tri, l2) + _dotf(tri, l3)
    lexc = linc - lw
    ltot = linc[0:1, :] if rev else linc[C - 1:C, :]
    b = kk * rate
    einv = jnp.exp(-linc)
    etail = jnp.exp(ltot - linc)

    def hs(x):
        return jnp.stack([x[:, h * N:(h + 1) * N] for h in range(H)], axis=0)

    lhs = jnp.concatenate([hs(-kk * jnp.exp(lexc)), hs(r * jnp.exp(linc))], axis=1)
    rhs = jnp.concatenate([hs(b * einv), hs(k * einv)], axis=1)
    tail = jnp.concatenate([hs(b * etail), hs(k * etail)], axis=1)
    return lhs, rhs, tail, hs(v), jnp.exp(hs(ltot))


def _rwkv_body(rf, vf, kkf, lwf, kf, af, rb, vb, kkb, lwb, kb, ab, yf_ref, yb_ref, s_ref, *, passes):
    @pl.when(pl.program_id(1) == 0)
    def _():
        s_ref[...] = jnp.zeros(s_ref.shape, F32)

    C = RWKV_CHUNK
    per_step = rf.shape[1] // C
    H = rf.shape[2] // RWKV_HEAD
    dirs = ((rf, vf, kkf, lwf, kf, af), (rb, vb, kkb, lwb, kb, ab))
    shape = (2 * H, C, 2 * C)
    col = lax.broadcasted_iota(jnp.int32, shape, 2)
    ahead = lax.broadcasted_iota(jnp.int32, shape, 1) - jnp.where(col >= C, col - C, col)
    ahead = jnp.where(lax.broadcasted_iota(jnp.int32, shape, 0) < H, ahead, -ahead)
    s = s_ref[...]
    for j in range(per_step):
        rows = (slice(j * C, (j + 1) * C), slice((per_step - 1 - j) * C, (per_step - j) * C))
        parts = [_rwkv_prep(r_ref[0, rows[d], :], lw_ref[0, rows[d], :], k_ref[0, rows[d], :], v_ref[0, rows[d], :],
                            kk_ref[0, rows[d], :], a_ref[0, rows[d], :], d == 1)
                 for d, (r_ref, v_ref, kk_ref, lw_ref, k_ref, a_ref) in enumerate(dirs)]
        lhs, rhs, tail, vh, ptot = [jnp.concatenate([parts[0][i], parts[1][i]], axis=0) for i in range(5)]
        sc = _dg(lhs, rhs, _BNT, passes)
        top = jnp.where(ahead > 0, sc[:, :C, :], 0.0)
        bot = jnp.where(ahead >= 0, sc[:, C:, :], 0.0)
        tinv = _neumann_inverse(top[:, :, :C], _BNN, passes)
        ars = _dg(lhs, s, _BNT, passes)
        zero_v = jnp.concatenate([jnp.zeros_like(vh), vh], axis=1)
        u = _dg(tinv, ars[:, :C, :] + _dg(top, zero_v, _BNN, passes), _BNN, passes)
        uv = jnp.concatenate([u, vh], axis=1)
        y = ars[:, C:, :] + _dg(bot, uv, _BNN, passes)
        s = s * ptot + _dg(uv, tail, _BTN, passes)
        yf_ref[0, rows[0], :] = jnp.concatenate([y[h] for h in range(H)], axis=-1)
        yb_ref[0, rows[1], :] = jnp.concatenate([y[H + h] for h in range(H)], axis=-1)
    s_ref[...] = s


def rwkv_scan(r, v, kk, lw, key, rate, n_ctx):
    B, T, D = r.shape
    N = RWKV_HEAD
    rows = _scan_rows(RWKV_CHUNK, T, n_ctx)
    nc = T // rows
    ncc = n_ctx // rows
    fwd = lambda b, i: (b, i, 0)
    bwd = lambda b, i: (b, _rev_chunk(i, ncc, nc), 0)
    blk = (1, rows, D)
    return pl.pallas_call(
        functools.partial(_rwkv_body, passes=RWKV_PASSES),
        grid=(B, nc),
        in_specs=[pl.BlockSpec(blk, fwd)] * 6 + [pl.BlockSpec(blk, bwd)] * 6,
        out_specs=[pl.BlockSpec(blk, fwd), pl.BlockSpec(blk, bwd)],
        out_shape=[jax.ShapeDtypeStruct((B, T, D), F32)] * 2,
        scratch_shapes=[pltpu.VMEM((2 * (D // N), N, N), F32)],
        compiler_params=_cparams("parallel", "arbitrary"),
        name="rwkv7_scan",
    )(r, v, kk, lw[0], key[0], rate[0], r, v, kk, lw[1], key[1], rate[1])


def _moe_body(be_ref, nu_ref, x_ref, w1_ref, w3_ref, w2_ref, o_ref, w1b, w3b, w2b):
    i = pl.program_id(0)
    prev = be_ref[jnp.maximum(i - 1, 0)]
    used = i < nu_ref[0]

    @pl.when(used & ((i == 0) | (be_ref[i] != prev)))
    def _():
        w1b[...] = w1_ref[0, 0].astype(BF16)
        w3b[...] = w3_ref[0, 0].astype(BF16)
        w2b[...] = w2_ref[0, 0].astype(BF16)

    @pl.when(used)
    def _():
        x = x_ref[...]
        h1 = jnp.dot(x, w1b[...], preferred_element_type=F32)
        h3 = jnp.dot(x, w3b[...], preferred_element_type=F32)
        hid = (h1 * jax.nn.sigmoid(h1)) * h3
        o_ref[...] = jnp.dot(hid.astype(BF16), w2b[...], preferred_element_type=F32)

    @pl.when(jnp.logical_not(used))
    def _():
        o_ref[...] = jnp.zeros(o_ref.shape, F32)


def moe_experts(xs, blk_e, n_used, w1, w3, w2, layer):
    n_slots, D = xs.shape
    hid = w1.shape[-1]
    n_blocks = n_slots // MOE_BLOCK
    return pl.pallas_call(
        _moe_body,
        grid_spec=pltpu.PrefetchScalarGridSpec(
            num_scalar_prefetch=2,
            grid=(n_blocks,),
            in_specs=[pl.BlockSpec((MOE_BLOCK, D), lambda i, be, nu: (i, 0)),
                      pl.BlockSpec((1, 1, D, hid), lambda i, be, nu: (layer, be[i], 0, 0)),
                      pl.BlockSpec((1, 1, D, hid), lambda i, be, nu: (layer, be[i], 0, 0)),
                      pl.BlockSpec((1, 1, hid, D), lambda i, be, nu: (layer, be[i], 0, 0))],
            out_specs=pl.BlockSpec((MOE_BLOCK, D), lambda i, be, nu: (i, 0)),
            scratch_shapes=[pltpu.VMEM((D, hid), BF16), pltpu.VMEM((D, hid), BF16), pltpu.VMEM((hid, D), BF16)],
        ),
        out_shape=jax.ShapeDtypeStruct((n_slots, D), F32),
        compiler_params=_cparams("arbitrary"),
        name="moe_experts",
    )(blk_e, n_used, xs, w1, w3, w2)


def _route_body(lg_ref, bias_ref, out_ref, cnt_ref, run_ref, *, tm):
    @pl.when(pl.program_id(0) == 0)
    def _():
        run_ref[...] = jnp.zeros(run_ref.shape, F32)

    x = lg_ref[...] + bias_ref[...]
    lane = lax.broadcasted_iota(jnp.int32, x.shape, 1)
    far = 1 << 20

    def first_lane(hit):
        return jnp.min(jnp.where(hit, lane, far), axis=-1, keepdims=True)

    def masked_softmax(mask):
        xm = jnp.where(mask, x, -1e30)
        e = jnp.where(mask, jnp.exp(xm - jnp.max(xm, axis=-1, keepdims=True)), 0.0)
        return e / jnp.sum(e, axis=-1, keepdims=True)

    is_group = lane < MOE_GROUPS
    pg = masked_softmax(is_group)
    pg_top = jnp.max(pg, axis=-1, keepdims=True)
    g_idx = first_lane(is_group & (pg == pg_top))
    lo = MOE_GROUPS + MOE_PER_GROUP * g_idx
    in_group = (lane >= lo) & (lane < lo + MOE_PER_GROUP)
    pe = masked_softmax(in_group)
    p1 = jnp.max(pe, axis=-1, keepdims=True)
    l1 = first_lane(in_group & (pe == p1))
    rest_ok = in_group & (lane != l1)
    rest = jnp.where(rest_ok, pe, -1.0)
    p2 = jnp.max(rest, axis=-1, keepdims=True)
    l2 = first_lane(rest_ok & (rest == p2))
    psum = p1 + p2
    w1 = pg_top * p1 / psum
    w2 = pg_top * p2 / psum

    oh1 = (lane == l1).astype(F32)
    oh2 = (lane == l2).astype(F32)
    both = oh1 + oh2
    earlier = (lax.broadcasted_iota(jnp.int32, (tm, tm), 0) > lax.broadcasted_iota(jnp.int32, (tm, tm), 1))
    base = _dotf(earlier.astype(BF16), both.astype(BF16)) + run_ref[...]
    r1 = jnp.sum(base * oh1, axis=-1, keepdims=True)
    r2 = jnp.sum(base * oh2, axis=-1, keepdims=True)
    run_ref[...] = run_ref[...] + jnp.sum(both, axis=0, keepdims=True)
    cnt_ref[...] = run_ref[...]
    cols = ((l1 - MOE_GROUPS).astype(F32), (l2 - MOE_GROUPS).astype(F32), r1, r2, w1, w2)
    out = jnp.zeros(x.shape, F32)
    for j, c in enumerate(cols):
        out = jnp.where(lane == j, c, out)
    out_ref[...] = out


def moe_route(logits, b_group, b_expert):
    N, W = logits.shape
    tm = _pick(N, (512, 256, 128, 64, 32, 16, 8))
    bias = _pad_cols(jnp.concatenate([b_group, b_expert])[None, :], W)
    return pl.pallas_call(
        functools.partial(_route_body, tm=tm),
        grid=(N // tm,),
        in_specs=[pl.BlockSpec((tm, W), lambda i: (i, 0)), pl.BlockSpec((1, W), lambda i: (0, 0))],
        out_specs=[pl.BlockSpec((tm, W), lambda i: (i, 0)), pl.BlockSpec((1, W), lambda i: (0, 0))],
        out_shape=[jax.ShapeDtypeStruct((N, W), F32), jax.ShapeDtypeStruct((1, W), F32)],
        scratch_shapes=[pltpu.VMEM((1, W), F32)],
        compiler_params=_cparams("arbitrary"),
        name="moe_route",
    )(logits, bias)


def hier_moe(tokens, logits, b_group, b_expert, w1, w3, w2, layer):
    N, D = tokens.shape
    route, cnt = moe_route(logits, b_group, b_expert)
    eid = route[:, 0:MOE_TOPK].astype(jnp.int32).reshape(-1)
    rank = route[:, MOE_TOPK:2 * MOE_TOPK].astype(jnp.int32).reshape(-1)
    wts = route[:, 2 * MOE_TOPK:3 * MOE_TOPK]
    counts = cnt[0, MOE_GROUPS:MOE_GROUPS + MOE_EXPERTS].astype(jnp.int32)
    tok = jnp.repeat(jnp.arange(N, dtype=jnp.int32), MOE_TOPK)
    A = N * MOE_TOPK
    padded = (counts + MOE_BLOCK - 1) // MOE_BLOCK * MOE_BLOCK
    pend = jnp.cumsum(padded)
    dest = (pend - padded)[eid] + rank
    n_blocks = -(-A // MOE_BLOCK) + MOE_EXPERTS
    n_slots = n_blocks * MOE_BLOCK
    slot_tok = jnp.zeros((n_slots,), jnp.int32).at[dest].set(tok)
    starts = jnp.arange(n_blocks, dtype=jnp.int32)[:, None] * MOE_BLOCK
    blk_e = jnp.minimum(jnp.sum((pend[None, :] <= starts).astype(jnp.int32), axis=1), MOE_EXPERTS - 1)
    xs = tokens[slot_tok]
    ys = moe_experts(xs, blk_e, (pend[-1:] // MOE_BLOCK).astype(jnp.int32), w1, w3, w2, layer)
    d2 = dest.reshape(N, MOE_TOPK)
    return ys[d2[:, 0]] * wts[:, 0:1] + ys[d2[:, 1]] * wts[:, 1:2]


def _rope_tables(n_lat, n_ctx):
    rows = n_lat // GRID_W
    row = jnp.repeat(jnp.arange(rows, dtype=F32), GRID_W)
    col = jnp.tile(jnp.arange(GRID_W, dtype=F32), rows)
    n_freq = MLA_ROPE // 4
    inv = ROPE_BASE ** (-jnp.arange(n_freq, dtype=F32) / n_freq)
    ang = jnp.stack([row[:, None] * inv, col[:, None] * inv], axis=1)
    cos, sin = jnp.cos(ang), jnp.sin(ang)
    zf = jnp.zeros((n_lat, n_freq), F32)
    lat = lambda parts, fill: jnp.concatenate(
        [jnp.full((n_lat, MLA_NOPE), fill, F32)] + parts + [jnp.full((n_lat, MLA_PAD - MLA_QK), fill, F32)], axis=1)
    c = lat([cos[:, 0], cos[:, 0], cos[:, 1], cos[:, 1]], 1.0)
    s_lo = lat([-sin[:, 0], zf, -sin[:, 1], zf], 0.0)
    s_hi = lat([zf, sin[:, 0], zf, sin[:, 1]], 0.0)
    ctx = lambda fill: jnp.full((n_ctx, MLA_PAD), fill, F32)
    return jnp.concatenate([ctx(1.0), c], 0), jnp.concatenate([ctx(0.0), s_lo + s_hi], 0)


def _const_spec(shape):
    return pl.BlockSpec(shape, lambda i: (0,) * len(shape), pipeline_mode=pl.Buffered(1))


def _normmod(x, gain, shift, scale):
    return x * lax.rsqrt(jnp.mean(x * x, -1, keepdims=True) + EPS) * gain * (1 + scale) + shift


def _head_indicator(D, N):
    e = (jnp.arange(D)[:, None] // N == jnp.arange(128)[None, :]).astype(BF16)
    return e, e.T


def _seg_dot(x, e):
    xh, xl = _split(x)
    return jnp.dot(xh, e, preferred_element_type=F32) + jnp.dot(xl, e, preferred_element_type=F32)


def _dotf(a, b):
    return jnp.dot(a, b, preferred_element_type=F32)


def _rwkv_pre_body(*refs, tm, blocks_per_batch, ctx_blocks, vres):
    (h_ref, hp_ref, hn_ref, m_ref, gain_ref, mu_ref, w0_ref, a0_ref, kk_ref, ka_ref, e_ref, et_ref,
     wr_ref, wk_ref, wv_ref, w1_ref, w2_ref, a1_ref, a2_ref, g1_ref, g2_ref) = refs[:21]
    rest = refs[21:]
    if vres:
        v0_ref, v1_ref, v2_ref, vf_ref = rest[:4]
        rest = rest[4:]
    r_o, v_o, kk_o, lw0_o, lw1_o, k0_o, k1_o, ra0_o, ra1_o, gate_o = rest

    tb = pl.program_id(0) % blocks_per_batch
    seg_start = (tb == 0) | (tb == ctx_blocks)
    seg_end = (tb == ctx_blocks - 1) | (tb == blocks_per_batch - 1)
    shift, scale, gain = m_ref[0, 0:1, :], m_ref[0, 1:2, :], gain_ref[...]
    u = _normmod(h_ref[...], gain, shift, scale)
    up = jnp.where(seg_start, 0.0, _normmod(hp_ref[7:8, :], gain, shift, scale))
    un = jnp.where(seg_end, 0.0, _normmod(hn_ref[0:1, :], gain, shift, scale))
    row = lax.broadcasted_iota(jnp.int32, (tm, 1), 0)
    u_prev = jnp.where(row == 0, up, pltpu.roll(u, 1, 0))
    u_next = jnp.where(row == tm - 1, un, pltpu.roll(u, tm - 1, 0))
    xx = 0.5 * (u_prev + u_next) - u
    xr, xw, xk, xv, xa, xg = [(u + xx * mu_ref[j:j + 1, :]).astype(BF16) for j in range(6)]

    r = _dotf(xr, wr_ref[...])
    k = _dotf(xk, wk_ref[...])
    v = _dotf(xv, wv_ref[...])
    if vres:
        lo = _dotf(xv, v1_ref[...]).astype(BF16)
        v = v + (vf_ref[...] - v) * jax.nn.sigmoid(v0_ref[...] + _dotf(lo, v2_ref[...]))
    tl = jnp.tanh(_dotf(xw, w1_ref[...])).astype(BF16)
    al = _dotf(xa, a1_ref[...]).astype(BF16)
    gl = jax.nn.sigmoid(_dotf(xg, g1_ref[...])).astype(BF16)
    gate_o[...] = _dotf(gl, g2_ref[...])
    kx = k * kk_ref[...]
    inv = lax.rsqrt(_seg_dot(kx * kx, e_ref[...]) + EPS)
    r_o[...] = r
    v_o[...] = v
    kk_o[...] = kx * _seg_dot(inv, et_ref[...])
    for d, (lw_o, k_o, ra_o) in enumerate(((lw0_o, k0_o, ra0_o), (lw1_o, k1_o, ra1_o))):
        z = w0_ref[d:d + 1, :] + _dotf(tl, w2_ref[d])
        lw_o[...] = -math.exp(-0.5) * jax.nn.sigmoid(z)
        a = jax.nn.sigmoid(a0_ref[d:d + 1, :] + _dotf(al, a2_ref[d]))
        ra_o[...] = a
        k_o[...] = k * (1 + (a - 1) * ka_ref[...])


def _row_tile(seg):
    return _pick(seg, (256, 128, 64, 32, 16, 8))


def _pad_cols(w, n):
    return jnp.pad(w, ((0, 0), (0, n - w.shape[1])))


def _pad_rows(w, n):
    return jnp.pad(w, ((0, n - w.shape[0]), (0, 0)))


def rwkv_pre(h, m_seg, seg, T, n_ctx, gain, mu, wr, wk, wv, w0, w1, w2, a0, a1, a2, g1, g2, k_k, k_a, vres, v_first):
    M, D = h.shape
    tm = _row_tile(seg)
    lora = w1.shape[-1]
    e, et = _head_indicator(D, RWKV_HEAD)
    zero = jnp.zeros((lora, D), F32)
    w2p = jnp.stack([jnp.concatenate([w2[0], zero], 0), jnp.concatenate([zero, w2[1]], 0)]).astype(BF16)
    a2p = jnp.stack([jnp.concatenate([a2[0], zero], 0), jnp.concatenate([zero, a2[1]], 0)]).astype(BF16)
    gp = -(-g1.shape[1] // 128) * 128
    row = lambda a: a.reshape(1, D)
    consts = [row(gain), mu, w0, a0, row(k_k), row(k_a), e, et,
              wr.astype(BF16), wk.astype(BF16), wv.astype(BF16),
              jnp.concatenate([w1[0], w1[1]], 1).astype(BF16), w2p,
              jnp.concatenate([a1[0], a1[1]], 1).astype(BF16), a2p,
              _pad_cols(g1, gp).astype(BF16), _pad_rows(g2, gp).astype(BF16)]
    row_spec = pl.BlockSpec((tm, D), lambda i: (i, 0))
    last8 = M // 8 - 1
    in_specs = [row_spec,
                pl.BlockSpec((8, D), lambda i: (jnp.maximum(i * (tm // 8) - 1, 0), 0)),
                pl.BlockSpec((8, D), lambda i: (jnp.minimum((i + 1) * (tm // 8), last8), 0)),
                pl.BlockSpec((1, 6, D), lambda i: (i * tm // seg, 0, 0))]
    in_specs += [_const_spec(c.shape) for c in consts]
    args = [h, h, h, m_seg] + consts
    if vres is not None:
        v0, v1, v2 = vres
        extra = [row(v0), _pad_cols(v1, 128).astype(BF16), _pad_rows(v2, 128).astype(BF16)]
        in_specs += [_const_spec(c.shape) for c in extra] + [row_spec]
        args += extra + [v_first]
    return pl.pallas_call(
        functools.partial(_rwkv_pre_body, tm=tm, blocks_per_batch=T // tm, ctx_blocks=n_ctx // tm,
                          vres=vres is not None),
        grid=(M // tm,),
        in_specs=in_specs,
        out_specs=[row_spec] * 10,
        out_shape=[jax.ShapeDtypeStruct((M, D), F32)] * 10,
        compiler_params=_cparams("parallel"),
        name="rwkv7_pre",
    )(*args)


def _post_tail(xo, h_ref, m_ref, gain_ref, w_ref, wrt_ref, h_o, f_o, lg_o):
    h_new = h_ref[...] + m_ref[0, 2:3, :] * _dotf(xo, w_ref[...])
    h_o[...] = h_new
    f = _normmod(h_new, gain_ref[...], m_ref[0, 3:4, :], m_ref[0, 4:5, :])
    f_o[...] = f.astype(BF16)
    lg_o[...] = _dg(f, wrt_ref[...], _NN, 3)


def _rwkv_post_body(yf_ref, yb_ref, r_ref, k0_ref, k1_ref, v_ref, gate_ref, lnw_ref, lnb_ref, rk_ref, e_ref, et_ref,
                    h_ref, m_ref, gain_ref, w_ref, wrt_ref, h_o, f_o, lg_o):
    e, et = e_ref[...], et_ref[...]
    inv_n = 1.0 / RWKV_HEAD
    y = yf_ref[...] + yb_ref[...]
    yc = y - _seg_dot(_seg_dot(y, e) * inv_n, et)
    var = _seg_dot(_seg_dot(yc * yc, e) * inv_n, et)
    yn = yc * lax.rsqrt(var + GN_EPS) * lnw_ref[...] + lnb_ref[...]
    k_bonus = 0.5 * (k0_ref[...] + k1_ref[...])
    bonus = _seg_dot(_seg_dot(r_ref[...] * k_bonus * rk_ref[...], e), et) * v_ref[...]
    xo = ((yn + bonus) * gate_ref[...]).astype(BF16)
    _post_tail(xo, h_ref, m_ref, gain_ref, w_ref, wrt_ref, h_o, f_o, lg_o)


def _post_call(body, name, row_args, consts, h, m_seg, seg, gain, w_out, w_router):
    M, D = h.shape
    tm = _row_tile(seg)
    row_spec = lambda a: pl.BlockSpec((tm, a.shape[1]), lambda i: (i, 0))
    tail = [gain.reshape(1, D), w_out.astype(BF16), w_router]
    in_specs = ([row_spec(a) for a in row_args] + [_const_spec(c.shape) for c in consts] +
                [row_spec(h), pl.BlockSpec((1, 6, D), lambda i: (i * tm // seg, 0, 0))] +
                [_const_spec(c.shape) for c in tail])
    nr = w_router.shape[1]
    return pl.pallas_call(
        body,
        grid=(M // tm,),
        in_specs=in_specs,
        out_specs=[pl.BlockSpec((tm, D), lambda i: (i, 0)), pl.BlockSpec((tm, D), lambda i: (i, 0)),
                   pl.BlockSpec((tm, nr), lambda i: (i, 0))],
        out_shape=[jax.ShapeDtypeStruct((M, D), F32), jax.ShapeDtypeStruct((M, D), BF16),
                   jax.ShapeDtypeStruct((M, nr), F32)],
        compiler_params=_cparams("parallel"),
        name=name,
    )(*row_args, *consts, h, m_seg, *tail)


def _hy_post_body(a_ref, of_ref, ob_ref, z_ref, og_ref, h_ref, m_ref, gain_ref, w_ref, wrt_ref, h_o, f_o, lg_o):
    o = of_ref[...] + ob_ref[...]
    z = z_ref[...]
    parts = [a_ref[...]]
    for hd in range(GDN_HEADS):
        sl = slice(hd * GDN_DV, (hd + 1) * GDN_DV)
        oh, zh = o[:, sl], z[:, sl]
        on = oh * lax.rsqrt(jnp.mean(oh * oh, -1, keepdims=True) + EPS) * og_ref[...]
        parts.append(on * (zh * jax.nn.sigmoid(zh)))
    xo = jnp.concatenate(parts, axis=-1).astype(BF16)
    _post_tail(xo, h_ref, m_ref, gain_ref, w_ref, wrt_ref, h_o, f_o, lg_o)


def hy_post(a, of, ob, z, out_g, h, m_seg, seg, gain, w_out, w_router):
    return _post_call(_hy_post_body, "hybrid_post", [a, of, ob, z], [out_g.reshape(1, -1)], h, m_seg, seg,
                      gain, w_out, w_router)


def _hy_pre_body(h_ref, m_ref, gain_ref, c_ref, s_ref, wq1, wkv1, wpe, wpe2, wgq, wz, wab, qag, kvag,
                 wqb, wqb2, wkn, wv, qng, qng2, kng, kng2, q_o, k_o, v_o, gq_o, z_o, ab_o):
    u = _normmod(h_ref[...], gain_ref[...], m_ref[0, 0:1, :], m_ref[0, 1:2, :]).astype(BF16)
    gq_o[...] = _dotf(u, wgq[...])
    z_o[...] = _dotf(u, wz[...])
    ab_o[...] = _dotf(u, wab[...])
    cq = _dotf(u, wq1[...])
    ckv = _dotf(u, wkv1[...])
    pe = _dotf(u, wpe[...])
    pe2 = _dotf(u, wpe2[...])
    cq = (cq * lax.rsqrt(jnp.mean(cq * cq, -1, keepdims=True) + EPS) * qag[...]).astype(BF16)
    ckv = (ckv * lax.rsqrt(jnp.mean(ckv * ckv, -1, keepdims=True) + EPS) * kvag[...]).astype(BF16)
    q = _dotf(cq, wqb[...])
    q2 = _dotf(cq, wqb2[...])
    kn = _dotf(ckv, wkn[...])
    vv = _dotf(ckv, wv[...])
    c, s = c_ref[...], s_ref[...]
    qc, qs = qng[...] * c, qng2[...] * s
    kc, ks = kng[...] * c, kng2[...] * s
    k_rot = pe2 * ks
    one_col = (lax.broadcasted_iota(jnp.int32, (1, MLA_PAD), 1) == MLA_V).astype(F32)
    inv_d = 1.0 / MLA_QK

    def inv_rms(t):
        return lax.rsqrt(jnp.sum(t * t, -1, keepdims=True) * inv_d + EPS)

    for hd in range(MLA_HEADS):
        sl = slice(hd * MLA_PAD, (hd + 1) * MLA_PAD)
        qh = q[:, sl]
        kh = kn[:, sl] + pe
        q_o[:, sl] = (inv_rms(qh) * (qh * qc + q2[:, sl] * qs)).astype(BF16)
        k_o[:, sl] = (inv_rms(kh) * (kh * kc + k_rot)).astype(BF16)
        v_o[:, sl] = (vv[:, sl] + one_col).astype(BF16)


def _pad_heads(w, heads, width, to):
    K = w.shape[0]
    return jnp.pad(w.reshape(K, heads, width), ((0, 0), (0, 0), (0, to - width))).reshape(K, heads * to)


def hy_pre(h, m_seg, seg, T, gain, rope, w_in, qa_g, w_qb, kva_g, w_kvb, qn_g, kn_g):
    M, D = h.shape
    tm = _row_tile(seg)
    H = MLA_HEADS
    c0, c1, c2 = MLA_COLS, MLA_COLS + GDN_QKV, MLA_COLS + GDN_QKV + GDN_Z
    kvl = MLA_Q_LORA + MLA_KV_LORA
    wb = w_in.astype(BF16)
    wpe = jnp.pad(wb[:, kvl:c0], ((0, 0), (MLA_NOPE, MLA_PAD - MLA_QK)))
    wkv = w_kvb.reshape(MLA_KV_LORA, H, MLA_NOPE + MLA_V)
    pad1 = lambda g: jnp.pad(g, (0, MLA_PAD - MLA_QK)).reshape(1, MLA_PAD)
    lane = jnp.arange(MLA_PAD)
    rot = (lane >= MLA_NOPE) & (lane < MLA_QK)
    n_freq = MLA_ROPE // 4
    partner = jnp.where(rot, jnp.where(((lane - MLA_NOPE) // n_freq) % 2 == 0, lane + n_freq, lane - n_freq), lane)
    wqb_pad = _pad_heads(w_qb, H, MLA_QK, MLA_PAD).astype(BF16)
    wqb2 = wqb_pad.reshape(MLA_Q_LORA, H, MLA_PAD)[:, :, partner].reshape(MLA_Q_LORA, H * MLA_PAD)
    consts = [wb[:, :MLA_Q_LORA], wb[:, MLA_Q_LORA:kvl], wpe, wpe[:, partner], wb[:, c0:c1], wb[:, c1:c2],
              _pad_cols(wb[:, c2:], 128), qa_g.reshape(1, -1), kva_g.reshape(1, -1),
              wqb_pad, wqb2,
              _pad_heads(wkv[:, :, :MLA_NOPE].reshape(MLA_KV_LORA, -1), H, MLA_NOPE, MLA_PAD).astype(BF16),
              _pad_heads(wkv[:, :, MLA_NOPE:].reshape(MLA_KV_LORA, -1), H, MLA_V, MLA_PAD).astype(BF16),
              pad1(qn_g), pad1(qn_g)[:, partner], pad1(kn_g), pad1(kn_g)[:, partner]]
    bpb = T // tm
    row = lambda n: pl.BlockSpec((tm, n), lambda i: (i, 0))
    tab = pl.BlockSpec((tm, MLA_PAD), lambda i: (i % bpb, 0))
    in_specs = ([row(D), pl.BlockSpec((1, 6, D), lambda i: (i * tm // seg, 0, 0)), _const_spec((1, D)), tab, tab]
                + [_const_spec(c.shape) for c in consts])
    wide = H * MLA_PAD
    return pl.pallas_call(
        _hy_pre_body,
        grid=(M // tm,),
        in_specs=in_specs,
        out_specs=[row(wide), row(wide), row(wide), row(GDN_QKV), row(GDN_Z), row(128)],
        out_shape=[jax.ShapeDtypeStruct((M, wide), BF16)] * 3 + [jax.ShapeDtypeStruct((M, GDN_QKV), F32),
                   jax.ShapeDtypeStruct((M, GDN_Z), F32), jax.ShapeDtypeStruct((M, 128), F32)],
        compiler_params=_cparams("parallel"),
        name="hybrid_pre",
    )(h, m_seg, gain.reshape(1, D), *rope, *consts)


def _gdn_prep_body(x_ref, xp_ref, xn_ref, w_ref, q_o, k_o, v_o, *, tm, blocks_per_batch, ctx_blocks):
    tb = pl.program_id(0) % blocks_per_batch
    seg_start = (tb == 0) | (tb == ctx_blocks)
    seg_end = (tb == ctx_blocks - 1) | (tb == blocks_per_batch - 1)
    x = x_ref[...]
    xp = jnp.where(seg_start, 0.0, xp_ref[...])
    xn = jnp.where(seg_end, 0.0, xn_ref[...])
    row = lax.broadcasted_iota(jnp.int32, (tm, 1), 0)
    half = GDN_CONV // 2
    acc = x * w_ref[half:half + 1, :]
    for s in range(1, half + 1):
        before = pltpu.roll(x, s, 0)
        after = pltpu.roll(x, tm - s, 0)
        for r in range(s):
            before = jnp.where(row == r, xp[8 - s + r:8 - s + r + 1, :], before)
            after = jnp.where(row == tm - s + r, xn[r:r + 1, :], after)
        acc = acc + before * w_ref[half - s:half - s + 1, :] + after * w_ref[half + s:half + s + 1, :]
    y = acc * jax.nn.sigmoid(acc)
    nk = GDN_HEADS * GDN_DK
    for hd in range(GDN_HEADS):
        sl = slice(hd * GDN_DK, (hd + 1) * GDN_DK)
        qh = y[:, sl]
        kh = y[:, nk + hd * GDN_DK:nk + (hd + 1) * GDN_DK]
        q_o[:, sl] = qh * lax.rsqrt(jnp.sum(qh * qh, -1, keepdims=True) + EPS) * GDN_DK ** -0.5
        k_o[:, sl] = kh * lax.rsqrt(jnp.sum(kh * kh, -1, keepdims=True) + EPS)
    v_o[...] = y[:, 2 * nk:]


def gdn_prep(gq, conv_w, seg, T, n_ctx):
    M, W = gq.shape
    tm = _row_tile(seg)
    last8 = M // 8 - 1
    nk = GDN_HEADS * GDN_DK
    row = lambda n: pl.BlockSpec((tm, n), lambda i: (i, 0))
    return pl.pallas_call(
        functools.partial(_gdn_prep_body, tm=tm, blocks_per_batch=T // tm, ctx_blocks=n_ctx // tm),
        grid=(M // tm,),
        in_specs=[row(W),
                  pl.BlockSpec((8, W), lambda i: (jnp.maximum(i * (tm // 8) - 1, 0), 0)),
                  pl.BlockSpec((8, W), lambda i: (jnp.minimum((i + 1) * (tm // 8), last8), 0)),
                  _const_spec(conv_w.shape)],
        out_specs=[row(nk), row(nk), row(W - 2 * nk)],
        out_shape=[jax.ShapeDtypeStruct((M, nk), F32), jax.ShapeDtypeStruct((M, nk), F32),
                   jax.ShapeDtypeStruct((M, W - 2 * nk), F32)],
        compiler_params=_cparams("parallel"),
        name="gdn_prep",
    )(gq, gq, gq, conv_w)


def rwkv_post(yf, yb, r, k0, k1, v, gate, ln_w, ln_b, r_k, h, m_seg, seg, gain, wo, w_router):
    D = h.shape[1]
    e, et = _head_indicator(D, RWKV_HEAD)
    consts = [ln_w.reshape(1, D), ln_b.reshape(1, D), r_k.reshape(1, D), e, et]
    return _post_call(_rwkv_post_body, "rwkv7_post", [yf, yb, r, k0, k1, v, gate], consts, h, m_seg, seg,
                      gain, wo, w_router)


def kernel(x, c, ctx, c_ctx, ada_w, ada_b, norm_mix, norm_ffn, hy_w_in, hy_w_out, mla_qa_norm, mla_w_qb, mla_kva_norm, mla_w_kvb, mla_q_norm, mla_k_norm, gdn_conv, gdn_a_log, gdn_dt_bias, gdn_out_norm, rk_mu, rk_wr, rk_wk, rk_wv, rk_wo, rk_w0, rk_w1, rk_w2, rk_a0, rk_a1, rk_a2, rk_g1, rk_g2, rk_kk, rk_ka, rk_rk, rk_ln_w, rk_ln_b, rk_v0, rk_v1, rk_v2, moe_w_group, moe_b_group, moe_w_expert, moe_b_expert, moe_w1, moe_w3, moe_w2):
    B, S, D = x.shape
    L = ctx.shape[1]
    T = L + S
    depth = ada_w.shape[0]
    rope = _rope_tables(S, L)
    n_rows = -(-(B + 1) // 8) * 8
    sc = jnp.concatenate([jax.nn.silu(c), jax.nn.silu(c_ctx)[None], jnp.zeros((n_rows - B - 1, D), F32)], 0)
    M = B * T
    h = jnp.concatenate([ctx, x], axis=1).reshape(M, D)
    seg = math.gcd(L, S)
    nseg = T // seg
    v_first = None
    for l in range(depth):
        m = mm(sc, ada_w[l], hi=True) + ada_b[l]
        m_lat = jnp.broadcast_to(m[:B].reshape(B, 1, 6, D), (B, S // seg, 6, D))
        m_ctx = jnp.broadcast_to(m[B].reshape(1, 1, 6, D), (B, L // seg, 6, D))
        m_seg = jnp.concatenate([m_ctx, m_lat], axis=1).reshape(B * nseg, 6, D)

        def mod(i, m_seg=m_seg):
            return m_seg[:, None, i, :]

        router = _pad_cols(jnp.concatenate([moe_w_group[l], moe_w_expert[l]], axis=1), 128)
        j = l // 2
        b3 = lambda a: a.reshape(B, T, a.shape[-1])
        if l % 2 == 0:
            q, k, v, gq, z, ab = hy_pre(h, m_seg, seg, T, norm_mix[l], rope, hy_w_in[j], mla_qa_norm[j],
                                        mla_w_qb[j], mla_kva_norm[j], mla_w_kvb[j], mla_q_norm[j], mla_k_norm[j])
            q, k, v = b3(q), b3(k), b3(v)
            a_lat = attention(q[:, L:], k, v)
            a_ctx = attention(q[:, :L], k[:, :L], v[:, :L])
            a = jnp.concatenate([a_ctx, a_lat], axis=1).reshape(M, -1)
            gq_, gk_, gv_ = gdn_prep(gq, gdn_conv[j], seg, T, L)
            ab = ab[:, :GDN_AB].reshape(B, T, 2, 2, GDN_HEADS)
            g = -jnp.exp(gdn_a_log[j]) * jax.nn.softplus(ab[:, :, :, 0] + gdn_dt_bias[j])
            beta = jax.nn.sigmoid(ab[:, :, :, 1])
            of, ob = gdn_scan(b3(gq_), b3(gk_), b3(gv_), g, beta, L)
            h, f, logits = hy_post(a, of.reshape(M, -1), ob.reshape(M, -1), z, gdn_out_norm[j], h, m_seg, seg,
                                   norm_ffn[l], hy_w_out[j], router)
        else:
            vres = None if j == 0 else (rk_v0[j - 1], rk_v1[j - 1], rk_v2[j - 1])
            r, v, kk, lw0, lw1, k0, k1, ra0, ra1, gate = rwkv_pre(
                h, m_seg, seg, T, L, norm_mix[l], rk_mu[j], rk_wr[j], rk_wk[j], rk_wv[j], rk_w0[j], rk_w1[j],
                rk_w2[j], rk_a0[j], rk_a1[j], rk_a2[j], rk_g1[j], rk_g2[j], rk_kk[j], rk_ka[j], vres, v_first)
            if j == 0:
                v_first = v
            b3 = lambda a: a.reshape(B, T, D)
            yf, yb = rwkv_scan(b3(r), b3(v), b3(kk), [b3(lw0), b3(lw1)], [b3(k0), b3(k1)], [b3(ra0), b3(ra1)], L)
            h, f, logits = rwkv_post(yf.reshape(M, D), yb.reshape(M, D), r, k0, k1, v, gate, rk_ln_w[j], rk_ln_b[j],
                                     rk_rk[j], h, m_seg, seg, norm_ffn[l], rk_wo[j], router)
        moe_out = hier_moe(f, logits, moe_b_group[l], moe_b_expert[l], moe_w1, moe_w3, moe_w2, l)
        h = (h.reshape(B * nseg, seg, D) + mod(5) * moe_out.reshape(B * nseg, seg, D)).reshape(M, D)
    return h.reshape(B, T, D)[:, L:]
```

```python
import functools
import math

import jax
import jax.numpy as jnp
from jax import lax
from jax.experimental import pallas as pl
from jax.experimental.pallas import tpu as pltpu

F32 = jnp.float32
BF16 = jnp.bfloat16
HI = lax.Precision.HIGHEST

DEPTH = 4
GRID_W = 64
EPS = 1e-6

MLA_HEADS = 8
MLA_Q_LORA = 256
MLA_KV_LORA = 128
MLA_NOPE = 64
MLA_ROPE = 32
MLA_V = 64
MLA_QK = MLA_NOPE + MLA_ROPE
MLA_SCALE = MLA_QK ** -0.5
ROPE_BASE = 10000.0
MLA_PAD = 128

GDN_HEADS = 4
GDN_DK = 128
GDN_DV = 128
GDN_CONV = 5
GDN_CHUNK = 64

RWKV_HEAD = 64
RWKV_CHUNK = 64
GN_EPS = 64e-5

MOE_GROUPS = 4
MOE_PER_GROUP = 8
MOE_EXPERTS = MOE_GROUPS * MOE_PER_GROUP
MOE_TOPK = 2
MOE_BLOCK = 256

MLA_COLS = MLA_Q_LORA + MLA_KV_LORA + MLA_ROPE
GDN_QKV = GDN_HEADS * (2 * GDN_DK + GDN_DV)
GDN_Z = GDN_HEADS * GDN_DV
GDN_AB = 2 * 2 * GDN_HEADS

VMEM_LIMIT_BYTES = 48 * 1024 * 1024

GDN_PASSES = 1
RWKV_PASSES = 1


def _cparams(*sem):
    return pltpu.CompilerParams(dimension_semantics=sem, vmem_limit_bytes=VMEM_LIMIT_BYTES)


def _pick(n, cands):
    for c in cands:
        if n % c == 0:
            return c
    return n


def _split(a):
    hi = a.astype(BF16)
    lo = (a - hi.astype(F32)).astype(BF16)
    return hi, lo


def _dg(a, b, dn, passes):
    if passes == 6:
        return lax.dot_general(a, b, dn, precision=HI, preferred_element_type=F32)
    if passes == 1:
        return lax.dot_general(a.astype(BF16), b.astype(BF16), dn, preferred_element_type=F32)
    ah, al = _split(a)
    bh, bl = _split(b)
    d = functools.partial(lax.dot_general, dimension_numbers=dn, preferred_element_type=F32)
    return d(ah, bh) + d(al, bh) + d(ah, bl)


_NN = (((1,), (0,)), ((), ()))
_NT = (((1,), (1,)), ((), ()))
_TN = (((0,), (0,)), ((), ()))
_BNN = (((2,), (1,)), ((0,), (0,)))
_BNT = (((2,), (2,)), ((0,), (0,)))
_BTN = (((1,), (1,)), ((0,), (0,)))


def _mm_body(x_ref, w_ref, o_ref, *, hi):
    if hi:
        o_ref[...] = jnp.dot(x_ref[...], w_ref[...], precision=HI, preferred_element_type=F32)
    else:
        o_ref[...] = jnp.dot(x_ref[...].astype(BF16), w_ref[...].astype(BF16),
                             preferred_element_type=F32)


def mm(x, w, hi=False):
    M, K = x.shape
    N = w.shape[1]
    tm = _pick(M, (512, 256, 128, 64, 32, 16, 8))
    tn = _pick(N, (512, 384, 256, 128))
    return pl.pallas_call(
        functools.partial(_mm_body, hi=hi),
        grid=(M // tm, N // tn),
        in_specs=[pl.BlockSpec((tm, K), lambda i, j: (i, 0)),
                  pl.BlockSpec((K, tn), lambda i, j: (0, j))],
        out_specs=pl.BlockSpec((tm, tn), lambda i, j: (i, j)),
        out_shape=jax.ShapeDtypeStruct((M, N), F32),
        compiler_params=_cparams("parallel", "parallel"),
        name="dense_mm",
    )(x, w)


def _attn_body(q_ref, k_ref, v_ref, o_ref, m_ref, acc_ref, *, c2):
    ki = pl.program_id(3)

    @pl.when(ki == 0)
    def _():
        m_ref[...] = jnp.full(m_ref.shape, -1e30, F32)
        acc_ref[...] = jnp.zeros(acc_ref.shape, F32)

    heads = range(2)
    sl = [slice(h * MLA_PAD, (h + 1) * MLA_PAD) for h in heads]
    m_prev = [m_ref[h] for h in heads]
    acc_prev = [acc_ref[h] for h in heads]
    s = [lax.dot_general(q_ref[0, :, sl[h]], k_ref[0, :, sl[h]], _NT, preferred_element_type=F32) for h in heads]
    m_new, alpha, p = [], [], []
    reps = s[0].shape[1] // MLA_PAD
    for h in heads:
        m_new.append(jnp.maximum(m_prev[h], jnp.max(s[h], axis=-1, keepdims=True)))
        alpha.append(jnp.exp2((m_prev[h] - m_new[h]) * c2))
        x = (s[h] - jnp.tile(m_new[h], (1, reps))) * c2
        p.append(jnp.exp2(x).astype(BF16))
    pv = [jnp.dot(p[h], v_ref[0, :, sl[h]], preferred_element_type=F32) for h in heads]
    for h in heads:
        acc_ref[h] = alpha[h] * acc_prev[h] + pv[h]
        m_ref[h] = m_new[h]

    @pl.when(ki == pl.num_programs(3) - 1)
    def _():
        outs = []
        for h in range(2):
            a = acc_ref[h]
            outs.append(a[:, :MLA_V] / a[:, MLA_V:MLA_V + 1])
        o_ref[0] = jnp.concatenate(outs, axis=-1)


def attention(q, k, v):
    B, Sq, _ = q.shape
    Sk = k.shape[1]
    tq = _pick(Sq, (1024, 512, 256, 128))
    tk = _pick(Sk, (1408, 768, 512, 384, 256, 128))
    return pl.pallas_call(
        functools.partial(_attn_body, c2=MLA_SCALE * math.log2(math.e)),
        grid=(B, MLA_HEADS // 2, Sq // tq, Sk // tk),
        in_specs=[pl.BlockSpec((1, tq, 2 * MLA_PAD), lambda b, p, i, j: (b, i, p)),
                  pl.BlockSpec((1, tk, 2 * MLA_PAD), lambda b, p, i, j: (b, j, p)),
                  pl.BlockSpec((1, tk, 2 * MLA_PAD), lambda b, p, i, j: (b, j, p))],
        out_specs=pl.BlockSpec((1, tq, 2 * MLA_V), lambda b, p, i, j: (b, i, p)),
        out_shape=jax.ShapeDtypeStruct((B, Sq, MLA_HEADS * MLA_V), F32),
        scratch_shapes=[pltpu.VMEM((2, tq, MLA_PAD), F32), pltpu.VMEM((2, tq, MLA_PAD), F32)],
        compiler_params=_cparams("parallel", "parallel", "parallel", "arbitrary"),
        name="mla_attention",
    )(q, k, v)


def _tri_masks(C, rev):
    row = lax.broadcasted_iota(jnp.int32, (C, C), 0)
    col = lax.broadcasted_iota(jnp.int32, (C, C), 1)
    if rev:
        return row <= col, row < col
    return row >= col, row > col


def _neumann_inverse(nil, dn, passes):
    C = nil.shape[-1]
    eye = (lax.broadcasted_iota(jnp.int32, (C, C), 0) ==
           lax.broadcasted_iota(jnp.int32, (C, C), 1)).astype(F32)
    x = eye + nil
    p = nil
    for _ in range(int(math.log2(C)) - 1):
        p = _dg(p, p, dn, passes)
        x = x + _dg(x, p, dn, passes)
    return x


def _gdn_body(qf, kf, vf, gcf, bcf, grf, qb, kb, vb, gcb, bcb, grb, of_ref, ob_ref, s_ref, *, passes):
    C = GDN_CHUNK
    H = GDN_HEADS

    @pl.when(pl.program_id(1) == 0)
    def _():
        s_ref[...] = jnp.zeros(s_ref.shape, F32)

    dirs = ((qf, kf, vf, gcf, bcf, grf), (qb, kb, vb, gcb, bcb, grb))
    per_step = qf.shape[1] // C
    n = 2 * H
    unit = lax.broadcasted_iota(jnp.int32, (n, C, C), 0)
    ahead = (lax.broadcasted_iota(jnp.int32, (n, C, C), 1) - lax.broadcasted_iota(jnp.int32, (n, C, C), 2))
    ahead = jnp.where(unit < H, ahead, -ahead)
    incl = ahead >= 0
    strict = ahead > 0
    s = s_ref[...]
    for j in range(per_step):
        sub = (j, per_step - 1 - j)
        qs, ks, vs, gcs, grs, betas, glast = [], [], [], [], [], [], []
        for d, (q_ref, k_ref, v_ref, gc_ref, bc_ref, gr_ref) in enumerate(dirs):
            rev = d == 1
            rows = slice(sub[d] * C, (sub[d] + 1) * C)
            tri = _tri_masks(C, rev)[0].astype(F32)
            gcum_col = _dg(tri, gc_ref[0, rows, :], _NN, 6)
            gcum_row = _dg(gr_ref[0, sub[d]], tri, _NT, 6)
            beta_all = bc_ref[0, rows, :]
            t_last = 0 if rev else C - 1
            for h in range(H):
                idx = d * H + h
                gcs.append(gcum_col[:, idx:idx + 1])
                grs.append(gcum_row[idx:idx + 1, :])
                glast.append(gcum_row[idx:idx + 1, t_last:t_last + 1])
                betas.append(beta_all[:, idx:idx + 1])
                qs.append(q_ref[0, rows, h * GDN_DK:(h + 1) * GDN_DK])
                ks.append(k_ref[0, rows, h * GDN_DK:(h + 1) * GDN_DK])
                vs.append(v_ref[0, rows, h * GDN_DV:(h + 1) * GDN_DV])
        q, k, v = jnp.stack(qs), jnp.stack(ks), jnp.stack(vs)
        gc, gr, beta, g_last = jnp.stack(gcs), jnp.stack(grs), jnp.stack(betas), jnp.stack(glast)

        decay = jnp.exp(jnp.where(incl, gc - gr, -1e30))
        kbeta = k * beta
        lower = jnp.where(strict, _dg(kbeta, k, _BNT, passes) * decay, 0.0)
        tinv = _neumann_inverse(-lower, _BNN, passes)
        eg = jnp.exp(gc)
        u = _dg(tinv, v * beta, _BNN, passes)
        w = _dg(tinv, kbeta * eg, _BNN, passes)
        aqk = jnp.where(incl, _dg(q, k, _BNT, passes) * decay, 0.0)
        v_new = u - _dg(w, s, _BNN, passes)
        o = _dg(q * eg, s, _BNN, passes) + _dg(aqk, v_new, _BNN, passes)
        s = s * jnp.exp(g_last) + _dg(k * jnp.exp(g_last - gc), v_new, _BTN, passes)
        for h in range(H):
            of_ref[0, sub[0] * C:(sub[0] + 1) * C, h * GDN_DV:(h + 1) * GDN_DV] = o[h]
            ob_ref[0, sub[1] * C:(sub[1] + 1) * C, h * GDN_DV:(h + 1) * GDN_DV] = o[H + h]
    s_ref[...] = s


SCAN_CHUNKS_PER_STEP = 2


def _scan_rows(chunk, T, n_ctx):
    rows = chunk * SCAN_CHUNKS_PER_STEP
    return rows if n_ctx % rows == 0 and (T - n_ctx) % rows == 0 else chunk


def _rev_chunk(i, ncc, nc):
    return jnp.where(i < ncc, ncc - 1 - i, nc - 1 + ncc - i)


def gdn_scan(q, k, v, g, beta, n_ctx):
    B, T, _ = q.shape
    C = GDN_CHUNK
    rows = _scan_rows(C, T, n_ctx)
    per_step = rows // C
    nc = T // rows
    ncc = n_ctx // rows
    gcol = g.reshape(B, T, 2 * GDN_HEADS)
    bcol = beta.reshape(B, T, 2 * GDN_HEADS)
    grow = jnp.swapaxes(gcol.reshape(B, T // C, C, 2 * GDN_HEADS), 2, 3)
    fwd = lambda b, i: (b, i, 0)
    bwd = lambda b, i: (b, _rev_chunk(i, ncc, nc), 0)
    fwd4 = lambda b, i: (b, i, 0, 0)
    bwd4 = lambda b, i: (b, _rev_chunk(i, ncc, nc), 0, 0)
    wide = q.shape[-1]
    wv = v.shape[-1]

    def specs(m3, m4):
        return [pl.BlockSpec((1, rows, wide), m3), pl.BlockSpec((1, rows, wide), m3), pl.BlockSpec((1, rows, wv), m3),
                pl.BlockSpec((1, rows, 2 * GDN_HEADS), m3), pl.BlockSpec((1, rows, 2 * GDN_HEADS), m3),
                pl.BlockSpec((1, per_step, 2 * GDN_HEADS, C), m4)]

    of, ob = pl.pallas_call(
        functools.partial(_gdn_body, passes=GDN_PASSES),
        grid=(B, nc),
        in_specs=specs(fwd, fwd4) + specs(bwd, bwd4),
        out_specs=[pl.BlockSpec((1, rows, wv), fwd), pl.BlockSpec((1, rows, wv), bwd)],
        out_shape=[jax.ShapeDtypeStruct((B, T, wv), F32)] * 2,
        scratch_shapes=[pltpu.VMEM((2 * GDN_HEADS, GDN_DK, GDN_DV), F32)],
        compiler_params=_cparams("parallel", "arbitrary"),
        name="gdn_scan",
    )(q, k, v, gcol, bcol, grow, q, k, v, gcol, bcol, grow)
    return of, ob


def _rwkv_prep(r, lw, k, v, kk, rate, rev):
    C, D = r.shape
    N = RWKV_HEAD
    H = D // N
    incl, _ = _tri_masks(C, rev)
    tri = incl.astype(BF16)
    l1 = lw.astype(BF16)
    rem = lw - l1.astype(F32)
    l2 = rem.astype(BF16)
    l3 = (rem - l2.astype(F32)).astype(BF16)
    linc = _dotf(tri, l1) + _dotf(tri, l2) + _dotf(tri, l3)
    lexc = linc - lw
    ltot = linc[0:1, :] if rev else linc[C - 1:C, :]
    b = kk * rate
    einv = jnp.exp(-linc)
    etail = jnp.exp(ltot - linc)

    def hs(x):
        return jnp.stack([x[:, h * N:(h + 1) * N] for h in range(H)], axis=0)

    lhs = jnp.concatenate([hs(-kk * jnp.exp(lexc)), hs(r * jnp.exp(linc))], axis=1)
    rhs = jnp.concatenate([hs(b * einv), hs(k * einv)], axis=1)
    tail = jnp.concatenate([hs(b * etail), hs(k * etail)], axis=1)
    return lhs, rhs, tail, hs(v), jnp.exp(hs(ltot))


def _rwkv_body(rf, vf, kkf, lwf, kf, af, rb, vb, kkb, lwb, kb, ab, yf_ref, yb_ref, s_ref, *, passes):
    @pl.when(pl.program_id(1) == 0)
    def _():
        s_ref[...] = jnp.zeros(s_ref.shape, F32)

    C = RWKV_CHUNK
    per_step = rf.shape[1] // C
    H = rf.shape[2] // RWKV_HEAD
    dirs = ((rf, vf, kkf, lwf, kf, af), (rb, vb, kkb, lwb, kb, ab))
    shape = (2 * H, C, 2 * C)
    col = lax.broadcasted_iota(jnp.int32, shape, 2)
    ahead = lax.broadcasted_iota(jnp.int32, shape, 1) - jnp.where(col >= C, col - C, col)
    ahead = jnp.where(lax.broadcasted_iota(jnp.int32, shape, 0) < H, ahead, -ahead)
    s = s_ref[...]
    for j in range(per_step):
        rows = (slice(j * C, (j + 1) * C), slice((per_step - 1 - j) * C, (per_step - j) * C))
        parts = [_rwkv_prep(r_ref[0, rows[d], :], lw_ref[0, rows[d], :], k_ref[0, rows[d], :], v_ref[0, rows[d], :],
                            kk_ref[0, rows[d], :], a_ref[0, rows[d], :], d == 1)
                 for d, (r_ref, v_ref, kk_ref, lw_ref, k_ref, a_ref) in enumerate(dirs)]
        lhs, rhs, tail, vh, ptot = [jnp.concatenate([parts[0][i], parts[1][i]], axis=0) for i in range(5)]
        sc = _dg(lhs, rhs, _BNT, passes)
        top = jnp.where(ahead > 0, sc[:, :C, :], 0.0)
        bot = jnp.where(ahead >= 0, sc[:, C:, :], 0.0)
        tinv = _neumann_inverse(top[:, :, :C], _BNN, passes)
        ars = _dg(lhs, s, _BNT, passes)
        zero_v = jnp.concatenate([jnp.zeros_like(vh), vh], axis=1)
        u = _dg(tinv, ars[:, :C, :] + _dg(top, zero_v, _BNN, passes), _BNN, passes)
        uv = jnp.concatenate([u, vh], axis=1)
        y = ars[:, C:, :] + _dg(bot, uv, _BNN, passes)
        s = s * ptot + _dg(uv, tail, _BTN, passes)
        yf_ref[0, rows[0], :] = jnp.concatenate([y[h] for h in range(H)], axis=-1)
        yb_ref[0, rows[1], :] = jnp.concatenate([y[H + h] for h in range(H)], axis=-1)
    s_ref[...] = s


def rwkv_scan(r, v, kk, lw, key, rate, n_ctx):
    B, T, D = r.shape
    N = RWKV_HEAD
    rows = _scan_rows(RWKV_CHUNK, T, n_ctx)
    nc = T // rows
    ncc = n_ctx // rows
    fwd = lambda b, i: (b, i, 0)
    bwd = lambda b, i: (b, _rev_chunk(i, ncc, nc), 0)
    blk = (1, rows, D)
    return pl.pallas_call(
        functools.partial(_rwkv_body, passes=RWKV_PASSES),
        grid=(B, nc),
        in_specs=[pl.BlockSpec(blk, fwd)] * 6 + [pl.BlockSpec(blk, bwd)] * 6,
        out_specs=[pl.BlockSpec(blk, fwd), pl.BlockSpec(blk, bwd)],
        out_shape=[jax.ShapeDtypeStruct((B, T, D), F32)] * 2,
        scratch_shapes=[pltpu.VMEM((2 * (D // N), N, N), F32)],
        compiler_params=_cparams("parallel", "arbitrary"),
        name="rwkv7_scan",
    )(r, v, kk, lw[0], key[0], rate[0], r, v, kk, lw[1], key[1], rate[1])


def _moe_body(be_ref, nu_ref, x_ref, w1_ref, w3_ref, w2_ref, o_ref, w1b, w3b, w2b):
    i = pl.program_id(0)
    prev = be_ref[jnp.maximum(i - 1, 0)]
    used = i < nu_ref[0]

    @pl.when(used & ((i == 0) | (be_ref[i] != prev)))
    def _():
        w1b[...] = w1_ref[0, 0].astype(BF16)
        w3b[...] = w3_ref[0, 0].astype(BF16)
        w2b[...] = w2_ref[0, 0].astype(BF16)

    @pl.when(used)
    def _():
        x = x_ref[...]
        h1 = jnp.dot(x, w1b[...], preferred_element_type=F32)
        h3 = jnp.dot(x, w3b[...], preferred_element_type=F32)
        hid = (h1 * jax.nn.sigmoid(h1)) * h3
        o_ref[...] = jnp.dot(hid.astype(BF16), w2b[...], preferred_element_type=F32)

    @pl.when(jnp.logical_not(used))
    def _():
        o_ref[...] = jnp.zeros(o_ref.shape, F32)


def moe_experts(xs, blk_e, n_used, w1, w3, w2, layer):
    n_slots, D = xs.shape
    hid = w1.shape[-1]
    n_blocks = n_slots // MOE_BLOCK
    return pl.pallas_call(
        _moe_body,
        grid_spec=pltpu.PrefetchScalarGridSpec(
            num_scalar_prefetch=2,
            grid=(n_blocks,),
            in_specs=[pl.BlockSpec((MOE_BLOCK, D), lambda i, be, nu: (i, 0)),
                      pl.BlockSpec((1, 1, D, hid), lambda i, be, nu: (layer, be[i], 0, 0)),
                      pl.BlockSpec((1, 1, D, hid), lambda i, be, nu: (layer, be[i], 0, 0)),
                      pl.BlockSpec((1, 1, hid, D), lambda i, be, nu: (layer, be[i], 0, 0))],
            out_specs=pl.BlockSpec((MOE_BLOCK, D), lambda i, be, nu: (i, 0)),
            scratch_shapes=[pltpu.VMEM((D, hid), BF16), pltpu.VMEM((D, hid), BF16), pltpu.VMEM((hid, D), BF16)],
        ),
        out_shape=jax.ShapeDtypeStruct((n_slots, D), F32),
        compiler_params=_cparams("arbitrary"),
        name="moe_experts",
    )(blk_e, n_used, xs, w1, w3, w2)


def _route_body(lg_ref, bias_ref, out_ref, cnt_ref, run_ref, *, tm):
    @pl.when(pl.program_id(0) == 0)
    def _():
        run_ref[...] = jnp.zeros(run_ref.shape, F32)

    x = lg_ref[...] + bias_ref[...]
    lane = lax.broadcasted_iota(jnp.int32, x.shape, 1)
    far = 1 << 20

    def first_lane(hit):
        return jnp.min(jnp.where(hit, lane, far), axis=-1, keepdims=True)

    def masked_softmax(mask):
        xm = jnp.where(mask, x, -1e30)
        e = jnp.where(mask, jnp.exp(xm - jnp.max(xm, axis=-1, keepdims=True)), 0.0)
        return e / jnp.sum(e, axis=-1, keepdims=True)

    is_group = lane < MOE_GROUPS
    pg = masked_softmax(is_group)
    pg_top = jnp.max(pg, axis=-1, keepdims=True)
    g_idx = first_lane(is_group & (pg == pg_top))
    lo = MOE_GROUPS + MOE_PER_GROUP * g_idx
    in_group = (lane >= lo) & (lane < lo + MOE_PER_GROUP)
    pe = masked_softmax(in_group)
    p1 = jnp.max(pe, axis=-1, keepdims=True)
    l1 = first_lane(in_group & (pe == p1))
    rest_ok = in_group & (lane != l1)
    rest = jnp.where(rest_ok, pe, -1.0)
    p2 = jnp.max(rest, axis=-1, keepdims=True)
    l2 = first_lane(rest_ok & (rest == p2))
    psum = p1 + p2
    w1 = pg_top * p1 / psum
    w2 = pg_top * p2 / psum

    oh1 = (lane == l1).astype(F32)
    oh2 = (lane == l2).astype(F32)
    both = oh1 + oh2
    earlier = (lax.broadcasted_iota(jnp.int32, (tm, tm), 0) > lax.broadcasted_iota(jnp.int32, (tm, tm), 1))
    base = _dotf(earlier.astype(BF16), both.astype(BF16)) + run_ref[...]
    r1 = jnp.sum(base * oh1, axis=-1, keepdims=True)
    r2 = jnp.sum(base * oh2, axis=-1, keepdims=True)
    run_ref[...] = run_ref[...] + jnp.sum(both, axis=0, keepdims=True)
    cnt_ref[...] = run_ref[...]
    cols = ((l1 - MOE_GROUPS).astype(F32), (l2 - MOE_GROUPS).astype(F32), r1, r2, w1, w2)
    out = jnp.zeros(x.shape, F32)
    for j, c in enumerate(cols):
        out = jnp.where(lane == j, c, out)
    out_ref[...] = out


def moe_route(logits, b_group, b_expert):
    N, W = logits.shape
    tm = _pick(N, (512, 256, 128, 64, 32, 16, 8))
    bias = _pad_cols(jnp.concatenate([b_group, b_expert])[None, :], W)
    return pl.pallas_call(
        functools.partial(_route_body, tm=tm),
        grid=(N // tm,),
        in_specs=[pl.BlockSpec((tm, W), lambda i: (i, 0)), pl.BlockSpec((1, W), lambda i: (0, 0))],
        out_specs=[pl.BlockSpec((tm, W), lambda i: (i, 0)), pl.BlockSpec((1, W), lambda i: (0, 0))],
        out_shape=[jax.ShapeDtypeStruct((N, W), F32), jax.ShapeDtypeStruct((1, W), F32)],
        scratch_shapes=[pltpu.VMEM((1, W), F32)],
        compiler_params=_cparams("arbitrary"),
        name="moe_route",
    )(logits, bias)


def hier_moe(tokens, logits, b_group, b_expert, w1, w3, w2, layer):
    N, D = tokens.shape
    route, cnt = moe_route(logits, b_group, b_expert)
    eid = route[:, 0:MOE_TOPK].astype(jnp.int32).reshape(-1)
    rank = route[:, MOE_TOPK:2 * MOE_TOPK].astype(jnp.int32).reshape(-1)
    wts = route[:, 2 * MOE_TOPK:3 * MOE_TOPK]
    counts = cnt[0, MOE_GROUPS:MOE_GROUPS + MOE_EXPERTS].astype(jnp.int32)
    A = N * MOE_TOPK
    padded = (counts + MOE_BLOCK - 1) // MOE_BLOCK * MOE_BLOCK
    pend = jnp.cumsum(padded)
    dest = (pend - padded)[eid] + rank
    n_blocks = -(-A // MOE_BLOCK) + MOE_EXPERTS
    n_slots = n_blocks * MOE_BLOCK
    starts = jnp.arange(n_blocks, dtype=jnp.int32)[:, None] * MOE_BLOCK
    blk_e = jnp.minimum(jnp.sum((pend[None, :] <= starts).astype(jnp.int32), axis=1), MOE_EXPERTS - 1)
    bits = max(1, (A - 1).bit_length())
    order = jnp.sort((eid << bits) | jnp.arange(A, dtype=jnp.int32)) & ((1 << bits) - 1)
    slot_e = jnp.repeat(blk_e, MOE_BLOCK)
    pos = jnp.arange(n_slots, dtype=jnp.int32) - (pend - padded)[slot_e]
    src = jnp.minimum((jnp.cumsum(counts) - counts)[slot_e] + pos, A - 1)
    slot_tok = jnp.where(pos < counts[slot_e], order[src] // MOE_TOPK, 0)
    xs = tokens[slot_tok]
    ys = moe_experts(xs, blk_e, (pend[-1:] // MOE_BLOCK).astype(jnp.int32), w1, w3, w2, layer)
    d2 = dest.reshape(N, MOE_TOPK)
    return ys[d2[:, 0]] * wts[:, 0:1] + ys[d2[:, 1]] * wts[:, 1:2]


def _rope_tables(n_lat, n_ctx):
    rows = n_lat // GRID_W
    row = jnp.repeat(jnp.arange(rows, dtype=F32), GRID_W)
    col = jnp.tile(jnp.arange(GRID_W, dtype=F32), rows)
    n_freq = MLA_ROPE // 4
    inv = ROPE_BASE ** (-jnp.arange(n_freq, dtype=F32) / n_freq)
    ang = jnp.stack([row[:, None] * inv, col[:, None] * inv], axis=1)
    cos, sin = jnp.cos(ang), jnp.sin(ang)
    zf = jnp.zeros((n_lat, n_freq), F32)
    lat = lambda parts, fill: jnp.concatenate(
        [jnp.full((n_lat, MLA_NOPE), fill, F32)] + parts + [jnp.full((n_lat, MLA_PAD - MLA_QK), fill, F32)], axis=1)
    c = lat([cos[:, 0], cos[:, 0], cos[:, 1], cos[:, 1]], 1.0)
    s_lo = lat([-sin[:, 0], zf, -sin[:, 1], zf], 0.0)
    s_hi = lat([zf, sin[:, 0], zf, sin[:, 1]], 0.0)
    ctx = lambda fill: jnp.full((n_ctx, MLA_PAD), fill, F32)
    return jnp.concatenate([ctx(1.0), c], 0), jnp.concatenate([ctx(0.0), s_lo + s_hi], 0)


def _const_spec(shape):
    return pl.BlockSpec(shape, lambda i: (0,) * len(shape), pipeline_mode=pl.Buffered(1))


def _normmod(x, gain, shift, scale):
    return x * lax.rsqrt(jnp.mean(x * x, -1, keepdims=True) + EPS) * gain * (1 + scale) + shift


def _head_indicator(D, N):
    e = (jnp.arange(D)[:, None] // N == jnp.arange(128)[None, :]).astype(BF16)
    return e, e.T


def _seg_dot(x, e):
    xh, xl = _split(x)
    return jnp.dot(xh, e, preferred_element_type=F32) + jnp.dot(xl, e, preferred_element_type=F32)


def _dotf(a, b):
    return jnp.dot(a, b, preferred_element_type=F32)


def _rwkv_pre_body(*refs, tm, blocks_per_batch, ctx_blocks, vres):
    (h_ref, hp_ref, hn_ref, m_ref, gain_ref, mu_ref, w0_ref, a0_ref, kk_ref, ka_ref, e_ref, et_ref,
     wr_ref, wk_ref, wv_ref, w1_ref, w2_ref, a1_ref, a2_ref, g1_ref, g2_ref) = refs[:21]
    rest = refs[21:]
    if vres:
        v0_ref, v1_ref, v2_ref, vf_ref = rest[:4]
        rest = rest[4:]
    r_o, v_o, kk_o, lw0_o, lw1_o, k0_o, k1_o, ra0_o, ra1_o, gate_o = rest

    tb = pl.program_id(0) % blocks_per_batch
    seg_start = (tb == 0) | (tb == ctx_blocks)
    seg_end = (tb == ctx_blocks - 1) | (tb == blocks_per_batch - 1)
    shift, scale, gain = m_ref[0, 0:1, :], m_ref[0, 1:2, :], gain_ref[...]
    u = _normmod(h_ref[...], gain, shift, scale)
    up = jnp.where(seg_start, 0.0, _normmod(hp_ref[7:8, :], gain, shift, scale))
    un = jnp.where(seg_end, 0.0, _normmod(hn_ref[0:1, :], gain, shift, scale))
    row = lax.broadcasted_iota(jnp.int32, (tm, 1), 0)
    u_prev = jnp.where(row == 0, up, pltpu.roll(u, 1, 0))
    u_next = jnp.where(row == tm - 1, un, pltpu.roll(u, tm - 1, 0))
    xx = 0.5 * (u_prev + u_next) - u
    xr, xw, xk, xv, xa, xg = [(u + xx * mu_ref[j:j + 1, :]).astype(BF16) for j in range(6)]

    r = _dotf(xr, wr_ref[...])
    k = _dotf(xk, wk_ref[...])
    v = _dotf(xv, wv_ref[...])
    if vres:
        lo = _dotf(xv, v1_ref[...]).astype(BF16)
        v = v + (vf_ref[...] - v) * jax.nn.sigmoid(v0_ref[...] + _dotf(lo, v2_ref[...]))
    tl = jnp.tanh(_dotf(xw, w1_ref[...])).astype(BF16)
    al = _dotf(xa, a1_ref[...]).astype(BF16)
    gl = jax.nn.sigmoid(_dotf(xg, g1_ref[...])).astype(BF16)
    gate_o[...] = _dotf(gl, g2_ref[...])
    kx = k * kk_ref[...]
    inv = lax.rsqrt(_seg_dot(kx * kx, e_ref[...]) + EPS)
    r_o[...] = r
    v_o[...] = v
    kk_o[...] = kx * _seg_dot(inv, et_ref[...])
    for d, (lw_o, k_o, ra_o) in enumerate(((lw0_o, k0_o, ra0_o), (lw1_o, k1_o, ra1_o))):
        z = w0_ref[d:d + 1, :] + _dotf(tl, w2_ref[d])
        lw_o[...] = -math.exp(-0.5) * jax.nn.sigmoid(z)
        a = jax.nn.sigmoid(a0_ref[d:d + 1, :] + _dotf(al, a2_ref[d]))
        ra_o[...] = a
        k_o[...] = k * (1 + (a - 1) * ka_ref[...])


def _row_tile(seg):
    return _pick(seg, (256, 128, 64, 32, 16, 8))


def _pad_cols(w, n):
    return jnp.pad(w, ((0, 0), (0, n - w.shape[1])))


def _pad_rows(w, n):
    return jnp.pad(w, ((0, n - w.shape[0]), (0, 0)))


def rwkv_pre(h, m_seg, seg, T, n_ctx, gain, mu, wr, wk, wv, w0, w1, w2, a0, a1, a2, g1, g2, k_k, k_a, vres, v_first):
    M, D = h.shape
    tm = _row_tile(seg)
    lora = w1.shape[-1]
    e, et = _head_indicator(D, RWKV_HEAD)
    zero = jnp.zeros((lora, D), F32)
    w2p = jnp.stack([jnp.concatenate([w2[0], zero], 0), jnp.concatenate([zero, w2[1]], 0)]).astype(BF16)
    a2p = jnp.stack([jnp.concatenate([a2[0], zero], 0), jnp.concatenate([zero, a2[1]], 0)]).astype(BF16)
    gp = -(-g1.shape[1] // 128) * 128
    row = lambda a: a.reshape(1, D)
    consts = [row(gain), mu, w0, a0, row(k_k), row(k_a), e, et,
              wr.astype(BF16), wk.astype(BF16), wv.astype(BF16),
              jnp.concatenate([w1[0], w1[1]], 1).astype(BF16), w2p,
              jnp.concatenate([a1[0], a1[1]], 1).astype(BF16), a2p,
              _pad_cols(g1, gp).astype(BF16), _pad_rows(g2, gp).astype(BF16)]
    row_spec = pl.BlockSpec((tm, D), lambda i: (i, 0))
    last8 = M // 8 - 1
    in_specs = [row_spec,
                pl.BlockSpec((8, D), lambda i: (jnp.maximum(i * (tm // 8) - 1, 0), 0)),
                pl.BlockSpec((8, D), lambda i: (jnp.minimum((i + 1) * (tm // 8), last8), 0)),
                pl.BlockSpec((1, 6, D), lambda i: (i * tm // seg, 0, 0))]
    in_specs += [_const_spec(c.shape) for c in consts]
    args = [h, h, h, m_seg] + consts
    if vres is not None:
        v0, v1, v2 = vres
        extra = [row(v0), _pad_cols(v1, 128).astype(BF16), _pad_rows(v2, 128).astype(BF16)]
        in_specs += [_const_spec(c.shape) for c in extra] + [row_spec]
        args += extra + [v_first]
    return pl.pallas_call(
        functools.partial(_rwkv_pre_body, tm=tm, blocks_per_batch=T // tm, ctx_blocks=n_ctx // tm,
                          vres=vres is not None),
        grid=(M // tm,),
        in_specs=in_specs,
        out_specs=[row_spec] * 10,
        out_shape=[jax.ShapeDtypeStruct((M, D), F32)] * 10,
        compiler_params=_cparams("parallel"),
        name="rwkv7_pre",
    )(*args)


def _post_tail(xo, h_ref, m_ref, gain_ref, w_ref, wrt_ref, h_o, f_o, lg_o):
    h_new = h_ref[...] + m_ref[0, 2:3, :] * _dotf(xo, w_ref[...])
    h_o[...] = h_new
    f = _normmod(h_new, gain_ref[...], m_ref[0, 3:4, :], m_ref[0, 4:5, :])
    f_o[...] = f.astype(BF16)
    lg_o[...] = _dg(f, wrt_ref[...], _NN, 3)


def _rwkv_post_body(yf_ref, yb_ref, r_ref, k0_ref, k1_ref, v_ref, gate_ref, lnw_ref, lnb_ref, rk_ref, e_ref, et_ref,
                    h_ref, m_ref, gain_ref, w_ref, wrt_ref, h_o, f_o, lg_o):
    e, et = e_ref[...], et_ref[...]
    inv_n = 1.0 / RWKV_HEAD
    y = yf_ref[...] + yb_ref[...]
    yc = y - _seg_dot(_seg_dot(y, e) * inv_n, et)
    var = _seg_dot(_seg_dot(yc * yc, e) * inv_n, et)
    yn = yc * lax.rsqrt(var + GN_EPS) * lnw_ref[...] + lnb_ref[...]
    k_bonus = 0.5 * (k0_ref[...] + k1_ref[...])
    bonus = _seg_dot(_seg_dot(r_ref[...] * k_bonus * rk_ref[...], e), et) * v_ref[...]
    xo = ((yn + bonus) * gate_ref[...]).astype(BF16)
    _post_tail(xo, h_ref, m_ref, gain_ref, w_ref, wrt_ref, h_o, f_o, lg_o)


def _post_call(body, name, row_args, consts, h, m_seg, seg, gain, w_out, w_router):
    M, D = h.shape
    tm = _row_tile(seg)
    row_spec = lambda a: pl.BlockSpec((tm, a.shape[1]), lambda i: (i, 0))
    tail = [gain.reshape(1, D), w_out.astype(BF16), w_router]
    in_specs = ([row_spec(a) for a in row_args] + [_const_spec(c.shape) for c in consts] +
                [row_spec(h), pl.BlockSpec((1, 6, D), lambda i: (i * tm // seg, 0, 0))] +
                [_const_spec(c.shape) for c in tail])
    nr = w_router.shape[1]
    return pl.pallas_call(
        body,
        grid=(M // tm,),
        in_specs=in_specs,
        out_specs=[pl.BlockSpec((tm, D), lambda i: (i, 0)), pl.BlockSpec((tm, D), lambda i: (i, 0)),
                   pl.BlockSpec((tm, nr), lambda i: (i, 0))],
        out_shape=[jax.ShapeDtypeStruct((M, D), F32), jax.ShapeDtypeStruct((M, D), BF16),
                   jax.ShapeDtypeStruct((M, nr), F32)],
        compiler_params=_cparams("parallel"),
        name=name,
    )(*row_args, *consts, h, m_seg, *tail)


def _hy_post_body(a_ref, of_ref, ob_ref, z_ref, og_ref, h_ref, m_ref, gain_ref, w_ref, wrt_ref, h_o, f_o, lg_o):
    o = of_ref[...] + ob_ref[...]
    z = z_ref[...]
    parts = [a_ref[...]]
    for hd in range(GDN_HEADS):
        sl = slice(hd * GDN_DV, (hd + 1) * GDN_DV)
        oh, zh = o[:, sl], z[:, sl]
        on = oh * lax.rsqrt(jnp.mean(oh * oh, -1, keepdims=True) + EPS) * og_ref[...]
        parts.append(on * (zh * jax.nn.sigmoid(zh)))
    xo = jnp.concatenate(parts, axis=-1).astype(BF16)
    _post_tail(xo, h_ref, m_ref, gain_ref, w_ref, wrt_ref, h_o, f_o, lg_o)


def hy_post(a, of, ob, z, out_g, h, m_seg, seg, gain, w_out, w_router):
    return _post_call(_hy_post_body, "hybrid_post", [a, of, ob, z], [out_g.reshape(1, -1)], h, m_seg, seg,
                      gain, w_out, w_router)


def _hy_pre_body(h_ref, m_ref, gain_ref, c_ref, s_ref, wq1, wkv1, wpe, wpe2, wgq, wz, wab, qag, kvag,
                 wqb, wqb2, wkn, wv, qng, qng2, kng, kng2, q_o, k_o, v_o, gq_o, z_o, ab_o):
    u = _normmod(h_ref[...], gain_ref[...], m_ref[0, 0:1, :], m_ref[0, 1:2, :]).astype(BF16)
    gq_o[...] = _dotf(u, wgq[...])
    z_o[...] = _dotf(u, wz[...])
    ab_o[...] = _dotf(u, wab[...])
    cq = _dotf(u, wq1[...])
    ckv = _dotf(u, wkv1[...])
    pe = _dotf(u, wpe[...])
    pe2 = _dotf(u, wpe2[...])
    cq = (cq * lax.rsqrt(jnp.mean(cq * cq, -1, keepdims=True) + EPS) * qag[...]).astype(BF16)
    ckv = (ckv * lax.rsqrt(jnp.mean(ckv * ckv, -1, keepdims=True) + EPS) * kvag[...]).astype(BF16)
    q = _dotf(cq, wqb[...])
    q2 = _dotf(cq, wqb2[...])
    kn = _dotf(ckv, wkn[...])
    vv = _dotf(ckv, wv[...])
    c, s = c_ref[...], s_ref[...]
    qc, qs = qng[...] * c, qng2[...] * s
    kc, ks = kng[...] * c, kng2[...] * s
    k_rot = pe2 * ks
    one_col = (lax.broadcasted_iota(jnp.int32, (1, MLA_PAD), 1) == MLA_V).astype(F32)
    inv_d = 1.0 / MLA_QK

    def inv_rms(t):
        return lax.rsqrt(jnp.sum(t * t, -1, keepdims=True) * inv_d + EPS)

    for hd in range(MLA_HEADS):
        sl = slice(hd * MLA_PAD, (hd + 1) * MLA_PAD)
        qh = q[:, sl]
        kh = kn[:, sl] + pe
        q_o[:, sl] = (inv_rms(qh) * (qh * qc + q2[:, sl] * qs)).astype(BF16)
        k_o[:, sl] = (inv_rms(kh) * (kh * kc + k_rot)).astype(BF16)
        v_o[:, sl] = (vv[:, sl] + one_col).astype(BF16)


def _pad_heads(w, heads, width, to):
    K = w.shape[0]
    return jnp.pad(w.reshape(K, heads, width), ((0, 0), (0, 0), (0, to - width))).reshape(K, heads * to)


def hy_pre(h, m_seg, seg, T, gain, rope, w_in, qa_g, w_qb, kva_g, w_kvb, qn_g, kn_g):
    M, D = h.shape
    tm = _row_tile(seg)
    H = MLA_HEADS
    c0, c1, c2 = MLA_COLS, MLA_COLS + GDN_QKV, MLA_COLS + GDN_QKV + GDN_Z
    kvl = MLA_Q_LORA + MLA_KV_LORA
    wb = w_in.astype(BF16)
    wpe = jnp.pad(wb[:, kvl:c0], ((0, 0), (MLA_NOPE, MLA_PAD - MLA_QK)))
    wkv = w_kvb.reshape(MLA_KV_LORA, H, MLA_NOPE + MLA_V)
    pad1 = lambda g: jnp.pad(g, (0, MLA_PAD - MLA_QK)).reshape(1, MLA_PAD)
    lane = jnp.arange(MLA_PAD)
    rot = (lane >= MLA_NOPE) & (lane < MLA_QK)
    n_freq = MLA_ROPE // 4
    partner = jnp.where(rot, jnp.where(((lane - MLA_NOPE) // n_freq) % 2 == 0, lane + n_freq, lane - n_freq), lane)
    wqb_pad = _pad_heads(w_qb, H, MLA_QK, MLA_PAD).astype(BF16)
    wqb2 = wqb_pad.reshape(MLA_Q_LORA, H, MLA_PAD)[:, :, partner].reshape(MLA_Q_LORA, H * MLA_PAD)
    consts = [wb[:, :MLA_Q_LORA], wb[:, MLA_Q_LORA:kvl], wpe, wpe[:, partner], wb[:, c0:c1], wb[:, c1:c2],
              _pad_cols(wb[:, c2:], 128), qa_g.reshape(1, -1), kva_g.reshape(1, -1),
              wqb_pad, wqb2,
              _pad_heads(wkv[:, :, :MLA_NOPE].reshape(MLA_KV_LORA, -1), H, MLA_NOPE, MLA_PAD).astype(BF16),
              _pad_heads(wkv[:, :, MLA_NOPE:].reshape(MLA_KV_LORA, -1), H, MLA_V, MLA_PAD).astype(BF16),
              pad1(qn_g), pad1(qn_g)[:, partner], pad1(kn_g), pad1(kn_g)[:, partner]]
    bpb = T // tm
    row = lambda n: pl.BlockSpec((tm, n), lambda i: (i, 0))
    tab = pl.BlockSpec((tm, MLA_PAD), lambda i: (i % bpb, 0))
    in_specs = ([row(D), pl.BlockSpec((1, 6, D), lambda i: (i * tm // seg, 0, 0)), _const_spec((1, D)), tab, tab]
                + [_const_spec(c.shape) for c in consts])
    wide = H * MLA_PAD
    return pl.pallas_call(
        _hy_pre_body,
        grid=(M // tm,),
        in_specs=in_specs,
        out_specs=[row(wide), row(wide), row(wide), row(GDN_QKV), row(GDN_Z), row(128)],
        out_shape=[jax.ShapeDtypeStruct((M, wide), BF16)] * 3 + [jax.ShapeDtypeStruct((M, GDN_QKV), F32),
                   jax.ShapeDtypeStruct((M, GDN_Z), F32), jax.ShapeDtypeStruct((M, 128), F32)],
        compiler_params=_cparams("parallel"),
        name="hybrid_pre",
    )(h, m_seg, gain.reshape(1, D), *rope, *consts)


def _gdn_prep_body(x_ref, xp_ref, xn_ref, w_ref, q_o, k_o, v_o, *, tm, blocks_per_batch, ctx_blocks):
    tb = pl.program_id(0) % blocks_per_batch
    seg_start = (tb == 0) | (tb == ctx_blocks)
    seg_end = (tb == ctx_blocks - 1) | (tb == blocks_per_batch - 1)
    x = x_ref[...]
    xp = jnp.where(seg_start, 0.0, xp_ref[...])
    xn = jnp.where(seg_end, 0.0, xn_ref[...])
    row = lax.broadcasted_iota(jnp.int32, (tm, 1), 0)
    half = GDN_CONV // 2
    acc = x * w_ref[half:half + 1, :]
    for s in range(1, half + 1):
        before = pltpu.roll(x, s, 0)
        after = pltpu.roll(x, tm - s, 0)
        for r in range(s):
            before = jnp.where(row == r, xp[8 - s + r:8 - s + r + 1, :], before)
            after = jnp.where(row == tm - s + r, xn[r:r + 1, :], after)
        acc = acc + before * w_ref[half - s:half - s + 1, :] + after * w_ref[half + s:half + s + 1, :]
    y = acc * jax.nn.sigmoid(acc)
    nk = GDN_HEADS * GDN_DK
    for hd in range(GDN_HEADS):
        sl = slice(hd * GDN_DK, (hd + 1) * GDN_DK)
        qh = y[:, sl]
        kh = y[:, nk + hd * GDN_DK:nk + (hd + 1) * GDN_DK]
        q_o[:, sl] = qh * lax.rsqrt(jnp.sum(qh * qh, -1, keepdims=True) + EPS) * GDN_DK ** -0.5
        k_o[:, sl] = kh * lax.rsqrt(jnp.sum(kh * kh, -1, keepdims=True) + EPS)
    v_o[...] = y[:, 2 * nk:]


def gdn_prep(gq, conv_w, seg, T, n_ctx):
    M, W = gq.shape
    tm = _row_tile(seg)
    last8 = M // 8 - 1
    nk = GDN_HEADS * GDN_DK
    row = lambda n: pl.BlockSpec((tm, n), lambda i: (i, 0))
    return pl.pallas_call(
        functools.partial(_gdn_prep_body, tm=tm, blocks_per_batch=T // tm, ctx_blocks=n_ctx // tm),
        grid=(M // tm,),
        in_specs=[row(W),
                  pl.BlockSpec((8, W), lambda i: (jnp.maximum(i * (tm // 8) - 1, 0), 0)),
                  pl.BlockSpec((8, W), lambda i: (jnp.minimum((i + 1) * (tm // 8), last8), 0)),
                  _const_spec(conv_w.shape)],
        out_specs=[row(nk), row(nk), row(W - 2 * nk)],
        out_shape=[jax.ShapeDtypeStruct((M, nk), F32), jax.ShapeDtypeStruct((M, nk), F32),
                   jax.ShapeDtypeStruct((M, W - 2 * nk), F32)],
        compiler_params=_cparams("parallel"),
        name="gdn_prep",
    )(gq, gq, gq, conv_w)


def rwkv_post(yf, yb, r, k0, k1, v, gate, ln_w, ln_b, r_k, h, m_seg, seg, gain, wo, w_router):
    D = h.shape[1]
    e, et = _head_indicator(D, RWKV_HEAD)
    consts = [ln_w.reshape(1, D), ln_b.reshape(1, D), r_k.reshape(1, D), e, et]
    return _post_call(_rwkv_post_body, "rwkv7_post", [yf, yb, r, k0, k1, v, gate], consts, h, m_seg, seg,
                      gain, wo, w_router)


def kernel(x, c, ctx, c_ctx, ada_w, ada_b, norm_mix, norm_ffn, hy_w_in, hy_w_out, mla_qa_norm, mla_w_qb, mla_kva_norm, mla_w_kvb, mla_q_norm, mla_k_norm, gdn_conv, gdn_a_log, gdn_dt_bias, gdn_out_norm, rk_mu, rk_wr, rk_wk, rk_wv, rk_wo, rk_w0, rk_w1, rk_w2, rk_a0, rk_a1, rk_a2, rk_g1, rk_g2, rk_kk, rk_ka, rk_rk, rk_ln_w, rk_ln_b, rk_v0, rk_v1, rk_v2, moe_w_group, moe_b_group, moe_w_expert, moe_b_expert, moe_w1, moe_w3, moe_w2):
    B, S, D = x.shape
    L = ctx.shape[1]
    T = L + S
    depth = ada_w.shape[0]
    rope = _rope_tables(S, L)
    n_rows = -(-(B + 1) // 8) * 8
    sc = jnp.concatenate([jax.nn.silu(c), jax.nn.silu(c_ctx)[None], jnp.zeros((n_rows - B - 1, D), F32)], 0)
    M = B * T
    h = jnp.concatenate([ctx, x], axis=1).reshape(M, D)
    seg = math.gcd(L, S)
    nseg = T // seg
    v_first = None
    for l in range(depth):
        m = mm(sc, ada_w[l], hi=True) + ada_b[l]
        m_lat = jnp.broadcast_to(m[:B].reshape(B, 1, 6, D), (B, S // seg, 6, D))
        m_ctx = jnp.broadcast_to(m[B].reshape(1, 1, 6, D), (B, L // seg, 6, D))
        m_seg = jnp.concatenate([m_ctx, m_lat], axis=1).reshape(B * nseg, 6, D)

        def mod(i, m_seg=m_seg):
            return m_seg[:, None, i, :]

        router = _pad_cols(jnp.concatenate([moe_w_group[l], moe_w_expert[l]], axis=1), 128)
        j = l // 2
        b3 = lambda a: a.reshape(B, T, a.shape[-1])
        if l % 2 == 0:
            q, k, v, gq, z, ab = hy_pre(h, m_seg, seg, T, norm_mix[l], rope, hy_w_in[j], mla_qa_norm[j],
                                        mla_w_qb[j], mla_kva_norm[j], mla_w_kvb[j], mla_q_norm[j], mla_k_norm[j])
            q, k, v = b3(q), b3(k), b3(v)
            a_lat = attention(q[:, L:], k, v)
            a_ctx = attention(q[:, :L], k[:, :L], v[:, :L])
            a = jnp.concatenate([a_ctx, a_lat], axis=1).reshape(M, -1)
            gq_, gk_, gv_ = gdn_prep(gq, gdn_conv[j], seg, T, L)
            ab = ab[:, :GDN_AB].reshape(B, T, 2, 2, GDN_HEADS)
            g = -jnp.exp(gdn_a_log[j]) * jax.nn.softplus(ab[:, :, :, 0] + gdn_dt_bias[j])
            beta = jax.nn.sigmoid(ab[:, :, :, 1])
            of, ob = gdn_scan(b3(gq_), b3(gk_), b3(gv_), g, beta, L)
            h, f, logits = hy_post(a, of.reshape(M, -1), ob.reshape(M, -1), z, gdn_out_norm[j], h, m_seg, seg,
                                   norm_ffn[l], hy_w_out[j], router)
        else:
            vres = None if j == 0 else (rk_v0[j - 1], rk_v1[j - 1], rk_v2[j - 1])
            r, v, kk, lw0, lw1, k0, k1, ra0, ra1, gate = rwkv_pre(
                h, m_seg, seg, T, L, norm_mix[l], rk_mu[j], rk_wr[j], rk_wk[j], rk_wv[j], rk_w0[j], rk_w1[j],
                rk_w2[j], rk_a0[j], rk_a1[j], rk_a2[j], rk_g1[j], rk_g2[j], rk_kk[j], rk_ka[j], vres, v_first)
            if j == 0:
                v_first = v
            b3 = lambda a: a.reshape(B, T, D)
            yf, yb = rwkv_scan(b3(r), b3(v), b3(kk), [b3(lw0), b3(lw1)], [b3(k0), b3(k1)], [b3(ra0), b3(ra1)], L)
            h, f, logits = rwkv_post(yf.reshape(M, D), yb.reshape(M, D), r, k0, k1, v, gate, rk_ln_w[j], rk_ln_b[j],
                                     rk_rk[j], h, m_seg, seg, norm_ffn[l], rk_wo[j], router)
        moe_out = hier_moe(f, logits, moe_b_group[l], moe_b_expert[l], moe_w1, moe_w3, moe_w2, l)
        h = (h.reshape(B * nseg, seg, D) + mod(5) * moe_out.reshape(B * nseg, seg, D)).reshape(M, D)
    return h.reshape(B, T, D)[:, L:]
```

```python
import functools
import math

import jax
import jax.numpy as jnp
from jax import lax
from jax.experimental import pallas as pl
from jax.experimental.pallas import tpu as pltpu

F32 = jnp.float32
BF16 = jnp.bfloat16
HI = lax.Precision.HIGHEST

DEPTH = 4
GRID_W = 64
EPS = 1e-6

MLA_HEADS = 8
MLA_Q_LORA = 256
MLA_KV_LORA = 128
MLA_NOPE = 64
MLA_ROPE = 32
MLA_V = 64
MLA_QK = MLA_NOPE + MLA_ROPE
MLA_SCALE = MLA_QK ** -0.5
ROPE_BASE = 10000.0
MLA_PAD = 128

GDN_HEADS = 4
GDN_DK = 128
GDN_DV = 128
GDN_CONV = 5
GDN_CHUNK = 64

RWKV_HEAD = 64
RWKV_CHUNK = 64
GN_EPS = 64e-5

MOE_GROUPS = 4
MOE_PER_GROUP = 8
MOE_EXPERTS = MOE_GROUPS * MOE_PER_GROUP
MOE_TOPK = 2
MOE_BLOCK = 256

MLA_COLS = MLA_Q_LORA + MLA_KV_LORA + MLA_ROPE
GDN_QKV = GDN_HEADS * (2 * GDN_DK + GDN_DV)
GDN_Z = GDN_HEADS * GDN_DV
GDN_AB = 2 * 2 * GDN_HEADS

VMEM_LIMIT_BYTES = 48 * 1024 * 1024

GDN_PASSES = 1
RWKV_PASSES = 1
MOE_OUT_DTYPE = BF16


def _cparams(*sem):
    return pltpu.CompilerParams(dimension_semantics=sem, vmem_limit_bytes=VMEM_LIMIT_BYTES)


def _pick(n, cands):
    for c in cands:
        if n % c == 0:
            return c
    return n


def _split(a):
    hi = a.astype(BF16)
    lo = (a - hi.astype(F32)).astype(BF16)
    return hi, lo


def _dg(a, b, dn, passes):
    if passes == 6:
        return lax.dot_general(a, b, dn, precision=HI, preferred_element_type=F32)
    if passes == 1:
        return lax.dot_general(a.astype(BF16), b.astype(BF16), dn, preferred_element_type=F32)
    ah, al = _split(a)
    bh, bl = _split(b)
    d = functools.partial(lax.dot_general, dimension_numbers=dn, preferred_element_type=F32)
    return d(ah, bh) + d(al, bh) + d(ah, bl)


_NN = (((1,), (0,)), ((), ()))
_NT = (((1,), (1,)), ((), ()))
_TN = (((0,), (0,)), ((), ()))
_BNN = (((2,), (1,)), ((0,), (0,)))
_BNT = (((2,), (2,)), ((0,), (0,)))
_BTN = (((1,), (1,)), ((0,), (0,)))


def _mm_body(x_ref, w_ref, o_ref, *, hi):
    if hi:
        o_ref[...] = jnp.dot(x_ref[...], w_ref[...], precision=HI, preferred_element_type=F32)
    else:
        o_ref[...] = jnp.dot(x_ref[...].astype(BF16), w_ref[...].astype(BF16),
                             preferred_element_type=F32)


def mm(x, w, hi=False):
    M, K = x.shape
    N = w.shape[1]
    tm = _pick(M, (512, 256, 128, 64, 32, 16, 8))
    tn = _pick(N, (512, 384, 256, 128))
    return pl.pallas_call(
        functools.partial(_mm_body, hi=hi),
        grid=(M // tm, N // tn),
        in_specs=[pl.BlockSpec((tm, K), lambda i, j: (i, 0)),
                  pl.BlockSpec((K, tn), lambda i, j: (0, j))],
        out_specs=pl.BlockSpec((tm, tn), lambda i, j: (i, j)),
        out_shape=jax.ShapeDtypeStruct((M, N), F32),
        compiler_params=_cparams("parallel", "parallel"),
        name="dense_mm",
    )(x, w)


def _attn_body(q_ref, k_ref, v_ref, o_ref, m_ref, acc_ref, *, c2):
    ki = pl.program_id(3)

    @pl.when(ki == 0)
    def _():
        m_ref[...] = jnp.full(m_ref.shape, -1e30, F32)
        acc_ref[...] = jnp.zeros(acc_ref.shape, F32)

    heads = range(2)
    sl = [slice(h * MLA_PAD, (h + 1) * MLA_PAD) for h in heads]
    m_prev = [m_ref[h] for h in heads]
    acc_prev = [acc_ref[h] for h in heads]
    s = [lax.dot_general(q_ref[0, :, sl[h]], k_ref[0, :, sl[h]], _NT, preferred_element_type=F32) for h in heads]
    m_new, alpha, p = [], [], []
    reps = s[0].shape[1] // MLA_PAD
    for h in heads:
        m_new.append(jnp.maximum(m_prev[h], jnp.max(s[h], axis=-1, keepdims=True)))
        alpha.append(jnp.exp2((m_prev[h] - m_new[h]) * c2))
        x = (s[h] - jnp.tile(m_new[h], (1, reps))) * c2
        p.append(jnp.exp2(x).astype(BF16))
    pv = [jnp.dot(p[h], v_ref[0, :, sl[h]], preferred_element_type=F32) for h in heads]
    for h in heads:
        acc_ref[h] = alpha[h] * acc_prev[h] + pv[h]
        m_ref[h] = m_new[h]

    @pl.when(ki == pl.num_programs(3) - 1)
    def _():
        outs = []
        for h in range(2):
            a = acc_ref[h]
            outs.append(a[:, :MLA_V] / a[:, MLA_V:MLA_V + 1])
        o_ref[0] = jnp.concatenate(outs, axis=-1)


def attention(q, k, v):
    B, Sq, _ = q.shape
    Sk = k.shape[1]
    tq = _pick(Sq, (1024, 512, 256, 128))
    tk = _pick(Sk, (1408, 768, 512, 384, 256, 128))
    return pl.pallas_call(
        functools.partial(_attn_body, c2=MLA_SCALE * math.log2(math.e)),
        grid=(B, MLA_HEADS // 2, Sq // tq, Sk // tk),
        in_specs=[pl.BlockSpec((1, tq, 2 * MLA_PAD), lambda b, p, i, j: (b, i, p)),
                  pl.BlockSpec((1, tk, 2 * MLA_PAD), lambda b, p, i, j: (b, j, p)),
                  pl.BlockSpec((1, tk, 2 * MLA_PAD), lambda b, p, i, j: (b, j, p))],
        out_specs=pl.BlockSpec((1, tq, 2 * MLA_V), lambda b, p, i, j: (b, i, p)),
        out_shape=jax.ShapeDtypeStruct((B, Sq, MLA_HEADS * MLA_V), F32),
        scratch_shapes=[pltpu.VMEM((2, tq, MLA_PAD), F32), pltpu.VMEM((2, tq, MLA_PAD), F32)],
        compiler_params=_cparams("parallel", "parallel", "parallel", "arbitrary"),
        name="mla_attention",
    )(q, k, v)


def _tri_masks(C, rev):
    row = lax.broadcasted_iota(jnp.int32, (C, C), 0)
    col = lax.broadcasted_iota(jnp.int32, (C, C), 1)
    if rev:
        return row <= col, row < col
    return row >= col, row > col


def _neumann_inverse(nil, dn, passes):
    C = nil.shape[-1]
    eye = (lax.broadcasted_iota(jnp.int32, (C, C), 0) ==
           lax.broadcasted_iota(jnp.int32, (C, C), 1)).astype(F32)
    x = eye + nil
    p = nil
    for _ in range(int(math.log2(C)) - 1):
        p = _dg(p, p, dn, passes)
        x = x + _dg(x, p, dn, passes)
    return x


def _gdn_body(qf, kf, vf, gcf, bcf, grf, qb, kb, vb, gcb, bcb, grb, of_ref, ob_ref, s_ref, *, passes):
    C = GDN_CHUNK
    H = GDN_HEADS

    @pl.when(pl.program_id(1) == 0)
    def _():
        s_ref[...] = jnp.zeros(s_ref.shape, F32)

    dirs = ((qf, kf, vf, gcf, bcf, grf), (qb, kb, vb, gcb, bcb, grb))
    per_step = qf.shape[1] // C
    n = 2 * H
    unit = lax.broadcasted_iota(jnp.int32, (n, C, C), 0)
    ahead = (lax.broadcasted_iota(jnp.int32, (n, C, C), 1) - lax.broadcasted_iota(jnp.int32, (n, C, C), 2))
    ahead = jnp.where(unit < H, ahead, -ahead)
    incl = ahead >= 0
    strict = ahead > 0
    s = s_ref[...]
    for j in range(per_step):
        sub = (j, per_step - 1 - j)
        qs, ks, vs, gcs, grs, betas, glast = [], [], [], [], [], [], []
        for d, (q_ref, k_ref, v_ref, gc_ref, bc_ref, gr_ref) in enumerate(dirs):
            rev = d == 1
            rows = slice(sub[d] * C, (sub[d] + 1) * C)
            tri = _tri_masks(C, rev)[0].astype(F32)
            gcum_col = _dg(tri, gc_ref[0, rows, :], _NN, 6)
            gcum_row = _dg(gr_ref[0, sub[d]], tri, _NT, 6)
            beta_all = bc_ref[0, rows, :]
            t_last = 0 if rev else C - 1
            for h in range(H):
                idx = d * H + h
                gcs.append(gcum_col[:, idx:idx + 1])
                grs.append(gcum_row[idx:idx + 1, :])
                glast.append(gcum_row[idx:idx + 1, t_last:t_last + 1])
                betas.append(beta_all[:, idx:idx + 1])
                qs.append(q_ref[0, rows, h * GDN_DK:(h + 1) * GDN_DK])
                ks.append(k_ref[0, rows, h * GDN_DK:(h + 1) * GDN_DK])
                vs.append(v_ref[0, rows, h * GDN_DV:(h + 1) * GDN_DV])
        q, k, v = jnp.stack(qs), jnp.stack(ks), jnp.stack(vs)
        gc, gr, beta, g_last = jnp.stack(gcs), jnp.stack(grs), jnp.stack(betas), jnp.stack(glast)

        decay = jnp.exp(jnp.where(incl, gc - gr, -1e30))
        kbeta = k * beta
        lower = jnp.where(strict, _dg(kbeta, k, _BNT, passes) * decay, 0.0)
        tinv = _neumann_inverse(-lower, _BNN, passes)
        eg = jnp.exp(gc)
        u = _dg(tinv, v * beta, _BNN, passes)
        w = _dg(tinv, kbeta * eg, _BNN, passes)
        aqk = jnp.where(incl, _dg(q, k, _BNT, passes) * decay, 0.0)
        v_new = u - _dg(w, s, _BNN, passes)
        o = _dg(q * eg, s, _BNN, passes) + _dg(aqk, v_new, _BNN, passes)
        s = s * jnp.exp(g_last) + _dg(k * jnp.exp(g_last - gc), v_new, _BTN, passes)
        for h in range(H):
            of_ref[0, sub[0] * C:(sub[0] + 1) * C, h * GDN_DV:(h + 1) * GDN_DV] = o[h]
            ob_ref[0, sub[1] * C:(sub[1] + 1) * C, h * GDN_DV:(h + 1) * GDN_DV] = o[H + h]
    s_ref[...] = s


GDN_CHUNKS_PER_STEP = 4
RWKV_CHUNKS_PER_STEP = 2


def _scan_rows(chunk, per_step, T, n_ctx):
    while per_step > 1 and (n_ctx % (chunk * per_step) or (T - n_ctx) % (chunk * per_step)):
        per_step //= 2
    return chunk * per_step


def _rev_chunk(i, ncc, nc):
    return jnp.where(i < ncc, ncc - 1 - i, nc - 1 + ncc - i)


def gdn_scan(q, k, v, g, beta, n_ctx):
    B, T, _ = q.shape
    C = GDN_CHUNK
    rows = _scan_rows(C, GDN_CHUNKS_PER_STEP, T, n_ctx)
    per_step = rows // C
    nc = T // rows
    ncc = n_ctx // rows
    gcol = g.reshape(B, T, 2 * GDN_HEADS)
    bcol = beta.reshape(B, T, 2 * GDN_HEADS)
    grow = jnp.swapaxes(gcol.reshape(B, T // C, C, 2 * GDN_HEADS), 2, 3)
    fwd = lambda b, i: (b, i, 0)
    bwd = lambda b, i: (b, _rev_chunk(i, ncc, nc), 0)
    fwd4 = lambda b, i: (b, i, 0, 0)
    bwd4 = lambda b, i: (b, _rev_chunk(i, ncc, nc), 0, 0)
    wide = q.shape[-1]
    wv = v.shape[-1]

    def specs(m3, m4):
        return [pl.BlockSpec((1, rows, wide), m3), pl.BlockSpec((1, rows, wide), m3), pl.BlockSpec((1, rows, wv), m3),
                pl.BlockSpec((1, rows, 2 * GDN_HEADS), m3), pl.BlockSpec((1, rows, 2 * GDN_HEADS), m3),
                pl.BlockSpec((1, per_step, 2 * GDN_HEADS, C), m4)]

    of, ob = pl.pallas_call(
        functools.partial(_gdn_body, passes=GDN_PASSES),
        grid=(B, nc),
        in_specs=specs(fwd, fwd4) + specs(bwd, bwd4),
        out_specs=[pl.BlockSpec((1, rows, wv), fwd), pl.BlockSpec((1, rows, wv), bwd)],
        out_shape=[jax.ShapeDtypeStruct((B, T, wv), F32)] * 2,
        scratch_shapes=[pltpu.VMEM((2 * GDN_HEADS, GDN_DK, GDN_DV), F32)],
        compiler_params=_cparams("parallel", "arbitrary"),
        name="gdn_scan",
    )(q, k, v, gcol, bcol, grow, q, k, v, gcol, bcol, grow)
    return of, ob


def _rwkv_prep(r, lw, k, v, kk, rate, rev):
    C, D = r.shape
    N = RWKV_HEAD
    H = D // N
    incl, _ = _tri_masks(C, rev)
    tri = incl.astype(BF16)
    l1 = lw.astype(BF16)
    rem = lw - l1.astype(F32)
    l2 = rem.astype(BF16)
    l3 = (rem - l2.astype(F32)).astype(BF16)
    linc = _dotf(tri, l1) + _dotf(tri, l2) + _dotf(tri, l3)
    lexc = linc - lw
    ltot = linc[0:1, :] if rev else linc[C - 1:C, :]
    b = kk * rate
    einv = jnp.exp(-linc)
    etail = jnp.exp(ltot - linc)

    def hs(x):
        return jnp.stack([x[:, h * N:(h + 1) * N] for h in range(H)], axis=0)

    lhs = jnp.concatenate([hs(-kk * jnp.exp(lexc)), hs(r * jnp.exp(linc))], axis=1)
    rhs = jnp.concatenate([hs(b * einv), hs(k * einv)], axis=1)
    tail = jnp.concatenate([hs(b * etail), hs(k * etail)], axis=1)
    return lhs, rhs, tail, hs(v), jnp.exp(hs(ltot))


def _rwkv_body(rf, vf, kkf, lwf, kf, af, rb, vb, kkb, lwb, kb, ab, yf_ref, yb_ref, s_ref, *, passes):
    @pl.when(pl.program_id(1) == 0)
    def _():
        s_ref[...] = jnp.zeros(s_ref.shape, F32)

    C = RWKV_CHUNK
    per_step = rf.shape[1] // C
    H = rf.shape[2] // RWKV_HEAD
    dirs = ((rf, vf, kkf, lwf, kf, af), (rb, vb, kkb, lwb, kb, ab))
    shape = (2 * H, C, 2 * C)
    col = lax.broadcasted_iota(jnp.int32, shape, 2)
    ahead = lax.broadcasted_iota(jnp.int32, shape, 1) - jnp.where(col >= C, col - C, col)
    ahead = jnp.where(lax.broadcasted_iota(jnp.int32, shape, 0) < H, ahead, -ahead)
    s = s_ref[...]
    for j in range(per_step):
        rows = (slice(j * C, (j + 1) * C), slice((per_step - 1 - j) * C, (per_step - j) * C))
        parts = [_rwkv_prep(r_ref[0, rows[d], :], lw_ref[0, rows[d], :], k_ref[0, rows[d], :], v_ref[0, rows[d], :],
                            kk_ref[0, rows[d], :], a_ref[0, rows[d], :], d == 1)
                 for d, (r_ref, v_ref, kk_ref, lw_ref, k_ref, a_ref) in enumerate(dirs)]
        lhs, rhs, tail, vh, ptot = [jnp.concatenate([parts[0][i], parts[1][i]], axis=0) for i in range(5)]
        sc = _dg(lhs, rhs, _BNT, passes)
        top = jnp.where(ahead > 0, sc[:, :C, :], 0.0)
        bot = jnp.where(ahead >= 0, sc[:, C:, :], 0.0)
        tinv = _neumann_inverse(top[:, :, :C], _BNN, passes)
        ars = _dg(lhs, s, _BNT, passes)
        zero_v = jnp.concatenate([jnp.zeros_like(vh), vh], axis=1)
        u = _dg(tinv, ars[:, :C, :] + _dg(top, zero_v, _BNN, passes), _BNN, passes)
        uv = jnp.concatenate([u, vh], axis=1)
        y = ars[:, C:, :] + _dg(bot, uv, _BNN, passes)
        s = s * ptot + _dg(uv, tail, _BTN, passes)
        yf_ref[0, rows[0], :] = jnp.concatenate([y[h] for h in range(H)], axis=-1)
        yb_ref[0, rows[1], :] = jnp.concatenate([y[H + h] for h in range(H)], axis=-1)
    s_ref[...] = s


def rwkv_scan(r, v, kk, lw, key, rate, n_ctx):
    B, T, D = r.shape
    N = RWKV_HEAD
    rows = _scan_rows(RWKV_CHUNK, RWKV_CHUNKS_PER_STEP, T, n_ctx)
    nc = T // rows
    ncc = n_ctx // rows
    fwd = lambda b, i: (b, i, 0)
    bwd = lambda b, i: (b, _rev_chunk(i, ncc, nc), 0)
    blk = (1, rows, D)
    return pl.pallas_call(
        functools.partial(_rwkv_body, passes=RWKV_PASSES),
        grid=(B, nc),
        in_specs=[pl.BlockSpec(blk, fwd)] * 6 + [pl.BlockSpec(blk, bwd)] * 6,
        out_specs=[pl.BlockSpec(blk, fwd), pl.BlockSpec(blk, bwd)],
        out_shape=[jax.ShapeDtypeStruct((B, T, D), F32)] * 2,
        scratch_shapes=[pltpu.VMEM((2 * (D // N), N, N), F32)],
        compiler_params=_cparams("parallel", "arbitrary"),
        name="rwkv7_scan",
    )(r, v, kk, lw[0], key[0], rate[0], r, v, kk, lw[1], key[1], rate[1])


def _moe_body(be_ref, nu_ref, x_ref, w1_ref, w3_ref, w2_ref, o_ref, w1b, w3b, w2b):
    i = pl.program_id(0)
    prev = be_ref[jnp.maximum(i - 1, 0)]
    used = i < nu_ref[0]

    @pl.when(used & ((i == 0) | (be_ref[i] != prev)))
    def _():
        w1b[...] = w1_ref[0, 0].astype(BF16)
        w3b[...] = w3_ref[0, 0].astype(BF16)
        w2b[...] = w2_ref[0, 0].astype(BF16)

    @pl.when(used)
    def _():
        x = x_ref[...]
        h1 = jnp.dot(x, w1b[...], preferred_element_type=F32)
        h3 = jnp.dot(x, w3b[...], preferred_element_type=F32)
        hid = (h1 * jax.nn.sigmoid(h1)) * h3
        o_ref[...] = jnp.dot(hid.astype(BF16), w2b[...], preferred_element_type=F32).astype(o_ref.dtype)

    @pl.when(jnp.logical_not(used))
    def _():
        o_ref[...] = jnp.zeros(o_ref.shape, o_ref.dtype)


def moe_experts(xs, blk_e, n_used, w1, w3, w2, layer):
    n_slots, D = xs.shape
    hid = w1.shape[-1]
    n_blocks = n_slots // MOE_BLOCK
    return pl.pallas_call(
        _moe_body,
        grid_spec=pltpu.PrefetchScalarGridSpec(
            num_scalar_prefetch=2,
            grid=(n_blocks,),
            in_specs=[pl.BlockSpec((MOE_BLOCK, D), lambda i, be, nu: (i, 0)),
                      pl.BlockSpec((1, 1, D, hid), lambda i, be, nu: (layer, be[i], 0, 0)),
                      pl.BlockSpec((1, 1, D, hid), lambda i, be, nu: (layer, be[i], 0, 0)),
                      pl.BlockSpec((1, 1, hid, D), lambda i, be, nu: (layer, be[i], 0, 0))],
            out_specs=pl.BlockSpec((MOE_BLOCK, D), lambda i, be, nu: (i, 0)),
            scratch_shapes=[pltpu.VMEM((D, hid), BF16), pltpu.VMEM((D, hid), BF16), pltpu.VMEM((hid, D), BF16)],
        ),
        out_shape=jax.ShapeDtypeStruct((n_slots, D), MOE_OUT_DTYPE),
        compiler_params=_cparams("arbitrary"),
        name="moe_experts",
    )(blk_e, n_used, xs, w1, w3, w2)


def _route_body(lg_ref, bias_ref, out_ref, cnt_ref, run_ref, *, tm):
    @pl.when(pl.program_id(0) == 0)
    def _():
        run_ref[...] = jnp.zeros(run_ref.shape, F32)

    x = lg_ref[...] + bias_ref[...]
    lane = lax.broadcasted_iota(jnp.int32, x.shape, 1)
    far = 1 << 20

    def first_lane(hit):
        return jnp.min(jnp.where(hit, lane, far), axis=-1, keepdims=True)

    def masked_softmax(mask):
        xm = jnp.where(mask, x, -1e30)
        e = jnp.where(mask, jnp.exp(xm - jnp.max(xm, axis=-1, keepdims=True)), 0.0)
        return e / jnp.sum(e, axis=-1, keepdims=True)

    is_group = lane < MOE_GROUPS
    pg = masked_softmax(is_group)
    pg_top = jnp.max(pg, axis=-1, keepdims=True)
    g_idx = first_lane(is_group & (pg == pg_top))
    lo = MOE_GROUPS + MOE_PER_GROUP * g_idx
    in_group = (lane >= lo) & (lane < lo + MOE_PER_GROUP)
    pe = masked_softmax(in_group)
    p1 = jnp.max(pe, axis=-1, keepdims=True)
    l1 = first_lane(in_group & (pe == p1))
    rest_ok = in_group & (lane != l1)
    rest = jnp.where(rest_ok, pe, -1.0)
    p2 = jnp.max(rest, axis=-1, keepdims=True)
    l2 = first_lane(rest_ok & (rest == p2))
    psum = p1 + p2
    w1 = pg_top * p1 / psum
    w2 = pg_top * p2 / psum

    oh1 = (lane == l1).astype(F32)
    oh2 = (lane == l2).astype(F32)
    both = oh1 + oh2
    earlier = (lax.broadcasted_iota(jnp.int32, (tm, tm), 0) > lax.broadcasted_iota(jnp.int32, (tm, tm), 1))
    base = _dotf(earlier.astype(BF16), both.astype(BF16)) + run_ref[...]
    r1 = jnp.sum(base * oh1, axis=-1, keepdims=True)
    r2 = jnp.sum(base * oh2, axis=-1, keepdims=True)
    run_ref[...] = run_ref[...] + jnp.sum(both, axis=0, keepdims=True)
    cnt_ref[...] = run_ref[...]
    cols = ((l1 - MOE_GROUPS).astype(F32), (l2 - MOE_GROUPS).astype(F32), r1, r2, w1, w2)
    out = jnp.zeros(x.shape, F32)
    for j, c in enumerate(cols):
        out = jnp.where(lane == j, c, out)
    out_ref[...] = out


def moe_route(logits, b_group, b_expert):
    N, W = logits.shape
    tm = _pick(N, (512, 256, 128, 64, 32, 16, 8))
    bias = _pad_cols(jnp.concatenate([b_group, b_expert])[None, :], W)
    return pl.pallas_call(
        functools.partial(_route_body, tm=tm),
        grid=(N // tm,),
        in_specs=[pl.BlockSpec((tm, W), lambda i: (i, 0)), pl.BlockSpec((1, W), lambda i: (0, 0))],
        out_specs=[pl.BlockSpec((tm, W), lambda i: (i, 0)), pl.BlockSpec((1, W), lambda i: (0, 0))],
        out_shape=[jax.ShapeDtypeStruct((N, W), F32), jax.ShapeDtypeStruct((1, W), F32)],
        scratch_shapes=[pltpu.VMEM((1, W), F32)],
        compiler_params=_cparams("arbitrary"),
        name="moe_route",
    )(logits, bias)


def hier_moe(tokens, logits, b_group, b_expert, w1, w3, w2, layer):
    N, D = tokens.shape
    route, cnt = moe_route(logits, b_group, b_expert)
    eid = route[:, 0:MOE_TOPK].astype(jnp.int32).reshape(-1)
    rank = route[:, MOE_TOPK:2 * MOE_TOPK].astype(jnp.int32).reshape(-1)
    wts = route[:, 2 * MOE_TOPK:3 * MOE_TOPK]
    counts = cnt[0, MOE_GROUPS:MOE_GROUPS + MOE_EXPERTS].astype(jnp.int32)
    A = N * MOE_TOPK
    padded = (counts + MOE_BLOCK - 1) // MOE_BLOCK * MOE_BLOCK
    pend = jnp.cumsum(padded)
    dest = (pend - padded)[eid] + rank
    n_blocks = -(-A // MOE_BLOCK) + MOE_EXPERTS
    n_slots = n_blocks * MOE_BLOCK
    starts = jnp.arange(n_blocks, dtype=jnp.int32)[:, None] * MOE_BLOCK
    blk_e = jnp.minimum(jnp.sum((pend[None, :] <= starts).astype(jnp.int32), axis=1), MOE_EXPERTS - 1)
    bits = max(1, (A - 1).bit_length())
    order = jnp.sort((eid << bits) | jnp.arange(A, dtype=jnp.int32)) & ((1 << bits) - 1)
    slot_e = jnp.repeat(blk_e, MOE_BLOCK)
    pos = jnp.arange(n_slots, dtype=jnp.int32) - (pend - padded)[slot_e]
    src = jnp.minimum((jnp.cumsum(counts) - counts)[slot_e] + pos, A - 1)
    slot_tok = jnp.where(pos < counts[slot_e], order[src] // MOE_TOPK, 0)
    xs = tokens[slot_tok]
    ys = moe_experts(xs, blk_e, (pend[-1:] // MOE_BLOCK).astype(jnp.int32), w1, w3, w2, layer)
    d2 = dest.reshape(N, MOE_TOPK)
    return ys[d2[:, 0]].astype(F32) * wts[:, 0:1] + ys[d2[:, 1]].astype(F32) * wts[:, 1:2]


def _rope_tables(n_lat, n_ctx):
    rows = n_lat // GRID_W
    row = jnp.repeat(jnp.arange(rows, dtype=F32), GRID_W)
    col = jnp.tile(jnp.arange(GRID_W, dtype=F32), rows)
    n_freq = MLA_ROPE // 4
    inv = ROPE_BASE ** (-jnp.arange(n_freq, dtype=F32) / n_freq)
    ang = jnp.stack([row[:, None] * inv, col[:, None] * inv], axis=1)
    cos, sin = jnp.cos(ang), jnp.sin(ang)
    zf = jnp.zeros((n_lat, n_freq), F32)
    lat = lambda parts, fill: jnp.concatenate(
        [jnp.full((n_lat, MLA_NOPE), fill, F32)] + parts + [jnp.full((n_lat, MLA_PAD - MLA_QK), fill, F32)], axis=1)
    c = lat([cos[:, 0], cos[:, 0], cos[:, 1], cos[:, 1]], 1.0)
    s_lo = lat([-sin[:, 0], zf, -sin[:, 1], zf], 0.0)
    s_hi = lat([zf, sin[:, 0], zf, sin[:, 1]], 0.0)
    ctx = lambda fill: jnp.full((n_ctx, MLA_PAD), fill, F32)
    return jnp.concatenate([ctx(1.0), c], 0), jnp.concatenate([ctx(0.0), s_lo + s_hi], 0)


def _const_spec(shape):
    return pl.BlockSpec(shape, lambda i: (0,) * len(shape), pipeline_mode=pl.Buffered(1))


def _normmod(x, gain, shift, scale):
    return x * lax.rsqrt(jnp.mean(x * x, -1, keepdims=True) + EPS) * gain * (1 + scale) + shift


def _head_indicator(D, N):
    e = (jnp.arange(D)[:, None] // N == jnp.arange(128)[None, :]).astype(BF16)
    return e, e.T


def _seg_dot(x, e):
    xh, xl = _split(x)
    return jnp.dot(xh, e, preferred_element_type=F32) + jnp.dot(xl, e, preferred_element_type=F32)


def _dotf(a, b):
    return jnp.dot(a, b, preferred_element_type=F32)


def _rwkv_pre_body(*refs, tm, blocks_per_batch, ctx_blocks, vres):
    (h_ref, hp_ref, hn_ref, m_ref, gain_ref, mu_ref, w0_ref, a0_ref, kk_ref, ka_ref, e_ref, et_ref,
     wr_ref, wk_ref, wv_ref, w1_ref, w2_ref, a1_ref, a2_ref, g1_ref, g2_ref) = refs[:21]
    rest = refs[21:]
    if vres:
        v0_ref, v1_ref, v2_ref, vf_ref = rest[:4]
        rest = rest[4:]
    r_o, v_o, kk_o, lw0_o, lw1_o, k0_o, k1_o, ra0_o, ra1_o, gate_o = rest

    tb = pl.program_id(0) % blocks_per_batch
    seg_start = (tb == 0) | (tb == ctx_blocks)
    seg_end = (tb == ctx_blocks - 1) | (tb == blocks_per_batch - 1)
    shift, scale, gain = m_ref[0, 0:1, :], m_ref[0, 1:2, :], gain_ref[...]
    u = _normmod(h_ref[...], gain, shift, scale)
    up = jnp.where(seg_start, 0.0, _normmod(hp_ref[7:8, :], gain, shift, scale))
    un = jnp.where(seg_end, 0.0, _normmod(hn_ref[0:1, :], gain, shift, scale))
    row = lax.broadcasted_iota(jnp.int32, (tm, 1), 0)
    u_prev = jnp.where(row == 0, up, pltpu.roll(u, 1, 0))
    u_next = jnp.where(row == tm - 1, un, pltpu.roll(u, tm - 1, 0))
    xx = 0.5 * (u_prev + u_next) - u
    xr, xw, xk, xv, xa, xg = [(u + xx * mu_ref[j:j + 1, :]).astype(BF16) for j in range(6)]

    r = _dotf(xr, wr_ref[...])
    k = _dotf(xk, wk_ref[...])
    v = _dotf(xv, wv_ref[...])
    if vres:
        lo = _dotf(xv, v1_ref[...]).astype(BF16)
        v = v + (vf_ref[...] - v) * jax.nn.sigmoid(v0_ref[...] + _dotf(lo, v2_ref[...]))
    tl = jnp.tanh(_dotf(xw, w1_ref[...])).astype(BF16)
    al = _dotf(xa, a1_ref[...]).astype(BF16)
    gl = jax.nn.sigmoid(_dotf(xg, g1_ref[...])).astype(BF16)
    gate_o[...] = _dotf(gl, g2_ref[...])
    kx = k * kk_ref[...]
    inv = lax.rsqrt(_seg_dot(kx * kx, e_ref[...]) + EPS)
    r_o[...] = r
    v_o[...] = v
    kk_o[...] = kx * _seg_dot(inv, et_ref[...])
    for d, (lw_o, k_o, ra_o) in enumerate(((lw0_o, k0_o, ra0_o), (lw1_o, k1_o, ra1_o))):
        z = w0_ref[d:d + 1, :] + _dotf(tl, w2_ref[d])
        lw_o[...] = -math.exp(-0.5) * jax.nn.sigmoid(z)
        a = jax.nn.sigmoid(a0_ref[d:d + 1, :] + _dotf(al, a2_ref[d]))
        ra_o[...] = a
        k_o[...] = k * (1 + (a - 1) * ka_ref[...])


def _row_tile(seg):
    return _pick(seg, (256, 128, 64, 32, 16, 8))


def _pad_cols(w, n):
    return jnp.pad(w, ((0, 0), (0, n - w.shape[1])))


def _pad_rows(w, n):
    return jnp.pad(w, ((0, n - w.shape[0]), (0, 0)))


def rwkv_pre(h, m_seg, seg, T, n_ctx, gain, mu, wr, wk, wv, w0, w1, w2, a0, a1, a2, g1, g2, k_k, k_a, vres, v_first):
    M, D = h.shape
    tm = _row_tile(seg)
    lora = w1.shape[-1]
    e, et = _head_indicator(D, RWKV_HEAD)
    zero = jnp.zeros((lora, D), F32)
    w2p = jnp.stack([jnp.concatenate([w2[0], zero], 0), jnp.concatenate([zero, w2[1]], 0)]).astype(BF16)
    a2p = jnp.stack([jnp.concatenate([a2[0], zero], 0), jnp.concatenate([zero, a2[1]], 0)]).astype(BF16)
    gp = -(-g1.shape[1] // 128) * 128
    row = lambda a: a.reshape(1, D)
    consts = [row(gain), mu, w0, a0, row(k_k), row(k_a), e, et,
              wr.astype(BF16), wk.astype(BF16), wv.astype(BF16),
              jnp.concatenate([w1[0], w1[1]], 1).astype(BF16), w2p,
              jnp.concatenate([a1[0], a1[1]], 1).astype(BF16), a2p,
              _pad_cols(g1, gp).astype(BF16), _pad_rows(g2, gp).astype(BF16)]
    row_spec = pl.BlockSpec((tm, D), lambda i: (i, 0))
    last8 = M // 8 - 1
    in_specs = [row_spec,
                pl.BlockSpec((8, D), lambda i: (jnp.maximum(i * (tm // 8) - 1, 0), 0)),
                pl.BlockSpec((8, D), lambda i: (jnp.minimum((i + 1) * (tm // 8), last8), 0)),
                pl.BlockSpec((1, 6, D), lambda i: (i * tm // seg, 0, 0))]
    in_specs += [_const_spec(c.shape) for c in consts]
    args = [h, h, h, m_seg] + consts
    if vres is not None:
        v0, v1, v2 = vres
        extra = [row(v0), _pad_cols(v1, 128).astype(BF16), _pad_rows(v2, 128).astype(BF16)]
        in_specs += [_const_spec(c.shape) for c in extra] + [row_spec]
        args += extra + [v_first]
    return pl.pallas_call(
        functools.partial(_rwkv_pre_body, tm=tm, blocks_per_batch=T // tm, ctx_blocks=n_ctx // tm,
                          vres=vres is not None),
        grid=(M // tm,),
        in_specs=in_specs,
        out_specs=[row_spec] * 10,
        out_shape=[jax.ShapeDtypeStruct((M, D), F32)] * 10,
        compiler_params=_cparams("parallel"),
        name="rwkv7_pre",
    )(*args)


def _post_tail(xo, h_ref, m_ref, gain_ref, w_ref, wrt_ref, h_o, f_o, lg_o):
    h_new = h_ref[...] + m_ref[0, 2:3, :] * _dotf(xo, w_ref[...])
    h_o[...] = h_new
    f = _normmod(h_new, gain_ref[...], m_ref[0, 3:4, :], m_ref[0, 4:5, :])
    f_o[...] = f.astype(BF16)
    lg_o[...] = _dg(f, wrt_ref[...], _NN, 3)


def _rwkv_post_body(yf_ref, yb_ref, r_ref, k0_ref, k1_ref, v_ref, gate_ref, lnw_ref, lnb_ref, rk_ref, e_ref, et_ref,
                    h_ref, m_ref, gain_ref, w_ref, wrt_ref, h_o, f_o, lg_o):
    e, et = e_ref[...], et_ref[...]
    inv_n = 1.0 / RWKV_HEAD
    y = yf_ref[...] + yb_ref[...]
    yc = y - _seg_dot(_seg_dot(y, e) * inv_n, et)
    var = _seg_dot(_seg_dot(yc * yc, e) * inv_n, et)
    yn = yc * lax.rsqrt(var + GN_EPS) * lnw_ref[...] + lnb_ref[...]
    k_bonus = 0.5 * (k0_ref[...] + k1_ref[...])
    bonus = _seg_dot(_seg_dot(r_ref[...] * k_bonus * rk_ref[...], e), et) * v_ref[...]
    xo = ((yn + bonus) * gate_ref[...]).astype(BF16)
    _post_tail(xo, h_ref, m_ref, gain_ref, w_ref, wrt_ref, h_o, f_o, lg_o)


def _post_call(body, name, row_args, consts, h, m_seg, seg, gain, w_out, w_router):
    M, D = h.shape
    tm = _row_tile(seg)
    row_spec = lambda a: pl.BlockSpec((tm, a.shape[1]), lambda i: (i, 0))
    tail = [gain.reshape(1, D), w_out.astype(BF16), w_router]
    in_specs = ([row_spec(a) for a in row_args] + [_const_spec(c.shape) for c in consts] +
                [row_spec(h), pl.BlockSpec((1, 6, D), lambda i: (i * tm // seg, 0, 0))] +
                [_const_spec(c.shape) for c in tail])
    nr = w_router.shape[1]
    return pl.pallas_call(
        body,
        grid=(M // tm,),
        in_specs=in_specs,
        out_specs=[pl.BlockSpec((tm, D), lambda i: (i, 0)), pl.BlockSpec((tm, D), lambda i: (i, 0)),
                   pl.BlockSpec((tm, nr), lambda i: (i, 0))],
        out_shape=[jax.ShapeDtypeStruct((M, D), F32), jax.ShapeDtypeStruct((M, D), BF16),
                   jax.ShapeDtypeStruct((M, nr), F32)],
        compiler_params=_cparams("parallel"),
        name=name,
    )(*row_args, *consts, h, m_seg, *tail)


def _hy_post_body(a_ref, of_ref, ob_ref, z_ref, og_ref, h_ref, m_ref, gain_ref, w_ref, wrt_ref, h_o, f_o, lg_o):
    o = of_ref[...] + ob_ref[...]
    z = z_ref[...]
    parts = [a_ref[...]]
    for hd in range(GDN_HEADS):
        sl = slice(hd * GDN_DV, (hd + 1) * GDN_DV)
        oh, zh = o[:, sl], z[:, sl]
        on = oh * lax.rsqrt(jnp.mean(oh * oh, -1, keepdims=True) + EPS) * og_ref[...]
        parts.append(on * (zh * jax.nn.sigmoid(zh)))
    xo = jnp.concatenate(parts, axis=-1).astype(BF16)
    _post_tail(xo, h_ref, m_ref, gain_ref, w_ref, wrt_ref, h_o, f_o, lg_o)


def hy_post(a, of, ob, z, out_g, h, m_seg, seg, gain, w_out, w_router):
    return _post_call(_hy_post_body, "hybrid_post", [a, of, ob, z], [out_g.reshape(1, -1)], h, m_seg, seg,
                      gain, w_out, w_router)


def _hy_pre_body(h_ref, m_ref, gain_ref, c_ref, s_ref, wq1, wkv1, wpe, wpe2, wgq, wz, wab, qag, kvag,
                 wqb, wqb2, wkn, wv, qng, qng2, kng, kng2, q_o, k_o, v_o, gq_o, z_o, ab_o):
    u = _normmod(h_ref[...], gain_ref[...], m_ref[0, 0:1, :], m_ref[0, 1:2, :]).astype(BF16)
    gq_o[...] = _dotf(u, wgq[...])
    z_o[...] = _dotf(u, wz[...])
    ab_o[...] = _dotf(u, wab[...])
    cq = _dotf(u, wq1[...])
    ckv = _dotf(u, wkv1[...])
    pe = _dotf(u, wpe[...])
    pe2 = _dotf(u, wpe2[...])
    cq = (cq * lax.rsqrt(jnp.mean(cq * cq, -1, keepdims=True) + EPS) * qag[...]).astype(BF16)
    ckv = (ckv * lax.rsqrt(jnp.mean(ckv * ckv, -1, keepdims=True) + EPS) * kvag[...]).astype(BF16)
    q = _dotf(cq, wqb[...])
    q2 = _dotf(cq, wqb2[...])
    kn = _dotf(ckv, wkn[...])
    vv = _dotf(ckv, wv[...])
    c, s = c_ref[...], s_ref[...]
    qc, qs = qng[...] * c, qng2[...] * s
    kc, ks = kng[...] * c, kng2[...] * s
    k_rot = pe2 * ks
    one_col = (lax.broadcasted_iota(jnp.int32, (1, MLA_PAD), 1) == MLA_V).astype(F32)
    inv_d = 1.0 / MLA_QK

    def inv_rms(t):
        return lax.rsqrt(jnp.sum(t * t, -1, keepdims=True) * inv_d + EPS)

    for hd in range(MLA_HEADS):
        sl = slice(hd * MLA_PAD, (hd + 1) * MLA_PAD)
        qh = q[:, sl]
        kh = kn[:, sl] + pe
        q_o[:, sl] = (inv_rms(qh) * (qh * qc + q2[:, sl] * qs)).astype(BF16)
        k_o[:, sl] = (inv_rms(kh) * (kh * kc + k_rot)).astype(BF16)
        v_o[:, sl] = (vv[:, sl] + one_col).astype(BF16)


def _pad_heads(w, heads, width, to):
    K = w.shape[0]
    return jnp.pad(w.reshape(K, heads, width), ((0, 0), (0, 0), (0, to - width))).reshape(K, heads * to)


def hy_pre(h, m_seg, seg, T, gain, rope, w_in, qa_g, w_qb, kva_g, w_kvb, qn_g, kn_g):
    M, D = h.shape
    tm = _row_tile(seg)
    H = MLA_HEADS
    c0, c1, c2 = MLA_COLS, MLA_COLS + GDN_QKV, MLA_COLS + GDN_QKV + GDN_Z
    kvl = MLA_Q_LORA + MLA_KV_LORA
    wb = w_in.astype(BF16)
    wpe = jnp.pad(wb[:, kvl:c0], ((0, 0), (MLA_NOPE, MLA_PAD - MLA_QK)))
    wkv = w_kvb.reshape(MLA_KV_LORA, H, MLA_NOPE + MLA_V)
    pad1 = lambda g: jnp.pad(g, (0, MLA_PAD - MLA_QK)).reshape(1, MLA_PAD)
    lane = jnp.arange(MLA_PAD)
    rot = (lane >= MLA_NOPE) & (lane < MLA_QK)
    n_freq = MLA_ROPE // 4
    partner = jnp.where(rot, jnp.where(((lane - MLA_NOPE) // n_freq) % 2 == 0, lane + n_freq, lane - n_freq), lane)
    wqb_pad = _pad_heads(w_qb, H, MLA_QK, MLA_PAD).astype(BF16)
    wqb2 = wqb_pad.reshape(MLA_Q_LORA, H, MLA_PAD)[:, :, partner].reshape(MLA_Q_LORA, H * MLA_PAD)
    consts = [wb[:, :MLA_Q_LORA], wb[:, MLA_Q_LORA:kvl], wpe, wpe[:, partner], wb[:, c0:c1], wb[:, c1:c2],
              _pad_cols(wb[:, c2:], 128), qa_g.reshape(1, -1), kva_g.reshape(1, -1),
              wqb_pad, wqb2,
              _pad_heads(wkv[:, :, :MLA_NOPE].reshape(MLA_KV_LORA, -1), H, MLA_NOPE, MLA_PAD).astype(BF16),
              _pad_heads(wkv[:, :, MLA_NOPE:].reshape(MLA_KV_LORA, -1), H, MLA_V, MLA_PAD).astype(BF16),
              pad1(qn_g), pad1(qn_g)[:, partner], pad1(kn_g), pad1(kn_g)[:, partner]]
    bpb = T // tm
    row = lambda n: pl.BlockSpec((tm, n), lambda i: (i, 0))
    tab = pl.BlockSpec((tm, MLA_PAD), lambda i: (i % bpb, 0))
    in_specs = ([row(D), pl.BlockSpec((1, 6, D), lambda i: (i * tm // seg, 0, 0)), _const_spec((1, D)), tab, tab]
                + [_const_spec(c.shape) for c in consts])
    wide = H * MLA_PAD
    return pl.pallas_call(
        _hy_pre_body,
        grid=(M // tm,),
        in_specs=in_specs,
        out_specs=[row(wide), row(wide), row(wide), row(GDN_QKV), row(GDN_Z), row(128)],
        out_shape=[jax.ShapeDtypeStruct((M, wide), BF16)] * 3 + [jax.ShapeDtypeStruct((M, GDN_QKV), F32),
                   jax.ShapeDtypeStruct((M, GDN_Z), F32), jax.ShapeDtypeStruct((M, 128), F32)],
        compiler_params=_cparams("parallel"),
        name="hybrid_pre",
    )(h, m_seg, gain.reshape(1, D), *rope, *consts)


def _gdn_prep_body(x_ref, xp_ref, xn_ref, w_ref, q_o, k_o, v_o, *, tm, blocks_per_batch, ctx_blocks):
    tb = pl.program_id(0) % blocks_per_batch
    seg_start = (tb == 0) | (tb == ctx_blocks)
    seg_end = (tb == ctx_blocks - 1) | (tb == blocks_per_batch - 1)
    x = x_ref[...]
    xp = jnp.where(seg_start, 0.0, xp_ref[...])
    xn = jnp.where(seg_end, 0.0, xn_ref[...])
    row = lax.broadcasted_iota(jnp.int32, (tm, 1), 0)
    half = GDN_CONV // 2
    acc = x * w_ref[half:half + 1, :]
    for s in range(1, half + 1):
        before = pltpu.roll(x, s, 0)
        after = pltpu.roll(x, tm - s, 0)
        for r in range(s):
            before = jnp.where(row == r, xp[8 - s + r:8 - s + r + 1, :], before)
            after = jnp.where(row == tm - s + r, xn[r:r + 1, :], after)
        acc = acc + before * w_ref[half - s:half - s + 1, :] + after * w_ref[half + s:half + s + 1, :]
    y = acc * jax.nn.sigmoid(acc)
    nk = GDN_HEADS * GDN_DK
    for hd in range(GDN_HEADS):
        sl = slice(hd * GDN_DK, (hd + 1) * GDN_DK)
        qh = y[:, sl]
        kh = y[:, nk + hd * GDN_DK:nk + (hd + 1) * GDN_DK]
        q_o[:, sl] = qh * lax.rsqrt(jnp.sum(qh * qh, -1, keepdims=True) + EPS) * GDN_DK ** -0.5
        k_o[:, sl] = kh * lax.rsqrt(jnp.sum(kh * kh, -1, keepdims=True) + EPS)
    v_o[...] = y[:, 2 * nk:]


def gdn_prep(gq, conv_w, seg, T, n_ctx):
    M, W = gq.shape
    tm = _row_tile(seg)
    last8 = M // 8 - 1
    nk = GDN_HEADS * GDN_DK
    row = lambda n: pl.BlockSpec((tm, n), lambda i: (i, 0))
    return pl.pallas_call(
        functools.partial(_gdn_prep_body, tm=tm, blocks_per_batch=T // tm, ctx_blocks=n_ctx // tm),
        grid=(M // tm,),
        in_specs=[row(W),
                  pl.BlockSpec((8, W), lambda i: (jnp.maximum(i * (tm // 8) - 1, 0), 0)),
                  pl.BlockSpec((8, W), lambda i: (jnp.minimum((i + 1) * (tm // 8), last8), 0)),
                  _const_spec(conv_w.shape)],
        out_specs=[row(nk), row(nk), row(W - 2 * nk)],
        out_shape=[jax.ShapeDtypeStruct((M, nk), F32), jax.ShapeDtypeStruct((M, nk), F32),
                   jax.ShapeDtypeStruct((M, W - 2 * nk), F32)],
        compiler_params=_cparams("parallel"),
        name="gdn_prep",
    )(gq, gq, gq, conv_w)


def rwkv_post(yf, yb, r, k0, k1, v, gate, ln_w, ln_b, r_k, h, m_seg, seg, gain, wo, w_router):
    D = h.shape[1]
    e, et = _head_indicator(D, RWKV_HEAD)
    consts = [ln_w.reshape(1, D), ln_b.reshape(1, D), r_k.reshape(1, D), e, et]
    return _post_call(_rwkv_post_body, "rwkv7_post", [yf, yb, r, k0, k1, v, gate], consts, h, m_seg, seg,
                      gain, wo, w_router)


def kernel(x, c, ctx, c_ctx, ada_w, ada_b, norm_mix, norm_ffn, hy_w_in, hy_w_out, mla_qa_norm, mla_w_qb, mla_kva_norm, mla_w_kvb, mla_q_norm, mla_k_norm, gdn_conv, gdn_a_log, gdn_dt_bias, gdn_out_norm, rk_mu, rk_wr, rk_wk, rk_wv, rk_wo, rk_w0, rk_w1, rk_w2, rk_a0, rk_a1, rk_a2, rk_g1, rk_g2, rk_kk, rk_ka, rk_rk, rk_ln_w, rk_ln_b, rk_v0, rk_v1, rk_v2, moe_w_group, moe_b_group, moe_w_expert, moe_b_expert, moe_w1, moe_w3, moe_w2):
    B, S, D = x.shape
    L = ctx.shape[1]
    T = L + S
    depth = ada_w.shape[0]
    rope = _rope_tables(S, L)
    n_rows = -(-(B + 1) // 8) * 8
    sc = jnp.concatenate([jax.nn.silu(c), jax.nn.silu(c_ctx)[None], jnp.zeros((n_rows - B - 1, D), F32)], 0)
    M = B * T
    h = jnp.concatenate([ctx, x], axis=1).reshape(M, D)
    seg = math.gcd(L, S)
    nseg = T // seg
    v_first = None
    for l in range(depth):
        m = mm(sc, ada_w[l], hi=True) + ada_b[l]
        m_lat = jnp.broadcast_to(m[:B].reshape(B, 1, 6, D), (B, S // seg, 6, D))
        m_ctx = jnp.broadcast_to(m[B].reshape(1, 1, 6, D), (B, L // seg, 6, D))
        m_seg = jnp.concatenate([m_ctx, m_lat], axis=1).reshape(B * nseg, 6, D)

        def mod(i, m_seg=m_seg):
            return m_seg[:, None, i, :]

        router = _pad_cols(jnp.concatenate([moe_w_group[l], moe_w_expert[l]], axis=1), 128)
        j = l // 2
        b3 = lambda a: a.reshape(B, T, a.shape[-1])
        if l % 2 == 0:
            q, k, v, gq, z, ab = hy_pre(h, m_seg, seg, T, norm_mix[l], rope, hy_w_in[j], mla_qa_norm[j],
                                        mla_w_qb[j], mla_kva_norm[j], mla_w_kvb[j], mla_q_norm[j], mla_k_norm[j])
            q, k, v = b3(q), b3(k), b3(v)
            a_lat = attention(q[:, L:], k, v)
            a_ctx = attention(q[:, :L], k[:, :L], v[:, :L])
            a = jnp.concatenate([a_ctx, a_lat], axis=1).reshape(M, -1)
            gq_, gk_, gv_ = gdn_prep(gq, gdn_conv[j], seg, T, L)
            ab = ab[:, :GDN_AB].reshape(B, T, 2, 2, GDN_HEADS)
            g = -jnp.exp(gdn_a_log[j]) * jax.nn.softplus(ab[:, :, :, 0] + gdn_dt_bias[j])
            beta = jax.nn.sigmoid(ab[:, :, :, 1])
            of, ob = gdn_scan(b3(gq_), b3(gk_), b3(gv_), g, beta, L)
            h, f, logits = hy_post(a, of.reshape(M, -1), ob.reshape(M, -1), z, gdn_out_norm[j], h, m_seg, seg,
                                   norm_ffn[l], hy_w_out[j], router)
        else:
            vres = None if j == 0 else (rk_v0[j - 1], rk_v1[j - 1], rk_v2[j - 1])
            r, v, kk, lw0, lw1, k0, k1, ra0, ra1, gate = rwkv_pre(
                h, m_seg, seg, T, L, norm_mix[l], rk_mu[j], rk_wr[j], rk_wk[j], rk_wv[j], rk_w0[j], rk_w1[j],
                rk_w2[j], rk_a0[j], rk_a1[j], rk_a2[j], rk_g1[j], rk_g2[j], rk_kk[j], rk_ka[j], vres, v_first)
            if j == 0:
                v_first = v
            b3 = lambda a: a.reshape(B, T, D)
            yf, yb = rwkv_scan(b3(r), b3(v), b3(kk), [b3(lw0), b3(lw1)], [b3(k0), b3(k1)], [b3(ra0), b3(ra1)], L)
            h, f, logits = rwkv_post(yf.reshape(M, D), yb.reshape(M, D), r, k0, k1, v, gate, rk_ln_w[j], rk_ln_b[j],
                                     rk_rk[j], h, m_seg, seg, norm_ffn[l], rk_wo[j], router)
        moe_out = hier_moe(f, logits, moe_b_group[l], moe_b_expert[l], moe_w1, moe_w3, moe_w2, l)
        h = (h.reshape(B * nseg, seg, D) + mod(5) * moe_out.reshape(B * nseg, seg, D)).reshape(M, D)
    return h.reshape(B, T, D)[:, L:]
```

```python
import functools
import math

import jax
import jax.numpy as jnp
from jax import lax
from jax.experimental import pallas as pl
from jax.experimental.pallas import tpu as pltpu

F32 = jnp.float32
BF16 = jnp.bfloat16
HI = lax.Precision.HIGHEST

GRID_W = 64
EPS = 1e-6

MLA_HEADS = 8
MLA_Q_LORA = 256
MLA_KV_LORA = 128
MLA_NOPE = 64
MLA_ROPE = 32
MLA_V = 64
MLA_QK = MLA_NOPE + MLA_ROPE
MLA_SCALE = MLA_QK ** -0.5
ROPE_BASE = 10000.0
MLA_PAD = 128

GDN_HEADS = 4
GDN_DK = 128
GDN_DV = 128
GDN_CONV = 5
GDN_CHUNK = 64

RWKV_HEAD = 64
RWKV_CHUNK = 64
GN_EPS = 64e-5

MOE_GROUPS = 4
MOE_PER_GROUP = 8
MOE_EXPERTS = MOE_GROUPS * MOE_PER_GROUP
MOE_TOPK = 2
MOE_BLOCK = 256

MLA_COLS = MLA_Q_LORA + MLA_KV_LORA + MLA_ROPE
GDN_QKV = GDN_HEADS * (2 * GDN_DK + GDN_DV)
GDN_Z = GDN_HEADS * GDN_DV
GDN_AB = 2 * 2 * GDN_HEADS

VMEM_LIMIT_BYTES = 48 * 1024 * 1024

GDN_PASSES = 1
RWKV_PASSES = 1
MOE_OUT_DTYPE = BF16


def _cparams(*sem):
    return pltpu.CompilerParams(dimension_semantics=sem, vmem_limit_bytes=VMEM_LIMIT_BYTES)


def _pick(n, cands):
    for c in cands:
        if n % c == 0:
            return c
    return n


def _split(a):
    hi = a.astype(BF16)
    lo = (a - hi.astype(F32)).astype(BF16)
    return hi, lo


def _dg(a, b, dn, passes):
    if passes == 6:
        return lax.dot_general(a, b, dn, precision=HI, preferred_element_type=F32)
    if passes == 1:
        return lax.dot_general(a.astype(BF16), b.astype(BF16), dn, preferred_element_type=F32)
    ah, al = _split(a)
    bh, bl = _split(b)
    d = functools.partial(lax.dot_general, dimension_numbers=dn, preferred_element_type=F32)
    return d(ah, bh) + d(al, bh) + d(ah, bl)


_NN = (((1,), (0,)), ((), ()))
_NT = (((1,), (1,)), ((), ()))
_BNN = (((2,), (1,)), ((0,), (0,)))
_BNT = (((2,), (2,)), ((0,), (0,)))
_BTN = (((1,), (1,)), ((0,), (0,)))


def _mm_body(x_ref, w_ref, o_ref, *, hi):
    if hi:
        o_ref[...] = jnp.dot(x_ref[...], w_ref[...], precision=HI, preferred_element_type=F32)
    else:
        o_ref[...] = jnp.dot(x_ref[...].astype(BF16), w_ref[...].astype(BF16),
                             preferred_element_type=F32)


def mm(x, w, hi=False):
    M, K = x.shape
    N = w.shape[1]
    tm = _pick(M, (512, 256, 128, 64, 32, 16, 8))
    tn = _pick(N, (512, 384, 256, 128))
    return pl.pallas_call(
        functools.partial(_mm_body, hi=hi),
        grid=(M // tm, N // tn),
        in_specs=[pl.BlockSpec((tm, K), lambda i, j: (i, 0)),
                  pl.BlockSpec((K, tn), lambda i, j: (0, j))],
        out_specs=pl.BlockSpec((tm, tn), lambda i, j: (i, j)),
        out_shape=jax.ShapeDtypeStruct((M, N), F32),
        compiler_params=_cparams("parallel", "parallel"),
        name="dense_mm",
    )(x, w)


def _attn_body(q_ref, k_ref, v_ref, o_ref, m_ref, acc_ref, *, c2):
    ki = pl.program_id(3)

    @pl.when(ki == 0)
    def _():
        m_ref[...] = jnp.full(m_ref.shape, -1e30, F32)
        acc_ref[...] = jnp.zeros(acc_ref.shape, F32)

    heads = range(2)
    sl = [slice(h * MLA_PAD, (h + 1) * MLA_PAD) for h in heads]
    m_prev = [m_ref[h] for h in heads]
    acc_prev = [acc_ref[h] for h in heads]
    s = [lax.dot_general(q_ref[0, :, sl[h]], k_ref[0, :, sl[h]], _NT, preferred_element_type=F32) for h in heads]
    m_new, alpha, p = [], [], []
    reps = s[0].shape[1] // MLA_PAD
    for h in heads:
        m_new.append(jnp.maximum(m_prev[h], jnp.max(s[h], axis=-1, keepdims=True)))
        alpha.append(jnp.exp2((m_prev[h] - m_new[h]) * c2))
        x = (s[h] - jnp.tile(m_new[h], (1, reps))) * c2
        p.append(jnp.exp2(x).astype(BF16))
    pv = [jnp.dot(p[h], v_ref[0, :, sl[h]], preferred_element_type=F32) for h in heads]
    for h in heads:
        acc_ref[h] = alpha[h] * acc_prev[h] + pv[h]
        m_ref[h] = m_new[h]

    @pl.when(ki == pl.num_programs(3) - 1)
    def _():
        outs = []
        for h in range(2):
            a = acc_ref[h]
            outs.append(a[:, :MLA_V] / a[:, MLA_V:MLA_V + 1])
        o_ref[0] = jnp.concatenate(outs, axis=-1)


def attention(q, k, v):
    B, Sq, _ = q.shape
    Sk = k.shape[1]
    tq = _pick(Sq, (1024, 512, 256, 128))
    tk = _pick(Sk, (1408, 768, 512, 384, 256, 128))
    return pl.pallas_call(
        functools.partial(_attn_body, c2=MLA_SCALE * math.log2(math.e)),
        grid=(B, MLA_HEADS // 2, Sq // tq, Sk // tk),
        in_specs=[pl.BlockSpec((1, tq, 2 * MLA_PAD), lambda b, p, i, j: (b, i, p)),
                  pl.BlockSpec((1, tk, 2 * MLA_PAD), lambda b, p, i, j: (b, j, p)),
                  pl.BlockSpec((1, tk, 2 * MLA_PAD), lambda b, p, i, j: (b, j, p))],
        out_specs=pl.BlockSpec((1, tq, 2 * MLA_V), lambda b, p, i, j: (b, i, p)),
        out_shape=jax.ShapeDtypeStruct((B, Sq, MLA_HEADS * MLA_V), F32),
        scratch_shapes=[pltpu.VMEM((2, tq, MLA_PAD), F32), pltpu.VMEM((2, tq, MLA_PAD), F32)],
        compiler_params=_cparams("parallel", "parallel", "parallel", "arbitrary"),
        name="mla_attention",
    )(q, k, v)


def _tri_masks(C, rev):
    row = lax.broadcasted_iota(jnp.int32, (C, C), 0)
    col = lax.broadcasted_iota(jnp.int32, (C, C), 1)
    if rev:
        return row <= col, row < col
    return row >= col, row > col


def _neumann_inverse(nil, dn, passes):
    C = nil.shape[-1]
    eye = (lax.broadcasted_iota(jnp.int32, (C, C), 0) ==
           lax.broadcasted_iota(jnp.int32, (C, C), 1)).astype(F32)
    x = eye + nil
    p = nil
    for _ in range(int(math.log2(C)) - 1):
        p = _dg(p, p, dn, passes)
        x = x + _dg(x, p, dn, passes)
    return x


def _gdn_body(qf, kf, vf, gcf, bcf, grf, qb, kb, vb, gcb, bcb, grb, of_ref, ob_ref, s_ref, *, passes):
    C = GDN_CHUNK
    H = GDN_HEADS

    @pl.when(pl.program_id(1) == 0)
    def _():
        s_ref[...] = jnp.zeros(s_ref.shape, F32)

    dirs = ((qf, kf, vf, gcf, bcf, grf), (qb, kb, vb, gcb, bcb, grb))
    per_step = qf.shape[1] // C
    n = 2 * H
    unit = lax.broadcasted_iota(jnp.int32, (n, C, C), 0)
    ahead = (lax.broadcasted_iota(jnp.int32, (n, C, C), 1) - lax.broadcasted_iota(jnp.int32, (n, C, C), 2))
    ahead = jnp.where(unit < H, ahead, -ahead)
    incl = ahead >= 0
    strict = ahead > 0
    s = s_ref[...]
    for j in range(per_step):
        sub = (j, per_step - 1 - j)
        qs, ks, vs, gcs, grs, betas, glast = [], [], [], [], [], [], []
        for d, (q_ref, k_ref, v_ref, gc_ref, bc_ref, gr_ref) in enumerate(dirs):
            rev = d == 1
            rows = slice(sub[d] * C, (sub[d] + 1) * C)
            tri = _tri_masks(C, rev)[0].astype(F32)
            gcum_col = _dg(tri, gc_ref[0, rows, :], _NN, 6)
            gcum_row = _dg(gr_ref[0, sub[d]], tri, _NT, 6)
            beta_all = bc_ref[0, rows, :]
            t_last = 0 if rev else C - 1
            for h in range(H):
                idx = d * H + h
                gcs.append(gcum_col[:, idx:idx + 1])
                grs.append(gcum_row[idx:idx + 1, :])
                glast.append(gcum_row[idx:idx + 1, t_last:t_last + 1])
                betas.append(beta_all[:, idx:idx + 1])
                qs.append(q_ref[0, rows, h * GDN_DK:(h + 1) * GDN_DK])
                ks.append(k_ref[0, rows, h * GDN_DK:(h + 1) * GDN_DK])
                vs.append(v_ref[0, rows, h * GDN_DV:(h + 1) * GDN_DV])
        q, k, v = jnp.stack(qs), jnp.stack(ks), jnp.stack(vs)
        gc, gr, beta, g_last = jnp.stack(gcs), jnp.stack(grs), jnp.stack(betas), jnp.stack(glast)

        decay = jnp.exp(jnp.where(incl, gc - gr, -1e30))
        kbeta = k * beta
        lower = jnp.where(strict, _dg(kbeta, k, _BNT, passes) * decay, 0.0)
        tinv = _neumann_inverse(-lower, _BNN, passes)
        eg = jnp.exp(gc)
        u = _dg(tinv, v * beta, _BNN, passes)
        w = _dg(tinv, kbeta * eg, _BNN, passes)
        aqk = jnp.where(incl, _dg(q, k, _BNT, passes) * decay, 0.0)
        v_new = u - _dg(w, s, _BNN, passes)
        o = _dg(q * eg, s, _BNN, passes) + _dg(aqk, v_new, _BNN, passes)
        s = s * jnp.exp(g_last) + _dg(k * jnp.exp(g_last - gc), v_new, _BTN, passes)
        for h in range(H):
            of_ref[0, sub[0] * C:(sub[0] + 1) * C, h * GDN_DV:(h + 1) * GDN_DV] = o[h]
            ob_ref[0, sub[1] * C:(sub[1] + 1) * C, h * GDN_DV:(h + 1) * GDN_DV] = o[H + h]
    s_ref[...] = s


GDN_CHUNKS_PER_STEP = 4
RWKV_CHUNKS_PER_STEP = 4


def _scan_rows(chunk, per_step, T, n_ctx):
    while per_step > 1 and (n_ctx % (chunk * per_step) or (T - n_ctx) % (chunk * per_step)):
        per_step //= 2
    return chunk * per_step


def _rev_chunk(i, ncc, nc):
    return jnp.where(i < ncc, ncc - 1 - i, nc - 1 + ncc - i)


def gdn_scan(q, k, v, g, beta, n_ctx):
    B, T, _ = q.shape
    C = GDN_CHUNK
    rows = _scan_rows(C, GDN_CHUNKS_PER_STEP, T, n_ctx)
    per_step = rows // C
    nc = T // rows
    ncc = n_ctx // rows
    gcol = g.reshape(B, T, 2 * GDN_HEADS)
    bcol = beta.reshape(B, T, 2 * GDN_HEADS)
    grow = jnp.swapaxes(gcol.reshape(B, T // C, C, 2 * GDN_HEADS), 2, 3)
    fwd = lambda b, i: (b, i, 0)
    bwd = lambda b, i: (b, _rev_chunk(i, ncc, nc), 0)
    fwd4 = lambda b, i: (b, i, 0, 0)
    bwd4 = lambda b, i: (b, _rev_chunk(i, ncc, nc), 0, 0)
    wide = q.shape[-1]
    wv = v.shape[-1]

    def specs(m3, m4):
        return [pl.BlockSpec((1, rows, wide), m3), pl.BlockSpec((1, rows, wide), m3), pl.BlockSpec((1, rows, wv), m3),
                pl.BlockSpec((1, rows, 2 * GDN_HEADS), m3), pl.BlockSpec((1, rows, 2 * GDN_HEADS), m3),
                pl.BlockSpec((1, per_step, 2 * GDN_HEADS, C), m4)]

    of, ob = pl.pallas_call(
        functools.partial(_gdn_body, passes=GDN_PASSES),
        grid=(B, nc),
        in_specs=specs(fwd, fwd4) + specs(bwd, bwd4),
        out_specs=[pl.BlockSpec((1, rows, wv), fwd), pl.BlockSpec((1, rows, wv), bwd)],
        out_shape=[jax.ShapeDtypeStruct((B, T, wv), F32)] * 2,
        scratch_shapes=[pltpu.VMEM((2 * GDN_HEADS, GDN_DK, GDN_DV), F32)],
        compiler_params=_cparams("parallel", "arbitrary"),
        name="gdn_scan",
    )(q, k, v, gcol, bcol, grow, q, k, v, gcol, bcol, grow)
    return of, ob


def _rwkv_prep(r, lw, k, v, kk, rate, rev):
    C, D = r.shape
    N = RWKV_HEAD
    H = D // N
    incl, _ = _tri_masks(C, rev)
    tri = incl.astype(BF16)
    l1 = lw.astype(BF16)
    rem = lw - l1.astype(F32)
    l2 = rem.astype(BF16)
    l3 = (rem - l2.astype(F32)).astype(BF16)
    linc = _dotf(tri, l1) + _dotf(tri, l2) + _dotf(tri, l3)
    lexc = linc - lw
    ltot = linc[0:1, :] if rev else linc[C - 1:C, :]
    b = kk * rate
    einv = jnp.exp(-linc)
    etail = jnp.exp(ltot - linc)

    def hs(x):
        return jnp.stack([x[:, h * N:(h + 1) * N] for h in range(H)], axis=0)

    lhs = jnp.concatenate([hs(-kk * jnp.exp(lexc)), hs(r * jnp.exp(linc))], axis=1)
    rhs = jnp.concatenate([hs(b * einv), hs(k * einv)], axis=1)
    tail = jnp.concatenate([hs(b * etail), hs(k * etail)], axis=1)
    return lhs, rhs, tail, hs(v), jnp.exp(hs(ltot))


def _rwkv_body(rf, vf, kkf, lwf, kf, af, rb, vb, kkb, lwb, kb, ab, yf_ref, yb_ref, s_ref, *, passes):
    @pl.when(pl.program_id(1) == 0)
    def _():
        s_ref[...] = jnp.zeros(s_ref.shape, F32)

    C = RWKV_CHUNK
    per_step = rf.shape[1] // C
    H = rf.shape[2] // RWKV_HEAD
    dirs = ((rf, vf, kkf, lwf, kf, af), (rb, vb, kkb, lwb, kb, ab))
    shape = (2 * H, C, 2 * C)
    col = lax.broadcasted_iota(jnp.int32, shape, 2)
    ahead = lax.broadcasted_iota(jnp.int32, shape, 1) - jnp.where(col >= C, col - C, col)
    ahead = jnp.where(lax.broadcasted_iota(jnp.int32, shape, 0) < H, ahead, -ahead)
    s = s_ref[...]
    for j in range(per_step):
        rows = (slice(j * C, (j + 1) * C), slice((per_step - 1 - j) * C, (per_step - j) * C))
        parts = [_rwkv_prep(r_ref[0, rows[d], :], lw_ref[0, rows[d], :], k_ref[0, rows[d], :], v_ref[0, rows[d], :],
                            kk_ref[0, rows[d], :], a_ref[0, rows[d], :], d == 1)
                 for d, (r_ref, v_ref, kk_ref, lw_ref, k_ref, a_ref) in enumerate(dirs)]
        lhs, rhs, tail, vh, ptot = [jnp.concatenate([parts[0][i], parts[1][i]], axis=0) for i in range(5)]
        sc = _dg(lhs, rhs, _BNT, passes)
        top = jnp.where(ahead > 0, sc[:, :C, :], 0.0)
        bot = jnp.where(ahead >= 0, sc[:, C:, :], 0.0)
        tinv = _neumann_inverse(top[:, :, :C], _BNN, passes)
        ars = _dg(lhs, s, _BNT, passes)
        zero_v = jnp.concatenate([jnp.zeros_like(vh), vh], axis=1)
        u = _dg(tinv, ars[:, :C, :] + _dg(top, zero_v, _BNN, passes), _BNN, passes)
        uv = jnp.concatenate([u, vh], axis=1)
        y = ars[:, C:, :] + _dg(bot, uv, _BNN, passes)
        s = s * ptot + _dg(uv, tail, _BTN, passes)
        yf_ref[0, rows[0], :] = jnp.concatenate([y[h] for h in range(H)], axis=-1)
        yb_ref[0, rows[1], :] = jnp.concatenate([y[H + h] for h in range(H)], axis=-1)
    s_ref[...] = s


def rwkv_scan(r, v, kk, lw, key, rate, n_ctx):
    B, T, D = r.shape
    N = RWKV_HEAD
    rows = _scan_rows(RWKV_CHUNK, RWKV_CHUNKS_PER_STEP, T, n_ctx)
    nc = T // rows
    ncc = n_ctx // rows
    fwd = lambda b, i: (b, i, 0)
    bwd = lambda b, i: (b, _rev_chunk(i, ncc, nc), 0)
    blk = (1, rows, D)
    return pl.pallas_call(
        functools.partial(_rwkv_body, passes=RWKV_PASSES),
        grid=(B, nc),
        in_specs=[pl.BlockSpec(blk, fwd)] * 6 + [pl.BlockSpec(blk, bwd)] * 6,
        out_specs=[pl.BlockSpec(blk, fwd), pl.BlockSpec(blk, bwd)],
        out_shape=[jax.ShapeDtypeStruct((B, T, D), F32)] * 2,
        scratch_shapes=[pltpu.VMEM((2 * (D // N), N, N), F32)],
        compiler_params=_cparams("parallel", "arbitrary"),
        name="rwkv7_scan",
    )(r, v, kk, lw[0], key[0], rate[0], r, v, kk, lw[1], key[1], rate[1])


def _moe_body(be_ref, nu_ref, x_ref, w1_ref, w3_ref, w2_ref, o_ref, w1b, w3b, w2b):
    i = pl.program_id(0)
    prev = be_ref[jnp.maximum(i - 1, 0)]
    used = i < nu_ref[0]

    @pl.when(used & ((i == 0) | (be_ref[i] != prev)))
    def _():
        w1b[...] = w1_ref[0, 0].astype(BF16)
        w3b[...] = w3_ref[0, 0].astype(BF16)
        w2b[...] = w2_ref[0, 0].astype(BF16)

    @pl.when(used)
    def _():
        x = x_ref[...]
        h1 = jnp.dot(x, w1b[...], preferred_element_type=F32)
        h3 = jnp.dot(x, w3b[...], preferred_element_type=F32)
        hid = (h1 * jax.nn.sigmoid(h1)) * h3
        o_ref[...] = jnp.dot(hid.astype(BF16), w2b[...], preferred_element_type=F32).astype(o_ref.dtype)

    @pl.when(jnp.logical_not(used))
    def _():
        o_ref[...] = jnp.zeros(o_ref.shape, o_ref.dtype)


def moe_experts(xs, blk_e, n_used, w1, w3, w2, layer):
    n_slots, D = xs.shape
    hid = w1.shape[-1]
    n_blocks = n_slots // MOE_BLOCK
    return pl.pallas_call(
        _moe_body,
        grid_spec=pltpu.PrefetchScalarGridSpec(
            num_scalar_prefetch=2,
            grid=(n_blocks,),
            in_specs=[pl.BlockSpec((MOE_BLOCK, D), lambda i, be, nu: (i, 0)),
                      pl.BlockSpec((1, 1, D, hid), lambda i, be, nu: (layer, be[i], 0, 0)),
                      pl.BlockSpec((1, 1, D, hid), lambda i, be, nu: (layer, be[i], 0, 0)),
                      pl.BlockSpec((1, 1, hid, D), lambda i, be, nu: (layer, be[i], 0, 0))],
            out_specs=pl.BlockSpec((MOE_BLOCK, D), lambda i, be, nu: (i, 0)),
            scratch_shapes=[pltpu.VMEM((D, hid), BF16), pltpu.VMEM((D, hid), BF16), pltpu.VMEM((hid, D), BF16)],
        ),
        out_shape=jax.ShapeDtypeStruct((n_slots, D), MOE_OUT_DTYPE),
        compiler_params=_cparams("arbitrary"),
        name="moe_experts",
    )(blk_e, n_used, xs, w1, w3, w2)


def _route_body(lg_ref, bias_ref, out_ref, cnt_ref, run_ref, *, tm):
    @pl.when(pl.program_id(0) == 0)
    def _():
        run_ref[...] = jnp.zeros(run_ref.shape, F32)

    x = lg_ref[...] + bias_ref[...]
    lane = lax.broadcasted_iota(jnp.int32, x.shape, 1)
    far = 1 << 20

    def first_lane(hit):
        return jnp.min(jnp.where(hit, lane, far), axis=-1, keepdims=True)

    def masked_softmax(mask):
        xm = jnp.where(mask, x, -1e30)
        e = jnp.where(mask, jnp.exp(xm - jnp.max(xm, axis=-1, keepdims=True)), 0.0)
        return e / jnp.sum(e, axis=-1, keepdims=True)

    is_group = lane < MOE_GROUPS
    pg = masked_softmax(is_group)
    pg_top = jnp.max(pg, axis=-1, keepdims=True)
    g_idx = first_lane(is_group & (pg == pg_top))
    lo = MOE_GROUPS + MOE_PER_GROUP * g_idx
    in_group = (lane >= lo) & (lane < lo + MOE_PER_GROUP)
    pe = masked_softmax(in_group)
    p1 = jnp.max(pe, axis=-1, keepdims=True)
    l1 = first_lane(in_group & (pe == p1))
    rest_ok = in_group & (lane != l1)
    rest = jnp.where(rest_ok, pe, -1.0)
    p2 = jnp.max(rest, axis=-1, keepdims=True)
    l2 = first_lane(rest_ok & (rest == p2))
    psum = p1 + p2
    w1 = pg_top * p1 / psum
    w2 = pg_top * p2 / psum

    oh1 = (lane == l1).astype(F32)
    oh2 = (lane == l2).astype(F32)
    both = oh1 + oh2
    earlier = (lax.broadcasted_iota(jnp.int32, (tm, tm), 0) > lax.broadcasted_iota(jnp.int32, (tm, tm), 1))
    base = _dotf(earlier.astype(BF16), both.astype(BF16)) + run_ref[...]
    r1 = jnp.sum(base * oh1, axis=-1, keepdims=True)
    r2 = jnp.sum(base * oh2, axis=-1, keepdims=True)
    run_ref[...] = run_ref[...] + jnp.sum(both, axis=0, keepdims=True)
    cnt_ref[...] = run_ref[...]
    cols = ((l1 - MOE_GROUPS).astype(F32), (l2 - MOE_GROUPS).astype(F32), r1, r2, w1, w2)
    out = jnp.zeros(x.shape, F32)
    for j, c in enumerate(cols):
        out = jnp.where(lane == j, c, out)
    out_ref[...] = out


def moe_route(logits, b_group, b_expert):
    N, W = logits.shape
    tm = _pick(N, (512, 256, 128, 64, 32, 16, 8))
    bias = _pad_cols(jnp.concatenate([b_group, b_expert])[None, :], W)
    return pl.pallas_call(
        functools.partial(_route_body, tm=tm),
        grid=(N // tm,),
        in_specs=[pl.BlockSpec((tm, W), lambda i: (i, 0)), pl.BlockSpec((1, W), lambda i: (0, 0))],
        out_specs=[pl.BlockSpec((tm, W), lambda i: (i, 0)), pl.BlockSpec((1, W), lambda i: (0, 0))],
        out_shape=[jax.ShapeDtypeStruct((N, W), F32), jax.ShapeDtypeStruct((1, W), F32)],
        scratch_shapes=[pltpu.VMEM((1, W), F32)],
        compiler_params=_cparams("arbitrary"),
        name="moe_route",
    )(logits, bias)


def hier_moe(tokens, logits, b_group, b_expert, w1, w3, w2, layer):
    N, D = tokens.shape
    route, cnt = moe_route(logits, b_group, b_expert)
    eid = route[:, 0:MOE_TOPK].astype(jnp.int32).reshape(-1)
    rank = route[:, MOE_TOPK:2 * MOE_TOPK].astype(jnp.int32).reshape(-1)
    wts = route[:, 2 * MOE_TOPK:3 * MOE_TOPK]
    counts = cnt[0, MOE_GROUPS:MOE_GROUPS + MOE_EXPERTS].astype(jnp.int32)
    A = N * MOE_TOPK
    padded = (counts + MOE_BLOCK - 1) // MOE_BLOCK * MOE_BLOCK
    pend = jnp.cumsum(padded)
    dest = (pend - padded)[eid] + rank
    n_blocks = -(-A // MOE_BLOCK) + MOE_EXPERTS
    n_slots = n_blocks * MOE_BLOCK
    starts = jnp.arange(n_blocks, dtype=jnp.int32)[:, None] * MOE_BLOCK
    blk_e = jnp.minimum(jnp.sum((pend[None, :] <= starts).astype(jnp.int32), axis=1), MOE_EXPERTS - 1)
    bits = max(1, (A - 1).bit_length())
    order = jnp.sort((eid << bits) | jnp.arange(A, dtype=jnp.int32)) & ((1 << bits) - 1)
    slot_e = jnp.repeat(blk_e, MOE_BLOCK)
    pos = jnp.arange(n_slots, dtype=jnp.int32) - (pend - padded)[slot_e]
    src = jnp.minimum((jnp.cumsum(counts) - counts)[slot_e] + pos, A - 1)
    slot_tok = jnp.where(pos < counts[slot_e], order[src] // MOE_TOPK, 0)
    xs = tokens[slot_tok]
    ys = moe_experts(xs, blk_e, (pend[-1:] // MOE_BLOCK).astype(jnp.int32), w1, w3, w2, layer)
    d2 = dest.reshape(N, MOE_TOPK)
    return ys[d2[:, 0]].astype(F32) * wts[:, 0:1] + ys[d2[:, 1]].astype(F32) * wts[:, 1:2]


def _rope_tables(n_lat, n_ctx):
    rows = n_lat // GRID_W
    row = jnp.repeat(jnp.arange(rows, dtype=F32), GRID_W)
    col = jnp.tile(jnp.arange(GRID_W, dtype=F32), rows)
    n_freq = MLA_ROPE // 4
    inv = ROPE_BASE ** (-jnp.arange(n_freq, dtype=F32) / n_freq)
    ang = jnp.stack([row[:, None] * inv, col[:, None] * inv], axis=1)
    cos, sin = jnp.cos(ang), jnp.sin(ang)
    zf = jnp.zeros((n_lat, n_freq), F32)
    lat = lambda parts, fill: jnp.concatenate(
        [jnp.full((n_lat, MLA_NOPE), fill, F32)] + parts + [jnp.full((n_lat, MLA_PAD - MLA_QK), fill, F32)], axis=1)
    c = lat([cos[:, 0], cos[:, 0], cos[:, 1], cos[:, 1]], 1.0)
    s_lo = lat([-sin[:, 0], zf, -sin[:, 1], zf], 0.0)
    s_hi = lat([zf, sin[:, 0], zf, sin[:, 1]], 0.0)
    ctx = lambda fill: jnp.full((n_ctx, MLA_PAD), fill, F32)
    return jnp.concatenate([ctx(1.0), c], 0), jnp.concatenate([ctx(0.0), s_lo + s_hi], 0)


def _const_spec(shape):
    return pl.BlockSpec(shape, lambda i: (0,) * len(shape), pipeline_mode=pl.Buffered(1))


def _normmod(x, gain, shift, scale):
    return x * lax.rsqrt(jnp.mean(x * x, -1, keepdims=True) + EPS) * gain * (1 + scale) + shift


def _head_indicator(D, N):
    e = (jnp.arange(D)[:, None] // N == jnp.arange(128)[None, :]).astype(BF16)
    return e, e.T


def _seg_dot(x, e):
    xh, xl = _split(x)
    return jnp.dot(xh, e, preferred_element_type=F32) + jnp.dot(xl, e, preferred_element_type=F32)


def _dotf(a, b):
    return jnp.dot(a, b, preferred_element_type=F32)


def _rwkv_pre_body(*refs, tm, blocks_per_batch, ctx_blocks, vres):
    (h_ref, hp_ref, hn_ref, m_ref, gain_ref, mu_ref, w0_ref, a0_ref, kk_ref, ka_ref, e_ref, et_ref,
     wr_ref, wk_ref, wv_ref, w1_ref, w2_ref, a1_ref, a2_ref, g1_ref, g2_ref) = refs[:21]
    rest = refs[21:]
    if vres:
        v0_ref, v1_ref, v2_ref, vf_ref = rest[:4]
        rest = rest[4:]
    r_o, v_o, kk_o, lw0_o, lw1_o, k0_o, k1_o, ra0_o, ra1_o, gate_o = rest

    tb = pl.program_id(0) % blocks_per_batch
    seg_start = (tb == 0) | (tb == ctx_blocks)
    seg_end = (tb == ctx_blocks - 1) | (tb == blocks_per_batch - 1)
    shift, scale, gain = m_ref[0, 0:1, :], m_ref[0, 1:2, :], gain_ref[...]
    u = _normmod(h_ref[...], gain, shift, scale)
    up = jnp.where(seg_start, 0.0, _normmod(hp_ref[7:8, :], gain, shift, scale))
    un = jnp.where(seg_end, 0.0, _normmod(hn_ref[0:1, :], gain, shift, scale))
    row = lax.broadcasted_iota(jnp.int32, (tm, 1), 0)
    u_prev = jnp.where(row == 0, up, pltpu.roll(u, 1, 0))
    u_next = jnp.where(row == tm - 1, un, pltpu.roll(u, tm - 1, 0))
    xx = 0.5 * (u_prev + u_next) - u
    xr, xw, xk, xv, xa, xg = [(u + xx * mu_ref[j:j + 1, :]).astype(BF16) for j in range(6)]

    r = _dotf(xr, wr_ref[...])
    k = _dotf(xk, wk_ref[...])
    v = _dotf(xv, wv_ref[...])
    if vres:
        lo = _dotf(xv, v1_ref[...]).astype(BF16)
        v = v + (vf_ref[...] - v) * jax.nn.sigmoid(v0_ref[...] + _dotf(lo, v2_ref[...]))
    tl = jnp.tanh(_dotf(xw, w1_ref[...])).astype(BF16)
    al = _dotf(xa, a1_ref[...]).astype(BF16)
    gl = jax.nn.sigmoid(_dotf(xg, g1_ref[...])).astype(BF16)
    gate_o[...] = _dotf(gl, g2_ref[...])
    kx = k * kk_ref[...]
    inv = lax.rsqrt(_seg_dot(kx * kx, e_ref[...]) + EPS)
    r_o[...] = r
    v_o[...] = v
    kk_o[...] = kx * _seg_dot(inv, et_ref[...])
    for d, (lw_o, k_o, ra_o) in enumerate(((lw0_o, k0_o, ra0_o), (lw1_o, k1_o, ra1_o))):
        z = w0_ref[d:d + 1, :] + _dotf(tl, w2_ref[d])
        lw_o[...] = -math.exp(-0.5) * jax.nn.sigmoid(z)
        a = jax.nn.sigmoid(a0_ref[d:d + 1, :] + _dotf(al, a2_ref[d]))
        ra_o[...] = a
        k_o[...] = k * (1 + (a - 1) * ka_ref[...])


def _row_tile(seg):
    return _pick(seg, (256, 128, 64, 32, 16, 8))


def _pad_cols(w, n):
    return jnp.pad(w, ((0, 0), (0, n - w.shape[1])))


def _pad_rows(w, n):
    return jnp.pad(w, ((0, n - w.shape[0]), (0, 0)))


def rwkv_pre(h, m_seg, seg, T, n_ctx, gain, mu, wr, wk, wv, w0, w1, w2, a0, a1, a2, g1, g2, k_k, k_a, vres, v_first):
    M, D = h.shape
    tm = _row_tile(seg)
    lora = w1.shape[-1]
    e, et = _head_indicator(D, RWKV_HEAD)
    zero = jnp.zeros((lora, D), F32)
    w2p = jnp.stack([jnp.concatenate([w2[0], zero], 0), jnp.concatenate([zero, w2[1]], 0)]).astype(BF16)
    a2p = jnp.stack([jnp.concatenate([a2[0], zero], 0), jnp.concatenate([zero, a2[1]], 0)]).astype(BF16)
    gp = -(-g1.shape[1] // 128) * 128
    row = lambda a: a.reshape(1, D)
    consts = [row(gain), mu, w0, a0, row(k_k), row(k_a), e, et,
              wr.astype(BF16), wk.astype(BF16), wv.astype(BF16),
              jnp.concatenate([w1[0], w1[1]], 1).astype(BF16), w2p,
              jnp.concatenate([a1[0], a1[1]], 1).astype(BF16), a2p,
              _pad_cols(g1, gp).astype(BF16), _pad_rows(g2, gp).astype(BF16)]
    row_spec = pl.BlockSpec((tm, D), lambda i: (i, 0))
    last8 = M // 8 - 1
    in_specs = [row_spec,
                pl.BlockSpec((8, D), lambda i: (jnp.maximum(i * (tm // 8) - 1, 0), 0)),
                pl.BlockSpec((8, D), lambda i: (jnp.minimum((i + 1) * (tm // 8), last8), 0)),
                pl.BlockSpec((1, 6, D), lambda i: (i * tm // seg, 0, 0))]
    in_specs += [_const_spec(c.shape) for c in consts]
    args = [h, h, h, m_seg] + consts
    if vres is not None:
        v0, v1, v2 = vres
        extra = [row(v0), _pad_cols(v1, 128).astype(BF16), _pad_rows(v2, 128).astype(BF16)]
        in_specs += [_const_spec(c.shape) for c in extra] + [row_spec]
        args += extra + [v_first]
    return pl.pallas_call(
        functools.partial(_rwkv_pre_body, tm=tm, blocks_per_batch=T // tm, ctx_blocks=n_ctx // tm,
                          vres=vres is not None),
        grid=(M // tm,),
        in_specs=in_specs,
        out_specs=[row_spec] * 10,
        out_shape=[jax.ShapeDtypeStruct((M, D), F32)] * 10,
        compiler_params=_cparams("parallel"),
        name="rwkv7_pre",
    )(*args)


def _post_tail(xo, h_ref, m_ref, gain_ref, w_ref, wrt_ref, h_o, f_o, lg_o):
    h_new = h_ref[...] + m_ref[0, 2:3, :] * _dotf(xo, w_ref[...])
    h_o[...] = h_new
    f = _normmod(h_new, gain_ref[...], m_ref[0, 3:4, :], m_ref[0, 4:5, :])
    f_o[...] = f.astype(BF16)
    lg_o[...] = _dg(f, wrt_ref[...], _NN, 3)


def _rwkv_post_body(yf_ref, yb_ref, r_ref, k0_ref, k1_ref, v_ref, gate_ref, lnw_ref, lnb_ref, rk_ref, e_ref, et_ref,
                    h_ref, m_ref, gain_ref, w_ref, wrt_ref, h_o, f_o, lg_o):
    e, et = e_ref[...], et_ref[...]
    inv_n = 1.0 / RWKV_HEAD
    y = yf_ref[...] + yb_ref[...]
    yc = y - _seg_dot(_seg_dot(y, e) * inv_n, et)
    var = _seg_dot(_seg_dot(yc * yc, e) * inv_n, et)
    yn = yc * lax.rsqrt(var + GN_EPS) * lnw_ref[...] + lnb_ref[...]
    k_bonus = 0.5 * (k0_ref[...] + k1_ref[...])
    bonus = _seg_dot(_seg_dot(r_ref[...] * k_bonus * rk_ref[...], e), et) * v_ref[...]
    xo = ((yn + bonus) * gate_ref[...]).astype(BF16)
    _post_tail(xo, h_ref, m_ref, gain_ref, w_ref, wrt_ref, h_o, f_o, lg_o)


def _post_call(body, name, row_args, consts, h, m_seg, seg, gain, w_out, w_router):
    M, D = h.shape
    tm = _row_tile(seg)
    row_spec = lambda a: pl.BlockSpec((tm, a.shape[1]), lambda i: (i, 0))
    tail = [gain.reshape(1, D), w_out.astype(BF16), w_router]
    in_specs = ([row_spec(a) for a in row_args] + [_const_spec(c.shape) for c in consts] +
                [row_spec(h), pl.BlockSpec((1, 6, D), lambda i: (i * tm // seg, 0, 0))] +
                [_const_spec(c.shape) for c in tail])
    nr = w_router.shape[1]
    return pl.pallas_call(
        body,
        grid=(M // tm,),
        in_specs=in_specs,
        out_specs=[pl.BlockSpec((tm, D), lambda i: (i, 0)), pl.BlockSpec((tm, D), lambda i: (i, 0)),
                   pl.BlockSpec((tm, nr), lambda i: (i, 0))],
        out_shape=[jax.ShapeDtypeStruct((M, D), F32), jax.ShapeDtypeStruct((M, D), BF16),
                   jax.ShapeDtypeStruct((M, nr), F32)],
        compiler_params=_cparams("parallel"),
        name=name,
    )(*row_args, *consts, h, m_seg, *tail)


def _hy_post_body(a_ref, of_ref, ob_ref, z_ref, og_ref, h_ref, m_ref, gain_ref, w_ref, wrt_ref, h_o, f_o, lg_o):
    o = of_ref[...] + ob_ref[...]
    z = z_ref[...]
    parts = [a_ref[...]]
    for hd in range(GDN_HEADS):
        sl = slice(hd * GDN_DV, (hd + 1) * GDN_DV)
        oh, zh = o[:, sl], z[:, sl]
        on = oh * lax.rsqrt(jnp.mean(oh * oh, -1, keepdims=True) + EPS) * og_ref[...]
        parts.append(on * (zh * jax.nn.sigmoid(zh)))
    xo = jnp.concatenate(parts, axis=-1).astype(BF16)
    _post_tail(xo, h_ref, m_ref, gain_ref, w_ref, wrt_ref, h_o, f_o, lg_o)


def hy_post(a, of, ob, z, out_g, h, m_seg, seg, gain, w_out, w_router):
    return _post_call(_hy_post_body, "hybrid_post", [a, of, ob, z], [out_g.reshape(1, -1)], h, m_seg, seg,
                      gain, w_out, w_router)


def _hy_pre_body(h_ref, m_ref, gain_ref, c_ref, s_ref, wq1, wkv1, wpe, wpe2, wgq, wz, wab, qag, kvag,
                 wqb, wqb2, wkn, wv, qng, qng2, kng, kng2, q_o, k_o, v_o, gq_o, z_o, ab_o):
    u = _normmod(h_ref[...], gain_ref[...], m_ref[0, 0:1, :], m_ref[0, 1:2, :]).astype(BF16)
    gq_o[...] = _dotf(u, wgq[...])
    z_o[...] = _dotf(u, wz[...])
    ab_o[...] = _dotf(u, wab[...])
    cq = _dotf(u, wq1[...])
    ckv = _dotf(u, wkv1[...])
    pe = _dotf(u, wpe[...])
    pe2 = _dotf(u, wpe2[...])
    cq = (cq * lax.rsqrt(jnp.mean(cq * cq, -1, keepdims=True) + EPS) * qag[...]).astype(BF16)
    ckv = (ckv * lax.rsqrt(jnp.mean(ckv * ckv, -1, keepdims=True) + EPS) * kvag[...]).astype(BF16)
    q = _dotf(cq, wqb[...])
    q2 = _dotf(cq, wqb2[...])
    kn = _dotf(ckv, wkn[...])
    vv = _dotf(ckv, wv[...])
    c, s = c_ref[...], s_ref[...]
    qc, qs = qng[...] * c, qng2[...] * s
    kc, ks = kng[...] * c, kng2[...] * s
    k_rot = pe2 * ks
    one_col = (lax.broadcasted_iota(jnp.int32, (1, MLA_PAD), 1) == MLA_V).astype(F32)
    inv_d = 1.0 / MLA_QK

    def inv_rms(t):
        return lax.rsqrt(jnp.sum(t * t, -1, keepdims=True) * inv_d + EPS)

    for hd in range(MLA_HEADS):
        sl = slice(hd * MLA_PAD, (hd + 1) * MLA_PAD)
        qh = q[:, sl]
        kh = kn[:, sl] + pe
        q_o[:, sl] = (inv_rms(qh) * (qh * qc + q2[:, sl] * qs)).astype(BF16)
        k_o[:, sl] = (inv_rms(kh) * (kh * kc + k_rot)).astype(BF16)
        v_o[:, sl] = (vv[:, sl] + one_col).astype(BF16)


def _pad_heads(w, heads, width, to):
    K = w.shape[0]
    return jnp.pad(w.reshape(K, heads, width), ((0, 0), (0, 0), (0, to - width))).reshape(K, heads * to)


def hy_pre(h, m_seg, seg, T, gain, rope, w_in, qa_g, w_qb, kva_g, w_kvb, qn_g, kn_g):
    M, D = h.shape
    tm = _row_tile(seg)
    H = MLA_HEADS
    c0, c1, c2 = MLA_COLS, MLA_COLS + GDN_QKV, MLA_COLS + GDN_QKV + GDN_Z
    kvl = MLA_Q_LORA + MLA_KV_LORA
    wb = w_in.astype(BF16)
    wpe = jnp.pad(wb[:, kvl:c0], ((0, 0), (MLA_NOPE, MLA_PAD - MLA_QK)))
    wkv = w_kvb.reshape(MLA_KV_LORA, H, MLA_NOPE + MLA_V)
    pad1 = lambda g: jnp.pad(g, (0, MLA_PAD - MLA_QK)).reshape(1, MLA_PAD)
    lane = jnp.arange(MLA_PAD)
    rot = (lane >= MLA_NOPE) & (lane < MLA_QK)
    n_freq = MLA_ROPE // 4
    partner = jnp.where(rot, jnp.where(((lane - MLA_NOPE) // n_freq) % 2 == 0, lane + n_freq, lane - n_freq), lane)
    wqb_pad = _pad_heads(w_qb, H, MLA_QK, MLA_PAD).astype(BF16)
    wqb2 = wqb_pad.reshape(MLA_Q_LORA, H, MLA_PAD)[:, :, partner].reshape(MLA_Q_LORA, H * MLA_PAD)
    consts = [wb[:, :MLA_Q_LORA], wb[:, MLA_Q_LORA:kvl], wpe, wpe[:, partner], wb[:, c0:c1], wb[:, c1:c2],
              _pad_cols(wb[:, c2:], 128), qa_g.reshape(1, -1), kva_g.reshape(1, -1),
              wqb_pad, wqb2,
              _pad_heads(wkv[:, :, :MLA_NOPE].reshape(MLA_KV_LORA, -1), H, MLA_NOPE, MLA_PAD).astype(BF16),
              _pad_heads(wkv[:, :, MLA_NOPE:].reshape(MLA_KV_LORA, -1), H, MLA_V, MLA_PAD).astype(BF16),
              pad1(qn_g), pad1(qn_g)[:, partner], pad1(kn_g), pad1(kn_g)[:, partner]]
    bpb = T // tm
    row = lambda n: pl.BlockSpec((tm, n), lambda i: (i, 0))
    tab = pl.BlockSpec((tm, MLA_PAD), lambda i: (i % bpb, 0))
    in_specs = ([row(D), pl.BlockSpec((1, 6, D), lambda i: (i * tm // seg, 0, 0)), _const_spec((1, D)), tab, tab]
                + [_const_spec(c.shape) for c in consts])
    wide = H * MLA_PAD
    return pl.pallas_call(
        _hy_pre_body,
        grid=(M // tm,),
        in_specs=in_specs,
        out_specs=[row(wide), row(wide), row(wide), row(GDN_QKV), row(GDN_Z), row(128)],
        out_shape=[jax.ShapeDtypeStruct((M, wide), BF16)] * 3 + [jax.ShapeDtypeStruct((M, GDN_QKV), F32),
                   jax.ShapeDtypeStruct((M, GDN_Z), F32), jax.ShapeDtypeStruct((M, 128), F32)],
        compiler_params=_cparams("parallel"),
        name="hybrid_pre",
    )(h, m_seg, gain.reshape(1, D), *rope, *consts)


def _gdn_prep_body(x_ref, xp_ref, xn_ref, w_ref, q_o, k_o, v_o, *, tm, blocks_per_batch, ctx_blocks):
    tb = pl.program_id(0) % blocks_per_batch
    seg_start = (tb == 0) | (tb == ctx_blocks)
    seg_end = (tb == ctx_blocks - 1) | (tb == blocks_per_batch - 1)
    x = x_ref[...]
    xp = jnp.where(seg_start, 0.0, xp_ref[...])
    xn = jnp.where(seg_end, 0.0, xn_ref[...])
    row = lax.broadcasted_iota(jnp.int32, (tm, 1), 0)
    half = GDN_CONV // 2
    acc = x * w_ref[half:half + 1, :]
    for s in range(1, half + 1):
        before = pltpu.roll(x, s, 0)
        after = pltpu.roll(x, tm - s, 0)
        for r in range(s):
            before = jnp.where(row == r, xp[8 - s + r:8 - s + r + 1, :], before)
            after = jnp.where(row == tm - s + r, xn[r:r + 1, :], after)
        acc = acc + before * w_ref[half - s:half - s + 1, :] + after * w_ref[half + s:half + s + 1, :]
    y = acc * jax.nn.sigmoid(acc)
    nk = GDN_HEADS * GDN_DK
    for hd in range(GDN_HEADS):
        sl = slice(hd * GDN_DK, (hd + 1) * GDN_DK)
        qh = y[:, sl]
        kh = y[:, nk + hd * GDN_DK:nk + (hd + 1) * GDN_DK]
        q_o[:, sl] = qh * lax.rsqrt(jnp.sum(qh * qh, -1, keepdims=True) + EPS) * GDN_DK ** -0.5
        k_o[:, sl] = kh * lax.rsqrt(jnp.sum(kh * kh, -1, keepdims=True) + EPS)
    v_o[...] = y[:, 2 * nk:]


def gdn_prep(gq, conv_w, seg, T, n_ctx):
    M, W = gq.shape
    tm = _row_tile(seg)
    last8 = M // 8 - 1
    nk = GDN_HEADS * GDN_DK
    row = lambda n: pl.BlockSpec((tm, n), lambda i: (i, 0))
    return pl.pallas_call(
        functools.partial(_gdn_prep_body, tm=tm, blocks_per_batch=T // tm, ctx_blocks=n_ctx // tm),
        grid=(M // tm,),
        in_specs=[row(W),
                  pl.BlockSpec((8, W), lambda i: (jnp.maximum(i * (tm // 8) - 1, 0), 0)),
                  pl.BlockSpec((8, W), lambda i: (jnp.minimum((i + 1) * (tm // 8), last8), 0)),
                  _const_spec(conv_w.shape)],
        out_specs=[row(nk), row(nk), row(W - 2 * nk)],
        out_shape=[jax.ShapeDtypeStruct((M, nk), F32), jax.ShapeDtypeStruct((M, nk), F32),
                   jax.ShapeDtypeStruct((M, W - 2 * nk), F32)],
        compiler_params=_cparams("parallel"),
        name="gdn_prep",
    )(gq, gq, gq, conv_w)


def rwkv_post(yf, yb, r, k0, k1, v, gate, ln_w, ln_b, r_k, h, m_seg, seg, gain, wo, w_router):
    D = h.shape[1]
    e, et = _head_indicator(D, RWKV_HEAD)
    consts = [ln_w.reshape(1, D), ln_b.reshape(1, D), r_k.reshape(1, D), e, et]
    return _post_call(_rwkv_post_body, "rwkv7_post", [yf, yb, r, k0, k1, v, gate], consts, h, m_seg, seg,
                      gain, wo, w_router)


def kernel(x, c, ctx, c_ctx, ada_w, ada_b, norm_mix, norm_ffn, hy_w_in, hy_w_out, mla_qa_norm, mla_w_qb, mla_kva_norm, mla_w_kvb, mla_q_norm, mla_k_norm, gdn_conv, gdn_a_log, gdn_dt_bias, gdn_out_norm, rk_mu, rk_wr, rk_wk, rk_wv, rk_wo, rk_w0, rk_w1, rk_w2, rk_a0, rk_a1, rk_a2, rk_g1, rk_g2, rk_kk, rk_ka, rk_rk, rk_ln_w, rk_ln_b, rk_v0, rk_v1, rk_v2, moe_w_group, moe_b_group, moe_w_expert, moe_b_expert, moe_w1, moe_w3, moe_w2):
    B, S, D = x.shape
    L = ctx.shape[1]
    T = L + S
    depth = ada_w.shape[0]
    rope = _rope_tables(S, L)
    n_rows = -(-(B + 1) // 8) * 8
    sc = jnp.concatenate([jax.nn.silu(c), jax.nn.silu(c_ctx)[None], jnp.zeros((n_rows - B - 1, D), F32)], 0)
    M = B * T
    h = jnp.concatenate([ctx, x], axis=1).reshape(M, D)
    seg = math.gcd(L, S)
    nseg = T // seg
    v_first = None
    for l in range(depth):
        m = mm(sc, ada_w[l], hi=True) + ada_b[l]
        m_lat = jnp.broadcast_to(m[:B].reshape(B, 1, 6, D), (B, S // seg, 6, D))
        m_ctx = jnp.broadcast_to(m[B].reshape(1, 1, 6, D), (B, L // seg, 6, D))
        m_seg = jnp.concatenate([m_ctx, m_lat], axis=1).reshape(B * nseg, 6, D)

        router = _pad_cols(jnp.concatenate([moe_w_group[l], moe_w_expert[l]], axis=1), 128)
        j = l // 2
        b3 = lambda a: a.reshape(B, T, a.shape[-1])
        if l % 2 == 0:
            q, k, v, gq, z, ab = hy_pre(h, m_seg, seg, T, norm_mix[l], rope, hy_w_in[j], mla_qa_norm[j],
                                        mla_w_qb[j], mla_kva_norm[j], mla_w_kvb[j], mla_q_norm[j], mla_k_norm[j])
            q, k, v = b3(q), b3(k), b3(v)
            a_lat = attention(q[:, L:], k, v)
            a_ctx = attention(q[:, :L], k[:, :L], v[:, :L])
            a = jnp.concatenate([a_ctx, a_lat], axis=1).reshape(M, -1)
            gq_, gk_, gv_ = gdn_prep(gq, gdn_conv[j], seg, T, L)
            ab = ab[:, :GDN_AB].reshape(B, T, 2, 2, GDN_HEADS)
            g = -jnp.exp(gdn_a_log[j]) * jax.nn.softplus(ab[:, :, :, 0] + gdn_dt_bias[j])
            beta = jax.nn.sigmoid(ab[:, :, :, 1])
            of, ob = gdn_scan(b3(gq_), b3(gk_), b3(gv_), g, beta, L)
            h, f, logits = hy_post(a, of.reshape(M, -1), ob.reshape(M, -1), z, gdn_out_norm[j], h, m_seg, seg,
                                   norm_ffn[l], hy_w_out[j], router)
        else:
            vres = None if j == 0 else (rk_v0[j - 1], rk_v1[j - 1], rk_v2[j - 1])
            r, v, kk, lw0, lw1, k0, k1, ra0, ra1, gate = rwkv_pre(
                h, m_seg, seg, T, L, norm_mix[l], rk_mu[j], rk_wr[j], rk_wk[j], rk_wv[j], rk_w0[j], rk_w1[j],
                rk_w2[j], rk_a0[j], rk_a1[j], rk_a2[j], rk_g1[j], rk_g2[j], rk_kk[j], rk_ka[j], vres, v_first)
            if j == 0:
                v_first = v
            yf, yb = rwkv_scan(b3(r), b3(v), b3(kk), [b3(lw0), b3(lw1)], [b3(k0), b3(k1)], [b3(ra0), b3(ra1)], L)
            h, f, logits = rwkv_post(yf.reshape(M, D), yb.reshape(M, D), r, k0, k1, v, gate, rk_ln_w[j], rk_ln_b[j],
                                     rk_rk[j], h, m_seg, seg, norm_ffn[l], rk_wo[j], router)
        moe_out = hier_moe(f, logits, moe_b_group[l], moe_b_expert[l], moe_w1, moe_w3, moe_w2, l)
        gate_ffn = m_seg[:, None, 5, :]
        h = (h.reshape(B * nseg, seg, D) + gate_ffn * moe_out.reshape(B * nseg, seg, D)).reshape(M, D)
    return h.reshape(B, T, D)[:, L:]
```

```python
import functools
import math

import jax
import jax.numpy as jnp
from jax import lax
from jax.experimental import pallas as pl
from jax.experimental.pallas import tpu as pltpu

F32 = jnp.float32
BF16 = jnp.bfloat16
HI = lax.Precision.HIGHEST

GRID_W = 64
EPS = 1e-6

MLA_HEADS = 8
MLA_Q_LORA = 256
MLA_KV_LORA = 128
MLA_NOPE = 64
MLA_ROPE = 32
MLA_V = 64
MLA_QK = MLA_NOPE + MLA_ROPE
MLA_SCALE = MLA_QK ** -0.5
ROPE_BASE = 10000.0
MLA_PAD = 128

GDN_HEADS = 4
GDN_DK = 128
GDN_DV = 128
GDN_CONV = 5
GDN_CHUNK = 64

RWKV_HEAD = 64
RWKV_CHUNK = 64
GN_EPS = 64e-5

MOE_GROUPS = 4
MOE_PER_GROUP = 8
MOE_EXPERTS = MOE_GROUPS * MOE_PER_GROUP
MOE_TOPK = 2
MOE_BLOCK = 512

MLA_COLS = MLA_Q_LORA + MLA_KV_LORA + MLA_ROPE
GDN_QKV = GDN_HEADS * (2 * GDN_DK + GDN_DV)
GDN_Z = GDN_HEADS * GDN_DV
GDN_AB = 2 * 2 * GDN_HEADS

VMEM_LIMIT_BYTES = 48 * 1024 * 1024

GDN_PASSES = 1
RWKV_PASSES = 1
MOE_OUT_DTYPE = BF16


def _cparams(*sem):
    return pltpu.CompilerParams(dimension_semantics=sem, vmem_limit_bytes=VMEM_LIMIT_BYTES)


def _pick(n, cands):
    for c in cands:
        if n % c == 0:
            return c
    return n


def _split(a):
    hi = a.astype(BF16)
    lo = (a - hi.astype(F32)).astype(BF16)
    return hi, lo


def _dg(a, b, dn, passes):
    if passes == 6:
        return lax.dot_general(a, b, dn, precision=HI, preferred_element_type=F32)
    if passes == 1:
        return lax.dot_general(a.astype(BF16), b.astype(BF16), dn, preferred_element_type=F32)
    ah, al = _split(a)
    bh, bl = _split(b)
    d = functools.partial(lax.dot_general, dimension_numbers=dn, preferred_element_type=F32)
    return d(ah, bh) + d(al, bh) + d(ah, bl)


_NN = (((1,), (0,)), ((), ()))
_NT = (((1,), (1,)), ((), ()))
_BNN = (((2,), (1,)), ((0,), (0,)))
_BNT = (((2,), (2,)), ((0,), (0,)))
_BTN = (((1,), (1,)), ((0,), (0,)))


def _mm_body(x_ref, w_ref, o_ref, *, hi):
    if hi:
        o_ref[...] = jnp.dot(x_ref[...], w_ref[...], precision=HI, preferred_element_type=F32)
    else:
        o_ref[...] = jnp.dot(x_ref[...].astype(BF16), w_ref[...].astype(BF16),
                             preferred_element_type=F32)


def mm(x, w, hi=False):
    M, K = x.shape
    N = w.shape[1]
    tm = _pick(M, (512, 256, 128, 64, 32, 16, 8))
    tn = _pick(N, (512, 384, 256, 128))
    return pl.pallas_call(
        functools.partial(_mm_body, hi=hi),
        grid=(M // tm, N // tn),
        in_specs=[pl.BlockSpec((tm, K), lambda i, j: (i, 0)),
                  pl.BlockSpec((K, tn), lambda i, j: (0, j))],
        out_specs=pl.BlockSpec((tm, tn), lambda i, j: (i, j)),
        out_shape=jax.ShapeDtypeStruct((M, N), F32),
        compiler_params=_cparams("parallel", "parallel"),
        name="dense_mm",
    )(x, w)


def _attn_body(q_ref, k_ref, v_ref, o_ref, m_ref, acc_ref, *, c2):
    ki = pl.program_id(3)

    @pl.when(ki == 0)
    def _():
        m_ref[...] = jnp.full(m_ref.shape, -1e30, F32)
        acc_ref[...] = jnp.zeros(acc_ref.shape, F32)

    heads = range(2)
    sl = [slice(h * MLA_PAD, (h + 1) * MLA_PAD) for h in heads]
    m_prev = [m_ref[h] for h in heads]
    acc_prev = [acc_ref[h] for h in heads]
    s = [lax.dot_general(q_ref[0, :, sl[h]], k_ref[0, :, sl[h]], _NT, preferred_element_type=F32) for h in heads]
    m_new, alpha, p = [], [], []
    reps = s[0].shape[1] // MLA_PAD
    for h in heads:
        m_new.append(jnp.maximum(m_prev[h], jnp.max(s[h], axis=-1, keepdims=True)))
        alpha.append(jnp.exp2((m_prev[h] - m_new[h]) * c2))
        x = (s[h] - jnp.tile(m_new[h], (1, reps))) * c2
        p.append(jnp.exp2(x).astype(BF16))
    pv = [jnp.dot(p[h], v_ref[0, :, sl[h]], preferred_element_type=F32) for h in heads]
    for h in heads:
        acc_ref[h] = alpha[h] * acc_prev[h] + pv[h]
        m_ref[h] = m_new[h]

    @pl.when(ki == pl.num_programs(3) - 1)
    def _():
        outs = []
        for h in range(2):
            a = acc_ref[h]
            outs.append(a[:, :MLA_V] / a[:, MLA_V:MLA_V + 1])
        o_ref[0] = jnp.concatenate(outs, axis=-1)


def attention(q, k, v):
    B, Sq, _ = q.shape
    Sk = k.shape[1]
    tq = _pick(Sq, (1024, 512, 256, 128))
    tk = _pick(Sk, (1408, 768, 512, 384, 256, 128))
    return pl.pallas_call(
        functools.partial(_attn_body, c2=MLA_SCALE * math.log2(math.e)),
        grid=(B, MLA_HEADS // 2, Sq // tq, Sk // tk),
        in_specs=[pl.BlockSpec((1, tq, 2 * MLA_PAD), lambda b, p, i, j: (b, i, p)),
                  pl.BlockSpec((1, tk, 2 * MLA_PAD), lambda b, p, i, j: (b, j, p)),
                  pl.BlockSpec((1, tk, 2 * MLA_PAD), lambda b, p, i, j: (b, j, p))],
        out_specs=pl.BlockSpec((1, tq, 2 * MLA_V), lambda b, p, i, j: (b, i, p)),
        out_shape=jax.ShapeDtypeStruct((B, Sq, MLA_HEADS * MLA_V), F32),
        scratch_shapes=[pltpu.VMEM((2, tq, MLA_PAD), F32), pltpu.VMEM((2, tq, MLA_PAD), F32)],
        compiler_params=_cparams("parallel", "parallel", "parallel", "arbitrary"),
        name="mla_attention",
    )(q, k, v)


def _tri_masks(C, rev):
    row = lax.broadcasted_iota(jnp.int32, (C, C), 0)
    col = lax.broadcasted_iota(jnp.int32, (C, C), 1)
    if rev:
        return row <= col, row < col
    return row >= col, row > col


def _neumann_inverse(nil, dn, passes):
    C = nil.shape[-1]
    eye = (lax.broadcasted_iota(jnp.int32, (C, C), 0) ==
           lax.broadcasted_iota(jnp.int32, (C, C), 1)).astype(F32)
    x = eye + nil
    p = nil
    for _ in range(int(math.log2(C)) - 1):
        p = _dg(p, p, dn, passes)
        x = x + _dg(x, p, dn, passes)
    return x


def _gdn_body(qf, kf, vf, gcf, bcf, grf, qb, kb, vb, gcb, bcb, grb, of_ref, ob_ref, s_ref, *, passes):
    C = GDN_CHUNK
    H = GDN_HEADS

    @pl.when(pl.program_id(1) == 0)
    def _():
        s_ref[...] = jnp.zeros(s_ref.shape, F32)

    dirs = ((qf, kf, vf, gcf, bcf, grf), (qb, kb, vb, gcb, bcb, grb))
    per_step = qf.shape[1] // C
    n = 2 * H
    unit = lax.broadcasted_iota(jnp.int32, (n, C, C), 0)
    ahead = (lax.broadcasted_iota(jnp.int32, (n, C, C), 1) - lax.broadcasted_iota(jnp.int32, (n, C, C), 2))
    ahead = jnp.where(unit < H, ahead, -ahead)
    incl = ahead >= 0
    strict = ahead > 0
    s = s_ref[...]
    for j in range(per_step):
        sub = (j, per_step - 1 - j)
        qs, ks, vs, gcs, grs, betas, glast = [], [], [], [], [], [], []
        for d, (q_ref, k_ref, v_ref, gc_ref, bc_ref, gr_ref) in enumerate(dirs):
            rev = d == 1
            rows = slice(sub[d] * C, (sub[d] + 1) * C)
            tri = _tri_masks(C, rev)[0].astype(F32)
            gcum_col = _dg(tri, gc_ref[0, rows, :], _NN, 6)
            gcum_row = _dg(gr_ref[0, sub[d]], tri, _NT, 6)
            beta_all = bc_ref[0, rows, :]
            t_last = 0 if rev else C - 1
            for h in range(H):
                idx = d * H + h
                gcs.append(gcum_col[:, idx:idx + 1])
                grs.append(gcum_row[idx:idx + 1, :])
                glast.append(gcum_row[idx:idx + 1, t_last:t_last + 1])
                betas.append(beta_all[:, idx:idx + 1])
                qs.append(q_ref[0, rows, h * GDN_DK:(h + 1) * GDN_DK])
                ks.append(k_ref[0, rows, h * GDN_DK:(h + 1) * GDN_DK])
                vs.append(v_ref[0, rows, h * GDN_DV:(h + 1) * GDN_DV])
        q, k, v = jnp.stack(qs), jnp.stack(ks), jnp.stack(vs)
        gc, gr, beta, g_last = jnp.stack(gcs), jnp.stack(grs), jnp.stack(betas), jnp.stack(glast)

        decay = jnp.exp(jnp.where(incl, gc - gr, -1e30))
        kbeta = k * beta
        lower = jnp.where(strict, _dg(kbeta, k, _BNT, passes) * decay, 0.0)
        tinv = _neumann_inverse(-lower, _BNN, passes)
        eg = jnp.exp(gc)
        u = _dg(tinv, v * beta, _BNN, passes)
        w = _dg(tinv, kbeta * eg, _BNN, passes)
        aqk = jnp.where(incl, _dg(q, k, _BNT, passes) * decay, 0.0)
        v_new = u - _dg(w, s, _BNN, passes)
        o = _dg(q * eg, s, _BNN, passes) + _dg(aqk, v_new, _BNN, passes)
        s = s * jnp.exp(g_last) + _dg(k * jnp.exp(g_last - gc), v_new, _BTN, passes)
        for h in range(H):
            of_ref[0, sub[0] * C:(sub[0] + 1) * C, h * GDN_DV:(h + 1) * GDN_DV] = o[h]
            ob_ref[0, sub[1] * C:(sub[1] + 1) * C, h * GDN_DV:(h + 1) * GDN_DV] = o[H + h]
    s_ref[...] = s


GDN_CHUNKS_PER_STEP = 4
RWKV_CHUNKS_PER_STEP = 4


def _scan_rows(chunk, per_step, T, n_ctx):
    while per_step > 1 and (n_ctx % (chunk * per_step) or (T - n_ctx) % (chunk * per_step)):
        per_step //= 2
    return chunk * per_step


def _rev_chunk(i, ncc, nc):
    return jnp.where(i < ncc, ncc - 1 - i, nc - 1 + ncc - i)


def gdn_scan(q, k, v, g, beta, n_ctx):
    B, T, _ = q.shape
    C = GDN_CHUNK
    rows = _scan_rows(C, GDN_CHUNKS_PER_STEP, T, n_ctx)
    per_step = rows // C
    nc = T // rows
    ncc = n_ctx // rows
    gcol = g.reshape(B, T, 2 * GDN_HEADS)
    bcol = beta.reshape(B, T, 2 * GDN_HEADS)
    grow = jnp.swapaxes(gcol.reshape(B, T // C, C, 2 * GDN_HEADS), 2, 3)
    fwd = lambda b, i: (b, i, 0)
    bwd = lambda b, i: (b, _rev_chunk(i, ncc, nc), 0)
    fwd4 = lambda b, i: (b, i, 0, 0)
    bwd4 = lambda b, i: (b, _rev_chunk(i, ncc, nc), 0, 0)
    wide = q.shape[-1]
    wv = v.shape[-1]

    def specs(m3, m4):
        return [pl.BlockSpec((1, rows, wide), m3), pl.BlockSpec((1, rows, wide), m3), pl.BlockSpec((1, rows, wv), m3),
                pl.BlockSpec((1, rows, 2 * GDN_HEADS), m3), pl.BlockSpec((1, rows, 2 * GDN_HEADS), m3),
                pl.BlockSpec((1, per_step, 2 * GDN_HEADS, C), m4)]

    of, ob = pl.pallas_call(
        functools.partial(_gdn_body, passes=GDN_PASSES),
        grid=(B, nc),
        in_specs=specs(fwd, fwd4) + specs(bwd, bwd4),
        out_specs=[pl.BlockSpec((1, rows, wv), fwd), pl.BlockSpec((1, rows, wv), bwd)],
        out_shape=[jax.ShapeDtypeStruct((B, T, wv), F32)] * 2,
        scratch_shapes=[pltpu.VMEM((2 * GDN_HEADS, GDN_DK, GDN_DV), F32)],
        compiler_params=_cparams("parallel", "arbitrary"),
        name="gdn_scan",
    )(q, k, v, gcol, bcol, grow, q, k, v, gcol, bcol, grow)
    return of, ob


def _rwkv_prep(r, lw, k, v, kk, rate, rev):
    C, D = r.shape
    N = RWKV_HEAD
    H = D // N
    incl, _ = _tri_masks(C, rev)
    tri = incl.astype(BF16)
    l1 = lw.astype(BF16)
    rem = lw - l1.astype(F32)
    l2 = rem.astype(BF16)
    l3 = (rem - l2.astype(F32)).astype(BF16)
    linc = _dotf(tri, l1) + _dotf(tri, l2) + _dotf(tri, l3)
    lexc = linc - lw
    ltot = linc[0:1, :] if rev else linc[C - 1:C, :]
    b = kk * rate
    einv = jnp.exp(-linc)
    etail = jnp.exp(ltot - linc)

    def hs(x):
        return jnp.stack([x[:, h * N:(h + 1) * N] for h in range(H)], axis=0)

    lhs = jnp.concatenate([hs(-kk * jnp.exp(lexc)), hs(r * jnp.exp(linc))], axis=1)
    rhs = jnp.concatenate([hs(b * einv), hs(k * einv)], axis=1)
    tail = jnp.concatenate([hs(b * etail), hs(k * etail)], axis=1)
    return lhs, rhs, tail, hs(v), jnp.exp(hs(ltot))


def _rwkv_body(rf, vf, kkf, lwf, kf, af, rb, vb, kkb, lwb, kb, ab, yf_ref, yb_ref, s_ref, *, passes):
    @pl.when(pl.program_id(1) == 0)
    def _():
        s_ref[...] = jnp.zeros(s_ref.shape, F32)

    C = RWKV_CHUNK
    per_step = rf.shape[1] // C
    H = rf.shape[2] // RWKV_HEAD
    dirs = ((rf, vf, kkf, lwf, kf, af), (rb, vb, kkb, lwb, kb, ab))
    shape = (2 * H, C, 2 * C)
    col = lax.broadcasted_iota(jnp.int32, shape, 2)
    ahead = lax.broadcasted_iota(jnp.int32, shape, 1) - jnp.where(col >= C, col - C, col)
    ahead = jnp.where(lax.broadcasted_iota(jnp.int32, shape, 0) < H, ahead, -ahead)
    s = s_ref[...]
    for j in range(per_step):
        rows = (slice(j * C, (j + 1) * C), slice((per_step - 1 - j) * C, (per_step - j) * C))
        parts = [_rwkv_prep(r_ref[0, rows[d], :], lw_ref[0, rows[d], :], k_ref[0, rows[d], :], v_ref[0, rows[d], :],
                            kk_ref[0, rows[d], :], a_ref[0, rows[d], :], d == 1)
                 for d, (r_ref, v_ref, kk_ref, lw_ref, k_ref, a_ref) in enumerate(dirs)]
        lhs, rhs, tail, vh, ptot = [jnp.concatenate([parts[0][i], parts[1][i]], axis=0) for i in range(5)]
        sc = _dg(lhs, rhs, _BNT, passes)
        top = jnp.where(ahead > 0, sc[:, :C, :], 0.0)
        bot = jnp.where(ahead >= 0, sc[:, C:, :], 0.0)
        tinv = _neumann_inverse(top[:, :, :C], _BNN, passes)
        ars = _dg(lhs, s, _BNT, passes)
        zero_v = jnp.concatenate([jnp.zeros_like(vh), vh], axis=1)
        u = _dg(tinv, ars[:, :C, :] + _dg(top, zero_v, _BNN, passes), _BNN, passes)
        uv = jnp.concatenate([u, vh], axis=1)
        y = ars[:, C:, :] + _dg(bot, uv, _BNN, passes)
        s = s * ptot + _dg(uv, tail, _BTN, passes)
        yf_ref[0, rows[0], :] = jnp.concatenate([y[h] for h in range(H)], axis=-1)
        yb_ref[0, rows[1], :] = jnp.concatenate([y[H + h] for h in range(H)], axis=-1)
    s_ref[...] = s


def rwkv_scan(r, v, kk, lw, key, rate, n_ctx):
    B, T, D = r.shape
    N = RWKV_HEAD
    rows = _scan_rows(RWKV_CHUNK, RWKV_CHUNKS_PER_STEP, T, n_ctx)
    nc = T // rows
    ncc = n_ctx // rows
    fwd = lambda b, i: (b, i, 0)
    bwd = lambda b, i: (b, _rev_chunk(i, ncc, nc), 0)
    blk = (1, rows, D)
    return pl.pallas_call(
        functools.partial(_rwkv_body, passes=RWKV_PASSES),
        grid=(B, nc),
        in_specs=[pl.BlockSpec(blk, fwd)] * 6 + [pl.BlockSpec(blk, bwd)] * 6,
        out_specs=[pl.BlockSpec(blk, fwd), pl.BlockSpec(blk, bwd)],
        out_shape=[jax.ShapeDtypeStruct((B, T, D), F32)] * 2,
        scratch_shapes=[pltpu.VMEM((2 * (D // N), N, N), F32)],
        compiler_params=_cparams("parallel", "arbitrary"),
        name="rwkv7_scan",
    )(r, v, kk, lw[0], key[0], rate[0], r, v, kk, lw[1], key[1], rate[1])


def _moe_body(be_ref, nu_ref, x_ref, w1_ref, w3_ref, w2_ref, o_ref, w1b, w3b, w2b):
    i = pl.program_id(0)
    prev = be_ref[jnp.maximum(i - 1, 0)]
    used = i < nu_ref[0]

    @pl.when(used & ((i == 0) | (be_ref[i] != prev)))
    def _():
        w1b[...] = w1_ref[0, 0].astype(BF16)
        w3b[...] = w3_ref[0, 0].astype(BF16)
        w2b[...] = w2_ref[0, 0].astype(BF16)

    @pl.when(used)
    def _():
        x = x_ref[...]
        h1 = jnp.dot(x, w1b[...], preferred_element_type=F32)
        h3 = jnp.dot(x, w3b[...], preferred_element_type=F32)
        hid = (h1 * jax.nn.sigmoid(h1)) * h3
        o_ref[...] = jnp.dot(hid.astype(BF16), w2b[...], preferred_element_type=F32).astype(o_ref.dtype)

    @pl.when(jnp.logical_not(used))
    def _():
        o_ref[...] = jnp.zeros(o_ref.shape, o_ref.dtype)


def moe_experts(xs, blk_e, n_used, w1, w3, w2, layer):
    n_slots, D = xs.shape
    hid = w1.shape[-1]
    n_blocks = n_slots // MOE_BLOCK
    return pl.pallas_call(
        _moe_body,
        grid_spec=pltpu.PrefetchScalarGridSpec(
            num_scalar_prefetch=2,
            grid=(n_blocks,),
            in_specs=[pl.BlockSpec((MOE_BLOCK, D), lambda i, be, nu: (i, 0)),
                      pl.BlockSpec((1, 1, D, hid), lambda i, be, nu: (layer, be[i], 0, 0)),
                      pl.BlockSpec((1, 1, D, hid), lambda i, be, nu: (layer, be[i], 0, 0)),
                      pl.BlockSpec((1, 1, hid, D), lambda i, be, nu: (layer, be[i], 0, 0))],
            out_specs=pl.BlockSpec((MOE_BLOCK, D), lambda i, be, nu: (i, 0)),
            scratch_shapes=[pltpu.VMEM((D, hid), BF16), pltpu.VMEM((D, hid), BF16), pltpu.VMEM((hid, D), BF16)],
        ),
        out_shape=jax.ShapeDtypeStruct((n_slots, D), MOE_OUT_DTYPE),
        compiler_params=_cparams("arbitrary"),
        name="moe_experts",
    )(blk_e, n_used, xs, w1, w3, w2)


def _route_body(lg_ref, bias_ref, out_ref, cnt_ref, run_ref, *, tm):
    @pl.when(pl.program_id(0) == 0)
    def _():
        run_ref[...] = jnp.zeros(run_ref.shape, F32)

    x = lg_ref[...] + bias_ref[...]
    lane = lax.broadcasted_iota(jnp.int32, x.shape, 1)
    far = 1 << 20

    def first_lane(hit):
        return jnp.min(jnp.where(hit, lane, far), axis=-1, keepdims=True)

    def masked_softmax(mask):
        xm = jnp.where(mask, x, -1e30)
        e = jnp.where(mask, jnp.exp(xm - jnp.max(xm, axis=-1, keepdims=True)), 0.0)
        return e / jnp.sum(e, axis=-1, keepdims=True)

    is_group = lane < MOE_GROUPS
    pg = masked_softmax(is_group)
    pg_top = jnp.max(pg, axis=-1, keepdims=True)
    g_idx = first_lane(is_group & (pg == pg_top))
    lo = MOE_GROUPS + MOE_PER_GROUP * g_idx
    in_group = (lane >= lo) & (lane < lo + MOE_PER_GROUP)
    pe = masked_softmax(in_group)
    p1 = jnp.max(pe, axis=-1, keepdims=True)
    l1 = first_lane(in_group & (pe == p1))
    rest_ok = in_group & (lane != l1)
    rest = jnp.where(rest_ok, pe, -1.0)
    p2 = jnp.max(rest, axis=-1, keepdims=True)
    l2 = first_lane(rest_ok & (rest == p2))
    psum = p1 + p2
    w1 = pg_top * p1 / psum
    w2 = pg_top * p2 / psum

    oh1 = (lane == l1).astype(F32)
    oh2 = (lane == l2).astype(F32)
    both = oh1 + oh2
    earlier = (lax.broadcasted_iota(jnp.int32, (tm, tm), 0) > lax.broadcasted_iota(jnp.int32, (tm, tm), 1))
    base = _dotf(earlier.astype(BF16), both.astype(BF16)) + run_ref[...]
    r1 = jnp.sum(base * oh1, axis=-1, keepdims=True)
    r2 = jnp.sum(base * oh2, axis=-1, keepdims=True)
    run_ref[...] = run_ref[...] + jnp.sum(both, axis=0, keepdims=True)
    cnt_ref[...] = run_ref[...]
    cols = ((l1 - MOE_GROUPS).astype(F32), (l2 - MOE_GROUPS).astype(F32), r1, r2, w1, w2)
    out = jnp.zeros(x.shape, F32)
    for j, c in enumerate(cols):
        out = jnp.where(lane == j, c, out)
    out_ref[...] = out


def moe_route(logits, b_group, b_expert):
    N, W = logits.shape
    tm = _pick(N, (512, 256, 128, 64, 32, 16, 8))
    bias = _pad_cols(jnp.concatenate([b_group, b_expert])[None, :], W)
    return pl.pallas_call(
        functools.partial(_route_body, tm=tm),
        grid=(N // tm,),
        in_specs=[pl.BlockSpec((tm, W), lambda i: (i, 0)), pl.BlockSpec((1, W), lambda i: (0, 0))],
        out_specs=[pl.BlockSpec((tm, W), lambda i: (i, 0)), pl.BlockSpec((1, W), lambda i: (0, 0))],
        out_shape=[jax.ShapeDtypeStruct((N, W), F32), jax.ShapeDtypeStruct((1, W), F32)],
        scratch_shapes=[pltpu.VMEM((1, W), F32)],
        compiler_params=_cparams("arbitrary"),
        name="moe_route",
    )(logits, bias)


def hier_moe(tokens, logits, b_group, b_expert, w1, w3, w2, layer):
    N, D = tokens.shape
    route, cnt = moe_route(logits, b_group, b_expert)
    eid = route[:, 0:MOE_TOPK].astype(jnp.int32).reshape(-1)
    rank = route[:, MOE_TOPK:2 * MOE_TOPK].astype(jnp.int32).reshape(-1)
    wts = route[:, 2 * MOE_TOPK:3 * MOE_TOPK]
    counts = cnt[0, MOE_GROUPS:MOE_GROUPS + MOE_EXPERTS].astype(jnp.int32)
    A = N * MOE_TOPK
    padded = (counts + MOE_BLOCK - 1) // MOE_BLOCK * MOE_BLOCK
    pend = jnp.cumsum(padded)
    dest = (pend - padded)[eid] + rank
    n_blocks = -(-A // MOE_BLOCK) + MOE_EXPERTS
    n_slots = n_blocks * MOE_BLOCK
    starts = jnp.arange(n_blocks, dtype=jnp.int32)[:, None] * MOE_BLOCK
    blk_e = jnp.minimum(jnp.sum((pend[None, :] <= starts).astype(jnp.int32), axis=1), MOE_EXPERTS - 1)
    bits = max(1, (A - 1).bit_length())
    order = jnp.sort((eid << bits) | jnp.arange(A, dtype=jnp.int32)) & ((1 << bits) - 1)
    slot_e = jnp.repeat(blk_e, MOE_BLOCK)
    pos = jnp.arange(n_slots, dtype=jnp.int32) - (pend - padded)[slot_e]
    src = jnp.minimum((jnp.cumsum(counts) - counts)[slot_e] + pos, A - 1)
    slot_tok = jnp.where(pos < counts[slot_e], order[src] // MOE_TOPK, 0)
    xs = tokens[slot_tok]
    ys = moe_experts(xs, blk_e, (pend[-1:] // MOE_BLOCK).astype(jnp.int32), w1, w3, w2, layer)
    d2 = dest.reshape(N, MOE_TOPK)
    return ys[d2[:, 0]].astype(F32) * wts[:, 0:1] + ys[d2[:, 1]].astype(F32) * wts[:, 1:2]


def _rope_tables(n_lat, n_ctx):
    rows = n_lat // GRID_W
    row = jnp.repeat(jnp.arange(rows, dtype=F32), GRID_W)
    col = jnp.tile(jnp.arange(GRID_W, dtype=F32), rows)
    n_freq = MLA_ROPE // 4
    inv = ROPE_BASE ** (-jnp.arange(n_freq, dtype=F32) / n_freq)
    ang = jnp.stack([row[:, None] * inv, col[:, None] * inv], axis=1)
    cos, sin = jnp.cos(ang), jnp.sin(ang)
    zf = jnp.zeros((n_lat, n_freq), F32)
    lat = lambda parts, fill: jnp.concatenate(
        [jnp.full((n_lat, MLA_NOPE), fill, F32)] + parts + [jnp.full((n_lat, MLA_PAD - MLA_QK), fill, F32)], axis=1)
    c = lat([cos[:, 0], cos[:, 0], cos[:, 1], cos[:, 1]], 1.0)
    s_lo = lat([-sin[:, 0], zf, -sin[:, 1], zf], 0.0)
    s_hi = lat([zf, sin[:, 0], zf, sin[:, 1]], 0.0)
    ctx = lambda fill: jnp.full((n_ctx, MLA_PAD), fill, F32)
    return jnp.concatenate([ctx(1.0), c], 0), jnp.concatenate([ctx(0.0), s_lo + s_hi], 0)


def _const_spec(shape):
    return pl.BlockSpec(shape, lambda i: (0,) * len(shape), pipeline_mode=pl.Buffered(1))


def _normmod(x, gain, shift, scale):
    return x * lax.rsqrt(jnp.mean(x * x, -1, keepdims=True) + EPS) * gain * (1 + scale) + shift


def _head_indicator(D, N):
    e = (jnp.arange(D)[:, None] // N == jnp.arange(128)[None, :]).astype(BF16)
    return e, e.T


def _seg_dot(x, e):
    xh, xl = _split(x)
    return jnp.dot(xh, e, preferred_element_type=F32) + jnp.dot(xl, e, preferred_element_type=F32)


def _dotf(a, b):
    return jnp.dot(a, b, preferred_element_type=F32)


def _rwkv_pre_body(*refs, tm, blocks_per_batch, ctx_blocks, vres):
    (h_ref, hp_ref, hn_ref, m_ref, gain_ref, mu_ref, w0_ref, a0_ref, kk_ref, ka_ref, e_ref, et_ref,
     wr_ref, wk_ref, wv_ref, w1_ref, w2_ref, a1_ref, a2_ref, g1_ref, g2_ref) = refs[:21]
    rest = refs[21:]
    if vres:
        v0_ref, v1_ref, v2_ref, vf_ref = rest[:4]
        rest = rest[4:]
    r_o, v_o, kk_o, lw0_o, lw1_o, k0_o, k1_o, ra0_o, ra1_o, gate_o = rest

    tb = pl.program_id(0) % blocks_per_batch
    seg_start = (tb == 0) | (tb == ctx_blocks)
    seg_end = (tb == ctx_blocks - 1) | (tb == blocks_per_batch - 1)
    shift, scale, gain = m_ref[0, 0:1, :], m_ref[0, 1:2, :], gain_ref[...]
    u = _normmod(h_ref[...], gain, shift, scale)
    up = jnp.where(seg_start, 0.0, _normmod(hp_ref[7:8, :], gain, shift, scale))
    un = jnp.where(seg_end, 0.0, _normmod(hn_ref[0:1, :], gain, shift, scale))
    row = lax.broadcasted_iota(jnp.int32, (tm, 1), 0)
    u_prev = jnp.where(row == 0, up, pltpu.roll(u, 1, 0))
    u_next = jnp.where(row == tm - 1, un, pltpu.roll(u, tm - 1, 0))
    xx = 0.5 * (u_prev + u_next) - u
    xr, xw, xk, xv, xa, xg = [(u + xx * mu_ref[j:j + 1, :]).astype(BF16) for j in range(6)]

    r = _dotf(xr, wr_ref[...])
    k = _dotf(xk, wk_ref[...])
    v = _dotf(xv, wv_ref[...])
    if vres:
        lo = _dotf(xv, v1_ref[...]).astype(BF16)
        v = v + (vf_ref[...] - v) * jax.nn.sigmoid(v0_ref[...] + _dotf(lo, v2_ref[...]))
    tl = jnp.tanh(_dotf(xw, w1_ref[...])).astype(BF16)
    al = _dotf(xa, a1_ref[...]).astype(BF16)
    gl = jax.nn.sigmoid(_dotf(xg, g1_ref[...])).astype(BF16)
    gate_o[...] = _dotf(gl, g2_ref[...])
    kx = k * kk_ref[...]
    inv = lax.rsqrt(_seg_dot(kx * kx, e_ref[...]) + EPS)
    r_o[...] = r
    v_o[...] = v
    kk_o[...] = kx * _seg_dot(inv, et_ref[...])
    for d, (lw_o, k_o, ra_o) in enumerate(((lw0_o, k0_o, ra0_o), (lw1_o, k1_o, ra1_o))):
        z = w0_ref[d:d + 1, :] + _dotf(tl, w2_ref[d])
        lw_o[...] = -math.exp(-0.5) * jax.nn.sigmoid(z)
        a = jax.nn.sigmoid(a0_ref[d:d + 1, :] + _dotf(al, a2_ref[d]))
        ra_o[...] = a
        k_o[...] = k * (1 + (a - 1) * ka_ref[...])


def _row_tile(seg):
    return _pick(seg, (256, 128, 64, 32, 16, 8))


def _pad_cols(w, n):
    return jnp.pad(w, ((0, 0), (0, n - w.shape[1])))


def _pad_rows(w, n):
    return jnp.pad(w, ((0, n - w.shape[0]), (0, 0)))


def rwkv_pre(h, m_seg, seg, T, n_ctx, gain, mu, wr, wk, wv, w0, w1, w2, a0, a1, a2, g1, g2, k_k, k_a, vres, v_first):
    M, D = h.shape
    tm = _row_tile(seg)
    lora = w1.shape[-1]
    e, et = _head_indicator(D, RWKV_HEAD)
    zero = jnp.zeros((lora, D), F32)
    w2p = jnp.stack([jnp.concatenate([w2[0], zero], 0), jnp.concatenate([zero, w2[1]], 0)]).astype(BF16)
    a2p = jnp.stack([jnp.concatenate([a2[0], zero], 0), jnp.concatenate([zero, a2[1]], 0)]).astype(BF16)
    gp = -(-g1.shape[1] // 128) * 128
    row = lambda a: a.reshape(1, D)
    consts = [row(gain), mu, w0, a0, row(k_k), row(k_a), e, et,
              wr.astype(BF16), wk.astype(BF16), wv.astype(BF16),
              jnp.concatenate([w1[0], w1[1]], 1).astype(BF16), w2p,
              jnp.concatenate([a1[0], a1[1]], 1).astype(BF16), a2p,
              _pad_cols(g1, gp).astype(BF16), _pad_rows(g2, gp).astype(BF16)]
    row_spec = pl.BlockSpec((tm, D), lambda i: (i, 0))
    last8 = M // 8 - 1
    in_specs = [row_spec,
                pl.BlockSpec((8, D), lambda i: (jnp.maximum(i * (tm // 8) - 1, 0), 0)),
                pl.BlockSpec((8, D), lambda i: (jnp.minimum((i + 1) * (tm // 8), last8), 0)),
                pl.BlockSpec((1, 6, D), lambda i: (i * tm // seg, 0, 0))]
    in_specs += [_const_spec(c.shape) for c in consts]
    args = [h, h, h, m_seg] + consts
    if vres is not None:
        v0, v1, v2 = vres
        extra = [row(v0), _pad_cols(v1, 128).astype(BF16), _pad_rows(v2, 128).astype(BF16)]
        in_specs += [_const_spec(c.shape) for c in extra] + [row_spec]
        args += extra + [v_first]
    return pl.pallas_call(
        functools.partial(_rwkv_pre_body, tm=tm, blocks_per_batch=T // tm, ctx_blocks=n_ctx // tm,
                          vres=vres is not None),
        grid=(M // tm,),
        in_specs=in_specs,
        out_specs=[row_spec] * 10,
        out_shape=[jax.ShapeDtypeStruct((M, D), F32)] * 10,
        compiler_params=_cparams("parallel"),
        name="rwkv7_pre",
    )(*args)


def _post_tail(xo, h_ref, m_ref, gain_ref, w_ref, wrt_ref, h_o, f_o, lg_o):
    h_new = h_ref[...] + m_ref[0, 2:3, :] * _dotf(xo, w_ref[...])
    h_o[...] = h_new
    f = _normmod(h_new, gain_ref[...], m_ref[0, 3:4, :], m_ref[0, 4:5, :])
    f_o[...] = f.astype(BF16)
    lg_o[...] = _dg(f, wrt_ref[...], _NN, 3)


def _rwkv_post_body(yf_ref, yb_ref, r_ref, k0_ref, k1_ref, v_ref, gate_ref, lnw_ref, lnb_ref, rk_ref, e_ref, et_ref,
                    h_ref, m_ref, gain_ref, w_ref, wrt_ref, h_o, f_o, lg_o):
    e, et = e_ref[...], et_ref[...]
    inv_n = 1.0 / RWKV_HEAD
    y = yf_ref[...] + yb_ref[...]
    yc = y - _seg_dot(_seg_dot(y, e) * inv_n, et)
    var = _seg_dot(_seg_dot(yc * yc, e) * inv_n, et)
    yn = yc * lax.rsqrt(var + GN_EPS) * lnw_ref[...] + lnb_ref[...]
    k_bonus = 0.5 * (k0_ref[...] + k1_ref[...])
    bonus = _seg_dot(_seg_dot(r_ref[...] * k_bonus * rk_ref[...], e), et) * v_ref[...]
    xo = ((yn + bonus) * gate_ref[...]).astype(BF16)
    _post_tail(xo, h_ref, m_ref, gain_ref, w_ref, wrt_ref, h_o, f_o, lg_o)


def _post_call(body, name, row_args, consts, h, m_seg, seg, gain, w_out, w_router):
    M, D = h.shape
    tm = _row_tile(seg)
    row_spec = lambda a: pl.BlockSpec((tm, a.shape[1]), lambda i: (i, 0))
    tail = [gain.reshape(1, D), w_out.astype(BF16), w_router]
    in_specs = ([row_spec(a) for a in row_args] + [_const_spec(c.shape) for c in consts] +
                [row_spec(h), pl.BlockSpec((1, 6, D), lambda i: (i * tm // seg, 0, 0))] +
                [_const_spec(c.shape) for c in tail])
    nr = w_router.shape[1]
    return pl.pallas_call(
        body,
        grid=(M // tm,),
        in_specs=in_specs,
        out_specs=[pl.BlockSpec((tm, D), lambda i: (i, 0)), pl.BlockSpec((tm, D), lambda i: (i, 0)),
                   pl.BlockSpec((tm, nr), lambda i: (i, 0))],
        out_shape=[jax.ShapeDtypeStruct((M, D), F32), jax.ShapeDtypeStruct((M, D), BF16),
                   jax.ShapeDtypeStruct((M, nr), F32)],
        compiler_params=_cparams("parallel"),
        name=name,
    )(*row_args, *consts, h, m_seg, *tail)


def _hy_post_body(a_ref, of_ref, ob_ref, z_ref, og_ref, h_ref, m_ref, gain_ref, w_ref, wrt_ref, h_o, f_o, lg_o):
    o = of_ref[...] + ob_ref[...]
    z = z_ref[...]
    parts = [a_ref[...]]
    for hd in range(GDN_HEADS):
        sl = slice(hd * GDN_DV, (hd + 1) * GDN_DV)
        oh, zh = o[:, sl], z[:, sl]
        on = oh * lax.rsqrt(jnp.mean(oh * oh, -1, keepdims=True) + EPS) * og_ref[...]
        parts.append(on * (zh * jax.nn.sigmoid(zh)))
    xo = jnp.concatenate(parts, axis=-1).astype(BF16)
    _post_tail(xo, h_ref, m_ref, gain_ref, w_ref, wrt_ref, h_o, f_o, lg_o)


def hy_post(a, of, ob, z, out_g, h, m_seg, seg, gain, w_out, w_router):
    return _post_call(_hy_post_body, "hybrid_post", [a, of, ob, z], [out_g.reshape(1, -1)], h, m_seg, seg,
                      gain, w_out, w_router)


def _hy_pre_body(h_ref, m_ref, gain_ref, c_ref, s_ref, wq1, wkv1, wpe, wpe2, wgq, wz, wab, qag, kvag,
                 wqb, wqb2, wkn, wv, qng, qng2, kng, kng2, q_o, k_o, v_o, gq_o, z_o, ab_o):
    u = _normmod(h_ref[...], gain_ref[...], m_ref[0, 0:1, :], m_ref[0, 1:2, :]).astype(BF16)
    gq_o[...] = _dotf(u, wgq[...])
    z_o[...] = _dotf(u, wz[...])
    ab_o[...] = _dotf(u, wab[...])
    cq = _dotf(u, wq1[...])
    ckv = _dotf(u, wkv1[...])
    pe = _dotf(u, wpe[...])
    pe2 = _dotf(u, wpe2[...])
    cq = (cq * lax.rsqrt(jnp.mean(cq * cq, -1, keepdims=True) + EPS) * qag[...]).astype(BF16)
    ckv = (ckv * lax.rsqrt(jnp.mean(ckv * ckv, -1, keepdims=True) + EPS) * kvag[...]).astype(BF16)
    q = _dotf(cq, wqb[...])
    q2 = _dotf(cq, wqb2[...])
    kn = _dotf(ckv, wkn[...])
    vv = _dotf(ckv, wv[...])
    c, s = c_ref[...], s_ref[...]
    qc, qs = qng[...] * c, qng2[...] * s
    kc, ks = kng[...] * c, kng2[...] * s
    k_rot = pe2 * ks
    one_col = (lax.broadcasted_iota(jnp.int32, (1, MLA_PAD), 1) == MLA_V).astype(F32)
    inv_d = 1.0 / MLA_QK

    def inv_rms(t):
        return lax.rsqrt(jnp.sum(t * t, -1, keepdims=True) * inv_d + EPS)

    for hd in range(MLA_HEADS):
        sl = slice(hd * MLA_PAD, (hd + 1) * MLA_PAD)
        qh = q[:, sl]
        kh = kn[:, sl] + pe
        q_o[:, sl] = (inv_rms(qh) * (qh * qc + q2[:, sl] * qs)).astype(BF16)
        k_o[:, sl] = (inv_rms(kh) * (kh * kc + k_rot)).astype(BF16)
        v_o[:, sl] = (vv[:, sl] + one_col).astype(BF16)


def _pad_heads(w, heads, width, to):
    K = w.shape[0]
    return jnp.pad(w.reshape(K, heads, width), ((0, 0), (0, 0), (0, to - width))).reshape(K, heads * to)


def hy_pre(h, m_seg, seg, T, gain, rope, w_in, qa_g, w_qb, kva_g, w_kvb, qn_g, kn_g):
    M, D = h.shape
    tm = _row_tile(seg)
    H = MLA_HEADS
    c0, c1, c2 = MLA_COLS, MLA_COLS + GDN_QKV, MLA_COLS + GDN_QKV + GDN_Z
    kvl = MLA_Q_LORA + MLA_KV_LORA
    wb = w_in.astype(BF16)
    wpe = jnp.pad(wb[:, kvl:c0], ((0, 0), (MLA_NOPE, MLA_PAD - MLA_QK)))
    wkv = w_kvb.reshape(MLA_KV_LORA, H, MLA_NOPE + MLA_V)
    pad1 = lambda g: jnp.pad(g, (0, MLA_PAD - MLA_QK)).reshape(1, MLA_PAD)
    lane = jnp.arange(MLA_PAD)
    rot = (lane >= MLA_NOPE) & (lane < MLA_QK)
    n_freq = MLA_ROPE // 4
    partner = jnp.where(rot, jnp.where(((lane - MLA_NOPE) // n_freq) % 2 == 0, lane + n_freq, lane - n_freq), lane)
    wqb_pad = _pad_heads(w_qb, H, MLA_QK, MLA_PAD).astype(BF16)
    wqb2 = wqb_pad.reshape(MLA_Q_LORA, H, MLA_PAD)[:, :, partner].reshape(MLA_Q_LORA, H * MLA_PAD)
    consts = [wb[:, :MLA_Q_LORA], wb[:, MLA_Q_LORA:kvl], wpe, wpe[:, partner], wb[:, c0:c1], wb[:, c1:c2],
              _pad_cols(wb[:, c2:], 128), qa_g.reshape(1, -1), kva_g.reshape(1, -1),
              wqb_pad, wqb2,
              _pad_heads(wkv[:, :, :MLA_NOPE].reshape(MLA_KV_LORA, -1), H, MLA_NOPE, MLA_PAD).astype(BF16),
              _pad_heads(wkv[:, :, MLA_NOPE:].reshape(MLA_KV_LORA, -1), H, MLA_V, MLA_PAD).astype(BF16),
              pad1(qn_g), pad1(qn_g)[:, partner], pad1(kn_g), pad1(kn_g)[:, partner]]
    bpb = T // tm
    row = lambda n: pl.BlockSpec((tm, n), lambda i: (i, 0))
    tab = pl.BlockSpec((tm, MLA_PAD), lambda i: (i % bpb, 0))
    in_specs = ([row(D), pl.BlockSpec((1, 6, D), lambda i: (i * tm // seg, 0, 0)), _const_spec((1, D)), tab, tab]
                + [_const_spec(c.shape) for c in consts])
    wide = H * MLA_PAD
    return pl.pallas_call(
        _hy_pre_body,
        grid=(M // tm,),
        in_specs=in_specs,
        out_specs=[row(wide), row(wide), row(wide), row(GDN_QKV), row(GDN_Z), row(128)],
        out_shape=[jax.ShapeDtypeStruct((M, wide), BF16)] * 3 + [jax.ShapeDtypeStruct((M, GDN_QKV), F32),
                   jax.ShapeDtypeStruct((M, GDN_Z), F32), jax.ShapeDtypeStruct((M, 128), F32)],
        compiler_params=_cparams("parallel"),
        name="hybrid_pre",
    )(h, m_seg, gain.reshape(1, D), *rope, *consts)


def _gdn_prep_body(x_ref, xp_ref, xn_ref, w_ref, q_o, k_o, v_o, *, tm, blocks_per_batch, ctx_blocks):
    tb = pl.program_id(0) % blocks_per_batch
    seg_start = (tb == 0) | (tb == ctx_blocks)
    seg_end = (tb == ctx_blocks - 1) | (tb == blocks_per_batch - 1)
    x = x_ref[...]
    xp = jnp.where(seg_start, 0.0, xp_ref[...])
    xn = jnp.where(seg_end, 0.0, xn_ref[...])
    row = lax.broadcasted_iota(jnp.int32, (tm, 1), 0)
    half = GDN_CONV // 2
    acc = x * w_ref[half:half + 1, :]
    for s in range(1, half + 1):
        before = pltpu.roll(x, s, 0)
        after = pltpu.roll(x, tm - s, 0)
        for r in range(s):
            before = jnp.where(row == r, xp[8 - s + r:8 - s + r + 1, :], before)
            after = jnp.where(row == tm - s + r, xn[r:r + 1, :], after)
        acc = acc + before * w_ref[half - s:half - s + 1, :] + after * w_ref[half + s:half + s + 1, :]
    y = acc * jax.nn.sigmoid(acc)
    nk = GDN_HEADS * GDN_DK
    for hd in range(GDN_HEADS):
        sl = slice(hd * GDN_DK, (hd + 1) * GDN_DK)
        qh = y[:, sl]
        kh = y[:, nk + hd * GDN_DK:nk + (hd + 1) * GDN_DK]
        q_o[:, sl] = qh * lax.rsqrt(jnp.sum(qh * qh, -1, keepdims=True) + EPS) * GDN_DK ** -0.5
        k_o[:, sl] = kh * lax.rsqrt(jnp.sum(kh * kh, -1, keepdims=True) + EPS)
    v_o[...] = y[:, 2 * nk:]


def gdn_prep(gq, conv_w, seg, T, n_ctx):
    M, W = gq.shape
    tm = _row_tile(seg)
    last8 = M // 8 - 1
    nk = GDN_HEADS * GDN_DK
    row = lambda n: pl.BlockSpec((tm, n), lambda i: (i, 0))
    return pl.pallas_call(
        functools.partial(_gdn_prep_body, tm=tm, blocks_per_batch=T // tm, ctx_blocks=n_ctx // tm),
        grid=(M // tm,),
        in_specs=[row(W),
                  pl.BlockSpec((8, W), lambda i: (jnp.maximum(i * (tm // 8) - 1, 0), 0)),
                  pl.BlockSpec((8, W), lambda i: (jnp.minimum((i + 1) * (tm // 8), last8), 0)),
                  _const_spec(conv_w.shape)],
        out_specs=[row(nk), row(nk), row(W - 2 * nk)],
        out_shape=[jax.ShapeDtypeStruct((M, nk), F32), jax.ShapeDtypeStruct((M, nk), F32),
                   jax.ShapeDtypeStruct((M, W - 2 * nk), F32)],
        compiler_params=_cparams("parallel"),
        name="gdn_prep",
    )(gq, gq, gq, conv_w)


def rwkv_post(yf, yb, r, k0, k1, v, gate, ln_w, ln_b, r_k, h, m_seg, seg, gain, wo, w_router):
    D = h.shape[1]
    e, et = _head_indicator(D, RWKV_HEAD)
    consts = [ln_w.reshape(1, D), ln_b.reshape(1, D), r_k.reshape(1, D), e, et]
    return _post_call(_rwkv_post_body, "rwkv7_post", [yf, yb, r, k0, k1, v, gate], consts, h, m_seg, seg,
                      gain, wo, w_router)


def kernel(x, c, ctx, c_ctx, ada_w, ada_b, norm_mix, norm_ffn, hy_w_in, hy_w_out, mla_qa_norm, mla_w_qb, mla_kva_norm, mla_w_kvb, mla_q_norm, mla_k_norm, gdn_conv, gdn_a_log, gdn_dt_bias, gdn_out_norm, rk_mu, rk_wr, rk_wk, rk_wv, rk_wo, rk_w0, rk_w1, rk_w2, rk_a0, rk_a1, rk_a2, rk_g1, rk_g2, rk_kk, rk_ka, rk_rk, rk_ln_w, rk_ln_b, rk_v0, rk_v1, rk_v2, moe_w_group, moe_b_group, moe_w_expert, moe_b_expert, moe_w1, moe_w3, moe_w2):
    B, S, D = x.shape
    L = ctx.shape[1]
    T = L + S
    depth = ada_w.shape[0]
    rope = _rope_tables(S, L)
    n_rows = -(-(B + 1) // 8) * 8
    sc = jnp.concatenate([jax.nn.silu(c), jax.nn.silu(c_ctx)[None], jnp.zeros((n_rows - B - 1, D), F32)], 0)
    M = B * T
    h = jnp.concatenate([ctx, x], axis=1).reshape(M, D)
    seg = math.gcd(L, S)
    nseg = T // seg
    v_first = None
    for l in range(depth):
        m = mm(sc, ada_w[l], hi=True) + ada_b[l]
        m_lat = jnp.broadcast_to(m[:B].reshape(B, 1, 6, D), (B, S // seg, 6, D))
        m_ctx = jnp.broadcast_to(m[B].reshape(1, 1, 6, D), (B, L // seg, 6, D))
        m_seg = jnp.concatenate([m_ctx, m_lat], axis=1).reshape(B * nseg, 6, D)

        router = _pad_cols(jnp.concatenate([moe_w_group[l], moe_w_expert[l]], axis=1), 128)
        j = l // 2
        b3 = lambda a: a.reshape(B, T, a.shape[-1])
        if l % 2 == 0:
            q, k, v, gq, z, ab = hy_pre(h, m_seg, seg, T, norm_mix[l], rope, hy_w_in[j], mla_qa_norm[j],
                                        mla_w_qb[j], mla_kva_norm[j], mla_w_kvb[j], mla_q_norm[j], mla_k_norm[j])
            q, k, v = b3(q), b3(k), b3(v)
            a_lat = attention(q[:, L:], k, v)
            a_ctx = attention(q[:, :L], k[:, :L], v[:, :L])
            a = jnp.concatenate([a_ctx, a_lat], axis=1).reshape(M, -1)
            gq_, gk_, gv_ = gdn_prep(gq, gdn_conv[j], seg, T, L)
            ab = ab[:, :GDN_AB].reshape(B, T, 2, 2, GDN_HEADS)
            g = -jnp.exp(gdn_a_log[j]) * jax.nn.softplus(ab[:, :, :, 0] + gdn_dt_bias[j])
            beta = jax.nn.sigmoid(ab[:, :, :, 1])
            of, ob = gdn_scan(b3(gq_), b3(gk_), b3(gv_), g, beta, L)
            h, f, logits = hy_post(a, of.reshape(M, -1), ob.reshape(M, -1), z, gdn_out_norm[j], h, m_seg, seg,
                                   norm_ffn[l], hy_w_out[j], router)
        else:
            vres = None if j == 0 else (rk_v0[j - 1], rk_v1[j - 1], rk_v2[j - 1])
            r, v, kk, lw0, lw1, k0, k1, ra0, ra1, gate = rwkv_pre(
                h, m_seg, seg, T, L, norm_mix[l], rk_mu[j], rk_wr[j], rk_wk[j], rk_wv[j], rk_w0[j], rk_w1[j],
                rk_w2[j], rk_a0[j], rk_a1[j], rk_a2[j], rk_g1[j], rk_g2[j], rk_kk[j], rk_ka[j], vres, v_first)
            if j == 0:
                v_first = v
            yf, yb = rwkv_scan(b3(r), b3(v), b3(kk), [b3(lw0), b3(lw1)], [b3(k0), b3(k1)], [b3(ra0), b3(ra1)], L)
            h, f, logits = rwkv_post(yf.reshape(M, D), yb.reshape(M, D), r, k0, k1, v, gate, rk_ln_w[j], rk_ln_b[j],
                                     rk_rk[j], h, m_seg, seg, norm_ffn[l], rk_wo[j], router)
        moe_out = hier_moe(f, logits, moe_b_group[l], moe_b_expert[l], moe_w1, moe_w3, moe_w2, l)
        gate_ffn = m_seg[:, None, 5, :]
        h = (h.reshape(B * nseg, seg, D) + gate_ffn * moe_out.reshape(B * nseg, seg, D)).reshape(M, D)
    return h.reshape(B, T, D)[:, L:]
```
